```python
import jax, jax.numpy as jnp
from jax import lax
import numpy as np

D_MODEL = 1024
BATCH = 16
SEQ = 2048
DEPTH = 2

CHUNK = 64
N_PREV_CHUNKS = 8
BAND = (N_PREV_CHUNKS + 1) * CHUNK
HEAD_DIM = 64
N_HEADS_ATTN = 8
N_HEADS_RWKV = 8
D_ATTN = N_HEADS_ATTN * HEAD_DIM
D_RWKV = N_HEADS_RWKV * HEAD_DIM
REL_CLIP = 128
DECAY_LORA = 64
AAA_LORA = 64
MV_LORA = 32
GATE_LORA = 160
N_GROUPS = 4
EXPERTS_PER_GROUP = 8
N_EXPERTS = N_GROUPS * EXPERTS_PER_GROUP
TOP_K_IN_GROUP = 2
D_EXPERT = 256
D_PLE = 256
DEEPNORM_ALPHA = (2 * DEPTH) ** 0.25
DEEPNORM_BETA = (8 * DEPTH) ** -0.25
LN_EPS = 1e-5
GN_EPS = 64e-5
NEG_INF = -1e30

Q0 = 0
K0 = D_ATTN
V0 = 2 * D_ATTN
RW0 = 3 * D_ATTN
RW_COLS = 3 * D_RWKV + DECAY_LORA + AAA_LORA + GATE_LORA
GATE0 = RW0 + RW_COLS
N_IN = GATE0 + 2 * D_MODEL

kernel_name = "hybrid_chunkattn_rwkv7_hiermoe_deepnorm"


def layer_norm(x, g, b):
    x32 = x.astype(jnp.float32)
    mu = jnp.mean(x32, axis=-1, keepdims=True)
    var = jnp.mean(jnp.square(x32 - mu), axis=-1, keepdims=True)
    y = (x32 - mu) * lax.rsqrt(var + LN_EPS)
    return (y * g.astype(jnp.float32) + b.astype(jnp.float32)).astype(x.dtype)


def chunked_band_attention(q, k, v, rel_bias):
    B, S, _ = q.shape
    nc = S // CHUNK
    q = q.reshape(B, nc, CHUNK, N_HEADS_ATTN, HEAD_DIM)
    pad = ((0, 0), (N_PREV_CHUNKS * CHUNK, 0), (0, 0))
    kp = jnp.pad(k, pad).reshape(B, nc + N_PREV_CHUNKS, CHUNK, N_HEADS_ATTN, HEAD_DIM)
    vp = jnp.pad(v, pad).reshape(B, nc + N_PREV_CHUNKS, CHUNK, N_HEADS_ATTN, HEAD_DIM)
    kb = jnp.concatenate([kp[:, j:j + nc] for j in range(N_PREV_CHUNKS + 1)], axis=2)
    vb = jnp.concatenate([vp[:, j:j + nc] for j in range(N_PREV_CHUNKS + 1)], axis=2)
    scores = jnp.einsum("bcqhd,bckhd->bhcqk", q, kb).astype(jnp.float32) * (HEAD_DIM ** -0.5)
    qi = jnp.arange(CHUNK)[:, None]
    kj = jnp.arange(BAND)[None, :]
    dist = qi - (kj - N_PREV_CHUNKS * CHUNK)
    idx = jnp.clip(dist, -REL_CLIP, REL_CLIP) + REL_CLIP
    bias = rel_bias[:, idx].astype(jnp.float32)
    key_chunk = jnp.arange(nc)[:, None] - N_PREV_CHUNKS + (jnp.arange(BAND) // CHUNK)[None, :]
    valid = key_chunk >= 0
    scores = jnp.where(valid[None, None, :, None, :], scores + bias[None, :, None], NEG_INF)
    prob = jax.nn.softmax(scores, axis=-1).astype(v.dtype)
    out = jnp.einsum("bhcqk,bckhd->bcqhd", prob, vb)
    return out.reshape(B, S, D_ATTN)


def rwkv7_scan(r, w, k, v, a, b):
    B, S, H, N = r.shape
    xs = tuple(jnp.swapaxes(t.astype(jnp.float32), 0, 1) for t in (r, w, k, v, a, b))

    def step(state, inp):
        rt, wt, kt, vt, at, bt = inp
        sa = jnp.einsum("bhij,bhj->bhi", state, at)
        state = state * wt[:, :, None, :] + sa[..., None] * bt[:, :, None, :] + vt[..., None] * kt[:, :, None, :]
        return state, jnp.einsum("bhij,bhj->bhi", state, rt)

    state0 = jnp.zeros((B, H, N, N), jnp.float32)
    _, ys = lax.scan(step, state0, xs)
    return jnp.swapaxes(ys, 0, 1)


def rwkv7_time_mix(cols, mu, v_first, decay_base, decay_up, aaa_base, aaa_up, gate_up,
                   k_k, k_a, r_k, gn_g, gn_b, vres):
    B, S, _ = cols.shape
    prev = jnp.pad(cols, ((0, 0), (1, 0), (0, 0)))[:, :-1]
    cols = cols + (prev - cols) * mu
    splits = [D_RWKV, 2 * D_RWKV, 3 * D_RWKV, 3 * D_RWKV + DECAY_LORA, 3 * D_RWKV + DECAY_LORA + AAA_LORA]
    r, k, v, wd, ad, gd = jnp.split(cols, splits, axis=-1)
    w_log = -jax.nn.softplus(-(decay_base + jnp.tanh(wd) @ decay_up)) - 0.5
    decay = jnp.exp(-jnp.exp(w_log.astype(jnp.float32)))
    a = jax.nn.sigmoid(aaa_base + ad @ aaa_up)
    g = jax.nn.sigmoid(gd) @ gate_up
    heads = lambda t: t.reshape(B, S, N_HEADS_RWKV, HEAD_DIM)
    kk = heads(k * k_k).astype(jnp.float32)
    kk = kk * lax.rsqrt(jnp.maximum(jnp.sum(kk * kk, axis=-1, keepdims=True), 1e-24))
    k = k * (1.0 + (a - 1.0) * k_a)
    if vres is None:
        v_first = v
    else:
        vres_base, vres_down, vres_up = vres
        v = v + (v_first - v) * jax.nn.sigmoid(vres_base + (v @ vres_down) @ vres_up)
    r_h, k_h, v_h, a_h = heads(r), heads(k), heads(v), heads(a)
    y = rwkv7_scan(r_h, heads(decay), k_h, v_h, -kk, kk * a_h.astype(jnp.float32))
    m = jnp.mean(y, axis=-1, keepdims=True)
    var = jnp.mean(jnp.square(y - m), axis=-1, keepdims=True)
    yn = ((y - m) * lax.rsqrt(var + GN_EPS)).reshape(B, S, D_RWKV)
    yn = yn * gn_g.astype(jnp.float32) + gn_b.astype(jnp.float32)
    bonus = (jnp.sum(r_h * k_h * r_k, axis=-1, keepdims=True) * v_h).reshape(B, S, D_RWKV)
    out = (yn.astype(cols.dtype) + bonus) * g
    return out, v_first


def hier_moe(x, w_grp, b_grp, w_exp, b_exp, e_gate, e_up, e_down):
    B, S, D = x.shape
    T = B * S
    xt = x.reshape(T, D)
    glog = (xt @ w_grp + b_grp).astype(jnp.float32)
    gprob = jax.nn.softmax(glog, axis=-1)
    grp = jnp.argmax(glog, axis=-1)
    g_gate = jnp.take_along_axis(gprob, grp[:, None], axis=1)
    elog = (xt @ w_exp + b_exp).astype(jnp.float32).reshape(T, N_GROUPS, EXPERTS_PER_GROUP)
    sel = jnp.take_along_axis(elog, grp[:, None, None], axis=1)[:, 0]
    top_v, top_i = lax.top_k(sel, TOP_K_IN_GROUP)
    wk = jax.nn.softmax(top_v, axis=-1) * g_gate
    eid = grp[:, None] * EXPERTS_PER_GROUP + top_i
    comb = jnp.sum(jax.nn.one_hot(eid, N_EXPERTS, dtype=jnp.float32) * wk[..., None], axis=1).astype(x.dtype)
    out = jnp.zeros_like(xt)
    for e in range(N_EXPERTS):
        h = jax.nn.silu(xt @ e_gate[e]) * (xt @ e_up[e])
        out = out + (h * comb[:, e:e + 1]) @ e_down[e]
    return out.reshape(B, S, D)


def setup_inputs(seed: int = 0) -> dict:
    key = jax.random.key(seed)
    keys = iter(jax.random.split(key, 48))
    nrm = lambda shape, s: jax.random.normal(next(keys), shape, jnp.float32) * s
    uni = lambda shape, lo, hi: jax.random.uniform(next(keys), shape, jnp.float32, lo, hi)
    col_scale = jnp.ones((N_IN,), jnp.float32)
    col_scale = col_scale.at[V0:RW0].set(DEEPNORM_BETA)
    col_scale = col_scale.at[RW0 + 2 * D_RWKV:RW0 + 3 * D_RWKV].set(DEEPNORM_BETA)
    return {
        "x": nrm((BATCH, SEQ, D_MODEL), 1.0),
        "p": nrm((DEPTH, BATCH, SEQ, D_PLE), 1.0),
        "ln_in_g": 1.0 + nrm((D_MODEL,), 0.02),
        "ln_in_b": nrm((D_MODEL,), 0.02),
        "rel_bias": nrm((N_HEADS_ATTN, 2 * REL_CLIP + 1), 0.1),
        "w_in": nrm((DEPTH, D_MODEL, N_IN), D_MODEL ** -0.5) * col_scale,
        "tok_mix": uni((DEPTH, RW_COLS), 0.0, 1.0),
        "decay_base": uni((DEPTH, D_RWKV), -6.0, -1.0),
        "decay_up": nrm((DEPTH, DECAY_LORA, D_RWKV), DECAY_LORA ** -0.5),
        "aaa_base": nrm((DEPTH, D_RWKV), 0.1),
        "aaa_up": nrm((DEPTH, AAA_LORA, D_RWKV), AAA_LORA ** -0.5),
        "gate_up": nrm((DEPTH, GATE_LORA, D_RWKV), GATE_LORA ** -0.5),
        "k_k": 0.85 + nrm((DEPTH, D_RWKV), 0.02),
        "k_a": 1.0 + nrm((DEPTH, D_RWKV), 0.02),
        "r_k": nrm((DEPTH, N_HEADS_RWKV, HEAD_DIM), 0.1),
        "vres_base": nrm((DEPTH - 1, D_RWKV), 0.1),
        "vres_down": nrm((DEPTH - 1, D_RWKV, MV_LORA), D_RWKV ** -0.5),
        "vres_up": nrm((DEPTH - 1, MV_LORA, D_RWKV), MV_LORA ** -0.5),
        "gn_g": 1.0 + nrm((DEPTH, D_RWKV), 0.02),
        "gn_b": nrm((DEPTH, D_RWKV), 0.02),
        "w_branch_attn": nrm((DEPTH, D_ATTN, D_MODEL), D_ATTN ** -0.5 * DEEPNORM_BETA),
        "w_branch_rwkv": nrm((DEPTH, D_RWKV, D_MODEL), D_RWKV ** -0.5 * DEEPNORM_BETA),
        "w_out": nrm((DEPTH, D_MODEL, D_MODEL), D_MODEL ** -0.5 * DEEPNORM_BETA),
        "router_grp": nrm((DEPTH, D_MODEL, N_GROUPS), D_MODEL ** -0.5),
        "router_grp_bias": nrm((DEPTH, N_GROUPS), 0.01),
        "router_exp": nrm((DEPTH, D_MODEL, N_EXPERTS), D_MODEL ** -0.5),
        "router_exp_bias": nrm((DEPTH, N_EXPERTS), 0.01),
        "exp_gate": nrm((DEPTH, N_EXPERTS, D_MODEL, D_EXPERT), D_MODEL ** -0.5),
        "exp_up": nrm((DEPTH, N_EXPERTS, D_MODEL, D_EXPERT), D_MODEL ** -0.5),
        "exp_down": nrm((DEPTH, N_EXPERTS, D_EXPERT, D_MODEL), D_EXPERT ** -0.5 * DEEPNORM_BETA),
        "ple_proj": nrm((DEPTH, D_PLE, D_MODEL), D_PLE ** -0.5 * DEEPNORM_BETA),
        "ple_gate": nrm((DEPTH, D_MODEL, D_MODEL), D_MODEL ** -0.5),
        "ln_g": 1.0 + nrm((DEPTH, 3, D_MODEL), 0.02),
        "ln_b": nrm((DEPTH, 3, D_MODEL), 0.02),
    }


def reference(x, p, ln_in_g, ln_in_b, rel_bias, w_in, tok_mix, decay_base, decay_up,
              aaa_base, aaa_up, gate_up, k_k, k_a, r_k, vres_base, vres_down, vres_up,
              gn_g, gn_b, w_branch_attn, w_branch_rwkv, w_out, router_grp, router_grp_bias,
              router_exp, router_exp_bias, exp_gate, exp_up, exp_down, ple_proj, ple_gate,
              ln_g, ln_b):
    x = layer_norm(x, ln_in_g, ln_in_b)
    v_first = None
    for i in range(DEPTH):
        proj = x @ w_in[i]
        y_attn = chunked_band_attention(proj[..., Q0:K0], proj[..., K0:V0], proj[..., V0:RW0], rel_bias)
        vres = None if i == 0 else (vres_base[i - 1], vres_down[i - 1], vres_up[i - 1])
        y_rwkv, v_first = rwkv7_time_mix(proj[..., RW0:GATE0], tok_mix[i], v_first, decay_base[i],
                                         decay_up[i], aaa_base[i], aaa_up[i], gate_up[i], k_k[i],
                                         k_a[i], r_k[i], gn_g[i], gn_b[i], vres)
        gate_a = jax.nn.sigmoid(proj[..., GATE0:GATE0 + D_MODEL])
        gate_b = jax.nn.sigmoid(proj[..., GATE0 + D_MODEL:])
        mixed = (gate_a * (y_attn @ w_branch_attn[i]) + gate_b * (y_rwkv @ w_branch_rwkv[i])) @ w_out[i]
        x = layer_norm(DEEPNORM_ALPHA * x + mixed, ln_g[i, 0], ln_b[i, 0])
        moe = hier_moe(x, router_grp[i], router_grp_bias[i], router_exp[i], router_exp_bias[i],
                       exp_gate[i], exp_up[i], exp_down[i])
        x = layer_norm(DEEPNORM_ALPHA * x + moe, ln_g[i, 1], ln_b[i, 1])
        ple = (p[i] @ ple_proj[i]) * jax.nn.sigmoid(x @ ple_gate[i])
        x = layer_norm(DEEPNORM_ALPHA * x + ple, ln_g[i, 2], ln_b[i, 2])
    return x
```

```python
import functools
import math

import jax
import jax.numpy as jnp
from jax import lax
from jax.experimental import pallas as pl
from jax.experimental.pallas import tpu as pltpu

F32 = jnp.float32
BF16 = jnp.bfloat16

D_MODEL = 1024
CHUNK = 64
N_PREV_CHUNKS = 8
HEAD_DIM = 64
N_HEADS = 8
D_HEADS = N_HEADS * HEAD_DIM
REL_CLIP = 128
DECAY_LORA = 64
AAA_LORA = 64
GATE_LORA = 160
MV_LORA = 32
N_GROUPS = 4
EXPERTS_PER_GROUP = 8
N_EXPERTS = N_GROUPS * EXPERTS_PER_GROUP
D_EXPERT = 256
DEPTH = 2
DEEPNORM_ALPHA = (2 * DEPTH) ** 0.25
LN_EPS = 1e-5
GN_EPS = 64e-5
NEG_INF = -1e30

LANES = 128
ATTN_QBLK = 2 * CHUNK
ATTN_WIN = (N_PREV_CHUNKS + 2) * CHUNK
RW_COLS_PAD = 2048
VMEM_LIMIT = 56 * 1024 * 1024


def _cparams(*sem):
    return pltpu.CompilerParams(dimension_semantics=sem, vmem_limit_bytes=VMEM_LIMIT)


def _dot(a, b):
    return jnp.dot(a, b, preferred_element_type=F32)


def _dot_nt(a, b):
    return lax.dot_general(a, b, (((1,), (1,)), ((), ())), preferred_element_type=F32)


def _dot_tn(a, b):
    return lax.dot_general(a, b, (((0,), (0,)), ((), ())), preferred_element_type=F32)


def _split2(x):
    hi = x.astype(BF16)
    lo = (x - hi.astype(F32)).astype(BF16)
    return hi, lo


def _dot3(a, b):
    ah, al = _split2(a)
    bh, bl = _split2(b)
    return _dot(ah, bh) + (_dot(al, bh) + _dot(ah, bl))


def _dot_exact_lhs(a_bf16, x):
    x1 = x.astype(BF16)
    r1 = x - x1.astype(F32)
    x2 = r1.astype(BF16)
    x3 = (r1 - x2.astype(F32)).astype(BF16)
    return _dot(a_bf16, x1) + (_dot(a_bf16, x2) + _dot(a_bf16, x3))


def _seg_sum(x, e_ones):
    hi, lo = _split2(x)
    return _dot(hi, e_ones) + _dot(lo, e_ones)


def _layer_norm(x, g, b):
    mu = jnp.mean(x, axis=-1, keepdims=True)
    xc = x - mu
    var = jnp.mean(xc * xc, axis=-1, keepdims=True)
    return xc * lax.rsqrt(var + LN_EPS) * g + b


def _sigmoid(x):
    return 1.0 / (1.0 + jnp.exp(-x))


def _full(shape):
    nd = len(shape)
    return pl.BlockSpec(shape, lambda *_: (0,) * nd)


def _proj_kernel(x_ref, g_ref, b_ref, wqkv_ref, wrw_ref, wgate_ref, *out_refs, apply_ln):
    x = x_ref[...]
    if apply_ln:
        xn_ref, qkv_ref, rw_ref, gate_ref = out_refs
        x = _layer_norm(x, g_ref[...], b_ref[...])
        xn_ref[...] = x
    else:
        qkv_ref, rw_ref, gate_ref = out_refs
    xb = x.astype(BF16)
    qkv_ref[...] = _dot(xb, wqkv_ref[...]).astype(BF16)
    rw_ref[...] = _dot(xb, wrw_ref[...])
    gate_ref[...] = _sigmoid(_dot(xb, wgate_ref[...])).astype(BF16)


def _proj(x, g, b, wqkv, wrw, wgate, apply_ln):
    t = x.shape[0]
    tm = min(256, t)
    row = lambda w: pl.BlockSpec((tm, w), lambda i: (i, 0))
    out_shape = [jax.ShapeDtypeStruct((t, 3 * D_HEADS), BF16),
                 jax.ShapeDtypeStruct((t, RW_COLS_PAD), F32),
                 jax.ShapeDtypeStruct((t, 2 * D_MODEL), BF16)]
    out_specs = [row(3 * D_HEADS), row(RW_COLS_PAD), row(2 * D_MODEL)]
    if apply_ln:
        out_shape = [jax.ShapeDtypeStruct((t, D_MODEL), F32)] + out_shape
        out_specs = [row(D_MODEL)] + out_specs
    return pl.pallas_call(
        functools.partial(_proj_kernel, apply_ln=apply_ln),
        grid=(t // tm,),
        in_specs=[row(D_MODEL), _full(g.shape), _full(b.shape),
                  _full(wqkv.shape), _full(wrw.shape), _full(wgate.shape)],
        out_specs=out_specs,
        out_shape=out_shape,
        compiler_params=_cparams("parallel"),
        name="proj_ln" if apply_ln else "proj",
    )(x, g, b, wqkv, wrw, wgate)


def _attn_kernel(q_ref, k_ref, v_ref, bias_ref, o_ref):
    qb = pl.program_id(1)
    ws = pl.multiple_of(jnp.maximum(qb * ATTN_QBLK - N_PREV_CHUNKS * CHUNK, 0), ATTN_QBLK)
    q = q_ref[0]
    kw = k_ref[0, pl.ds(ws, ATTN_WIN), :]
    vw = v_ref[0, pl.ds(ws, ATTN_WIN), :]
    outs = []
    for h in range(N_HEADS):
        hs = slice(h * HEAD_DIM, (h + 1) * HEAD_DIM)
        s = _dot_nt(q[:, hs], kw[:, hs]) + bias_ref[0, h]
        m = jnp.max(s, axis=-1, keepdims=True)
        p = jnp.exp(s - m)
        l = jnp.sum(p, axis=-1, keepdims=True)
        outs.append(_dot(p.astype(BF16), vw[:, hs]) * (1.0 / l))
    o_ref[0] = jnp.concatenate(outs, axis=-1).astype(BF16)


def _attention(qkv, bias_tabs, batch, seq):
    qkv3 = qkv.reshape(batch, seq, 3 * D_HEADS)
    n_tabs = bias_tabs.shape[0]
    out = pl.pallas_call(
        _attn_kernel,
        grid=(batch, seq // ATTN_QBLK),
        in_specs=[
            pl.BlockSpec((1, ATTN_QBLK, D_HEADS), lambda b, i: (b, i, 0)),
            pl.BlockSpec((1, seq, D_HEADS), lambda b, i: (b, 0, 1)),
            pl.BlockSpec((1, seq, D_HEADS), lambda b, i: (b, 0, 2)),
            pl.BlockSpec((1, N_HEADS, ATTN_QBLK, ATTN_WIN),
                         lambda b, i: (jnp.minimum(i, n_tabs - 1), 0, 0, 0)),
        ],
        out_specs=pl.BlockSpec((1, ATTN_QBLK, D_HEADS), lambda b, i: (b, i, 0)),
        out_shape=jax.ShapeDtypeStruct((batch, seq, D_HEADS), BF16),
        compiler_params=_cparams("parallel", "arbitrary"),
        name="band_attn",
    )(qkv3, qkv3, qkv3, bias_tabs)
    return out.reshape(batch * seq, D_HEADS)


def _attn_bias_tables(rel_bias):
    n_tabs = N_PREV_CHUNKS * CHUNK // ATTN_QBLK + 1
    start = (jnp.arange(n_tabs) * ATTN_QBLK)[:, None, None]
    qpos = start + jnp.arange(ATTN_QBLK)[None, :, None]
    kpos = jnp.arange(ATTN_WIN)[None, None, :]
    idx = jnp.clip(qpos - kpos, -REL_CLIP, REL_CLIP) + REL_CLIP
    qc, kc = qpos // CHUNK, kpos // CHUNK
    valid = (kc <= qc) & (kc >= qc - N_PREV_CHUNKS)
    bias = rel_bias.astype(F32)[:, idx]
    bias = jnp.where(valid[None], bias, NEG_INF)
    return jnp.transpose(bias, (1, 0, 2, 3))


def _rwkv_chunk_kernel(*refs, seq, tm, has_vres):
    if has_vres:
        (rw_ref, prev_ref, mu_ref, dbase_ref, dup_ref, abase_ref, aup_ref, gup_ref, kk_ref, ka_ref,
         rk_ref, eones_ref, tri_ref, blk_ref, vfirst_ref, vbase_ref, vdown_ref, vup_ref,
         q1_ref, y0_ref, m_ref, n0_ref, g_ref, bonus_ref,
         rw_s, aw_s, bi_s, ki_s, be_s, ke_s, v_s, wc_s) = refs
    else:
        (rw_ref, prev_ref, mu_ref, dbase_ref, dup_ref, abase_ref, aup_ref, gup_ref, kk_ref, ka_ref,
         rk_ref, eones_ref, tri_ref, blk_ref,
         q1_ref, y0_ref, m_ref, n0_ref, g_ref, bonus_ref, vfirst_out_ref,
         rw_s, aw_s, bi_s, ki_s, be_s, ke_s, v_s, wc_s) = refs

    cols = rw_ref[...]
    first = (pl.program_id(0) % (seq // tm)) == 0
    prev_row = jnp.where(first, 0.0, prev_ref[7:8, :])
    rows = lax.broadcasted_iota(jnp.int32, cols.shape, 0)
    prev = jnp.where(rows == 0, prev_row, pltpu.roll(cols, 1, axis=0))
    xs = cols + (prev - cols) * mu_ref[...]

    d = D_HEADS
    r, k, v = xs[:, 0:d], xs[:, d:2 * d], xs[:, 2 * d:3 * d]
    wd = xs[:, 3 * d:3 * d + LANES]
    ad = xs[:, 3 * d + LANES:3 * d + 2 * LANES]
    gd = xs[:, 3 * d + 2 * LANES:3 * d + 4 * LANES]
    e_ones = eones_ref[...]

    u = dbase_ref[...] + _dot3(jnp.tanh(wd), dup_ref[...])
    w_log = jnp.minimum(u, 0.0) - jnp.log(1.0 + jnp.exp(-jnp.abs(u))) - 0.5
    logw = -jnp.exp(w_log)
    a_icl = _sigmoid(abase_ref[...] + _dot3(ad, aup_ref[...]))
    g_ref[...] = _dot3(_sigmoid(gd), gup_ref[...])
    kk = k * kk_ref[...]
    kk = kk * lax.rsqrt(jnp.maximum(_seg_sum(kk * kk, e_ones), 1e-24))
    k2 = k * (1.0 + (a_icl - 1.0) * ka_ref[...])
    if has_vres:
        mix = _sigmoid(vbase_ref[...] + _dot3(_dot3(v, vdown_ref[...]), vup_ref[...]))
        v2 = v + (vfirst_ref[...] - v) * mix
    else:
        v2 = v
        vfirst_out_ref[...] = v
    a_vec = -kk
    b_vec = kk * a_icl
    bonus_ref[...] = _seg_sum(r * k2 * rk_ref[...], e_ones) * v2

    cw = _dot_exact_lhs(tri_ref[...], logw)
    cw_tot = _dot_exact_lhs(blk_ref[...], logw)
    rw_s[...] = r * jnp.exp(cw)
    aw_s[...] = a_vec * jnp.exp(cw - logw)
    inv = jnp.exp(-cw)
    bi_s[...] = b_vec * inv
    ki_s[...] = k2 * inv
    rest = jnp.exp(cw_tot - cw)
    be_s[...] = b_vec * rest
    ke_s[...] = k2 * rest
    v_s[...] = v2
    wc_s[...] = jnp.exp(cw_tot)

    ti = lax.broadcasted_iota(jnp.int32, (CHUNK, CHUNK), 0)
    si = lax.broadcasted_iota(jnp.int32, (CHUNK, CHUNK), 1)
    strict = ti > si
    incl = ti >= si
    eye = ti == si

    def chunk_body(c, carry):
        rs = pl.ds(pl.multiple_of(c * CHUNK, CHUNK), CHUNK)
        for h in range(N_HEADS):
            hs = slice(h * HEAD_DIM, (h + 1) * HEAD_DIM)
            rw = rw_s[rs, hs]
            aw = aw_s[rs, hs]
            bi = bi_s[rs, hs].astype(BF16)
            ki = ki_s[rs, hs].astype(BF16)
            be = be_s[rs, hs].astype(BF16)
            ke = ke_s[rs, hs].astype(BF16)
            vv = v_s[rs, hs].astype(BF16)
            wc = wc_s[pl.ds(c * CHUNK, 1), hs]
            awb = aw.astype(BF16)
            rwb = rw.astype(BF16)
            a_ab = jnp.where(strict, _dot_nt(awb, bi), 0.0)
            a_ak = jnp.where(strict, _dot_nt(awb, ki), 0.0)
            a_rb = jnp.where(incl, _dot_nt(rwb, bi), 0.0).astype(BF16)
            a_rk = jnp.where(incl, _dot_nt(rwb, ki), 0.0).astype(BF16)
            lp = a_ab
            tinv = jnp.where(eye, 1.0, a_ab)
            for _ in range(5):
                lpb = lp.astype(BF16)
                lp = _dot(lpb, lpb)
                tinv = tinv + _dot(tinv.astype(BF16), lp.astype(BF16))
            tb = tinv.astype(BF16)
            x0 = _dot(a_ak.astype(BF16), vv)
            p1 = _dot(tb, awb)
            u0 = _dot(tb, x0.astype(BF16))
            p1b = p1.astype(BF16)
            u0b = u0.astype(BF16)
            q1_ref[rs, hs] = rw + _dot(a_rb, p1b)
            y0_ref[rs, hs] = _dot(a_rb, u0b) + _dot(a_rk, vv)
            m_ref[c, h] = jnp.where(eye, wc, 0.0) + _dot_tn(be, p1b)
            n0_ref[c, h] = _dot_tn(be, u0b) + _dot_tn(ke, vv)
        return carry

    lax.fori_loop(0, tm // CHUNK, chunk_body, 0)


def _rwkv_chunk(rw, lp, seq, v_first):
    t = rw.shape[0]
    tm = 256
    has_vres = v_first is not None
    nct = tm // CHUNK
    row = lambda w: pl.BlockSpec((tm, w), lambda i: (i, 0))
    prev_spec = pl.BlockSpec((8, RW_COLS_PAD), lambda i: (jnp.maximum(i * (tm // 8) - 1, 0), 0))
    params = [lp["mu"], lp["decay_base"], lp["decay_up"], lp["aaa_base"], lp["aaa_up"], lp["gate_up"],
              lp["k_k"], lp["k_a"], lp["r_k"], lp["e_ones"], lp["tri"], lp["blk"]]
    inputs = [rw, rw] + params
    in_specs = [row(RW_COLS_PAD), prev_spec] + [_full(p.shape) for p in params]
    if has_vres:
        extra = [lp["vres_base"], lp["vres_down"], lp["vres_up"]]
        inputs += [v_first] + extra
        in_specs += [row(D_HEADS)] + [_full(p.shape) for p in extra]
    mat_spec = pl.BlockSpec((nct, N_HEADS, HEAD_DIM, HEAD_DIM), lambda i: (i, 0, 0, 0))
    mat_shape = jax.ShapeDtypeStruct((t // CHUNK, N_HEADS, HEAD_DIM, HEAD_DIM), F32)
    tok_shape = jax.ShapeDtypeStruct((t, D_HEADS), F32)
    out_shape = [tok_shape, tok_shape, mat_shape, mat_shape, tok_shape, tok_shape]
    out_specs = [row(D_HEADS), row(D_HEADS), mat_spec, mat_spec, row(D_HEADS), row(D_HEADS)]
    if not has_vres:
        out_shape.append(tok_shape)
        out_specs.append(row(D_HEADS))
    scratch = [pltpu.VMEM((tm, D_HEADS), F32) for _ in range(8)]
    return pl.pallas_call(
        functools.partial(_rwkv_chunk_kernel, seq=seq, tm=tm, has_vres=has_vres),
        grid=(t // tm,),
        in_specs=in_specs,
        out_specs=out_specs,
        out_shape=out_shape,
        scratch_shapes=scratch,
        compiler_params=_cparams("parallel"),
        name="rwkv_chunk_vres" if has_vres else "rwkv_chunk",
    )(*inputs)


def _rwkv_scan_kernel(q1_ref, y0_ref, m_ref, n0_ref, y_ref, st_ref, *, n_chunks):
    @pl.when(pl.program_id(1) == 0)
    def _():
        st_ref[...] = jnp.zeros_like(st_ref)

    for c in range(n_chunks):
        rs = slice(c * CHUNK, (c + 1) * CHUNK)
        outs = []
        for h in range(N_HEADS):
            hs = slice(h * HEAD_DIM, (h + 1) * HEAD_DIM)
            st = st_ref[h]
            outs.append(_dot3(q1_ref[rs, hs], st) + y0_ref[rs, hs])
            st_ref[h] = _dot3(m_ref[c, h], st) + n0_ref[c, h]
        y_ref[rs, :] = jnp.concatenate(outs, axis=-1)


def _rwkv_scan(q1, y0, m, n0, batch, seq):
    t = q1.shape[0]
    cb = 8
    steps = seq // (cb * CHUNK)
    tok_spec = pl.BlockSpec((cb * CHUNK, D_HEADS), lambda b, j: (b * steps + j, 0))
    mat_spec = pl.BlockSpec((cb, N_HEADS, HEAD_DIM, HEAD_DIM), lambda b, j: (b * steps + j, 0, 0, 0))
    return pl.pallas_call(
        functools.partial(_rwkv_scan_kernel, n_chunks=cb),
        grid=(batch, steps),
        in_specs=[tok_spec, tok_spec, mat_spec, mat_spec],
        out_specs=tok_spec,
        out_shape=jax.ShapeDtypeStruct((t, D_HEADS), F32),
        scratch_shapes=[pltpu.VMEM((N_HEADS, HEAD_DIM, HEAD_DIM), F32)],
        compiler_params=_cparams("parallel", "arbitrary"),
        name="rwkv_scan",
    )(q1, y0, m, n0)


def _mix_kernel(y_ref, g_ref, bonus_ref, ya_ref, gate_ref, x_ref, wa_ref, wb_ref, wo_ref,
                eones_ref, gng_ref, gnb_ref, lng_ref, lnb_ref, o_ref):
    e_ones = eones_ref[...]
    y = y_ref[...]
    mean = _seg_sum(y, e_ones) * (1.0 / HEAD_DIM)
    yc = y - mean
    var = _seg_sum(yc * yc, e_ones) * (1.0 / HEAD_DIM)
    yn = yc * lax.rsqrt(var + GN_EPS) * gng_ref[...] + gnb_ref[...]
    yr = (yn + bonus_ref[...]) * g_ref[...]
    za = _dot(ya_ref[...], wa_ref[...])
    zb = _dot(yr.astype(BF16), wb_ref[...])
    gate = gate_ref[...]
    mixed_in = gate[:, :D_MODEL].astype(F32) * za + gate[:, D_MODEL:].astype(F32) * zb
    mixed = _dot(mixed_in.astype(BF16), wo_ref[...])
    o_ref[...] = _layer_norm(DEEPNORM_ALPHA * x_ref[...] + mixed, lng_ref[...], lnb_ref[...])


def _mix(y, g, bonus, y_attn, gate, x, lp):
    t = x.shape[0]
    tm = min(256, t)
    row = lambda w: pl.BlockSpec((tm, w), lambda i: (i, 0))
    params = [lp["w_a"], lp["w_b"], lp["w_o"], lp["e_ones"], lp["gn_g"], lp["gn_b"], lp["ln_g0"], lp["ln_b0"]]
    return pl.pallas_call(
        _mix_kernel,
        grid=(t // tm,),
        in_specs=[row(D_HEADS), row(D_HEADS), row(D_HEADS), row(D_HEADS), row(2 * D_MODEL), row(D_MODEL)]
        + [_full(p.shape) for p in params],
        out_specs=row(D_MODEL),
        out_shape=jax.ShapeDtypeStruct((t, D_MODEL), F32),
        compiler_params=_cparams("parallel"),
        name="branch_mix",
    )(y, g, bonus, y_attn, gate, x, *params)


def _moe_kernel(x_ref, wgrp_ref, bgrp_ref, wexp_ref, bexp_ref, eg_ref, eu_ref, ed_ref,
                lng_ref, lnb_ref, o_ref, acc_s, comb_s, xb_s):
    e = pl.program_id(1)
    lane = lax.broadcasted_iota(jnp.int32, comb_s.shape, 1)

    @pl.when(e == 0)
    def _():
        x = x_ref[...]
        xb_s[...] = x.astype(BF16)
        acc_s[...] = jnp.zeros_like(acc_s)
        hp = lax.Precision.HIGHEST
        glog = jnp.dot(x, wgrp_ref[...], precision=hp, preferred_element_type=F32) + bgrp_ref[...]
        elog = jnp.dot(x, wexp_ref[...], precision=hp, preferred_element_type=F32) + bexp_ref[...]
        lanef = lane.astype(F32)
        big = float(LANES)
        glog = jnp.where(lane < N_GROUPS, glog, -jnp.inf)
        gmax = jnp.max(glog, axis=-1, keepdims=True)
        g_gate = 1.0 / jnp.sum(jnp.exp(glog - gmax), axis=-1, keepdims=True)
        grp = jnp.min(jnp.where(glog == gmax, lanef, big), axis=-1, keepdims=True)
        in_grp = (lanef >= grp * EXPERTS_PER_GROUP) & (lanef < (grp + 1.0) * EXPERTS_PER_GROUP)
        sel = jnp.where(in_grp, elog, -jnp.inf)
        v1 = jnp.max(sel, axis=-1, keepdims=True)
        i1 = jnp.min(jnp.where(sel == v1, lanef, big), axis=-1, keepdims=True)
        sel2 = jnp.where(lanef == i1, -jnp.inf, sel)
        v2 = jnp.max(sel2, axis=-1, keepdims=True)
        i2 = jnp.min(jnp.where(sel2 == v2, lanef, big), axis=-1, keepdims=True)
        e2 = jnp.exp(v2 - v1)
        w1 = g_gate / (1.0 + e2)
        w2 = g_gate * e2 / (1.0 + e2)
        comb_s[...] = jnp.where(lanef == i1, w1, 0.0) + jnp.where(lanef == i2, w2, 0.0)

    xb = xb_s[...]
    c_e = jnp.sum(jnp.where(lane == e, comb_s[...], 0.0), axis=-1, keepdims=True)
    hg = _dot(xb, eg_ref[0])
    hu = _dot(xb, eu_ref[0])
    hh = hg * _sigmoid(hg) * hu
    acc_s[...] += _dot((hh * c_e).astype(BF16), ed_ref[0])

    @pl.when(e == N_EXPERTS - 1)
    def _():
        o_ref[...] = _layer_norm(DEEPNORM_ALPHA * x_ref[...] + acc_s[...], lng_ref[...], lnb_ref[...])


def _moe(x, lp):
    t = x.shape[0]
    tm = min(512, t)
    row = pl.BlockSpec((tm, D_MODEL), lambda i, e: (i, 0))
    small = [lp["w_grp"], lp["b_grp"], lp["w_exp"], lp["b_exp"]]
    tail = [lp["ln_g1"], lp["ln_b1"]]
    full2 = lambda p: pl.BlockSpec(p.shape, lambda i, e: (0,) * p.ndim)
    return pl.pallas_call(
        _moe_kernel,
        grid=(t // tm, N_EXPERTS),
        in_specs=[row] + [full2(p) for p in small]
        + [pl.BlockSpec((1, D_MODEL, D_EXPERT), lambda i, e: (e, 0, 0)),
           pl.BlockSpec((1, D_MODEL, D_EXPERT), lambda i, e: (e, 0, 0)),
           pl.BlockSpec((1, D_EXPERT, D_MODEL), lambda i, e: (e, 0, 0))]
        + [full2(p) for p in tail],
        out_specs=row,
        out_shape=jax.ShapeDtypeStruct((t, D_MODEL), F32),
        scratch_shapes=[pltpu.VMEM((tm, D_MODEL), F32), pltpu.VMEM((tm, LANES), F32),
                        pltpu.VMEM((tm, D_MODEL), BF16)],
        compiler_params=_cparams("parallel", "arbitrary"),
        name="hier_moe",
    )(x, *small, lp["e_gate"], lp["e_up"], lp["e_down"], *tail)


def _ple_kernel(x_ref, p_ref, wp_ref, wg_ref, lng_ref, lnb_ref, o_ref):
    x = x_ref[...]
    emb = _dot(p_ref[...].astype(BF16), wp_ref[...])
    gate = _sigmoid(_dot(x.astype(BF16), wg_ref[...]))
    o_ref[...] = _layer_norm(DEEPNORM_ALPHA * x + emb * gate, lng_ref[...], lnb_ref[...])


def _ple(x, p, lp):
    t = x.shape[0]
    tm = min(512, t)
    d_ple = p.shape[1]
    row = lambda w: pl.BlockSpec((tm, w), lambda i: (i, 0))
    params = [lp["w_ple"], lp["w_pg"], lp["ln_g2"], lp["ln_b2"]]
    return pl.pallas_call(
        _ple_kernel,
        grid=(t // tm,),
        in_specs=[row(D_MODEL), row(d_ple)] + [_full(q.shape) for q in params],
        out_specs=row(D_MODEL),
        out_shape=jax.ShapeDtypeStruct((t, D_MODEL), F32),
        compiler_params=_cparams("parallel"),
        name="ple",
    )(x, p, *params)


def _pad_to(a, axis, size):
    pad = [(0, 0)] * a.ndim
    pad[axis] = (0, size - a.shape[axis])
    return jnp.pad(a, pad)


def _row(v):
    return v.reshape(1, -1).astype(F32)


def _layer_params(i, w_in, tok_mix, decay_base, decay_up, aaa_base, aaa_up, gate_up, k_k, k_a, r_k,
                  vres_base, vres_down, vres_up, gn_g, gn_b, w_branch_attn, w_branch_rwkv, w_out,
                  router_grp, router_grp_bias, router_exp, router_exp_bias, exp_gate, exp_up, exp_down,
                  ple_proj, ple_gate, ln_g, ln_b, consts):
    d = D_HEADS
    w = w_in[i]
    rw0 = 3 * d
    gate0 = rw0 + 3 * d + DECAY_LORA + AAA_LORA + GATE_LORA
    wqkv = jnp.concatenate([w[:, 0:d] * (HEAD_DIM ** -0.5), w[:, d:rw0]], axis=1).astype(BF16)
    wr = w[:, rw0:gate0]
    lora0 = 3 * d
    pieces = [wr[:, :lora0],
              _pad_to(wr[:, lora0:lora0 + DECAY_LORA], 1, LANES),
              _pad_to(wr[:, lora0 + DECAY_LORA:lora0 + DECAY_LORA + AAA_LORA], 1, LANES),
              _pad_to(wr[:, lora0 + DECAY_LORA + AAA_LORA:], 1, 2 * LANES)]
    wrw = jnp.concatenate(pieces, axis=1).astype(BF16)
    mu = tok_mix[i]
    mu_pieces = [mu[:lora0],
                 _pad_to(mu[lora0:lora0 + DECAY_LORA], 0, LANES),
                 _pad_to(mu[lora0 + DECAY_LORA:lora0 + DECAY_LORA + AAA_LORA], 0, LANES),
                 _pad_to(mu[lora0 + DECAY_LORA + AAA_LORA:], 0, 2 * LANES)]
    lp = dict(consts)
    lp.update(
        wqkv=wqkv, wrw=wrw, wgate=w[:, gate0:].astype(BF16),
        mu=_row(jnp.concatenate(mu_pieces)),
        decay_base=_row(decay_base[i]), decay_up=_pad_to(decay_up[i], 0, LANES).astype(F32),
        aaa_base=_row(aaa_base[i]), aaa_up=_pad_to(aaa_up[i], 0, LANES).astype(F32),
        gate_up=_pad_to(gate_up[i], 0, 2 * LANES).astype(F32),
        k_k=_row(k_k[i]), k_a=_row(k_a[i]), r_k=_row(r_k[i]),
        gn_g=_row(gn_g[i]), gn_b=_row(gn_b[i]),
        w_a=w_branch_attn[i].astype(BF16), w_b=w_branch_rwkv[i].astype(BF16), w_o=w_out[i].astype(BF16),
        w_grp=_pad_to(router_grp[i], 1, LANES).astype(F32), b_grp=_row(_pad_to(router_grp_bias[i], 0, LANES)),
        w_exp=_pad_to(router_exp[i], 1, LANES).astype(F32), b_exp=_row(_pad_to(router_exp_bias[i], 0, LANES)),
        e_gate=exp_gate[i].astype(BF16), e_up=exp_up[i].astype(BF16), e_down=exp_down[i].astype(BF16),
        w_ple=ple_proj[i].astype(BF16), w_pg=ple_gate[i].astype(BF16),
        ln_g0=_row(ln_g[i, 0]), ln_b0=_row(ln_b[i, 0]),
        ln_g1=_row(ln_g[i, 1]), ln_b1=_row(ln_b[i, 1]),
        ln_g2=_row(ln_g[i, 2]), ln_b2=_row(ln_b[i, 2]),
    )
    if i > 0:
        lp.update(vres_base=_row(vres_base[i - 1]),
                  vres_down=_pad_to(vres_down[i - 1], 1, LANES).astype(F32),
                  vres_up=_pad_to(vres_up[i - 1], 0, LANES).astype(F32))
    return lp


def _const_mats(tm):
    head = jnp.arange(D_HEADS) // HEAD_DIM
    e_ones = (head[:, None] == head[None, :]).astype(BF16)
    tok = jnp.arange(tm)
    same = (tok[:, None] // CHUNK) == (tok[None, :] // CHUNK)
    tri = (same & (tok[:, None] >= tok[None, :])).astype(BF16)
    blk = same.astype(BF16)
    return dict(e_ones=e_ones, tri=tri, blk=blk)


def kernel(x, p, ln_in_g, ln_in_b, rel_bias, w_in, tok_mix, decay_base, decay_up, aaa_base, aaa_up,
           gate_up, k_k, k_a, r_k, vres_base, vres_down, vres_up, gn_g, gn_b, w_branch_attn,
           w_branch_rwkv, w_out, router_grp, router_grp_bias, router_exp, router_exp_bias, exp_gate,
           exp_up, exp_down, ple_proj, ple_gate, ln_g, ln_b):
    batch, seq, _ = x.shape
    t = batch * seq
    consts = _const_mats(256)
    bias_tabs = _attn_bias_tables(rel_bias)
    xt = x.reshape(t, D_MODEL)
    pt = p.reshape(p.shape[0], t, p.shape[-1])
    v_first = None
    for i in range(DEPTH):
        lp = _layer_params(i, w_in, tok_mix, decay_base, decay_up, aaa_base, aaa_up, gate_up, k_k, k_a,
                           r_k, vres_base, vres_down, vres_up, gn_g, gn_b, w_branch_attn, w_branch_rwkv,
                           w_out, router_grp, router_grp_bias, router_exp, router_exp_bias, exp_gate,
                           exp_up, exp_down, ple_proj, ple_gate, ln_g, ln_b, consts)
        if i == 0:
            xt, qkv, rw, gate = _proj(xt, _row(ln_in_g), _row(ln_in_b), lp["wqkv"], lp["wrw"],
                                      lp["wgate"], apply_ln=True)
        else:
            qkv, rw, gate = _proj(xt, _row(ln_in_g), _row(ln_in_b), lp["wqkv"], lp["wrw"],
                                  lp["wgate"], apply_ln=False)
        y_attn = _attention(qkv, bias_tabs, batch, seq)
        if i == 0:
            q1, y0, m, n0, g, bonus, v_first = _rwkv_chunk(rw, lp, seq, None)
        else:
            q1, y0, m, n0, g, bonus = _rwkv_chunk(rw, lp, seq, v_first)
        y = _rwkv_scan(q1, y0, m, n0, batch, seq)
        xt = _mix(y, g, bonus, y_attn, gate, xt, lp)
        xt = _moe(xt, lp)
        xt = _ple(xt, pt[i], lp)
    return xt.reshape(batch, seq, D_MODEL)
```

```python
import functools
import math

import jax
import jax.numpy as jnp
from jax import lax
from jax.experimental import pallas as pl
from jax.experimental.pallas import tpu as pltpu

F32 = jnp.float32
BF16 = jnp.bfloat16

D_MODEL = 1024
CHUNK = 64
N_PREV_CHUNKS = 8
HEAD_DIM = 64
N_HEADS = 8
D_HEADS = N_HEADS * HEAD_DIM
REL_CLIP = 128
DECAY_LORA = 64
AAA_LORA = 64
GATE_LORA = 160
MV_LORA = 32
N_GROUPS = 4
EXPERTS_PER_GROUP = 8
N_EXPERTS = N_GROUPS * EXPERTS_PER_GROUP
D_EXPERT = 256
DEPTH = 2
DEEPNORM_ALPHA = (2 * DEPTH) ** 0.25
LN_EPS = 1e-5
GN_EPS = 64e-5
NEG_INF = -1e30

LANES = 128
PAIR = 2 * HEAD_DIM
ATTN_QBLK = 2 * CHUNK
ATTN_WIN = (N_PREV_CHUNKS + 2) * CHUNK
RW_COLS_PAD = 2048
VMEM_LIMIT = 56 * 1024 * 1024


def _cparams(*sem):
    return pltpu.CompilerParams(dimension_semantics=sem, vmem_limit_bytes=VMEM_LIMIT)


def _dot(a, b):
    return jnp.dot(a, b, preferred_element_type=F32)


def _dot_nt(a, b):
    return lax.dot_general(a, b, (((1,), (1,)), ((), ())), preferred_element_type=F32)


def _dot_tn(a, b):
    return lax.dot_general(a, b, (((0,), (0,)), ((), ())), preferred_element_type=F32)


def _split2(x):
    hi = x.astype(BF16)
    lo = (x - hi.astype(F32)).astype(BF16)
    return hi, lo


def _dot3(a, b):
    ah, al = _split2(a)
    bh, bl = _split2(b)
    return _dot(ah, bh) + (_dot(al, bh) + _dot(ah, bl))


def _dot_exact_lhs(a_bf16, x):
    x1 = x.astype(BF16)
    r1 = x - x1.astype(F32)
    x2 = r1.astype(BF16)
    x3 = (r1 - x2.astype(F32)).astype(BF16)
    return _dot(a_bf16, x1) + (_dot(a_bf16, x2) + _dot(a_bf16, x3))


def _seg_sum(x, e_ones):
    hi, lo = _split2(x)
    return _dot(hi, e_ones) + _dot(lo, e_ones)


def _layer_norm(x, g, b):
    mu = jnp.mean(x, axis=-1, keepdims=True)
    xc = x - mu
    var = jnp.mean(xc * xc, axis=-1, keepdims=True)
    return xc * lax.rsqrt(var + LN_EPS) * g + b


def _sigmoid(x):
    return 1.0 / (1.0 + jnp.exp(-x))


def _full(shape):
    nd = len(shape)
    return pl.BlockSpec(shape, lambda *_: (0,) * nd)


def _proj_kernel(x_ref, g_ref, b_ref, wqkv_ref, wrw_ref, wgate_ref, *out_refs, apply_ln):
    x = x_ref[...]
    if apply_ln:
        xn_ref, qkv_ref, rw_ref, gate_ref = out_refs
        x = _layer_norm(x, g_ref[...], b_ref[...])
        xn_ref[...] = x
    else:
        qkv_ref, rw_ref, gate_ref = out_refs
    xb = x.astype(BF16)
    qkv_ref[...] = _dot(xb, wqkv_ref[...]).astype(BF16)
    rw_ref[...] = _dot(xb, wrw_ref[...])
    gate_ref[...] = _sigmoid(_dot(xb, wgate_ref[...])).astype(BF16)


def _proj(x, g, b, wqkv, wrw, wgate, apply_ln):
    t = x.shape[0]
    tm = min(256, t)
    row = lambda w: pl.BlockSpec((tm, w), lambda i: (i, 0))
    out_shape = [jax.ShapeDtypeStruct((t, 3 * D_HEADS), BF16),
                 jax.ShapeDtypeStruct((t, RW_COLS_PAD), F32),
                 jax.ShapeDtypeStruct((t, 2 * D_MODEL), BF16)]
    out_specs = [row(3 * D_HEADS), row(RW_COLS_PAD), row(2 * D_MODEL)]
    if apply_ln:
        out_shape = [jax.ShapeDtypeStruct((t, D_MODEL), F32)] + out_shape
        out_specs = [row(D_MODEL)] + out_specs
    return pl.pallas_call(
        functools.partial(_proj_kernel, apply_ln=apply_ln),
        grid=(t // tm,),
        in_specs=[row(D_MODEL), _full(g.shape), _full(b.shape),
                  _full(wqkv.shape), _full(wrw.shape), _full(wgate.shape)],
        out_specs=out_specs,
        out_shape=out_shape,
        compiler_params=_cparams("parallel"),
        name="proj_ln" if apply_ln else "proj",
    )(x, g, b, wqkv, wrw, wgate)


def _attn_kernel(q_ref, k_ref, v_ref, bias_ref, o_ref):
    qb = pl.program_id(1)
    ws = pl.multiple_of(jnp.maximum(qb * ATTN_QBLK - N_PREV_CHUNKS * CHUNK, 0), ATTN_QBLK)
    q = q_ref[0]
    kw = k_ref[0, pl.ds(ws, ATTN_WIN), :]
    vw = v_ref[0, pl.ds(ws, ATTN_WIN), :]
    outs = []
    for h in range(N_HEADS):
        hs = slice(h * HEAD_DIM, (h + 1) * HEAD_DIM)
        s = _dot_nt(q[:, hs], kw[:, hs]) + bias_ref[0, h]
        m = jnp.max(s, axis=-1, keepdims=True)
        p = jnp.exp(s - m)
        l = jnp.sum(p, axis=-1, keepdims=True)
        outs.append(_dot(p.astype(BF16), vw[:, hs]) * (1.0 / l))
    o_ref[0] = jnp.concatenate(outs, axis=-1).astype(BF16)


def _attention(qkv, bias_tabs, batch, seq):
    qkv3 = qkv.reshape(batch, seq, 3 * D_HEADS)
    n_tabs = bias_tabs.shape[0]
    out = pl.pallas_call(
        _attn_kernel,
        grid=(batch, seq // ATTN_QBLK),
        in_specs=[
            pl.BlockSpec((1, ATTN_QBLK, D_HEADS), lambda b, i: (b, i, 0)),
            pl.BlockSpec((1, seq, D_HEADS), lambda b, i: (b, 0, 1)),
            pl.BlockSpec((1, seq, D_HEADS), lambda b, i: (b, 0, 2)),
            pl.BlockSpec((1, N_HEADS, ATTN_QBLK, ATTN_WIN),
                         lambda b, i: (jnp.minimum(i, n_tabs - 1), 0, 0, 0)),
        ],
        out_specs=pl.BlockSpec((1, ATTN_QBLK, D_HEADS), lambda b, i: (b, i, 0)),
        out_shape=jax.ShapeDtypeStruct((batch, seq, D_HEADS), BF16),
        compiler_params=_cparams("parallel", "arbitrary"),
        name="band_attn",
    )(qkv3, qkv3, qkv3, bias_tabs)
    return out.reshape(batch * seq, D_HEADS)


def _attn_bias_tables(rel_bias):
    n_tabs = N_PREV_CHUNKS * CHUNK // ATTN_QBLK + 1
    start = (jnp.arange(n_tabs) * ATTN_QBLK)[:, None, None]
    qpos = start + jnp.arange(ATTN_QBLK)[None, :, None]
    kpos = jnp.arange(ATTN_WIN)[None, None, :]
    qc, kc = qpos // CHUNK, kpos // CHUNK
    valid = (kc <= qc) & (kc >= qc - N_PREV_CHUNKS)
    n = jnp.arange(ATTN_QBLK - 1 + ATTN_WIN)
    dist = start[:, :, 0] + (ATTN_QBLK - 1) - n[None, :]
    vals = rel_bias.astype(F32)[:, jnp.clip(dist, -REL_CLIP, REL_CLIP) + REL_CLIP]
    rows = [vals[:, :, ATTN_QBLK - 1 - i:ATTN_QBLK - 1 - i + ATTN_WIN] for i in range(ATTN_QBLK)]
    bias = jnp.stack(rows, axis=2)
    bias = jnp.where(valid[None], bias, NEG_INF)
    return jnp.transpose(bias, (1, 0, 2, 3))


def _rwkv_chunk_kernel(*refs, seq, tm, has_vres):
    if has_vres:
        (rw_ref, prev_ref, mu_ref, dbase_ref, dup_ref, abase_ref, aup_ref, gup_ref, kk_ref, ka_ref,
         rk_ref, eones_ref, tri_ref, blk_ref, vfirst_ref, vbase_ref, vdown_ref, vup_ref,
         q1_ref, y0_ref, m_ref, n0_ref, g_ref, bonus_ref,
         rw_s, aw_s, bi_s, ki_s, be_s, ke_s, v_s, wc_s) = refs
    else:
        (rw_ref, prev_ref, mu_ref, dbase_ref, dup_ref, abase_ref, aup_ref, gup_ref, kk_ref, ka_ref,
         rk_ref, eones_ref, tri_ref, blk_ref,
         q1_ref, y0_ref, m_ref, n0_ref, g_ref, bonus_ref, vfirst_out_ref,
         rw_s, aw_s, bi_s, ki_s, be_s, ke_s, v_s, wc_s) = refs

    cols = rw_ref[...]
    first = (pl.program_id(0) % (seq // tm)) == 0
    prev_row = jnp.where(first, 0.0, prev_ref[7:8, :])
    rows = lax.broadcasted_iota(jnp.int32, cols.shape, 0)
    prev = jnp.where(rows == 0, prev_row, pltpu.roll(cols, 1, axis=0))
    xs = cols + (prev - cols) * mu_ref[...]

    d = D_HEADS
    r, k, v = xs[:, 0:d], xs[:, d:2 * d], xs[:, 2 * d:3 * d]
    wd = xs[:, 3 * d:3 * d + LANES]
    ad = xs[:, 3 * d + LANES:3 * d + 2 * LANES]
    gd = xs[:, 3 * d + 2 * LANES:3 * d + 4 * LANES]
    e_ones = eones_ref[...]

    u = dbase_ref[...] + _dot3(jnp.tanh(wd), dup_ref[...])
    w_log = jnp.minimum(u, 0.0) - jnp.log(1.0 + jnp.exp(-jnp.abs(u))) - 0.5
    logw = -jnp.exp(w_log)
    a_icl = _sigmoid(abase_ref[...] + _dot3(ad, aup_ref[...]))
    g_ref[...] = _dot3(_sigmoid(gd), gup_ref[...])
    kk = k * kk_ref[...]
    kk = kk * lax.rsqrt(jnp.maximum(_seg_sum(kk * kk, e_ones), 1e-24))
    k2 = k * (1.0 + (a_icl - 1.0) * ka_ref[...])
    if has_vres:
        mix = _sigmoid(vbase_ref[...] + _dot3(_dot3(v, vdown_ref[...]), vup_ref[...]))
        v2 = v + (vfirst_ref[...] - v) * mix
    else:
        v2 = v
        vfirst_out_ref[...] = v
    a_vec = -kk
    b_vec = kk * a_icl
    bonus_ref[...] = _seg_sum(r * k2 * rk_ref[...], e_ones) * v2

    cw = _dot_exact_lhs(tri_ref[...], logw)
    cw_tot = _dot_exact_lhs(blk_ref[...], logw)
    rw_s[...] = r * jnp.exp(cw)
    aw_s[...] = a_vec * jnp.exp(cw - logw)
    inv = jnp.exp(-cw)
    bi_s[...] = b_vec * inv
    ki_s[...] = k2 * inv
    rest = jnp.exp(cw_tot - cw)
    be_s[...] = b_vec * rest
    ke_s[...] = k2 * rest
    v_s[...] = v2
    wc_s[...] = jnp.exp(cw_tot)

    shape = (CHUNK, PAIR)
    ti = lax.broadcasted_iota(jnp.int32, shape, 0)
    li = lax.broadcasted_iota(jnp.int32, shape, 1)
    si = li & (HEAD_DIM - 1)
    left = li < HEAD_DIM
    strict = ti > si
    incl = ti >= si
    eye = ti == si

    def bd(x):
        z = jnp.zeros_like(x)
        return jnp.concatenate([jnp.where(left, x, z), jnp.where(left, z, x)], axis=0)

    def fold(x):
        return jnp.where(left, x[:CHUNK], x[CHUNK:])

    def b16(xs):
        return [x.astype(BF16) for x in xs]

    def summaries(insts):
        sl = [(slice(c * CHUNK, (c + 1) * CHUNK), slice(p * PAIR, (p + 1) * PAIR)) for c, p in insts]
        rw = [rw_s[s] for s in sl]
        awb = b16([aw_s[s] for s in sl])
        vb = b16([v_s[s] for s in sl])
        lhs = [jnp.concatenate([a, r.astype(BF16)], axis=0) for a, r in zip(awb, rw)]
        gb = [_dot_nt(l, bd(x)) for l, x in zip(lhs, b16([bi_s[s] for s in sl]))]
        gk = [_dot_nt(l, bd(x)) for l, x in zip(lhs, b16([ki_s[s] for s in sl]))]
        a_ab = [jnp.where(strict, g[:CHUNK], 0.0) for g in gb]
        a_rb = b16([jnp.where(incl, g[CHUNK:], 0.0) for g in gb])
        a_ak = b16([jnp.where(strict, g[:CHUNK], 0.0) for g in gk])
        a_rk = b16([jnp.where(incl, g[CHUNK:], 0.0) for g in gk])
        bdv = [bd(x) for x in vb]
        x0 = [_dot(a, v) for a, v in zip(a_ak, bdv)]
        lp = a_ab
        tinv = [jnp.where(eye, 1.0, a) for a in a_ab]
        for _ in range(5):
            lpb = b16(lp)
            lp = [_dot(l, bd(l)) for l in lpb]
            tinv = [t + _dot(t.astype(BF16), bd(l.astype(BF16))) for t, l in zip(tinv, lp)]
        tb = b16(tinv)
        p1b = b16([_dot(t, bd(a)) for t, a in zip(tb, awb)])
        u0b = b16([_dot(t, bd(x.astype(BF16))) for t, x in zip(tb, x0)])
        q1 = [r + _dot(a, bd(p)) for r, a, p in zip(rw, a_rb, p1b)]
        y0 = [_dot(a, bd(u)) + _dot(k, v) for a, u, k, v in zip(a_rb, u0b, a_rk, bdv)]
        beb = b16([be_s[s] for s in sl])
        keb = b16([ke_s[s] for s in sl])
        mm = [fold(_dot_tn(b, p)) for b, p in zip(beb, p1b)]
        nn = [fold(_dot_tn(b, u) + _dot_tn(k, v)) for b, u, k, v in zip(beb, u0b, keb, vb)]
        for i, (s, (c, p)) in enumerate(zip(sl, insts)):
            wc = wc_s[c * CHUNK:c * CHUNK + 1, s[1]]
            q1_ref[s] = q1[i]
            y0_ref[s] = y0[i]
            m_ref[s] = jnp.where(eye, wc, 0.0) + mm[i]
            n0_ref[s] = nn[i]

    n_pairs = D_HEADS // PAIR
    for c in range(tm // CHUNK):
        summaries([(c, p) for p in range(n_pairs)])


def _rwkv_chunk(rw, lp, seq, v_first):
    t = rw.shape[0]
    tm = 256
    has_vres = v_first is not None
    nct = tm // CHUNK
    row = lambda w: pl.BlockSpec((tm, w), lambda i: (i, 0))
    prev_spec = pl.BlockSpec((8, RW_COLS_PAD), lambda i: (jnp.maximum(i * (tm // 8) - 1, 0), 0))
    params = [lp["mu"], lp["decay_base"], lp["decay_up"], lp["aaa_base"], lp["aaa_up"], lp["gate_up"],
              lp["k_k"], lp["k_a"], lp["r_k"], lp["e_ones"], lp["tri"], lp["blk"]]
    inputs = [rw, rw] + params
    in_specs = [row(RW_COLS_PAD), prev_spec] + [_full(p.shape) for p in params]
    if has_vres:
        extra = [lp["vres_base"], lp["vres_down"], lp["vres_up"]]
        inputs += [v_first] + extra
        in_specs += [row(D_HEADS)] + [_full(p.shape) for p in extra]
    tok_shape = jax.ShapeDtypeStruct((t, D_HEADS), F32)
    out_shape = [tok_shape] * 6
    out_specs = [row(D_HEADS)] * 6
    if not has_vres:
        out_shape.append(tok_shape)
        out_specs.append(row(D_HEADS))
    scratch = [pltpu.VMEM((tm, D_HEADS), F32) for _ in range(8)]
    return pl.pallas_call(
        functools.partial(_rwkv_chunk_kernel, seq=seq, tm=tm, has_vres=has_vres),
        grid=(t // tm,),
        in_specs=in_specs,
        out_specs=out_specs,
        out_shape=out_shape,
        scratch_shapes=scratch,
        compiler_params=_cparams("parallel"),
        name="rwkv_chunk_vres" if has_vres else "rwkv_chunk",
    )(*inputs)


def _rwkv_scan_kernel(q1_ref, y0_ref, m_ref, n0_ref, y_ref, st_ref, *, n_chunks):
    @pl.when(pl.program_id(1) == 0)
    def _():
        st_ref[...] = jnp.zeros_like(st_ref)

    left = lax.broadcasted_iota(jnp.int32, (CHUNK, PAIR), 1) < HEAD_DIM

    def bd(x):
        z = jnp.zeros_like(x)
        return jnp.concatenate([jnp.where(left, x, z), jnp.where(left, z, x)], axis=0)

    n_pairs = D_HEADS // PAIR
    st = [st_ref[:, p * PAIR:(p + 1) * PAIR] for p in range(n_pairs)]
    for c in range(n_chunks):
        rs = slice(c * CHUNK, (c + 1) * CHUNK)
        for p in range(n_pairs):
            ls = slice(p * PAIR, (p + 1) * PAIR)
            lh, ll = _split2(jnp.concatenate([q1_ref[rs, ls], m_ref[rs, ls]], axis=0))
            sh, sl = _split2(st[p])
            bh = bd(sh)
            res = _dot(lh, bh) + (_dot(ll, bh) + _dot(lh, bd(sl)))
            y_ref[rs, ls] = res[:CHUNK] + y0_ref[rs, ls]
            st[p] = res[CHUNK:] + n0_ref[rs, ls]
    for p in range(n_pairs):
        st_ref[:, p * PAIR:(p + 1) * PAIR] = st[p]


def _rwkv_scan(q1, y0, m, n0, batch, seq):
    t = q1.shape[0]
    cb = 8
    steps = seq // (cb * CHUNK)
    tok_spec = pl.BlockSpec((cb * CHUNK, D_HEADS), lambda b, j: (b * steps + j, 0))
    return pl.pallas_call(
        functools.partial(_rwkv_scan_kernel, n_chunks=cb),
        grid=(batch, steps),
        in_specs=[tok_spec] * 4,
        out_specs=tok_spec,
        out_shape=jax.ShapeDtypeStruct((t, D_HEADS), F32),
        scratch_shapes=[pltpu.VMEM((HEAD_DIM, D_HEADS), F32)],
        compiler_params=_cparams("parallel", "arbitrary"),
        name="rwkv_scan",
    )(q1, y0, m, n0)


def _mix_kernel(y_ref, g_ref, bonus_ref, ya_ref, gate_ref, x_ref, wa_ref, wb_ref, wo_ref,
                eones_ref, gng_ref, gnb_ref, lng_ref, lnb_ref, o_ref):
    e_ones = eones_ref[...]
    y = y_ref[...]
    mean = _seg_sum(y, e_ones) * (1.0 / HEAD_DIM)
    yc = y - mean
    var = _seg_sum(yc * yc, e_ones) * (1.0 / HEAD_DIM)
    yn = yc * lax.rsqrt(var + GN_EPS) * gng_ref[...] + gnb_ref[...]
    yr = (yn + bonus_ref[...]) * g_ref[...]
    za = _dot(ya_ref[...], wa_ref[...])
    zb = _dot(yr.astype(BF16), wb_ref[...])
    gate = gate_ref[...]
    mixed_in = gate[:, :D_MODEL].astype(F32) * za + gate[:, D_MODEL:].astype(F32) * zb
    mixed = _dot(mixed_in.astype(BF16), wo_ref[...])
    o_ref[...] = _layer_norm(DEEPNORM_ALPHA * x_ref[...] + mixed, lng_ref[...], lnb_ref[...])


def _mix(y, g, bonus, y_attn, gate, x, lp):
    t = x.shape[0]
    tm = min(256, t)
    row = lambda w: pl.BlockSpec((tm, w), lambda i: (i, 0))
    params = [lp["w_a"], lp["w_b"], lp["w_o"], lp["e_ones"], lp["gn_g"], lp["gn_b"], lp["ln_g0"], lp["ln_b0"]]
    return pl.pallas_call(
        _mix_kernel,
        grid=(t // tm,),
        in_specs=[row(D_HEADS), row(D_HEADS), row(D_HEADS), row(D_HEADS), row(2 * D_MODEL), row(D_MODEL)]
        + [_full(p.shape) for p in params],
        out_specs=row(D_MODEL),
        out_shape=jax.ShapeDtypeStruct((t, D_MODEL), F32),
        compiler_params=_cparams("parallel"),
        name="branch_mix",
    )(y, g, bonus, y_attn, gate, x, *params)


def _moe_kernel(x_ref, wgrp_ref, bgrp_ref, wexp_ref, bexp_ref, eg_ref, eu_ref, ed_ref,
                lng_ref, lnb_ref, o_ref, acc_s, comb_s, xb_s):
    e = pl.program_id(1)
    lane = lax.broadcasted_iota(jnp.int32, comb_s.shape, 1)

    @pl.when(e == 0)
    def _():
        x = x_ref[...]
        xb_s[...] = x.astype(BF16)
        acc_s[...] = jnp.zeros_like(acc_s)
        hp = lax.Precision.HIGHEST
        glog = jnp.dot(x, wgrp_ref[...], precision=hp, preferred_element_type=F32) + bgrp_ref[...]
        elog = jnp.dot(x, wexp_ref[...], precision=hp, preferred_element_type=F32) + bexp_ref[...]
        lanef = lane.astype(F32)
        big = float(LANES)
        glog = jnp.where(lane < N_GROUPS, glog, -jnp.inf)
        gmax = jnp.max(glog, axis=-1, keepdims=True)
        g_gate = 1.0 / jnp.sum(jnp.exp(glog - gmax), axis=-1, keepdims=True)
        grp = jnp.min(jnp.where(glog == gmax, lanef, big), axis=-1, keepdims=True)
        in_grp = (lanef >= grp * EXPERTS_PER_GROUP) & (lanef < (grp + 1.0) * EXPERTS_PER_GROUP)
        sel = jnp.where(in_grp, elog, -jnp.inf)
        v1 = jnp.max(sel, axis=-1, keepdims=True)
        i1 = jnp.min(jnp.where(sel == v1, lanef, big), axis=-1, keepdims=True)
        sel2 = jnp.where(lanef == i1, -jnp.inf, sel)
        v2 = jnp.max(sel2, axis=-1, keepdims=True)
        i2 = jnp.min(jnp.where(sel2 == v2, lanef, big), axis=-1, keepdims=True)
        e2 = jnp.exp(v2 - v1)
        w1 = g_gate / (1.0 + e2)
        w2 = g_gate * e2 / (1.0 + e2)
        comb_s[...] = jnp.where(lanef == i1, w1, 0.0) + jnp.where(lanef == i2, w2, 0.0)

    xb = xb_s[...]
    c_e = jnp.sum(jnp.where(lane == e, comb_s[...], 0.0), axis=-1, keepdims=True)
    hg = _dot(xb, eg_ref[0])
    hu = _dot(xb, eu_ref[0])
    hh = hg * _sigmoid(hg) * hu
    acc_s[...] += _dot((hh * c_e).astype(BF16), ed_ref[0])

    @pl.when(e == N_EXPERTS - 1)
    def _():
        o_ref[...] = _layer_norm(DEEPNORM_ALPHA * x_ref[...] + acc_s[...], lng_ref[...], lnb_ref[...])


def _moe(x, lp):
    t = x.shape[0]
    tm = min(512, t)
    row = pl.BlockSpec((tm, D_MODEL), lambda i, e: (i, 0))
    small = [lp["w_grp"], lp["b_grp"], lp["w_exp"], lp["b_exp"]]
    tail = [lp["ln_g1"], lp["ln_b1"]]
    full2 = lambda p: pl.BlockSpec(p.shape, lambda i, e: (0,) * p.ndim)
    return pl.pallas_call(
        _moe_kernel,
        grid=(t // tm, N_EXPERTS),
        in_specs=[row] + [full2(p) for p in small]
        + [pl.BlockSpec((1, D_MODEL, D_EXPERT), lambda i, e: (e, 0, 0)),
           pl.BlockSpec((1, D_MODEL, D_EXPERT), lambda i, e: (e, 0, 0)),
           pl.BlockSpec((1, D_EXPERT, D_MODEL), lambda i, e: (e, 0, 0))]
        + [full2(p) for p in tail],
        out_specs=row,
        out_shape=jax.ShapeDtypeStruct((t, D_MODEL), F32),
        scratch_shapes=[pltpu.VMEM((tm, D_MODEL), F32), pltpu.VMEM((tm, LANES), F32),
                        pltpu.VMEM((tm, D_MODEL), BF16)],
        compiler_params=_cparams("parallel", "arbitrary"),
        name="hier_moe",
    )(x, *small, lp["e_gate"], lp["e_up"], lp["e_down"], *tail)


def _ple_kernel(x_ref, p_ref, wp_ref, wg_ref, lng_ref, lnb_ref, o_ref):
    x = x_ref[...]
    emb = _dot(p_ref[...].astype(BF16), wp_ref[...])
    gate = _sigmoid(_dot(x.astype(BF16), wg_ref[...]))
    o_ref[...] = _layer_norm(DEEPNORM_ALPHA * x + emb * gate, lng_ref[...], lnb_ref[...])


def _ple(x, p, lp):
    t = x.shape[0]
    tm = min(512, t)
    d_ple = p.shape[1]
    row = lambda w: pl.BlockSpec((tm, w), lambda i: (i, 0))
    params = [lp["w_ple"], lp["w_pg"], lp["ln_g2"], lp["ln_b2"]]
    return pl.pallas_call(
        _ple_kernel,
        grid=(t // tm,),
        in_specs=[row(D_MODEL), row(d_ple)] + [_full(q.shape) for q in params],
        out_specs=row(D_MODEL),
        out_shape=jax.ShapeDtypeStruct((t, D_MODEL), F32),
        compiler_params=_cparams("parallel"),
        name="ple",
    )(x, p, *params)


def _pad_to(a, axis, size):
    pad = [(0, 0)] * a.ndim
    pad[axis] = (0, size - a.shape[axis])
    return jnp.pad(a, pad)


def _row(v):
    return v.reshape(1, -1).astype(F32)


def _layer_params(i, w_in, tok_mix, decay_base, decay_up, aaa_base, aaa_up, gate_up, k_k, k_a, r_k,
                  vres_base, vres_down, vres_up, gn_g, gn_b, w_branch_attn, w_branch_rwkv, w_out,
                  router_grp, router_grp_bias, router_exp, router_exp_bias, exp_gate, exp_up, exp_down,
                  ple_proj, ple_gate, ln_g, ln_b, consts):
    d = D_HEADS
    w = w_in[i]
    rw0 = 3 * d
    gate0 = rw0 + 3 * d + DECAY_LORA + AAA_LORA + GATE_LORA
    wqkv = jnp.concatenate([w[:, 0:d] * (HEAD_DIM ** -0.5), w[:, d:rw0]], axis=1).astype(BF16)
    wr = w[:, rw0:gate0]
    lora0 = 3 * d
    pieces = [wr[:, :lora0],
              _pad_to(wr[:, lora0:lora0 + DECAY_LORA], 1, LANES),
              _pad_to(wr[:, lora0 + DECAY_LORA:lora0 + DECAY_LORA + AAA_LORA], 1, LANES),
              _pad_to(wr[:, lora0 + DECAY_LORA + AAA_LORA:], 1, 2 * LANES)]
    wrw = jnp.concatenate(pieces, axis=1).astype(BF16)
    mu = tok_mix[i]
    mu_pieces = [mu[:lora0],
                 _pad_to(mu[lora0:lora0 + DECAY_LORA], 0, LANES),
                 _pad_to(mu[lora0 + DECAY_LORA:lora0 + DECAY_LORA + AAA_LORA], 0, LANES),
                 _pad_to(mu[lora0 + DECAY_LORA + AAA_LORA:], 0, 2 * LANES)]
    lp = dict(consts)
    lp.update(
        wqkv=wqkv, wrw=wrw, wgate=w[:, gate0:].astype(BF16),
        mu=_row(jnp.concatenate(mu_pieces)),
        decay_base=_row(decay_base[i]), decay_up=_pad_to(decay_up[i], 0, LANES).astype(F32),
        aaa_base=_row(aaa_base[i]), aaa_up=_pad_to(aaa_up[i], 0, LANES).astype(F32),
        gate_up=_pad_to(gate_up[i], 0, 2 * LANES).astype(F32),
        k_k=_row(k_k[i]), k_a=_row(k_a[i]), r_k=_row(r_k[i]),
        gn_g=_row(gn_g[i]), gn_b=_row(gn_b[i]),
        w_a=w_branch_attn[i].astype(BF16), w_b=w_branch_rwkv[i].astype(BF16), w_o=w_out[i].astype(BF16),
        w_grp=_pad_to(router_grp[i], 1, LANES).astype(F32), b_grp=_row(_pad_to(router_grp_bias[i], 0, LANES)),
        w_exp=_pad_to(router_exp[i], 1, LANES).astype(F32), b_exp=_row(_pad_to(router_exp_bias[i], 0, LANES)),
        e_gate=exp_gate[i].astype(BF16), e_up=exp_up[i].astype(BF16), e_down=exp_down[i].astype(BF16),
        w_ple=ple_proj[i].astype(BF16), w_pg=ple_gate[i].astype(BF16),
        ln_g0=_row(ln_g[i, 0]), ln_b0=_row(ln_b[i, 0]),
        ln_g1=_row(ln_g[i, 1]), ln_b1=_row(ln_b[i, 1]),
        ln_g2=_row(ln_g[i, 2]), ln_b2=_row(ln_b[i, 2]),
    )
    if i > 0:
        lp.update(vres_base=_row(vres_base[i - 1]),
                  vres_down=_pad_to(vres_down[i - 1], 1, LANES).astype(F32),
                  vres_up=_pad_to(vres_up[i - 1], 0, LANES).astype(F32))
    return lp


def _const_mats(tm):
    head = jnp.arange(D_HEADS) // HEAD_DIM
    e_ones = (head[:, None] == head[None, :]).astype(BF16)
    tok = jnp.arange(tm)
    same = (tok[:, None] // CHUNK) == (tok[None, :] // CHUNK)
    tri = (same & (tok[:, None] >= tok[None, :])).astype(BF16)
    blk = same.astype(BF16)
    return dict(e_ones=e_ones, tri=tri, blk=blk)


def kernel(x, p, ln_in_g, ln_in_b, rel_bias, w_in, tok_mix, decay_base, decay_up, aaa_base, aaa_up,
           gate_up, k_k, k_a, r_k, vres_base, vres_down, vres_up, gn_g, gn_b, w_branch_attn,
           w_branch_rwkv, w_out, router_grp, router_grp_bias, router_exp, router_exp_bias, exp_gate,
           exp_up, exp_down, ple_proj, ple_gate, ln_g, ln_b):
    batch, seq, _ = x.shape
    t = batch * seq
    consts = _const_mats(256)
    bias_tabs = _attn_bias_tables(rel_bias)
    xt = x.reshape(t, D_MODEL)
    pt = p.reshape(p.shape[0], t, p.shape[-1])
    v_first = None
    for i in range(DEPTH):
        lp = _layer_params(i, w_in, tok_mix, decay_base, decay_up, aaa_base, aaa_up, gate_up, k_k, k_a,
                           r_k, vres_base, vres_down, vres_up, gn_g, gn_b, w_branch_attn, w_branch_rwkv,
                           w_out, router_grp, router_grp_bias, router_exp, router_exp_bias, exp_gate,
                           exp_up, exp_down, ple_proj, ple_gate, ln_g, ln_b, consts)
        if i == 0:
            xt, qkv, rw, gate = _proj(xt, _row(ln_in_g), _row(ln_in_b), lp["wqkv"], lp["wrw"],
                                      lp["wgate"], apply_ln=True)
        else:
            qkv, rw, gate = _proj(xt, _row(ln_in_g), _row(ln_in_b), lp["wqkv"], lp["wrw"],
                                  lp["wgate"], apply_ln=False)
        y_attn = _attention(qkv, bias_tabs, batch, seq)
        if i == 0:
            q1, y0, m, n0, g, bonus, v_first = _rwkv_chunk(rw, lp, seq, None)
        else:
            q1, y0, m, n0, g, bonus = _rwkv_chunk(rw, lp, seq, v_first)
        y = _rwkv_scan(q1, y0, m, n0, batch, seq)
        xt = _mix(y, g, bonus, y_attn, gate, xt, lp)
        xt = _moe(xt, lp)
        xt = _ple(xt, pt[i], lp)
    return xt.reshape(batch, seq, D_MODEL)
```

```python
import functools
import math

import jax
import jax.numpy as jnp
from jax import lax
from jax.experimental import pallas as pl
from jax.experimental.pallas import tpu as pltpu

F32 = jnp.float32
BF16 = jnp.bfloat16

D_MODEL = 1024
CHUNK = 64
N_PREV_CHUNKS = 8
HEAD_DIM = 64
N_HEADS = 8
D_HEADS = N_HEADS * HEAD_DIM
REL_CLIP = 128
DECAY_LORA = 64
AAA_LORA = 64
GATE_LORA = 160
MV_LORA = 32
N_GROUPS = 4
EXPERTS_PER_GROUP = 8
N_EXPERTS = N_GROUPS * EXPERTS_PER_GROUP
D_EXPERT = 256
DEPTH = 2
DEEPNORM_ALPHA = (2 * DEPTH) ** 0.25
LN_EPS = 1e-5
GN_EPS = 64e-5
NEG_INF = -1e30

LANES = 128
PAIR = 2 * HEAD_DIM
ATTN_QBLK = 2 * CHUNK
ATTN_WIN = (N_PREV_CHUNKS + 2) * CHUNK
RW_COLS_PAD = 2048
VMEM_LIMIT = 56 * 1024 * 1024


def _cparams(*sem):
    return pltpu.CompilerParams(dimension_semantics=sem, vmem_limit_bytes=VMEM_LIMIT)


def _dot(a, b):
    return jnp.dot(a, b, preferred_element_type=F32)


def _dot_nt(a, b):
    return lax.dot_general(a, b, (((1,), (1,)), ((), ())), preferred_element_type=F32)


def _dot_tn(a, b):
    return lax.dot_general(a, b, (((0,), (0,)), ((), ())), preferred_element_type=F32)


def _split2(x):
    hi = x.astype(BF16)
    lo = (x - hi.astype(F32)).astype(BF16)
    return hi, lo


def _dot3(a, b):
    ah, al = _split2(a)
    bh, bl = _split2(b)
    return _dot(ah, bh) + (_dot(al, bh) + _dot(ah, bl))


def _dot_exact_lhs(a_bf16, x):
    x1 = x.astype(BF16)
    r1 = x - x1.astype(F32)
    x2 = r1.astype(BF16)
    x3 = (r1 - x2.astype(F32)).astype(BF16)
    return _dot(a_bf16, x1) + (_dot(a_bf16, x2) + _dot(a_bf16, x3))


def _seg_sum(x, e_ones):
    hi, lo = _split2(x)
    return _dot(hi, e_ones) + _dot(lo, e_ones)


def _layer_norm(x, g, b):
    mu = jnp.mean(x, axis=-1, keepdims=True)
    xc = x - mu
    var = jnp.mean(xc * xc, axis=-1, keepdims=True)
    return xc * lax.rsqrt(var + LN_EPS) * g + b


def _sigmoid(x):
    return 1.0 / (1.0 + jnp.exp(-x))


def _full(shape):
    nd = len(shape)
    return pl.BlockSpec(shape, lambda *_: (0,) * nd)


def _proj_kernel(x_ref, g_ref, b_ref, wqkv_ref, wrw_ref, wgate_ref, *out_refs, apply_ln):
    x = x_ref[...]
    if apply_ln:
        xn_ref, qkv_ref, rw_ref, gate_ref = out_refs
        x = _layer_norm(x, g_ref[...], b_ref[...])
        xn_ref[...] = x
    else:
        qkv_ref, rw_ref, gate_ref = out_refs
    xb = x.astype(BF16)
    qkv_ref[...] = _dot(xb, wqkv_ref[...]).astype(BF16)
    rw_ref[...] = _dot(xb, wrw_ref[...])
    gate_ref[...] = _sigmoid(_dot(xb, wgate_ref[...])).astype(BF16)


def _proj(x, g, b, wqkv, wrw, wgate, apply_ln):
    t = x.shape[0]
    tm = min(256, t)
    row = lambda w: pl.BlockSpec((tm, w), lambda i: (i, 0))
    out_shape = [jax.ShapeDtypeStruct((t, 3 * D_HEADS), BF16),
                 jax.ShapeDtypeStruct((t, RW_COLS_PAD), F32),
                 jax.ShapeDtypeStruct((t, 2 * D_MODEL), BF16)]
    out_specs = [row(3 * D_HEADS), row(RW_COLS_PAD), row(2 * D_MODEL)]
    if apply_ln:
        out_shape = [jax.ShapeDtypeStruct((t, D_MODEL), F32)] + out_shape
        out_specs = [row(D_MODEL)] + out_specs
    return pl.pallas_call(
        functools.partial(_proj_kernel, apply_ln=apply_ln),
        grid=(t // tm,),
        in_specs=[row(D_MODEL), _full(g.shape), _full(b.shape),
                  _full(wqkv.shape), _full(wrw.shape), _full(wgate.shape)],
        out_specs=out_specs,
        out_shape=out_shape,
        compiler_params=_cparams("parallel"),
        name="proj_ln" if apply_ln else "proj",
    )(x, g, b, wqkv, wrw, wgate)


def _attn_kernel(q_ref, k_ref, v_ref, bias_ref, o_ref):
    qb = pl.program_id(1)
    ws = pl.multiple_of(jnp.maximum(qb * ATTN_QBLK - N_PREV_CHUNKS * CHUNK, 0), ATTN_QBLK)
    q = q_ref[0]
    kw = k_ref[0, pl.ds(ws, ATTN_WIN), :]
    vw = v_ref[0, pl.ds(ws, ATTN_WIN), :]
    outs = []
    for h in range(N_HEADS):
        hs = slice(h * HEAD_DIM, (h + 1) * HEAD_DIM)
        s = _dot_nt(q[:, hs], kw[:, hs]) + bias_ref[0, h]
        m = jnp.max(s, axis=-1, keepdims=True)
        p = jnp.exp(s - m)
        l = jnp.sum(p, axis=-1, keepdims=True)
        outs.append(_dot(p.astype(BF16), vw[:, hs]) * (1.0 / l))
    o_ref[0] = jnp.concatenate(outs, axis=-1).astype(BF16)


def _attention(qkv, bias_tabs, batch, seq):
    qkv3 = qkv.reshape(batch, seq, 3 * D_HEADS)
    n_tabs = bias_tabs.shape[0]
    out = pl.pallas_call(
        _attn_kernel,
        grid=(batch, seq // ATTN_QBLK),
        in_specs=[
            pl.BlockSpec((1, ATTN_QBLK, D_HEADS), lambda b, i: (b, i, 0)),
            pl.BlockSpec((1, seq, D_HEADS), lambda b, i: (b, 0, 1)),
            pl.BlockSpec((1, seq, D_HEADS), lambda b, i: (b, 0, 2)),
            pl.BlockSpec((1, N_HEADS, ATTN_QBLK, ATTN_WIN),
                         lambda b, i: (jnp.minimum(i, n_tabs - 1), 0, 0, 0)),
        ],
        out_specs=pl.BlockSpec((1, ATTN_QBLK, D_HEADS), lambda b, i: (b, i, 0)),
        out_shape=jax.ShapeDtypeStruct((batch, seq, D_HEADS), BF16),
        compiler_params=_cparams("parallel", "arbitrary"),
        name="band_attn",
    )(qkv3, qkv3, qkv3, bias_tabs)
    return out.reshape(batch * seq, D_HEADS)


def _attn_bias_tables(rel_bias):
    n_tabs = N_PREV_CHUNKS * CHUNK // ATTN_QBLK + 1
    start = (jnp.arange(n_tabs) * ATTN_QBLK)[:, None, None]
    qpos = start + jnp.arange(ATTN_QBLK)[None, :, None]
    kpos = jnp.arange(ATTN_WIN)[None, None, :]
    qc, kc = qpos // CHUNK, kpos // CHUNK
    valid = (kc <= qc) & (kc >= qc - N_PREV_CHUNKS)
    n = jnp.arange(ATTN_QBLK - 1 + ATTN_WIN)
    dist = start[:, :, 0] + (ATTN_QBLK - 1) - n[None, :]
    vals = rel_bias.astype(F32)[:, jnp.clip(dist, -REL_CLIP, REL_CLIP) + REL_CLIP]
    rows = [vals[:, :, ATTN_QBLK - 1 - i:ATTN_QBLK - 1 - i + ATTN_WIN] for i in range(ATTN_QBLK)]
    bias = jnp.stack(rows, axis=2)
    bias = jnp.where(valid[None], bias, NEG_INF)
    return jnp.transpose(bias, (1, 0, 2, 3))


def _rwkv_chunk_kernel(*refs, seq, tm, has_vres):
    if has_vres:
        (rw_ref, prev_ref, mu_ref, dbase_ref, dup_ref, abase_ref, aup_ref, gup_ref, kk_ref, ka_ref,
         rk_ref, eones_ref, tri_ref, blk_ref, vfirst_ref, vbase_ref, vdown_ref, vup_ref,
         q1_ref, y0_ref, m_ref, n0_ref, g_ref, bonus_ref,
         rw_s, aw_s, bi_s, ki_s, be_s, ke_s, v_s, wc_s) = refs
    else:
        (rw_ref, prev_ref, mu_ref, dbase_ref, dup_ref, abase_ref, aup_ref, gup_ref, kk_ref, ka_ref,
         rk_ref, eones_ref, tri_ref, blk_ref,
         q1_ref, y0_ref, m_ref, n0_ref, g_ref, bonus_ref, vfirst_out_ref,
         rw_s, aw_s, bi_s, ki_s, be_s, ke_s, v_s, wc_s) = refs

    cols = rw_ref[...]
    first = (pl.program_id(0) % (seq // tm)) == 0
    prev_row = jnp.where(first, 0.0, prev_ref[7:8, :])
    rows = lax.broadcasted_iota(jnp.int32, cols.shape, 0)
    prev = jnp.where(rows == 0, prev_row, pltpu.roll(cols, 1, axis=0))
    xs = cols + (prev - cols) * mu_ref[...]

    d = D_HEADS
    r, k, v = xs[:, 0:d], xs[:, d:2 * d], xs[:, 2 * d:3 * d]
    wd = xs[:, 3 * d:3 * d + LANES]
    ad = xs[:, 3 * d + LANES:3 * d + 2 * LANES]
    gd = xs[:, 3 * d + 2 * LANES:3 * d + 4 * LANES]
    e_ones = eones_ref[...]

    u = dbase_ref[...] + _dot3(jnp.tanh(wd), dup_ref[...])
    w_log = jnp.minimum(u, 0.0) - jnp.log(1.0 + jnp.exp(-jnp.abs(u))) - 0.5
    logw = -jnp.exp(w_log)
    a_icl = _sigmoid(abase_ref[...] + _dot3(ad, aup_ref[...]))
    g_ref[...] = _dot3(_sigmoid(gd), gup_ref[...])
    kk = k * kk_ref[...]
    kk = kk * lax.rsqrt(jnp.maximum(_seg_sum(kk * kk, e_ones), 1e-24))
    k2 = k * (1.0 + (a_icl - 1.0) * ka_ref[...])
    if has_vres:
        mix = _sigmoid(vbase_ref[...] + _dot3(_dot3(v, vdown_ref[...]), vup_ref[...]))
        v2 = v + (vfirst_ref[...] - v) * mix
    else:
        v2 = v
        vfirst_out_ref[...] = v
    a_vec = -kk
    b_vec = kk * a_icl
    bonus_ref[...] = _seg_sum(r * k2 * rk_ref[...], e_ones) * v2

    cw = _dot_exact_lhs(tri_ref[...], logw)
    cw_tot = _dot_exact_lhs(blk_ref[...], logw)
    rw_s[...] = r * jnp.exp(cw)
    aw_s[...] = a_vec * jnp.exp(cw - logw)
    inv = jnp.exp(-cw)
    bi_s[...] = b_vec * inv
    ki_s[...] = k2 * inv
    rest = jnp.exp(cw_tot - cw)
    be_s[...] = b_vec * rest
    ke_s[...] = k2 * rest
    v_s[...] = v2
    wc_s[...] = jnp.exp(cw_tot)

    shape = (CHUNK, PAIR)
    ti = lax.broadcasted_iota(jnp.int32, shape, 0)
    li = lax.broadcasted_iota(jnp.int32, shape, 1)
    si = li & (HEAD_DIM - 1)
    left = li < HEAD_DIM
    strict = ti > si
    incl = ti >= si
    eye = ti == si

    def bd(x):
        z = jnp.zeros_like(x)
        return jnp.concatenate([jnp.where(left, x, z), jnp.where(left, z, x)], axis=0)

    def fold(x):
        return jnp.where(left, x[:CHUNK], x[CHUNK:])

    def b16(xs):
        return [x.astype(BF16) for x in xs]

    def summaries(insts):
        sl = [(slice(c * CHUNK, (c + 1) * CHUNK), slice(p * PAIR, (p + 1) * PAIR)) for c, p in insts]
        rw = [rw_s[s] for s in sl]
        awb = b16([aw_s[s] for s in sl])
        vb = b16([v_s[s] for s in sl])
        lhs = [jnp.concatenate([a, r.astype(BF16)], axis=0) for a, r in zip(awb, rw)]
        gb = [_dot_nt(l, bd(x)) for l, x in zip(lhs, b16([bi_s[s] for s in sl]))]
        gk = [_dot_nt(l, bd(x)) for l, x in zip(lhs, b16([ki_s[s] for s in sl]))]
        a_ab = [jnp.where(strict, g[:CHUNK], 0.0) for g in gb]
        a_rb = b16([jnp.where(incl, g[CHUNK:], 0.0) for g in gb])
        a_ak = b16([jnp.where(strict, g[:CHUNK], 0.0) for g in gk])
        a_rk = b16([jnp.where(incl, g[CHUNK:], 0.0) for g in gk])
        bdv = [bd(x) for x in vb]
        x0 = [_dot(a, v) for a, v in zip(a_ak, bdv)]
        lp = a_ab
        tinv = [jnp.where(eye, 1.0, a) for a in a_ab]
        for _ in range(5):
            lpb = b16(lp)
            lp = [_dot(l, bd(l)) for l in lpb]
            tinv = [t + _dot(t.astype(BF16), bd(l.astype(BF16))) for t, l in zip(tinv, lp)]
        tb = b16(tinv)
        p1b = b16([_dot(t, bd(a)) for t, a in zip(tb, awb)])
        u0b = b16([_dot(t, bd(x.astype(BF16))) for t, x in zip(tb, x0)])
        q1 = [r + _dot(a, bd(p)) for r, a, p in zip(rw, a_rb, p1b)]
        y0 = [_dot(a, bd(u)) + _dot(k, v) for a, u, k, v in zip(a_rb, u0b, a_rk, bdv)]
        beb = b16([be_s[s] for s in sl])
        keb = b16([ke_s[s] for s in sl])
        mm = [fold(_dot_tn(b, p)) for b, p in zip(beb, p1b)]
        nn = [fold(_dot_tn(b, u) + _dot_tn(k, v)) for b, u, k, v in zip(beb, u0b, keb, vb)]
        for i, (s, (c, p)) in enumerate(zip(sl, insts)):
            wc = wc_s[c * CHUNK:c * CHUNK + 1, s[1]]
            q1_ref[s] = q1[i]
            y0_ref[s] = y0[i]
            m_ref[s] = jnp.where(eye, wc, 0.0) + mm[i]
            n0_ref[s] = nn[i]

    n_pairs = D_HEADS // PAIR
    for c in range(tm // CHUNK):
        summaries([(c, p) for p in range(n_pairs)])


def _rwkv_chunk(rw, lp, seq, v_first):
    t = rw.shape[0]
    tm = 256
    has_vres = v_first is not None
    nct = tm // CHUNK
    row = lambda w: pl.BlockSpec((tm, w), lambda i: (i, 0))
    prev_spec = pl.BlockSpec((8, RW_COLS_PAD), lambda i: (jnp.maximum(i * (tm // 8) - 1, 0), 0))
    params = [lp["mu"], lp["decay_base"], lp["decay_up"], lp["aaa_base"], lp["aaa_up"], lp["gate_up"],
              lp["k_k"], lp["k_a"], lp["r_k"], lp["e_ones"], lp["tri"], lp["blk"]]
    inputs = [rw, rw] + params
    in_specs = [row(RW_COLS_PAD), prev_spec] + [_full(p.shape) for p in params]
    if has_vres:
        extra = [lp["vres_base"], lp["vres_down"], lp["vres_up"]]
        inputs += [v_first] + extra
        in_specs += [row(D_HEADS)] + [_full(p.shape) for p in extra]
    tok_shape = jax.ShapeDtypeStruct((t, D_HEADS), F32)
    out_shape = [tok_shape] * 6
    out_specs = [row(D_HEADS)] * 6
    if not has_vres:
        out_shape.append(tok_shape)
        out_specs.append(row(D_HEADS))
    scratch = [pltpu.VMEM((tm, D_HEADS), F32) for _ in range(8)]
    return pl.pallas_call(
        functools.partial(_rwkv_chunk_kernel, seq=seq, tm=tm, has_vres=has_vres),
        grid=(t // tm,),
        in_specs=in_specs,
        out_specs=out_specs,
        out_shape=out_shape,
        scratch_shapes=scratch,
        compiler_params=_cparams("parallel"),
        name="rwkv_chunk_vres" if has_vres else "rwkv_chunk",
    )(*inputs)


def _rwkv_scan_kernel(q1_ref, y0_ref, m_ref, n0_ref, y_ref, st_ref, *, n_chunks):
    @pl.when(pl.program_id(1) == 0)
    def _():
        st_ref[...] = jnp.zeros_like(st_ref)

    left = lax.broadcasted_iota(jnp.int32, (CHUNK, PAIR), 1) < HEAD_DIM

    def bd(x):
        z = jnp.zeros_like(x)
        return jnp.concatenate([jnp.where(left, x, z), jnp.where(left, z, x)], axis=0)

    n_pairs = D_HEADS // PAIR
    st = [st_ref[:, p * PAIR:(p + 1) * PAIR] for p in range(n_pairs)]
    for c in range(n_chunks):
        rs = slice(c * CHUNK, (c + 1) * CHUNK)
        for p in range(n_pairs):
            ls = slice(p * PAIR, (p + 1) * PAIR)
            lh, ll = _split2(jnp.concatenate([q1_ref[rs, ls], m_ref[rs, ls]], axis=0))
            sh, sl = _split2(st[p])
            bh = bd(sh)
            res = _dot(lh, bh) + (_dot(ll, bh) + _dot(lh, bd(sl)))
            y_ref[rs, ls] = res[:CHUNK] + y0_ref[rs, ls]
            st[p] = res[CHUNK:] + n0_ref[rs, ls]
    for p in range(n_pairs):
        st_ref[:, p * PAIR:(p + 1) * PAIR] = st[p]


def _rwkv_scan(q1, y0, m, n0, batch, seq):
    t = q1.shape[0]
    cb = 8
    steps = seq // (cb * CHUNK)
    tok_spec = pl.BlockSpec((cb * CHUNK, D_HEADS), lambda b, j: (b * steps + j, 0))
    return pl.pallas_call(
        functools.partial(_rwkv_scan_kernel, n_chunks=cb),
        grid=(batch, steps),
        in_specs=[tok_spec] * 4,
        out_specs=tok_spec,
        out_shape=jax.ShapeDtypeStruct((t, D_HEADS), F32),
        scratch_shapes=[pltpu.VMEM((HEAD_DIM, D_HEADS), F32)],
        compiler_params=_cparams("parallel", "arbitrary"),
        name="rwkv_scan",
    )(q1, y0, m, n0)


def _mix_kernel(y_ref, g_ref, bonus_ref, ya_ref, gate_ref, x_ref, wa_ref, wb_ref, wo_ref,
                eones_ref, gng_ref, gnb_ref, lng_ref, lnb_ref, o_ref):
    e_ones = eones_ref[...]
    y = y_ref[...]
    mean = _seg_sum(y, e_ones) * (1.0 / HEAD_DIM)
    yc = y - mean
    var = _seg_sum(yc * yc, e_ones) * (1.0 / HEAD_DIM)
    yn = yc * lax.rsqrt(var + GN_EPS) * gng_ref[...] + gnb_ref[...]
    yr = (yn + bonus_ref[...]) * g_ref[...]
    za = _dot(ya_ref[...], wa_ref[...])
    zb = _dot(yr.astype(BF16), wb_ref[...])
    gate = gate_ref[...]
    mixed_in = gate[:, :D_MODEL].astype(F32) * za + gate[:, D_MODEL:].astype(F32) * zb
    mixed = _dot(mixed_in.astype(BF16), wo_ref[...])
    o_ref[...] = _layer_norm(DEEPNORM_ALPHA * x_ref[...] + mixed, lng_ref[...], lnb_ref[...])


def _mix(y, g, bonus, y_attn, gate, x, lp):
    t = x.shape[0]
    tm = min(256, t)
    row = lambda w: pl.BlockSpec((tm, w), lambda i: (i, 0))
    params = [lp["w_a"], lp["w_b"], lp["w_o"], lp["e_ones"], lp["gn_g"], lp["gn_b"], lp["ln_g0"], lp["ln_b0"]]
    return pl.pallas_call(
        _mix_kernel,
        grid=(t // tm,),
        in_specs=[row(D_HEADS), row(D_HEADS), row(D_HEADS), row(D_HEADS), row(2 * D_MODEL), row(D_MODEL)]
        + [_full(p.shape) for p in params],
        out_specs=row(D_MODEL),
        out_shape=jax.ShapeDtypeStruct((t, D_MODEL), F32),
        compiler_params=_cparams("parallel"),
        name="branch_mix",
    )(y, g, bonus, y_attn, gate, x, *params)


MOE_SLAB = 1024
MOE_WIN = 128
MOE_ALIGN = 8
MOE_ROWS = 2 * MOE_SLAB + N_EXPERTS * (MOE_ALIGN - 1) + MOE_WIN
MOE_ROWS = -(-MOE_ROWS // MOE_ALIGN) * MOE_ALIGN
RANK_BLK = 256


def _route_kernel(x_ref, wgrp_ref, bgrp_ref, wexp_ref, bexp_ref, tri_ref, w_ref, pos_ref, seg_ref):
    x = x_ref[...]
    tn = x.shape[0]
    lane = lax.broadcasted_iota(jnp.int32, (tn, LANES), 1)
    lanef = lane.astype(F32)
    hp = lax.Precision.HIGHEST
    glog = jnp.dot(x, wgrp_ref[...], precision=hp, preferred_element_type=F32) + bgrp_ref[...]
    elog = jnp.dot(x, wexp_ref[...], precision=hp, preferred_element_type=F32) + bexp_ref[...]
    big = float(LANES)
    glog = jnp.where(lane < N_GROUPS, glog, -jnp.inf)
    gmax = jnp.max(glog, axis=-1, keepdims=True)
    g_gate = 1.0 / jnp.sum(jnp.exp(glog - gmax), axis=-1, keepdims=True)
    grp = jnp.min(jnp.where(glog == gmax, lanef, big), axis=-1, keepdims=True)
    in_grp = (lanef >= grp * EXPERTS_PER_GROUP) & (lanef < (grp + 1.0) * EXPERTS_PER_GROUP)
    sel = jnp.where(in_grp, elog, -jnp.inf)
    v1 = jnp.max(sel, axis=-1, keepdims=True)
    i1 = jnp.min(jnp.where(sel == v1, lanef, big), axis=-1, keepdims=True)
    sel2 = jnp.where(lanef == i1, -jnp.inf, sel)
    v2 = jnp.max(sel2, axis=-1, keepdims=True)
    i2 = jnp.min(jnp.where(sel2 == v2, lanef, big), axis=-1, keepdims=True)
    e2 = jnp.exp(v2 - v1)
    w1 = g_gate / (1.0 + e2)
    w2 = g_gate * e2 / (1.0 + e2)
    w_ref[...] = jnp.where(lane == 0, w1, jnp.where(lane == 1, w2, 0.0))

    hit1 = lanef == i1
    hit2 = lanef == i2
    onehot = jnp.where(hit1 | hit2, 1.0, 0.0)
    tri = tri_ref[...]
    carry = jnp.zeros((1, LANES), F32)
    ranks = []
    for j in range(tn // RANK_BLK):
        blk = onehot[j * RANK_BLK:(j + 1) * RANK_BLK]
        ranks.append(_dot(tri, blk.astype(BF16)) + carry)
        carry = carry + jnp.sum(blk, axis=0, keepdims=True)
    rank = jnp.concatenate(ranks, axis=0)
    cnt = carry
    padded = jnp.floor((cnt + (MOE_ALIGN - 1.0)) * (1.0 / MOE_ALIGN)) * MOE_ALIGN
    li = lax.broadcasted_iota(jnp.int32, (LANES, LANES), 0)
    lj = lax.broadcasted_iota(jnp.int32, (LANES, LANES), 1)
    before = jnp.where(li < lj, 1.0, 0.0)
    off = jnp.dot(jnp.broadcast_to(padded, (8, LANES)), before, precision=hp,
                  preferred_element_type=F32)[0:1]
    slot = off + rank
    pos1 = jnp.sum(jnp.where(hit1, slot, 0.0), axis=-1, keepdims=True)
    pos2 = jnp.sum(jnp.where(hit2, slot, 0.0), axis=-1, keepdims=True)
    pos_ref[...] = jnp.where(lane == 0, pos1, jnp.where(lane == 1, pos2, 0.0)).astype(jnp.int32)
    row8 = lax.broadcasted_iota(jnp.int32, (8, LANES), 0)
    seg_ref[0] = jnp.where(row8 == 0, off, jnp.where(row8 == 1, cnt, 0.0)).astype(jnp.int32)


def _route(x, lp):
    t = x.shape[0]
    tn = min(MOE_SLAB, t)
    n_slabs = t // tn
    small = [lp["w_grp"], lp["b_grp"], lp["w_exp"], lp["b_exp"], lp["tri_rank"]]
    row = lambda w: pl.BlockSpec((tn, w), lambda i: (i, 0))
    return pl.pallas_call(
        _route_kernel,
        grid=(n_slabs,),
        in_specs=[row(D_MODEL)] + [_full(p.shape) for p in small],
        out_specs=[row(LANES), row(LANES), pl.BlockSpec((1, 8, LANES), lambda i: (i, 0, 0))],
        out_shape=[jax.ShapeDtypeStruct((t, LANES), F32), jax.ShapeDtypeStruct((t, LANES), jnp.int32),
                   jax.ShapeDtypeStruct((n_slabs, 8, LANES), jnp.int32)],
        compiler_params=_cparams("parallel"),
        name="moe_route",
    )(x, *small)


def _moe_sparse_kernel(off_ref, cnt_ref, pos_ref, w_ref, x_ref, eg_ref, eu_ref, ed_ref, lng_ref, lnb_ref,
                       o_ref, xs_ref):
    s = pl.program_id(0)
    e = pl.program_id(1)
    tn = x_ref.shape[0]

    @pl.when((s == 0) & (e == 0))
    def _():
        xs_ref[...] = jnp.zeros_like(xs_ref)

    @pl.when(e == 0)
    def _():
        def scatter(t, carry):
            row = x_ref[pl.ds(t, 1), :]
            xs_ref[pl.ds(pos_ref[0, 0, 2 * t], 1), :] = row
            xs_ref[pl.ds(pos_ref[0, 0, 2 * t + 1], 1), :] = row
            return carry
        lax.fori_loop(0, tn, scatter, 0, unroll=8)

    off = off_ref[s * N_EXPERTS + e]
    cnt = cnt_ref[s * N_EXPERTS + e]
    ridx = lax.broadcasted_iota(jnp.int32, (MOE_WIN, 1), 0)

    def window(w, carry):
        r0 = pl.multiple_of(off + w * MOE_WIN, MOE_ALIGN)
        xw = xs_ref[pl.ds(r0, MOE_WIN), :]
        xb = xw.astype(BF16)
        hg = _dot(xb, eg_ref[0])
        hu = _dot(xb, eu_ref[0])
        hh = hg * _sigmoid(hg) * hu
        yw = _dot(hh.astype(BF16), ed_ref[0])
        xs_ref[pl.ds(r0, MOE_WIN), :] = jnp.where(ridx + w * MOE_WIN < cnt, yw, xw)
        return carry

    lax.fori_loop(0, (cnt + MOE_WIN - 1) // MOE_WIN, window, 0)

    @pl.when(e == N_EXPERTS - 1)
    def _():
        def combine(t, carry):
            y1 = xs_ref[pl.ds(pos_ref[0, 0, 2 * t], 1), :]
            y2 = xs_ref[pl.ds(pos_ref[0, 0, 2 * t + 1], 1), :]
            o_ref[pl.ds(t, 1), :] = w_ref[0, 0, 2 * t] * y1 + w_ref[0, 0, 2 * t + 1] * y2
            return carry
        lax.fori_loop(0, tn, combine, 0, unroll=8)
        for j in range(tn // RANK_BLK):
            rs = slice(j * RANK_BLK, (j + 1) * RANK_BLK)
            o_ref[rs, :] = _layer_norm(DEEPNORM_ALPHA * x_ref[rs, :] + o_ref[rs, :], lng_ref[...], lnb_ref[...])


def _moe_sparse(x, lp):
    t = x.shape[0]
    tn = min(MOE_SLAB, t)
    n_slabs = t // tn
    w, pos, seg = _route(x, lp)
    w_flat = w[:, :2].reshape(n_slabs, 1, 2 * tn)
    pos_flat = pos[:, :2].reshape(n_slabs, 1, 2 * tn)
    off = seg[:, 0, :N_EXPERTS].reshape(-1)
    cnt = seg[:, 1, :N_EXPERTS].reshape(-1)
    smem_row = pl.BlockSpec((1, 1, 2 * tn), lambda s, e, *_: (s, 0, 0), memory_space=pltpu.SMEM)
    slab = pl.BlockSpec((tn, D_MODEL), lambda s, e, *_: (s, 0))
    const = lambda p: pl.BlockSpec(p.shape, lambda s, e, *_: (0,) * p.ndim)
    grid_spec = pltpu.PrefetchScalarGridSpec(
        num_scalar_prefetch=2,
        grid=(n_slabs, N_EXPERTS),
        in_specs=[smem_row, smem_row, slab,
                  pl.BlockSpec((1, D_MODEL, D_EXPERT), lambda s, e, *_: (e, 0, 0)),
                  pl.BlockSpec((1, D_MODEL, D_EXPERT), lambda s, e, *_: (e, 0, 0)),
                  pl.BlockSpec((1, D_EXPERT, D_MODEL), lambda s, e, *_: (e, 0, 0)),
                  const(lp["ln_g1"]), const(lp["ln_b1"])],
        out_specs=slab,
        scratch_shapes=[pltpu.VMEM((MOE_ROWS, D_MODEL), F32)],
    )
    return pl.pallas_call(
        _moe_sparse_kernel,
        grid_spec=grid_spec,
        out_shape=jax.ShapeDtypeStruct((t, D_MODEL), F32),
        compiler_params=_cparams("arbitrary", "arbitrary"),
        name="moe_sparse",
    )(off, cnt, pos_flat, w_flat, x, lp["e_gate"], lp["e_up"], lp["e_down"], lp["ln_g1"], lp["ln_b1"])


def _ple_kernel(x_ref, p_ref, wp_ref, wg_ref, lng_ref, lnb_ref, o_ref):
    x = x_ref[...]
    emb = _dot(p_ref[...].astype(BF16), wp_ref[...])
    gate = _sigmoid(_dot(x.astype(BF16), wg_ref[...]))
    o_ref[...] = _layer_norm(DEEPNORM_ALPHA * x + emb * gate, lng_ref[...], lnb_ref[...])


def _ple(x, p, lp):
    t = x.shape[0]
    tm = min(512, t)
    d_ple = p.shape[1]
    row = lambda w: pl.BlockSpec((tm, w), lambda i: (i, 0))
    params = [lp["w_ple"], lp["w_pg"], lp["ln_g2"], lp["ln_b2"]]
    return pl.pallas_call(
        _ple_kernel,
        grid=(t // tm,),
        in_specs=[row(D_MODEL), row(d_ple)] + [_full(q.shape) for q in params],
        out_specs=row(D_MODEL),
        out_shape=jax.ShapeDtypeStruct((t, D_MODEL), F32),
        compiler_params=_cparams("parallel"),
        name="ple",
    )(x, p, *params)


def _pad_to(a, axis, size):
    pad = [(0, 0)] * a.ndim
    pad[axis] = (0, size - a.shape[axis])
    return jnp.pad(a, pad)


def _row(v):
    return v.reshape(1, -1).astype(F32)


def _layer_params(i, w_in, tok_mix, decay_base, decay_up, aaa_base, aaa_up, gate_up, k_k, k_a, r_k,
                  vres_base, vres_down, vres_up, gn_g, gn_b, w_branch_attn, w_branch_rwkv, w_out,
                  router_grp, router_grp_bias, router_exp, router_exp_bias, exp_gate, exp_up, exp_down,
                  ple_proj, ple_gate, ln_g, ln_b, consts):
    d = D_HEADS
    w = w_in[i]
    rw0 = 3 * d
    gate0 = rw0 + 3 * d + DECAY_LORA + AAA_LORA + GATE_LORA
    wqkv = jnp.concatenate([w[:, 0:d] * (HEAD_DIM ** -0.5), w[:, d:rw0]], axis=1).astype(BF16)
    wr = w[:, rw0:gate0]
    lora0 = 3 * d
    pieces = [wr[:, :lora0],
              _pad_to(wr[:, lora0:lora0 + DECAY_LORA], 1, LANES),
              _pad_to(wr[:, lora0 + DECAY_LORA:lora0 + DECAY_LORA + AAA_LORA], 1, LANES),
              _pad_to(wr[:, lora0 + DECAY_LORA + AAA_LORA:], 1, 2 * LANES)]
    wrw = jnp.concatenate(pieces, axis=1).astype(BF16)
    mu = tok_mix[i]
    mu_pieces = [mu[:lora0],
                 _pad_to(mu[lora0:lora0 + DECAY_LORA], 0, LANES),
                 _pad_to(mu[lora0 + DECAY_LORA:lora0 + DECAY_LORA + AAA_LORA], 0, LANES),
                 _pad_to(mu[lora0 + DECAY_LORA + AAA_LORA:], 0, 2 * LANES)]
    lp = dict(consts)
    lp.update(
        wqkv=wqkv, wrw=wrw, wgate=w[:, gate0:].astype(BF16),
        mu=_row(jnp.concatenate(mu_pieces)),
        decay_base=_row(decay_base[i]), decay_up=_pad_to(decay_up[i], 0, LANES).astype(F32),
        aaa_base=_row(aaa_base[i]), aaa_up=_pad_to(aaa_up[i], 0, LANES).astype(F32),
        gate_up=_pad_to(gate_up[i], 0, 2 * LANES).astype(F32),
        k_k=_row(k_k[i]), k_a=_row(k_a[i]), r_k=_row(r_k[i]),
        gn_g=_row(gn_g[i]), gn_b=_row(gn_b[i]),
        w_a=w_branch_attn[i].astype(BF16), w_b=w_branch_rwkv[i].astype(BF16), w_o=w_out[i].astype(BF16),
        w_grp=_pad_to(router_grp[i], 1, LANES).astype(F32), b_grp=_row(_pad_to(router_grp_bias[i], 0, LANES)),
        w_exp=_pad_to(router_exp[i], 1, LANES).astype(F32), b_exp=_row(_pad_to(router_exp_bias[i], 0, LANES)),
        e_gate=exp_gate[i].astype(BF16), e_up=exp_up[i].astype(BF16), e_down=exp_down[i].astype(BF16),
        w_ple=ple_proj[i].astype(BF16), w_pg=ple_gate[i].astype(BF16),
        ln_g0=_row(ln_g[i, 0]), ln_b0=_row(ln_b[i, 0]),
        ln_g1=_row(ln_g[i, 1]), ln_b1=_row(ln_b[i, 1]),
        ln_g2=_row(ln_g[i, 2]), ln_b2=_row(ln_b[i, 2]),
    )
    if i > 0:
        lp.update(vres_base=_row(vres_base[i - 1]),
                  vres_down=_pad_to(vres_down[i - 1], 1, LANES).astype(F32),
                  vres_up=_pad_to(vres_up[i - 1], 0, LANES).astype(F32))
    return lp


def _const_mats(tm):
    head = jnp.arange(D_HEADS) // HEAD_DIM
    e_ones = (head[:, None] == head[None, :]).astype(BF16)
    tok = jnp.arange(tm)
    same = (tok[:, None] // CHUNK) == (tok[None, :] // CHUNK)
    tri = (same & (tok[:, None] >= tok[None, :])).astype(BF16)
    blk = same.astype(BF16)
    rtok = jnp.arange(RANK_BLK)
    tri_rank = (rtok[None, :] < rtok[:, None]).astype(BF16)
    return dict(e_ones=e_ones, tri=tri, blk=blk, tri_rank=tri_rank)


def kernel(x, p, ln_in_g, ln_in_b, rel_bias, w_in, tok_mix, decay_base, decay_up, aaa_base, aaa_up,
           gate_up, k_k, k_a, r_k, vres_base, vres_down, vres_up, gn_g, gn_b, w_branch_attn,
           w_branch_rwkv, w_out, router_grp, router_grp_bias, router_exp, router_exp_bias, exp_gate,
           exp_up, exp_down, ple_proj, ple_gate, ln_g, ln_b):
    batch, seq, _ = x.shape
    t = batch * seq
    consts = _const_mats(256)
    bias_tabs = _attn_bias_tables(rel_bias)
    xt = x.reshape(t, D_MODEL)
    pt = p.reshape(p.shape[0], t, p.shape[-1])
    v_first = None
    for i in range(DEPTH):
        lp = _layer_params(i, w_in, tok_mix, decay_base, decay_up, aaa_base, aaa_up, gate_up, k_k, k_a,
                           r_k, vres_base, vres_down, vres_up, gn_g, gn_b, w_branch_attn, w_branch_rwkv,
                           w_out, router_grp, router_grp_bias, router_exp, router_exp_bias, exp_gate,
                           exp_up, exp_down, ple_proj, ple_gate, ln_g, ln_b, consts)
        if i == 0:
            xt, qkv, rw, gate = _proj(xt, _row(ln_in_g), _row(ln_in_b), lp["wqkv"], lp["wrw"],
                                      lp["wgate"], apply_ln=True)
        else:
            qkv, rw, gate = _proj(xt, _row(ln_in_g), _row(ln_in_b), lp["wqkv"], lp["wrw"],
                                  lp["wgate"], apply_ln=False)
        y_attn = _attention(qkv, bias_tabs, batch, seq)
        if i == 0:
            q1, y0, m, n0, g, bonus, v_first = _rwkv_chunk(rw, lp, seq, None)
        else:
            q1, y0, m, n0, g, bonus = _rwkv_chunk(rw, lp, seq, v_first)
        y = _rwkv_scan(q1, y0, m, n0, batch, seq)
        xt = _mix(y, g, bonus, y_attn, gate, xt, lp)
        xt = _moe_sparse(xt, lp)
        xt = _ple(xt, pt[i], lp)
    return xt.reshape(batch, seq, D_MODEL)
```

```python
import functools
import math

import jax
import jax.numpy as jnp
from jax import lax
from jax.experimental import pallas as pl
from jax.experimental.pallas import tpu as pltpu

F32 = jnp.float32
BF16 = jnp.bfloat16

D_MODEL = 1024
CHUNK = 64
N_PREV_CHUNKS = 8
HEAD_DIM = 64
N_HEADS = 8
D_HEADS = N_HEADS * HEAD_DIM
REL_CLIP = 128
DECAY_LORA = 64
AAA_LORA = 64
GATE_LORA = 160
MV_LORA = 32
N_GROUPS = 4
EXPERTS_PER_GROUP = 8
N_EXPERTS = N_GROUPS * EXPERTS_PER_GROUP
D_EXPERT = 256
DEPTH = 2
DEEPNORM_ALPHA = (2 * DEPTH) ** 0.25
LN_EPS = 1e-5
GN_EPS = 64e-5
NEG_INF = -1e30

LANES = 128
PAIR = 2 * HEAD_DIM
ATTN_QBLK = 2 * CHUNK
ATTN_WIN = (N_PREV_CHUNKS + 2) * CHUNK
RW_COLS_PAD = 2048
VMEM_LIMIT = 56 * 1024 * 1024


def _cparams(*sem):
    return pltpu.CompilerParams(dimension_semantics=sem, vmem_limit_bytes=VMEM_LIMIT)


def _dot(a, b):
    return jnp.dot(a, b, preferred_element_type=F32)


def _dot_nt(a, b):
    return lax.dot_general(a, b, (((1,), (1,)), ((), ())), preferred_element_type=F32)


def _dot_tn(a, b):
    return lax.dot_general(a, b, (((0,), (0,)), ((), ())), preferred_element_type=F32)


def _split2(x):
    hi = x.astype(BF16)
    lo = (x - hi.astype(F32)).astype(BF16)
    return hi, lo


def _dot3(a, b):
    ah, al = _split2(a)
    bh, bl = _split2(b)
    return _dot(ah, bh) + (_dot(al, bh) + _dot(ah, bl))


def _dot_exact_lhs(a_bf16, x):
    x1 = x.astype(BF16)
    r1 = x - x1.astype(F32)
    x2 = r1.astype(BF16)
    x3 = (r1 - x2.astype(F32)).astype(BF16)
    return _dot(a_bf16, x1) + (_dot(a_bf16, x2) + _dot(a_bf16, x3))


def _seg_sum(x, e_ones):
    hi, lo = _split2(x)
    return _dot(hi, e_ones) + _dot(lo, e_ones)


def _layer_norm(x, g, b):
    mu = jnp.mean(x, axis=-1, keepdims=True)
    xc = x - mu
    var = jnp.mean(xc * xc, axis=-1, keepdims=True)
    return xc * lax.rsqrt(var + LN_EPS) * g + b


def _sigmoid(x):
    return 1.0 / (1.0 + jnp.exp(-x))


def _full(shape):
    nd = len(shape)
    return pl.BlockSpec(shape, lambda *_: (0,) * nd)


def _proj_kernel(x_ref, g_ref, b_ref, wqkv_ref, wrw_ref, wgate_ref, *out_refs, apply_ln):
    x = x_ref[...]
    if apply_ln:
        xn_ref, qkv_ref, rw_ref, gate_ref = out_refs
        x = _layer_norm(x, g_ref[...], b_ref[...])
        xn_ref[...] = x
    else:
        qkv_ref, rw_ref, gate_ref = out_refs
    xb = x.astype(BF16)
    qkv_ref[...] = _dot(xb, wqkv_ref[...]).astype(BF16)
    rw_ref[...] = _dot(xb, wrw_ref[...])
    gate_ref[...] = _sigmoid(_dot(xb, wgate_ref[...])).astype(BF16)


def _proj(x, g, b, wqkv, wrw, wgate, apply_ln):
    t = x.shape[0]
    tm = min(256, t)
    row = lambda w: pl.BlockSpec((tm, w), lambda i: (i, 0))
    out_shape = [jax.ShapeDtypeStruct((t, 3 * D_HEADS), BF16),
                 jax.ShapeDtypeStruct((t, RW_COLS_PAD), F32),
                 jax.ShapeDtypeStruct((t, 2 * D_MODEL), BF16)]
    out_specs = [row(3 * D_HEADS), row(RW_COLS_PAD), row(2 * D_MODEL)]
    if apply_ln:
        out_shape = [jax.ShapeDtypeStruct((t, D_MODEL), F32)] + out_shape
        out_specs = [row(D_MODEL)] + out_specs
    return pl.pallas_call(
        functools.partial(_proj_kernel, apply_ln=apply_ln),
        grid=(t // tm,),
        in_specs=[row(D_MODEL), _full(g.shape), _full(b.shape),
                  _full(wqkv.shape), _full(wrw.shape), _full(wgate.shape)],
        out_specs=out_specs,
        out_shape=out_shape,
        compiler_params=_cparams("parallel"),
        name="proj_ln" if apply_ln else "proj",
    )(x, g, b, wqkv, wrw, wgate)


def _attn_kernel(q_ref, k_ref, v_ref, bias_ref, o_ref):
    qb = pl.program_id(1)
    ws = pl.multiple_of(jnp.maximum(qb * ATTN_QBLK - N_PREV_CHUNKS * CHUNK, 0), ATTN_QBLK)
    q = q_ref[0]
    kw = k_ref[0, pl.ds(ws, ATTN_WIN), :]
    vw = v_ref[0, pl.ds(ws, ATTN_WIN), :]
    outs = []
    for h in range(N_HEADS):
        hs = slice(h * HEAD_DIM, (h + 1) * HEAD_DIM)
        s = _dot_nt(q[:, hs], kw[:, hs]) + bias_ref[0, h]
        m = jnp.max(s, axis=-1, keepdims=True)
        p = jnp.exp(s - m)
        l = jnp.sum(p, axis=-1, keepdims=True)
        outs.append(_dot(p.astype(BF16), vw[:, hs]) * (1.0 / l))
    o_ref[0] = jnp.concatenate(outs, axis=-1).astype(BF16)


def _attention(qkv, bias_tabs, batch, seq):
    qkv3 = qkv.reshape(batch, seq, 3 * D_HEADS)
    n_tabs = bias_tabs.shape[0]
    out = pl.pallas_call(
        _attn_kernel,
        grid=(batch, seq // ATTN_QBLK),
        in_specs=[
            pl.BlockSpec((1, ATTN_QBLK, D_HEADS), lambda b, i: (b, i, 0)),
            pl.BlockSpec((1, seq, D_HEADS), lambda b, i: (b, 0, 1)),
            pl.BlockSpec((1, seq, D_HEADS), lambda b, i: (b, 0, 2)),
            pl.BlockSpec((1, N_HEADS, ATTN_QBLK, ATTN_WIN),
                         lambda b, i: (jnp.minimum(i, n_tabs - 1), 0, 0, 0)),
        ],
        out_specs=pl.BlockSpec((1, ATTN_QBLK, D_HEADS), lambda b, i: (b, i, 0)),
        out_shape=jax.ShapeDtypeStruct((batch, seq, D_HEADS), BF16),
        compiler_params=_cparams("parallel", "arbitrary"),
        name="band_attn",
    )(qkv3, qkv3, qkv3, bias_tabs)
    return out.reshape(batch * seq, D_HEADS)


def _attn_bias_tables(rel_bias):
    n_tabs = N_PREV_CHUNKS * CHUNK // ATTN_QBLK + 1
    start = (jnp.arange(n_tabs) * ATTN_QBLK)[:, None, None]
    qpos = start + jnp.arange(ATTN_QBLK)[None, :, None]
    kpos = jnp.arange(ATTN_WIN)[None, None, :]
    qc, kc = qpos // CHUNK, kpos // CHUNK
    valid = (kc <= qc) & (kc >= qc - N_PREV_CHUNKS)
    n = jnp.arange(ATTN_QBLK - 1 + ATTN_WIN)
    dist = start[:, :, 0] + (ATTN_QBLK - 1) - n[None, :]
    vals = rel_bias.astype(F32)[:, jnp.clip(dist, -REL_CLIP, REL_CLIP) + REL_CLIP]
    rows = [vals[:, :, ATTN_QBLK - 1 - i:ATTN_QBLK - 1 - i + ATTN_WIN] for i in range(ATTN_QBLK)]
    bias = jnp.stack(rows, axis=2)
    bias = jnp.where(valid[None], bias, NEG_INF)
    return jnp.transpose(bias, (1, 0, 2, 3))


def _rwkv_chunk_kernel(*refs, seq, tm, has_vres):
    if has_vres:
        (rw_ref, prev_ref, mu_ref, dbase_ref, dup_ref, abase_ref, aup_ref, gup_ref, kk_ref, ka_ref,
         rk_ref, eones_ref, tri_ref, blk_ref, vfirst_ref, vbase_ref, vdown_ref, vup_ref,
         q1_ref, y0_ref, m_ref, n0_ref, g_ref, bonus_ref,
         rw_s, aw_s, bi_s, ki_s, be_s, ke_s, v_s, wc_s) = refs
    else:
        (rw_ref, prev_ref, mu_ref, dbase_ref, dup_ref, abase_ref, aup_ref, gup_ref, kk_ref, ka_ref,
         rk_ref, eones_ref, tri_ref, blk_ref,
         q1_ref, y0_ref, m_ref, n0_ref, g_ref, bonus_ref, vfirst_out_ref,
         rw_s, aw_s, bi_s, ki_s, be_s, ke_s, v_s, wc_s) = refs

    cols = rw_ref[...]
    first = (pl.program_id(0) % (seq // tm)) == 0
    prev_row = jnp.where(first, 0.0, prev_ref[7:8, :])
    rows = lax.broadcasted_iota(jnp.int32, cols.shape, 0)
    prev = jnp.where(rows == 0, prev_row, pltpu.roll(cols, 1, axis=0))
    xs = cols + (prev - cols) * mu_ref[...]

    d = D_HEADS
    r, k, v = xs[:, 0:d], xs[:, d:2 * d], xs[:, 2 * d:3 * d]
    wd = xs[:, 3 * d:3 * d + LANES]
    ad = xs[:, 3 * d + LANES:3 * d + 2 * LANES]
    gd = xs[:, 3 * d + 2 * LANES:3 * d + 4 * LANES]
    e_ones = eones_ref[...]

    u = dbase_ref[...] + _dot3(jnp.tanh(wd), dup_ref[...])
    w_log = jnp.minimum(u, 0.0) - jnp.log(1.0 + jnp.exp(-jnp.abs(u))) - 0.5
    logw = -jnp.exp(w_log)
    a_icl = _sigmoid(abase_ref[...] + _dot3(ad, aup_ref[...]))
    g_ref[...] = _dot3(_sigmoid(gd), gup_ref[...])
    kk = k * kk_ref[...]
    kk = kk * lax.rsqrt(jnp.maximum(_seg_sum(kk * kk, e_ones), 1e-24))
    k2 = k * (1.0 + (a_icl - 1.0) * ka_ref[...])
    if has_vres:
        mix = _sigmoid(vbase_ref[...] + _dot3(_dot3(v, vdown_ref[...]), vup_ref[...]))
        v2 = v + (vfirst_ref[...] - v) * mix
    else:
        v2 = v
        vfirst_out_ref[...] = v
    a_vec = -kk
    b_vec = kk * a_icl
    bonus_ref[...] = _seg_sum(r * k2 * rk_ref[...], e_ones) * v2

    cw = _dot_exact_lhs(tri_ref[...], logw)
    cw_tot = _dot_exact_lhs(blk_ref[...], logw)
    rw_s[...] = r * jnp.exp(cw)
    aw_s[...] = a_vec * jnp.exp(cw - logw)
    inv = jnp.exp(-cw)
    bi_s[...] = b_vec * inv
    ki_s[...] = k2 * inv
    rest = jnp.exp(cw_tot - cw)
    be_s[...] = b_vec * rest
    ke_s[...] = k2 * rest
    v_s[...] = v2
    wc_s[...] = jnp.exp(cw_tot)

    shape = (CHUNK, PAIR)
    ti = lax.broadcasted_iota(jnp.int32, shape, 0)
    li = lax.broadcasted_iota(jnp.int32, shape, 1)
    si = li & (HEAD_DIM - 1)
    left = li < HEAD_DIM
    strict = ti > si
    incl = ti >= si
    eye = ti == si

    def bd(x):
        z = jnp.zeros_like(x)
        return jnp.concatenate([jnp.where(left, x, z), jnp.where(left, z, x)], axis=0)

    def fold(x):
        return jnp.where(left, x[:CHUNK], x[CHUNK:])

    def b16(xs):
        return [x.astype(BF16) for x in xs]

    def summaries(insts):
        sl = [(slice(c * CHUNK, (c + 1) * CHUNK), slice(p * PAIR, (p + 1) * PAIR)) for c, p in insts]
        rw = [rw_s[s] for s in sl]
        awb = b16([aw_s[s] for s in sl])
        vb = b16([v_s[s] for s in sl])
        lhs = [jnp.concatenate([a, r.astype(BF16)], axis=0) for a, r in zip(awb, rw)]
        gb = [_dot_nt(l, bd(x)) for l, x in zip(lhs, b16([bi_s[s] for s in sl]))]
        gk = [_dot_nt(l, bd(x)) for l, x in zip(lhs, b16([ki_s[s] for s in sl]))]
        a_ab = [jnp.where(strict, g[:CHUNK], 0.0) for g in gb]
        a_rb = b16([jnp.where(incl, g[CHUNK:], 0.0) for g in gb])
        a_ak = b16([jnp.where(strict, g[:CHUNK], 0.0) for g in gk])
        a_rk = b16([jnp.where(incl, g[CHUNK:], 0.0) for g in gk])
        bdv = [bd(x) for x in vb]
        x0 = [_dot(a, v) for a, v in zip(a_ak, bdv)]
        lp = a_ab
        tinv = [jnp.where(eye, 1.0, a) for a in a_ab]
        for _ in range(5):
            lpb = b16(lp)
            lp = [_dot(l, bd(l)) for l in lpb]
            tinv = [t + _dot(t.astype(BF16), bd(l.astype(BF16))) for t, l in zip(tinv, lp)]
        tb = b16(tinv)
        p1b = b16([_dot(t, bd(a)) for t, a in zip(tb, awb)])
        u0b = b16([_dot(t, bd(x.astype(BF16))) for t, x in zip(tb, x0)])
        q1 = [r + _dot(a, bd(p)) for r, a, p in zip(rw, a_rb, p1b)]
        y0 = [_dot(a, bd(u)) + _dot(k, v) for a, u, k, v in zip(a_rb, u0b, a_rk, bdv)]
        beb = b16([be_s[s] for s in sl])
        keb = b16([ke_s[s] for s in sl])
        mm = [fold(_dot_tn(b, p)) for b, p in zip(beb, p1b)]
        nn = [fold(_dot_tn(b, u) + _dot_tn(k, v)) for b, u, k, v in zip(beb, u0b, keb, vb)]
        for i, (s, (c, p)) in enumerate(zip(sl, insts)):
            wc = wc_s[c * CHUNK:c * CHUNK + 1, s[1]]
            q1_ref[s] = q1[i]
            y0_ref[s] = y0[i]
            m_ref[s] = jnp.where(eye, wc, 0.0) + mm[i]
            n0_ref[s] = nn[i]

    n_pairs = D_HEADS // PAIR
    for c in range(tm // CHUNK):
        summaries([(c, p) for p in range(n_pairs)])


def _rwkv_chunk(rw, lp, seq, v_first):
    t = rw.shape[0]
    tm = 256
    has_vres = v_first is not None
    nct = tm // CHUNK
    row = lambda w: pl.BlockSpec((tm, w), lambda i: (i, 0))
    prev_spec = pl.BlockSpec((8, RW_COLS_PAD), lambda i: (jnp.maximum(i * (tm // 8) - 1, 0), 0))
    params = [lp["mu"], lp["decay_base"], lp["decay_up"], lp["aaa_base"], lp["aaa_up"], lp["gate_up"],
              lp["k_k"], lp["k_a"], lp["r_k"], lp["e_ones"], lp["tri"], lp["blk"]]
    inputs = [rw, rw] + params
    in_specs = [row(RW_COLS_PAD), prev_spec] + [_full(p.shape) for p in params]
    if has_vres:
        extra = [lp["vres_base"], lp["vres_down"], lp["vres_up"]]
        inputs += [v_first] + extra
        in_specs += [row(D_HEADS)] + [_full(p.shape) for p in extra]
    tok_shape = jax.ShapeDtypeStruct((t, D_HEADS), F32)
    out_shape = [tok_shape] * 6
    out_specs = [row(D_HEADS)] * 6
    if not has_vres:
        out_shape.append(tok_shape)
        out_specs.append(row(D_HEADS))
    scratch = [pltpu.VMEM((tm, D_HEADS), F32) for _ in range(8)]
    return pl.pallas_call(
        functools.partial(_rwkv_chunk_kernel, seq=seq, tm=tm, has_vres=has_vres),
        grid=(t // tm,),
        in_specs=in_specs,
        out_specs=out_specs,
        out_shape=out_shape,
        scratch_shapes=scratch,
        compiler_params=_cparams("parallel"),
        name="rwkv_chunk_vres" if has_vres else "rwkv_chunk",
    )(*inputs)


def _rwkv_scan_kernel(q1_ref, y0_ref, m_ref, n0_ref, y_ref, st_ref, *, n_chunks):
    @pl.when(pl.program_id(1) == 0)
    def _():
        st_ref[...] = jnp.zeros_like(st_ref)

    left = lax.broadcasted_iota(jnp.int32, (CHUNK, PAIR), 1) < HEAD_DIM

    def bd(x):
        z = jnp.zeros_like(x)
        return jnp.concatenate([jnp.where(left, x, z), jnp.where(left, z, x)], axis=0)

    n_pairs = D_HEADS // PAIR
    st = [st_ref[:, p * PAIR:(p + 1) * PAIR] for p in range(n_pairs)]
    for c in range(n_chunks):
        rs = slice(c * CHUNK, (c + 1) * CHUNK)
        for p in range(n_pairs):
            ls = slice(p * PAIR, (p + 1) * PAIR)
            lh, ll = _split2(jnp.concatenate([q1_ref[rs, ls], m_ref[rs, ls]], axis=0))
            sh, sl = _split2(st[p])
            bh = bd(sh)
            res = _dot(lh, bh) + (_dot(ll, bh) + _dot(lh, bd(sl)))
            y_ref[rs, ls] = res[:CHUNK] + y0_ref[rs, ls]
            st[p] = res[CHUNK:] + n0_ref[rs, ls]
    for p in range(n_pairs):
        st_ref[:, p * PAIR:(p + 1) * PAIR] = st[p]


def _rwkv_scan(q1, y0, m, n0, batch, seq):
    t = q1.shape[0]
    cb = 8
    steps = seq // (cb * CHUNK)
    tok_spec = pl.BlockSpec((cb * CHUNK, D_HEADS), lambda b, j: (b * steps + j, 0))
    return pl.pallas_call(
        functools.partial(_rwkv_scan_kernel, n_chunks=cb),
        grid=(batch, steps),
        in_specs=[tok_spec] * 4,
        out_specs=tok_spec,
        out_shape=jax.ShapeDtypeStruct((t, D_HEADS), F32),
        scratch_shapes=[pltpu.VMEM((HEAD_DIM, D_HEADS), F32)],
        compiler_params=_cparams("parallel", "arbitrary"),
        name="rwkv_scan",
    )(q1, y0, m, n0)


def _mix_kernel(y_ref, g_ref, bonus_ref, ya_ref, gate_ref, x_ref, wa_ref, wb_ref, wo_ref,
                eones_ref, gng_ref, gnb_ref, lng_ref, lnb_ref, o_ref):
    e_ones = eones_ref[...]
    y = y_ref[...]
    mean = _seg_sum(y, e_ones) * (1.0 / HEAD_DIM)
    yc = y - mean
    var = _seg_sum(yc * yc, e_ones) * (1.0 / HEAD_DIM)
    yn = yc * lax.rsqrt(var + GN_EPS) * gng_ref[...] + gnb_ref[...]
    yr = (yn + bonus_ref[...]) * g_ref[...]
    za = _dot(ya_ref[...], wa_ref[...])
    zb = _dot(yr.astype(BF16), wb_ref[...])
    gate = gate_ref[...]
    mixed_in = gate[:, :D_MODEL].astype(F32) * za + gate[:, D_MODEL:].astype(F32) * zb
    mixed = _dot(mixed_in.astype(BF16), wo_ref[...])
    o_ref[...] = _layer_norm(DEEPNORM_ALPHA * x_ref[...] + mixed, lng_ref[...], lnb_ref[...])


def _mix(y, g, bonus, y_attn, gate, x, lp):
    t = x.shape[0]
    tm = min(256, t)
    row = lambda w: pl.BlockSpec((tm, w), lambda i: (i, 0))
    params = [lp["w_a"], lp["w_b"], lp["w_o"], lp["e_ones"], lp["gn_g"], lp["gn_b"], lp["ln_g0"], lp["ln_b0"]]
    return pl.pallas_call(
        _mix_kernel,
        grid=(t // tm,),
        in_specs=[row(D_HEADS), row(D_HEADS), row(D_HEADS), row(D_HEADS), row(2 * D_MODEL), row(D_MODEL)]
        + [_full(p.shape) for p in params],
        out_specs=row(D_MODEL),
        out_shape=jax.ShapeDtypeStruct((t, D_MODEL), F32),
        compiler_params=_cparams("parallel"),
        name="branch_mix",
    )(y, g, bonus, y_attn, gate, x, *params)


MOE_SLAB = 2048
MOE_WIN = 128
MOE_ROWS = 2 * MOE_SLAB + MOE_WIN
SUBLANES = 8
RANK_BLK = 256


def _route_kernel(x_ref, wr_ref, br_ref, tri_ref, w_ref, pos_ref, seg_ref):
    x = x_ref[...]
    tn = x.shape[0]
    lane = lax.broadcasted_iota(jnp.int32, (tn, LANES), 1)
    lanef = lane.astype(F32)
    hp = lax.Precision.HIGHEST
    elog = _dot3(x, wr_ref[...]) + br_ref[...]
    big = float(LANES)
    glog = jnp.where((lane >= N_EXPERTS) & (lane < N_EXPERTS + N_GROUPS), elog, -jnp.inf)
    gmax = jnp.max(glog, axis=-1, keepdims=True)
    g_gate = 1.0 / jnp.sum(jnp.exp(glog - gmax), axis=-1, keepdims=True)
    grp = jnp.min(jnp.where(glog == gmax, lanef, big), axis=-1, keepdims=True) - N_EXPERTS
    in_grp = (lanef >= grp * EXPERTS_PER_GROUP) & (lanef < (grp + 1.0) * EXPERTS_PER_GROUP)
    sel = jnp.where(in_grp, elog, -jnp.inf)
    v1 = jnp.max(sel, axis=-1, keepdims=True)
    i1 = jnp.min(jnp.where(sel == v1, lanef, big), axis=-1, keepdims=True)
    sel2 = jnp.where(lanef == i1, -jnp.inf, sel)
    v2 = jnp.max(sel2, axis=-1, keepdims=True)
    i2 = jnp.min(jnp.where(sel2 == v2, lanef, big), axis=-1, keepdims=True)
    e2 = jnp.exp(v2 - v1)
    w1 = g_gate / (1.0 + e2)
    w2 = g_gate * e2 / (1.0 + e2)
    w_ref[...] = jnp.where(lane == 0, w1, jnp.where(lane == 1, w2, 0.0))

    hit1 = lanef == i1
    hit2 = lanef == i2
    onehot = jnp.where(hit1 | hit2, 1.0, 0.0)
    tri = tri_ref[...]
    carry = jnp.zeros((1, LANES), F32)
    ranks = []
    for j in range(tn // RANK_BLK):
        blk = onehot[j * RANK_BLK:(j + 1) * RANK_BLK]
        ranks.append(_dot(tri, blk.astype(BF16)) + carry)
        carry = carry + jnp.sum(blk, axis=0, keepdims=True)
    rank = jnp.concatenate(ranks, axis=0)
    cnt = carry
    li = lax.broadcasted_iota(jnp.int32, (LANES, LANES), 0)
    lj = lax.broadcasted_iota(jnp.int32, (LANES, LANES), 1)
    before = jnp.where(li < lj, 1.0, 0.0)
    off = jnp.dot(jnp.broadcast_to(cnt, (8, LANES)), before, precision=hp,
                  preferred_element_type=F32)[0:1]
    slot = off + rank
    pos1 = jnp.sum(jnp.where(hit1, slot, 0.0), axis=-1, keepdims=True)
    pos2 = jnp.sum(jnp.where(hit2, slot, 0.0), axis=-1, keepdims=True)
    pos_ref[...] = jnp.where(lane == 0, pos1, jnp.where(lane == 1, pos2, 0.0)).astype(jnp.int32)
    row8 = lax.broadcasted_iota(jnp.int32, (8, LANES), 0)
    seg_ref[0] = jnp.where(row8 == 0, off, jnp.where(row8 == 1, cnt, 0.0)).astype(jnp.int32)


def _route(x, lp):
    t = x.shape[0]
    tn = min(MOE_SLAB, t)
    n_slabs = t // tn
    small = [lp["w_route"], lp["b_route"], lp["tri_rank"]]
    row = lambda w: pl.BlockSpec((tn, w), lambda i: (i, 0))
    return pl.pallas_call(
        _route_kernel,
        grid=(n_slabs,),
        in_specs=[row(D_MODEL)] + [_full(p.shape) for p in small],
        out_specs=[row(LANES), row(LANES), pl.BlockSpec((1, 8, LANES), lambda i: (i, 0, 0))],
        out_shape=[jax.ShapeDtypeStruct((t, LANES), F32), jax.ShapeDtypeStruct((t, LANES), jnp.int32),
                   jax.ShapeDtypeStruct((n_slabs, 8, LANES), jnp.int32)],
        compiler_params=_cparams("parallel"),
        name="moe_route",
    )(x, *small)


def _moe_sparse_kernel(off_ref, cnt_ref, pos_ref, w_ref, x_ref, eg_ref, eu_ref, ed_ref, lng_ref, lnb_ref,
                       o_ref, xs_ref):
    s = pl.program_id(0)
    e = pl.program_id(1)
    tn = x_ref.shape[0]

    n_feat = D_MODEL // LANES

    def tile_rows(r):
        return pl.ds(pl.multiple_of(r * SUBLANES, SUBLANES), SUBLANES)

    @pl.when((s == 0) & (e == 0))
    def _():
        xs_ref[pl.ds(2 * tn * SUBLANES, MOE_WIN * SUBLANES), :] = jnp.zeros((MOE_WIN * SUBLANES, LANES), F32)

    @pl.when(e == 0)
    def _():
        def scatter(i, carry):
            t0 = pl.multiple_of(i * SUBLANES, SUBLANES)
            tiles = x_ref[pl.ds(t0, SUBLANES), :].reshape(SUBLANES, n_feat, LANES)
            for j in range(SUBLANES):
                xs_ref[tile_rows(pos_ref[0, 0, 2 * (t0 + j)]), :] = tiles[j]
                xs_ref[tile_rows(pos_ref[0, 0, 2 * (t0 + j) + 1]), :] = tiles[j]
            return carry
        lax.fori_loop(0, tn // SUBLANES, scatter, 0)

    off = off_ref[s * N_EXPERTS + e]
    cnt = cnt_ref[s * N_EXPERTS + e]
    ridx = lax.broadcasted_iota(jnp.int32, (MOE_WIN, 1), 0)

    def window(w, carry):
        r0 = off + w * MOE_WIN
        feat = [pl.ds(r0 * SUBLANES + f, MOE_WIN, stride=SUBLANES) for f in range(n_feat)]
        xw = [xs_ref[fs, :] for fs in feat]
        xb = jnp.concatenate(xw, axis=1).astype(BF16)
        hg = _dot(xb, eg_ref[0])
        hu = _dot(xb, eu_ref[0])
        hh = hg * _sigmoid(hg) * hu
        yw = _dot(hh.astype(BF16), ed_ref[0])
        valid = ridx + w * MOE_WIN < cnt
        for f in range(n_feat):
            xs_ref[feat[f], :] = jnp.where(valid, yw[:, f * LANES:(f + 1) * LANES], xw[f])
        return carry

    lax.fori_loop(0, (cnt + MOE_WIN - 1) // MOE_WIN, window, 0)

    @pl.when(e == N_EXPERTS - 1)
    def _():
        def combine(i, carry):
            t0 = pl.multiple_of(i * SUBLANES, SUBLANES)
            tiles = []
            for j in range(SUBLANES):
                y1 = xs_ref[tile_rows(pos_ref[0, 0, 2 * (t0 + j)]), :]
                y2 = xs_ref[tile_rows(pos_ref[0, 0, 2 * (t0 + j) + 1]), :]
                tiles.append(w_ref[0, 0, 2 * (t0 + j)] * y1 + w_ref[0, 0, 2 * (t0 + j) + 1] * y2)
            o_ref[pl.ds(t0, SUBLANES), :] = jnp.stack(tiles, axis=0).reshape(SUBLANES, D_MODEL)
            return carry
        lax.fori_loop(0, tn // SUBLANES, combine, 0)
        for j in range(tn // RANK_BLK):
            rs = slice(j * RANK_BLK, (j + 1) * RANK_BLK)
            o_ref[rs, :] = _layer_norm(DEEPNORM_ALPHA * x_ref[rs, :] + o_ref[rs, :], lng_ref[...], lnb_ref[...])


def _moe_sparse(x, lp):
    t = x.shape[0]
    tn = min(MOE_SLAB, t)
    n_slabs = t // tn
    w, pos, seg = _route(x, lp)
    w_flat = w[:, :2].reshape(n_slabs, 1, 2 * tn)
    pos_flat = pos[:, :2].reshape(n_slabs, 1, 2 * tn)
    off = seg[:, 0, :N_EXPERTS].reshape(-1)
    cnt = seg[:, 1, :N_EXPERTS].reshape(-1)
    smem_row = pl.BlockSpec((1, 1, 2 * tn), lambda s, e, *_: (s, 0, 0), memory_space=pltpu.SMEM)
    slab = pl.BlockSpec((tn, D_MODEL), lambda s, e, *_: (s, 0), pipeline_mode=pl.Buffered(1))
    const = lambda p: pl.BlockSpec(p.shape, lambda s, e, *_: (0,) * p.ndim)
    grid_spec = pltpu.PrefetchScalarGridSpec(
        num_scalar_prefetch=2,
        grid=(n_slabs, N_EXPERTS),
        in_specs=[smem_row, smem_row, slab,
                  pl.BlockSpec((1, D_MODEL, D_EXPERT), lambda s, e, *_: (e, 0, 0)),
                  pl.BlockSpec((1, D_MODEL, D_EXPERT), lambda s, e, *_: (e, 0, 0)),
                  pl.BlockSpec((1, D_EXPERT, D_MODEL), lambda s, e, *_: (e, 0, 0)),
                  const(lp["ln_g1"]), const(lp["ln_b1"])],
        out_specs=slab,
        scratch_shapes=[pltpu.VMEM((MOE_ROWS * SUBLANES, LANES), F32)],
    )
    return pl.pallas_call(
        _moe_sparse_kernel,
        grid_spec=grid_spec,
        out_shape=jax.ShapeDtypeStruct((t, D_MODEL), F32),
        compiler_params=_cparams("arbitrary", "arbitrary"),
        name="moe_sparse",
    )(off, cnt, pos_flat, w_flat, x, lp["e_gate"], lp["e_up"], lp["e_down"], lp["ln_g1"], lp["ln_b1"])


def _ple_kernel(x_ref, p_ref, wp_ref, wg_ref, lng_ref, lnb_ref, o_ref):
    x = x_ref[...]
    emb = _dot(p_ref[...].astype(BF16), wp_ref[...])
    gate = _sigmoid(_dot(x.astype(BF16), wg_ref[...]))
    o_ref[...] = _layer_norm(DEEPNORM_ALPHA * x + emb * gate, lng_ref[...], lnb_ref[...])


def _ple(x, p, lp):
    t = x.shape[0]
    tm = min(512, t)
    d_ple = p.shape[1]
    row = lambda w: pl.BlockSpec((tm, w), lambda i: (i, 0))
    params = [lp["w_ple"], lp["w_pg"], lp["ln_g2"], lp["ln_b2"]]
    return pl.pallas_call(
        _ple_kernel,
        grid=(t // tm,),
        in_specs=[row(D_MODEL), row(d_ple)] + [_full(q.shape) for q in params],
        out_specs=row(D_MODEL),
        out_shape=jax.ShapeDtypeStruct((t, D_MODEL), F32),
        compiler_params=_cparams("parallel"),
        name="ple",
    )(x, p, *params)


def _pad_to(a, axis, size):
    pad = [(0, 0)] * a.ndim
    pad[axis] = (0, size - a.shape[axis])
    return jnp.pad(a, pad)


def _row(v):
    return v.reshape(1, -1).astype(F32)


def _layer_params(i, w_in, tok_mix, decay_base, decay_up, aaa_base, aaa_up, gate_up, k_k, k_a, r_k,
                  vres_base, vres_down, vres_up, gn_g, gn_b, w_branch_attn, w_branch_rwkv, w_out,
                  router_grp, router_grp_bias, router_exp, router_exp_bias, exp_gate, exp_up, exp_down,
                  ple_proj, ple_gate, ln_g, ln_b, consts):
    d = D_HEADS
    w = w_in[i]
    rw0 = 3 * d
    gate0 = rw0 + 3 * d + DECAY_LORA + AAA_LORA + GATE_LORA
    wqkv = jnp.concatenate([w[:, 0:d] * (HEAD_DIM ** -0.5), w[:, d:rw0]], axis=1).astype(BF16)
    wr = w[:, rw0:gate0]
    lora0 = 3 * d
    pieces = [wr[:, :lora0],
              _pad_to(wr[:, lora0:lora0 + DECAY_LORA], 1, LANES),
              _pad_to(wr[:, lora0 + DECAY_LORA:lora0 + DECAY_LORA + AAA_LORA], 1, LANES),
              _pad_to(wr[:, lora0 + DECAY_LORA + AAA_LORA:], 1, 2 * LANES)]
    wrw = jnp.concatenate(pieces, axis=1).astype(BF16)
    mu = tok_mix[i]
    mu_pieces = [mu[:lora0],
                 _pad_to(mu[lora0:lora0 + DECAY_LORA], 0, LANES),
                 _pad_to(mu[lora0 + DECAY_LORA:lora0 + DECAY_LORA + AAA_LORA], 0, LANES),
                 _pad_to(mu[lora0 + DECAY_LORA + AAA_LORA:], 0, 2 * LANES)]
    lp = dict(consts)
    lp.update(
        wqkv=wqkv, wrw=wrw, wgate=w[:, gate0:].astype(BF16),
        mu=_row(jnp.concatenate(mu_pieces)),
        decay_base=_row(decay_base[i]), decay_up=_pad_to(decay_up[i], 0, LANES).astype(F32),
        aaa_base=_row(aaa_base[i]), aaa_up=_pad_to(aaa_up[i], 0, LANES).astype(F32),
        gate_up=_pad_to(gate_up[i], 0, 2 * LANES).astype(F32),
        k_k=_row(k_k[i]), k_a=_row(k_a[i]), r_k=_row(r_k[i]),
        gn_g=_row(gn_g[i]), gn_b=_row(gn_b[i]),
        w_a=w_branch_attn[i].astype(BF16), w_b=w_branch_rwkv[i].astype(BF16), w_o=w_out[i].astype(BF16),
        w_route=_pad_to(jnp.concatenate([router_exp[i], router_grp[i]], axis=1), 1, LANES).astype(F32),
        b_route=_row(_pad_to(jnp.concatenate([router_exp_bias[i], router_grp_bias[i]]), 0, LANES)),
        e_gate=exp_gate[i].astype(BF16), e_up=exp_up[i].astype(BF16), e_down=exp_down[i].astype(BF16),
        w_ple=ple_proj[i].astype(BF16), w_pg=ple_gate[i].astype(BF16),
        ln_g0=_row(ln_g[i, 0]), ln_b0=_row(ln_b[i, 0]),
        ln_g1=_row(ln_g[i, 1]), ln_b1=_row(ln_b[i, 1]),
        ln_g2=_row(ln_g[i, 2]), ln_b2=_row(ln_b[i, 2]),
    )
    if i > 0:
        lp.update(vres_base=_row(vres_base[i - 1]),
                  vres_down=_pad_to(vres_down[i - 1], 1, LANES).astype(F32),
                  vres_up=_pad_to(vres_up[i - 1], 0, LANES).astype(F32))
    return lp


def _const_mats(tm):
    head = jnp.arange(D_HEADS) // HEAD_DIM
    e_ones = (head[:, None] == head[None, :]).astype(BF16)
    tok = jnp.arange(tm)
    same = (tok[:, None] // CHUNK) == (tok[None, :] // CHUNK)
    tri = (same & (tok[:, None] >= tok[None, :])).astype(BF16)
    blk = same.astype(BF16)
    rtok = jnp.arange(RANK_BLK)
    tri_rank = (rtok[None, :] < rtok[:, None]).astype(BF16)
    return dict(e_ones=e_ones, tri=tri, blk=blk, tri_rank=tri_rank)


def kernel(x, p, ln_in_g, ln_in_b, rel_bias, w_in, tok_mix, decay_base, decay_up, aaa_base, aaa_up,
           gate_up, k_k, k_a, r_k, vres_base, vres_down, vres_up, gn_g, gn_b, w_branch_attn,
           w_branch_rwkv, w_out, router_grp, router_grp_bias, router_exp, router_exp_bias, exp_gate,
           exp_up, exp_down, ple_proj, ple_gate, ln_g, ln_b):
    batch, seq, _ = x.shape
    t = batch * seq
    consts = _const_mats(256)
    bias_tabs = _attn_bias_tables(rel_bias)
    xt = x.reshape(t, D_MODEL)
    pt = p.reshape(p.shape[0], t, p.shape[-1])
    v_first = None
    for i in range(DEPTH):
        lp = _layer_params(i, w_in, tok_mix, decay_base, decay_up, aaa_base, aaa_up, gate_up, k_k, k_a,
                           r_k, vres_base, vres_down, vres_up, gn_g, gn_b, w_branch_attn, w_branch_rwkv,
                           w_out, router_grp, router_grp_bias, router_exp, router_exp_bias, exp_gate,
                           exp_up, exp_down, ple_proj, ple_gate, ln_g, ln_b, consts)
        if i == 0:
            xt, qkv, rw, gate = _proj(xt, _row(ln_in_g), _row(ln_in_b), lp["wqkv"], lp["wrw"],
                                      lp["wgate"], apply_ln=True)
        else:
            qkv, rw, gate = _proj(xt, _row(ln_in_g), _row(ln_in_b), lp["wqkv"], lp["wrw"],
                                  lp["wgate"], apply_ln=False)
        y_attn = _attention(qkv, bias_tabs, batch, seq)
        if i == 0:
            q1, y0, m, n0, g, bonus, v_first = _rwkv_chunk(rw, lp, seq, None)
        else:
            q1, y0, m, n0, g, bonus = _rwkv_chunk(rw, lp, seq, v_first)
        y = _rwkv_scan(q1, y0, m, n0, batch, seq)
        xt = _mix(y, g, bonus, y_attn, gate, xt, lp)
        xt = _moe_sparse(xt, lp)
        xt = _ple(xt, pt[i], lp)
    return xt.reshape(batch, seq, D_MODEL)
```

```python
import functools
import math

import jax
import jax.numpy as jnp
from jax import lax
from jax.experimental import pallas as pl
from jax.experimental.pallas import tpu as pltpu

F32 = jnp.float32
BF16 = jnp.bfloat16

D_MODEL = 1024
CHUNK = 64
N_PREV_CHUNKS = 8
HEAD_DIM = 64
N_HEADS = 8
D_HEADS = N_HEADS * HEAD_DIM
REL_CLIP = 128
DECAY_LORA = 64
AAA_LORA = 64
GATE_LORA = 160
MV_LORA = 32
N_GROUPS = 4
EXPERTS_PER_GROUP = 8
N_EXPERTS = N_GROUPS * EXPERTS_PER_GROUP
D_EXPERT = 256
DEPTH = 2
DEEPNORM_ALPHA = (2 * DEPTH) ** 0.25
LN_EPS = 1e-5
GN_EPS = 64e-5
NEG_INF = -1e30

LANES = 128
PAIR = 2 * HEAD_DIM
ATTN_QBLK = 2 * CHUNK
ATTN_WIN = (N_PREV_CHUNKS + 2) * CHUNK
RW_COLS_PAD = 2048
VMEM_LIMIT = 56 * 1024 * 1024


def _cparams(*sem):
    return pltpu.CompilerParams(dimension_semantics=sem, vmem_limit_bytes=VMEM_LIMIT)


def _dot(a, b):
    return jnp.dot(a, b, preferred_element_type=F32)


def _dot_nt(a, b):
    return lax.dot_general(a, b, (((1,), (1,)), ((), ())), preferred_element_type=F32)


def _dot_tn(a, b):
    return lax.dot_general(a, b, (((0,), (0,)), ((), ())), preferred_element_type=F32)


def _split2(x):
    hi = x.astype(BF16)
    lo = (x - hi.astype(F32)).astype(BF16)
    return hi, lo


def _dot3(a, b):
    ah, al = _split2(a)
    bh, bl = _split2(b)
    return _dot(ah, bh) + (_dot(al, bh) + _dot(ah, bl))


def _dot_exact_lhs(a_bf16, x):
    x1 = x.astype(BF16)
    r1 = x - x1.astype(F32)
    x2 = r1.astype(BF16)
    x3 = (r1 - x2.astype(F32)).astype(BF16)
    return _dot(a_bf16, x1) + (_dot(a_bf16, x2) + _dot(a_bf16, x3))


def _seg_sum(x, e_ones):
    hi, lo = _split2(x)
    return _dot(hi, e_ones) + _dot(lo, e_ones)


def _layer_norm(x, g, b):
    mu = jnp.mean(x, axis=-1, keepdims=True)
    xc = x - mu
    var = jnp.mean(xc * xc, axis=-1, keepdims=True)
    return xc * lax.rsqrt(var + LN_EPS) * g + b


def _sigmoid(x):
    return 1.0 / (1.0 + jnp.exp(-x))


def _full(shape):
    nd = len(shape)
    return pl.BlockSpec(shape, lambda *_: (0,) * nd)


def _proj_kernel(x_ref, g_ref, b_ref, wqkv_ref, wrw_ref, wgate_ref, *out_refs, apply_ln):
    x = x_ref[...]
    if apply_ln:
        xn_ref, qkv_ref, rw_ref, gate_ref = out_refs
        x = _layer_norm(x, g_ref[...], b_ref[...])
        xn_ref[...] = x
    else:
        qkv_ref, rw_ref, gate_ref = out_refs
    xb = x.astype(BF16)
    qkv_ref[...] = _dot(xb, wqkv_ref[...]).astype(BF16)
    rw_ref[...] = _dot(xb, wrw_ref[...])
    gate_ref[...] = _sigmoid(_dot(xb, wgate_ref[...])).astype(BF16)


def _proj(x, g, b, wqkv, wrw, wgate, apply_ln):
    t = x.shape[0]
    tm = min(256, t)
    row = lambda w: pl.BlockSpec((tm, w), lambda i: (i, 0))
    out_shape = [jax.ShapeDtypeStruct((t, 3 * D_HEADS), BF16),
                 jax.ShapeDtypeStruct((t, RW_COLS_PAD), F32),
                 jax.ShapeDtypeStruct((t, 2 * D_MODEL), BF16)]
    out_specs = [row(3 * D_HEADS), row(RW_COLS_PAD), row(2 * D_MODEL)]
    if apply_ln:
        out_shape = [jax.ShapeDtypeStruct((t, D_MODEL), F32)] + out_shape
        out_specs = [row(D_MODEL)] + out_specs
    return pl.pallas_call(
        functools.partial(_proj_kernel, apply_ln=apply_ln),
        grid=(t // tm,),
        in_specs=[row(D_MODEL), _full(g.shape), _full(b.shape),
                  _full(wqkv.shape), _full(wrw.shape), _full(wgate.shape)],
        out_specs=out_specs,
        out_shape=out_shape,
        compiler_params=_cparams("parallel"),
        name="proj_ln" if apply_ln else "proj",
    )(x, g, b, wqkv, wrw, wgate)


def _attn_kernel(q_ref, k_ref, v_ref, bias_ref, o_ref):
    qb = pl.program_id(1)
    ws = pl.multiple_of(jnp.maximum(qb * ATTN_QBLK - N_PREV_CHUNKS * CHUNK, 0), ATTN_QBLK)
    q = q_ref[0]
    kw = k_ref[0, pl.ds(ws, ATTN_WIN), :]
    vw = v_ref[0, pl.ds(ws, ATTN_WIN), :]
    left = lax.broadcasted_iota(jnp.int32, (ATTN_QBLK, PAIR), 1) < HEAD_DIM
    zero = jnp.zeros((ATTN_QBLK, PAIR), BF16)
    heads = [(h // 2, h % 2) for h in range(N_HEADS)]
    spans = [slice(p * PAIR, (p + 1) * PAIR) for p in range(N_HEADS // 2)]
    qm = [jnp.where(left, q[:, spans[p]], zero) if side == 0 else jnp.where(left, zero, q[:, spans[p]])
          for p, side in heads]
    s = [_dot_nt(qm[h], kw[:, spans[p]]) + bias_ref[0, h] for h, (p, _) in enumerate(heads)]
    m = [jnp.max(x, axis=-1, keepdims=True) for x in s]
    e = [jnp.exp(x - mx) for x, mx in zip(s, m)]
    inv = [1.0 / jnp.sum(x, axis=-1, keepdims=True) for x in e]
    o = [_dot(x.astype(BF16), vw[:, spans[p]]) * r for x, r, (p, _) in zip(e, inv, heads)]
    pairs = [jnp.where(left, o[2 * p], o[2 * p + 1]) for p in range(N_HEADS // 2)]
    o_ref[0] = jnp.concatenate(pairs, axis=-1).astype(BF16)


def _attention(qkv, bias_tabs, batch, seq):
    qkv3 = qkv.reshape(batch, seq, 3 * D_HEADS)
    n_tabs = bias_tabs.shape[0]
    out = pl.pallas_call(
        _attn_kernel,
        grid=(batch, seq // ATTN_QBLK),
        in_specs=[
            pl.BlockSpec((1, ATTN_QBLK, D_HEADS), lambda b, i: (b, i, 0)),
            pl.BlockSpec((1, seq, D_HEADS), lambda b, i: (b, 0, 1)),
            pl.BlockSpec((1, seq, D_HEADS), lambda b, i: (b, 0, 2)),
            pl.BlockSpec((1, N_HEADS, ATTN_QBLK, ATTN_WIN),
                         lambda b, i: (jnp.minimum(i, n_tabs - 1), 0, 0, 0)),
        ],
        out_specs=pl.BlockSpec((1, ATTN_QBLK, D_HEADS), lambda b, i: (b, i, 0)),
        out_shape=jax.ShapeDtypeStruct((batch, seq, D_HEADS), BF16),
        compiler_params=_cparams("parallel", "arbitrary"),
        name="band_attn",
    )(qkv3, qkv3, qkv3, bias_tabs)
    return out.reshape(batch * seq, D_HEADS)


def _attn_bias_tables(rel_bias):
    n_tabs = N_PREV_CHUNKS * CHUNK // ATTN_QBLK + 1
    start = (jnp.arange(n_tabs) * ATTN_QBLK)[:, None, None]
    qpos = start + jnp.arange(ATTN_QBLK)[None, :, None]
    kpos = jnp.arange(ATTN_WIN)[None, None, :]
    qc, kc = qpos // CHUNK, kpos // CHUNK
    valid = (kc <= qc) & (kc >= qc - N_PREV_CHUNKS)
    n = jnp.arange(ATTN_QBLK - 1 + ATTN_WIN)
    dist = start[:, :, 0] + (ATTN_QBLK - 1) - n[None, :]
    vals = rel_bias.astype(F32)[:, jnp.clip(dist, -REL_CLIP, REL_CLIP) + REL_CLIP]
    rows = [vals[:, :, ATTN_QBLK - 1 - i:ATTN_QBLK - 1 - i + ATTN_WIN] for i in range(ATTN_QBLK)]
    bias = jnp.stack(rows, axis=2)
    bias = jnp.where(valid[None], bias, NEG_INF)
    return jnp.transpose(bias, (1, 0, 2, 3))


def _rwkv_chunk_kernel(*refs, seq, tm, has_vres):
    if has_vres:
        (rw_ref, prev_ref, mu_ref, dbase_ref, dup_ref, abase_ref, aup_ref, gup_ref, kk_ref, ka_ref,
         rk_ref, eones_ref, tri_ref, blk_ref, vfirst_ref, vbase_ref, vdown_ref, vup_ref,
         q1_ref, y0_ref, m_ref, n0_ref, g_ref, bonus_ref,
         rw_s, aw_s, bi_s, ki_s, be_s, ke_s, v_s, wc_s) = refs
    else:
        (rw_ref, prev_ref, mu_ref, dbase_ref, dup_ref, abase_ref, aup_ref, gup_ref, kk_ref, ka_ref,
         rk_ref, eones_ref, tri_ref, blk_ref,
         q1_ref, y0_ref, m_ref, n0_ref, g_ref, bonus_ref, vfirst_out_ref,
         rw_s, aw_s, bi_s, ki_s, be_s, ke_s, v_s, wc_s) = refs

    cols = rw_ref[...]
    first = (pl.program_id(0) % (seq // tm)) == 0
    prev_row = jnp.where(first, 0.0, prev_ref[7:8, :])
    rows = lax.broadcasted_iota(jnp.int32, cols.shape, 0)
    prev = jnp.where(rows == 0, prev_row, pltpu.roll(cols, 1, axis=0))
    xs = cols + (prev - cols) * mu_ref[...]

    d = D_HEADS
    r, k, v = xs[:, 0:d], xs[:, d:2 * d], xs[:, 2 * d:3 * d]
    wd = xs[:, 3 * d:3 * d + LANES]
    ad = xs[:, 3 * d + LANES:3 * d + 2 * LANES]
    gd = xs[:, 3 * d + 2 * LANES:3 * d + 4 * LANES]
    e_ones = eones_ref[...]

    u = dbase_ref[...] + _dot3(jnp.tanh(wd), dup_ref[...])
    w_log = jnp.minimum(u, 0.0) - jnp.log(1.0 + jnp.exp(-jnp.abs(u))) - 0.5
    logw = -jnp.exp(w_log)
    a_icl = _sigmoid(abase_ref[...] + _dot3(ad, aup_ref[...]))
    g_ref[...] = _dot3(_sigmoid(gd), gup_ref[...])
    kk = k * kk_ref[...]
    kk = kk * lax.rsqrt(jnp.maximum(_seg_sum(kk * kk, e_ones), 1e-24))
    k2 = k * (1.0 + (a_icl - 1.0) * ka_ref[...])
    if has_vres:
        mix = _sigmoid(vbase_ref[...] + _dot3(_dot3(v, vdown_ref[...]), vup_ref[...]))
        v2 = v + (vfirst_ref[...] - v) * mix
    else:
        v2 = v
        vfirst_out_ref[...] = v
    a_vec = -kk
    b_vec = kk * a_icl
    bonus_ref[...] = _seg_sum(r * k2 * rk_ref[...], e_ones) * v2

    cw = _dot_exact_lhs(tri_ref[...], logw)
    cw_tot = _dot_exact_lhs(blk_ref[...], logw)
    rw_s[...] = r * jnp.exp(cw)
    aw_s[...] = a_vec * jnp.exp(cw - logw)
    inv = jnp.exp(-cw)
    bi_s[...] = b_vec * inv
    ki_s[...] = k2 * inv
    rest = jnp.exp(cw_tot - cw)
    be_s[...] = b_vec * rest
    ke_s[...] = k2 * rest
    v_s[...] = v2
    wc_s[...] = jnp.exp(cw_tot)

    shape = (CHUNK, PAIR)
    ti = lax.broadcasted_iota(jnp.int32, shape, 0)
    li = lax.broadcasted_iota(jnp.int32, shape, 1)
    si = li & (HEAD_DIM - 1)
    left = li < HEAD_DIM
    strict = ti > si
    incl = ti >= si
    eye = ti == si

    def bd(x):
        z = jnp.zeros_like(x)
        return jnp.concatenate([jnp.where(left, x, z), jnp.where(left, z, x)], axis=0)

    def fold(x):
        return jnp.where(left, x[:CHUNK], x[CHUNK:])

    def b16(xs):
        return [x.astype(BF16) for x in xs]

    def summaries(insts):
        sl = [(slice(c * CHUNK, (c + 1) * CHUNK), slice(p * PAIR, (p + 1) * PAIR)) for c, p in insts]
        rw = [rw_s[s] for s in sl]
        awb = b16([aw_s[s] for s in sl])
        vb = b16([v_s[s] for s in sl])
        lhs = [jnp.concatenate([a, r.astype(BF16)], axis=0) for a, r in zip(awb, rw)]
        gb = [_dot_nt(l, bd(x)) for l, x in zip(lhs, b16([bi_s[s] for s in sl]))]
        gk = [_dot_nt(l, bd(x)) for l, x in zip(lhs, b16([ki_s[s] for s in sl]))]
        a_ab = [jnp.where(strict, g[:CHUNK], 0.0) for g in gb]
        a_rb = b16([jnp.where(incl, g[CHUNK:], 0.0) for g in gb])
        a_ak = b16([jnp.where(strict, g[:CHUNK], 0.0) for g in gk])
        a_rk = b16([jnp.where(incl, g[CHUNK:], 0.0) for g in gk])
        bdv = [bd(x) for x in vb]
        x0 = [_dot(a, v) for a, v in zip(a_ak, bdv)]
        lp = a_ab
        tinv = [jnp.where(eye, 1.0, a) for a in a_ab]
        for _ in range(5):
            lpb = b16(lp)
            lp = [_dot(l, bd(l)) for l in lpb]
            tinv = [t + _dot(t.astype(BF16), bd(l.astype(BF16))) for t, l in zip(tinv, lp)]
        tb = b16(tinv)
        p1b = b16([_dot(t, bd(a)) for t, a in zip(tb, awb)])
        u0b = b16([_dot(t, bd(x.astype(BF16))) for t, x in zip(tb, x0)])
        q1 = [r + _dot(a, bd(p)) for r, a, p in zip(rw, a_rb, p1b)]
        y0 = [_dot(a, bd(u)) + _dot(k, v) for a, u, k, v in zip(a_rb, u0b, a_rk, bdv)]
        beb = b16([be_s[s] for s in sl])
        keb = b16([ke_s[s] for s in sl])
        mm = [fold(_dot_tn(b, p)) for b, p in zip(beb, p1b)]
        nn = [fold(_dot_tn(b, u) + _dot_tn(k, v)) for b, u, k, v in zip(beb, u0b, keb, vb)]
        for i, (s, (c, p)) in enumerate(zip(sl, insts)):
            wc = wc_s[c * CHUNK:c * CHUNK + 1, s[1]]
            q1_ref[s] = q1[i]
            y0_ref[s] = y0[i]
            m_ref[s] = jnp.where(eye, wc, 0.0) + mm[i]
            n0_ref[s] = nn[i]

    n_pairs = D_HEADS // PAIR
    summaries([(c, p) for c in range(tm // CHUNK) for p in range(n_pairs)])


def _rwkv_chunk(rw, lp, seq, v_first):
    t = rw.shape[0]
    tm = 256
    has_vres = v_first is not None
    nct = tm // CHUNK
    row = lambda w: pl.BlockSpec((tm, w), lambda i: (i, 0))
    prev_spec = pl.BlockSpec((8, RW_COLS_PAD), lambda i: (jnp.maximum(i * (tm // 8) - 1, 0), 0))
    params = [lp["mu"], lp["decay_base"], lp["decay_up"], lp["aaa_base"], lp["aaa_up"], lp["gate_up"],
              lp["k_k"], lp["k_a"], lp["r_k"], lp["e_ones"], lp["tri"], lp["blk"]]
    inputs = [rw, rw] + params
    in_specs = [row(RW_COLS_PAD), prev_spec] + [_full(p.shape) for p in params]
    if has_vres:
        extra = [lp["vres_base"], lp["vres_down"], lp["vres_up"]]
        inputs += [v_first] + extra
        in_specs += [row(D_HEADS)] + [_full(p.shape) for p in extra]
    tok_shape = jax.ShapeDtypeStruct((t, D_HEADS), F32)
    out_shape = [tok_shape] * 6
    out_specs = [row(D_HEADS)] * 6
    if not has_vres:
        out_shape.append(tok_shape)
        out_specs.append(row(D_HEADS))
    scratch = [pltpu.VMEM((tm, D_HEADS), F32) for _ in range(8)]
    return pl.pallas_call(
        functools.partial(_rwkv_chunk_kernel, seq=seq, tm=tm, has_vres=has_vres),
        grid=(t // tm,),
        in_specs=in_specs,
        out_specs=out_specs,
        out_shape=out_shape,
        scratch_shapes=scratch,
        compiler_params=_cparams("parallel"),
        name="rwkv_chunk_vres" if has_vres else "rwkv_chunk",
    )(*inputs)


def _rwkv_scan_kernel(q1_ref, y0_ref, m_ref, n0_ref, y_ref, st_ref, *, n_chunks):
    @pl.when(pl.program_id(1) == 0)
    def _():
        st_ref[...] = jnp.zeros_like(st_ref)

    left = lax.broadcasted_iota(jnp.int32, (CHUNK, PAIR), 1) < HEAD_DIM

    def bd(x):
        z = jnp.zeros_like(x)
        return jnp.concatenate([jnp.where(left, x, z), jnp.where(left, z, x)], axis=0)

    n_pairs = D_HEADS // PAIR
    st = [st_ref[:, p * PAIR:(p + 1) * PAIR] for p in range(n_pairs)]
    for c in range(n_chunks):
        rs = slice(c * CHUNK, (c + 1) * CHUNK)
        for p in range(n_pairs):
            ls = slice(p * PAIR, (p + 1) * PAIR)
            lh, ll = _split2(jnp.concatenate([q1_ref[rs, ls], m_ref[rs, ls]], axis=0))
            sh, sl = _split2(st[p])
            bh = bd(sh)
            res = _dot(lh, bh) + (_dot(ll, bh) + _dot(lh, bd(sl)))
            y_ref[rs, ls] = res[:CHUNK] + y0_ref[rs, ls]
            st[p] = res[CHUNK:] + n0_ref[rs, ls]
    for p in range(n_pairs):
        st_ref[:, p * PAIR:(p + 1) * PAIR] = st[p]


def _rwkv_scan(q1, y0, m, n0, batch, seq):
    t = q1.shape[0]
    cb = 8
    steps = seq // (cb * CHUNK)
    tok_spec = pl.BlockSpec((cb * CHUNK, D_HEADS), lambda b, j: (b * steps + j, 0))
    return pl.pallas_call(
        functools.partial(_rwkv_scan_kernel, n_chunks=cb),
        grid=(batch, steps),
        in_specs=[tok_spec] * 4,
        out_specs=tok_spec,
        out_shape=jax.ShapeDtypeStruct((t, D_HEADS), F32),
        scratch_shapes=[pltpu.VMEM((HEAD_DIM, D_HEADS), F32)],
        compiler_params=_cparams("parallel", "arbitrary"),
        name="rwkv_scan",
    )(q1, y0, m, n0)


def _mix_kernel(y_ref, g_ref, bonus_ref, ya_ref, gate_ref, x_ref, wa_ref, wb_ref, wo_ref,
                eones_ref, gng_ref, gnb_ref, lng_ref, lnb_ref, o_ref):
    e_ones = eones_ref[...]
    y = y_ref[...]
    mean = _seg_sum(y, e_ones) * (1.0 / HEAD_DIM)
    yc = y - mean
    var = _seg_sum(yc * yc, e_ones) * (1.0 / HEAD_DIM)
    yn = yc * lax.rsqrt(var + GN_EPS) * gng_ref[...] + gnb_ref[...]
    yr = (yn + bonus_ref[...]) * g_ref[...]
    za = _dot(ya_ref[...], wa_ref[...])
    zb = _dot(yr.astype(BF16), wb_ref[...])
    gate = gate_ref[...]
    mixed_in = gate[:, :D_MODEL].astype(F32) * za + gate[:, D_MODEL:].astype(F32) * zb
    mixed = _dot(mixed_in.astype(BF16), wo_ref[...])
    o_ref[...] = _layer_norm(DEEPNORM_ALPHA * x_ref[...] + mixed, lng_ref[...], lnb_ref[...])


def _mix(y, g, bonus, y_attn, gate, x, lp):
    t = x.shape[0]
    tm = min(256, t)
    row = lambda w: pl.BlockSpec((tm, w), lambda i: (i, 0))
    params = [lp["w_a"], lp["w_b"], lp["w_o"], lp["e_ones"], lp["gn_g"], lp["gn_b"], lp["ln_g0"], lp["ln_b0"]]
    return pl.pallas_call(
        _mix_kernel,
        grid=(t // tm,),
        in_specs=[row(D_HEADS), row(D_HEADS), row(D_HEADS), row(D_HEADS), row(2 * D_MODEL), row(D_MODEL)]
        + [_full(p.shape) for p in params],
        out_specs=row(D_MODEL),
        out_shape=jax.ShapeDtypeStruct((t, D_MODEL), F32),
        compiler_params=_cparams("parallel"),
        name="branch_mix",
    )(y, g, bonus, y_attn, gate, x, *params)


MOE_SLAB = 2048
MOE_WIN = 128
MOE_ROWS = 2 * MOE_SLAB + MOE_WIN
SUBLANES = 8
RANK_BLK = 256


def _route_kernel(x_ref, wr_ref, br_ref, tri_ref, w_ref, pos_ref, seg_ref):
    x = x_ref[...]
    tn = x.shape[0]
    lane = lax.broadcasted_iota(jnp.int32, (tn, LANES), 1)
    lanef = lane.astype(F32)
    hp = lax.Precision.HIGHEST
    elog = _dot3(x, wr_ref[...]) + br_ref[...]
    big = float(LANES)
    glog = jnp.where((lane >= N_EXPERTS) & (lane < N_EXPERTS + N_GROUPS), elog, -jnp.inf)
    gmax = jnp.max(glog, axis=-1, keepdims=True)
    g_gate = 1.0 / jnp.sum(jnp.exp(glog - gmax), axis=-1, keepdims=True)
    grp = jnp.min(jnp.where(glog == gmax, lanef, big), axis=-1, keepdims=True) - N_EXPERTS
    in_grp = (lanef >= grp * EXPERTS_PER_GROUP) & (lanef < (grp + 1.0) * EXPERTS_PER_GROUP)
    sel = jnp.where(in_grp, elog, -jnp.inf)
    v1 = jnp.max(sel, axis=-1, keepdims=True)
    i1 = jnp.min(jnp.where(sel == v1, lanef, big), axis=-1, keepdims=True)
    sel2 = jnp.where(lanef == i1, -jnp.inf, sel)
    v2 = jnp.max(sel2, axis=-1, keepdims=True)
    i2 = jnp.min(jnp.where(sel2 == v2, lanef, big), axis=-1, keepdims=True)
    e2 = jnp.exp(v2 - v1)
    w1 = g_gate / (1.0 + e2)
    w2 = g_gate * e2 / (1.0 + e2)
    w_ref[...] = jnp.where(lane == 0, w1, jnp.where(lane == 1, w2, 0.0))

    hit1 = lanef == i1
    hit2 = lanef == i2
    onehot = jnp.where(hit1 | hit2, 1.0, 0.0)
    tri = tri_ref[...]
    carry = jnp.zeros((1, LANES), F32)
    ranks = []
    for j in range(tn // RANK_BLK):
        blk = onehot[j * RANK_BLK:(j + 1) * RANK_BLK]
        ranks.append(_dot(tri, blk.astype(BF16)) + carry)
        carry = carry + jnp.sum(blk, axis=0, keepdims=True)
    rank = jnp.concatenate(ranks, axis=0)
    cnt = carry
    li = lax.broadcasted_iota(jnp.int32, (LANES, LANES), 0)
    lj = lax.broadcasted_iota(jnp.int32, (LANES, LANES), 1)
    before = jnp.where(li < lj, 1.0, 0.0)
    off = jnp.dot(jnp.broadcast_to(cnt, (8, LANES)), before, precision=hp,
                  preferred_element_type=F32)[0:1]
    slot = off + rank
    pos1 = jnp.sum(jnp.where(hit1, slot, 0.0), axis=-1, keepdims=True)
    pos2 = jnp.sum(jnp.where(hit2, slot, 0.0), axis=-1, keepdims=True)
    pos_ref[...] = jnp.where(lane == 0, pos1, jnp.where(lane == 1, pos2, 0.0)).astype(jnp.int32)
    row8 = lax.broadcasted_iota(jnp.int32, (8, LANES), 0)
    seg_ref[0] = jnp.where(row8 == 0, off, jnp.where(row8 == 1, cnt, 0.0)).astype(jnp.int32)


def _route(x, lp):
    t = x.shape[0]
    tn = min(MOE_SLAB, t)
    n_slabs = t // tn
    small = [lp["w_route"], lp["b_route"], lp["tri_rank"]]
    row = lambda w: pl.BlockSpec((tn, w), lambda i: (i, 0))
    return pl.pallas_call(
        _route_kernel,
        grid=(n_slabs,),
        in_specs=[row(D_MODEL)] + [_full(p.shape) for p in small],
        out_specs=[row(LANES), row(LANES), pl.BlockSpec((1, 8, LANES), lambda i: (i, 0, 0))],
        out_shape=[jax.ShapeDtypeStruct((t, LANES), F32), jax.ShapeDtypeStruct((t, LANES), jnp.int32),
                   jax.ShapeDtypeStruct((n_slabs, 8, LANES), jnp.int32)],
        compiler_params=_cparams("parallel"),
        name="moe_route",
    )(x, *small)


def _moe_sparse_kernel(off_ref, cnt_ref, pos_ref, w_ref, x_ref, eg_ref, eu_ref, ed_ref, lng_ref, lnb_ref,
                       o_ref, xs_ref):
    s = pl.program_id(0)
    e = pl.program_id(1)
    tn = x_ref.shape[0]

    n_feat = D_MODEL // LANES

    def tile_rows(r):
        return pl.ds(pl.multiple_of(r * SUBLANES, SUBLANES), SUBLANES)

    @pl.when((s == 0) & (e == 0))
    def _():
        xs_ref[pl.ds(2 * tn * SUBLANES, MOE_WIN * SUBLANES), :] = jnp.zeros((MOE_WIN * SUBLANES, LANES), F32)

    @pl.when(e == 0)
    def _():
        def scatter(i, carry):
            t0 = pl.multiple_of(i * SUBLANES, SUBLANES)
            tiles = x_ref[pl.ds(t0, SUBLANES), :].reshape(SUBLANES, n_feat, LANES)
            for j in range(SUBLANES):
                xs_ref[tile_rows(pos_ref[0, 0, 2 * (t0 + j)]), :] = tiles[j]
                xs_ref[tile_rows(pos_ref[0, 0, 2 * (t0 + j) + 1]), :] = tiles[j]
            return carry
        lax.fori_loop(0, tn // SUBLANES, scatter, 0)

    off = off_ref[s * N_EXPERTS + e]
    cnt = cnt_ref[s * N_EXPERTS + e]
    ridx = lax.broadcasted_iota(jnp.int32, (MOE_WIN, 1), 0)

    def window(w, carry):
        r0 = off + w * MOE_WIN
        feat = [pl.ds(r0 * SUBLANES + f, MOE_WIN, stride=SUBLANES) for f in range(n_feat)]
        xw = [xs_ref[fs, :] for fs in feat]
        xb = jnp.concatenate(xw, axis=1).astype(BF16)
        hg = _dot(xb, eg_ref[0])
        hu = _dot(xb, eu_ref[0])
        hh = hg * _sigmoid(hg) * hu
        yw = _dot(hh.astype(BF16), ed_ref[0])
        valid = ridx + w * MOE_WIN < cnt
        for f in range(n_feat):
            xs_ref[feat[f], :] = jnp.where(valid, yw[:, f * LANES:(f + 1) * LANES], xw[f])
        return carry

    lax.fori_loop(0, (cnt + MOE_WIN - 1) // MOE_WIN, window, 0)

    @pl.when(e == N_EXPERTS - 1)
    def _():
        def combine(i, carry):
            t0 = pl.multiple_of(i * SUBLANES, SUBLANES)
            tiles = []
            for j in range(SUBLANES):
                y1 = xs_ref[tile_rows(pos_ref[0, 0, 2 * (t0 + j)]), :]
                y2 = xs_ref[tile_rows(pos_ref[0, 0, 2 * (t0 + j) + 1]), :]
                tiles.append(w_ref[0, 0, 2 * (t0 + j)] * y1 + w_ref[0, 0, 2 * (t0 + j) + 1] * y2)
            o_ref[pl.ds(t0, SUBLANES), :] = jnp.stack(tiles, axis=0).reshape(SUBLANES, D_MODEL)
            return carry
        lax.fori_loop(0, tn // SUBLANES, combine, 0)
        for j in range(tn // RANK_BLK):
            rs = slice(j * RANK_BLK, (j + 1) * RANK_BLK)
            o_ref[rs, :] = _layer_norm(DEEPNORM_ALPHA * x_ref[rs, :] + o_ref[rs, :], lng_ref[...], lnb_ref[...])


def _moe_sparse(x, lp):
    t = x.shape[0]
    tn = min(MOE_SLAB, t)
    n_slabs = t // tn
    w, pos, seg = _route(x, lp)
    w_flat = w[:, :2].reshape(n_slabs, 1, 2 * tn)
    pos_flat = pos[:, :2].reshape(n_slabs, 1, 2 * tn)
    off = seg[:, 0, :N_EXPERTS].reshape(-1)
    cnt = seg[:, 1, :N_EXPERTS].reshape(-1)
    smem_row = pl.BlockSpec((1, 1, 2 * tn), lambda s, e, *_: (s, 0, 0), memory_space=pltpu.SMEM)
    slab = pl.BlockSpec((tn, D_MODEL), lambda s, e, *_: (s, 0), pipeline_mode=pl.Buffered(1))
    const = lambda p: pl.BlockSpec(p.shape, lambda s, e, *_: (0,) * p.ndim)
    grid_spec = pltpu.PrefetchScalarGridSpec(
        num_scalar_prefetch=2,
        grid=(n_slabs, N_EXPERTS),
        in_specs=[smem_row, smem_row, slab,
                  pl.BlockSpec((1, D_MODEL, D_EXPERT), lambda s, e, *_: (e, 0, 0)),
                  pl.BlockSpec((1, D_MODEL, D_EXPERT), lambda s, e, *_: (e, 0, 0)),
                  pl.BlockSpec((1, D_EXPERT, D_MODEL), lambda s, e, *_: (e, 0, 0)),
                  const(lp["ln_g1"]), const(lp["ln_b1"])],
        out_specs=slab,
        scratch_shapes=[pltpu.VMEM((MOE_ROWS * SUBLANES, LANES), F32)],
    )
    return pl.pallas_call(
        _moe_sparse_kernel,
        grid_spec=grid_spec,
        out_shape=jax.ShapeDtypeStruct((t, D_MODEL), F32),
        compiler_params=_cparams("arbitrary", "arbitrary"),
        name="moe_sparse",
    )(off, cnt, pos_flat, w_flat, x, lp["e_gate"], lp["e_up"], lp["e_down"], lp["ln_g1"], lp["ln_b1"])


def _ple_kernel(x_ref, p_ref, wp_ref, wg_ref, lng_ref, lnb_ref, o_ref):
    x = x_ref[...]
    emb = _dot(p_ref[...].astype(BF16), wp_ref[...])
    gate = _sigmoid(_dot(x.astype(BF16), wg_ref[...]))
    o_ref[...] = _layer_norm(DEEPNORM_ALPHA * x + emb * gate, lng_ref[...], lnb_ref[...])


def _ple(x, p, lp):
    t = x.shape[0]
    tm = min(512, t)
    d_ple = p.shape[1]
    row = lambda w: pl.BlockSpec((tm, w), lambda i: (i, 0))
    params = [lp["w_ple"], lp["w_pg"], lp["ln_g2"], lp["ln_b2"]]
    return pl.pallas_call(
        _ple_kernel,
        grid=(t // tm,),
        in_specs=[row(D_MODEL), row(d_ple)] + [_full(q.shape) for q in params],
        out_specs=row(D_MODEL),
        out_shape=jax.ShapeDtypeStruct((t, D_MODEL), F32),
        compiler_params=_cparams("parallel"),
        name="ple",
    )(x, p, *params)


def _pad_to(a, axis, size):
    pad = [(0, 0)] * a.ndim
    pad[axis] = (0, size - a.shape[axis])
    return jnp.pad(a, pad)


def _row(v):
    return v.reshape(1, -1).astype(F32)


def _layer_params(i, w_in, tok_mix, decay_base, decay_up, aaa_base, aaa_up, gate_up, k_k, k_a, r_k,
                  vres_base, vres_down, vres_up, gn_g, gn_b, w_branch_attn, w_branch_rwkv, w_out,
                  router_grp, router_grp_bias, router_exp, router_exp_bias, exp_gate, exp_up, exp_down,
                  ple_proj, ple_gate, ln_g, ln_b, consts):
    d = D_HEADS
    w = w_in[i]
    rw0 = 3 * d
    gate0 = rw0 + 3 * d + DECAY_LORA + AAA_LORA + GATE_LORA
    wqkv = jnp.concatenate([w[:, 0:d] * (HEAD_DIM ** -0.5), w[:, d:rw0]], axis=1).astype(BF16)
    wr = w[:, rw0:gate0]
    lora0 = 3 * d
    pieces = [wr[:, :lora0],
              _pad_to(wr[:, lora0:lora0 + DECAY_LORA], 1, LANES),
              _pad_to(wr[:, lora0 + DECAY_LORA:lora0 + DECAY_LORA + AAA_LORA], 1, LANES),
              _pad_to(wr[:, lora0 + DECAY_LORA + AAA_LORA:], 1, 2 * LANES)]
    wrw = jnp.concatenate(pieces, axis=1).astype(BF16)
    mu = tok_mix[i]
    mu_pieces = [mu[:lora0],
                 _pad_to(mu[lora0:lora0 + DECAY_LORA], 0, LANES),
                 _pad_to(mu[lora0 + DECAY_LORA:lora0 + DECAY_LORA + AAA_LORA], 0, LANES),
                 _pad_to(mu[lora0 + DECAY_LORA + AAA_LORA:], 0, 2 * LANES)]
    lp = dict(consts)
    lp.update(
        wqkv=wqkv, wrw=wrw, wgate=w[:, gate0:].astype(BF16),
        mu=_row(jnp.concatenate(mu_pieces)),
        decay_base=_row(decay_base[i]), decay_up=_pad_to(decay_up[i], 0, LANES).astype(F32),
        aaa_base=_row(aaa_base[i]), aaa_up=_pad_to(aaa_up[i], 0, LANES).astype(F32),
        gate_up=_pad_to(gate_up[i], 0, 2 * LANES).astype(F32),
        k_k=_row(k_k[i]), k_a=_row(k_a[i]), r_k=_row(r_k[i]),
        gn_g=_row(gn_g[i]), gn_b=_row(gn_b[i]),
        w_a=w_branch_attn[i].astype(BF16), w_b=w_branch_rwkv[i].astype(BF16), w_o=w_out[i].astype(BF16),
        w_route=_pad_to(jnp.concatenate([router_exp[i], router_grp[i]], axis=1), 1, LANES).astype(F32),
        b_route=_row(_pad_to(jnp.concatenate([router_exp_bias[i], router_grp_bias[i]]), 0, LANES)),
        e_gate=exp_gate[i].astype(BF16), e_up=exp_up[i].astype(BF16), e_down=exp_down[i].astype(BF16),
        w_ple=ple_proj[i].astype(BF16), w_pg=ple_gate[i].astype(BF16),
        ln_g0=_row(ln_g[i, 0]), ln_b0=_row(ln_b[i, 0]),
        ln_g1=_row(ln_g[i, 1]), ln_b1=_row(ln_b[i, 1]),
        ln_g2=_row(ln_g[i, 2]), ln_b2=_row(ln_b[i, 2]),
    )
    if i > 0:
        lp.update(vres_base=_row(vres_base[i - 1]),
                  vres_down=_pad_to(vres_down[i - 1], 1, LANES).astype(F32),
                  vres_up=_pad_to(vres_up[i - 1], 0, LANES).astype(F32))
    return lp


def _const_mats(tm):
    head = jnp.arange(D_HEADS) // HEAD_DIM
    e_ones = (head[:, None] == head[None, :]).astype(BF16)
    tok = jnp.arange(tm)
    same = (tok[:, None] // CHUNK) == (tok[None, :] // CHUNK)
    tri = (same & (tok[:, None] >= tok[None, :])).astype(BF16)
    blk = same.astype(BF16)
    rtok = jnp.arange(RANK_BLK)
    tri_rank = (rtok[None, :] < rtok[:, None]).astype(BF16)
    return dict(e_ones=e_ones, tri=tri, blk=blk, tri_rank=tri_rank)


def kernel(x, p, ln_in_g, ln_in_b, rel_bias, w_in, tok_mix, decay_base, decay_up, aaa_base, aaa_up,
           gate_up, k_k, k_a, r_k, vres_base, vres_down, vres_up, gn_g, gn_b, w_branch_attn,
           w_branch_rwkv, w_out, router_grp, router_grp_bias, router_exp, router_exp_bias, exp_gate,
           exp_up, exp_down, ple_proj, ple_gate, ln_g, ln_b):
    batch, seq, _ = x.shape
    t = batch * seq
    consts = _const_mats(256)
    bias_tabs = _attn_bias_tables(rel_bias)
    xt = x.reshape(t, D_MODEL)
    pt = p.reshape(p.shape[0], t, p.shape[-1])
    v_first = None
    for i in range(DEPTH):
        lp = _layer_params(i, w_in, tok_mix, decay_base, decay_up, aaa_base, aaa_up, gate_up, k_k, k_a,
                           r_k, vres_base, vres_down, vres_up, gn_g, gn_b, w_branch_attn, w_branch_rwkv,
                           w_out, router_grp, router_grp_bias, router_exp, router_exp_bias, exp_gate,
                           exp_up, exp_down, ple_proj, ple_gate, ln_g, ln_b, consts)
        if i == 0:
            xt, qkv, rw, gate = _proj(xt, _row(ln_in_g), _row(ln_in_b), lp["wqkv"], lp["wrw"],
                                      lp["wgate"], apply_ln=True)
        else:
            qkv, rw, gate = _proj(xt, _row(ln_in_g), _row(ln_in_b), lp["wqkv"], lp["wrw"],
                                  lp["wgate"], apply_ln=False)
        y_attn = _attention(qkv, bias_tabs, batch, seq)
        if i == 0:
            q1, y0, m, n0, g, bonus, v_first = _rwkv_chunk(rw, lp, seq, None)
        else:
            q1, y0, m, n0, g, bonus = _rwkv_chunk(rw, lp, seq, v_first)
        y = _rwkv_scan(q1, y0, m, n0, batch, seq)
        xt = _mix(y, g, bonus, y_attn, gate, xt, lp)
        xt = _moe_sparse(xt, lp)
        xt = _ple(xt, pt[i], lp)
    return xt.reshape(batch, seq, D_MODEL)
```

```python
import functools
import math

import jax
import jax.numpy as jnp
from jax import lax
from jax.experimental import pallas as pl
from jax.experimental.pallas import tpu as pltpu

F32 = jnp.float32
BF16 = jnp.bfloat16

D_MODEL = 1024
CHUNK = 64
N_PREV_CHUNKS = 8
HEAD_DIM = 64
N_HEADS = 8
D_HEADS = N_HEADS * HEAD_DIM
REL_CLIP = 128
DECAY_LORA = 64
AAA_LORA = 64
GATE_LORA = 160
MV_LORA = 32
N_GROUPS = 4
EXPERTS_PER_GROUP = 8
N_EXPERTS = N_GROUPS * EXPERTS_PER_GROUP
D_EXPERT = 256
DEPTH = 2
DEEPNORM_ALPHA = (2 * DEPTH) ** 0.25
LN_EPS = 1e-5
GN_EPS = 64e-5
NEG_INF = -1e30

LANES = 128
PAIR = 2 * HEAD_DIM
ATTN_QBLK = 2 * CHUNK
ATTN_WIN = (N_PREV_CHUNKS + 2) * CHUNK
RW_COLS_PAD = 2048
VMEM_LIMIT = 56 * 1024 * 1024


def _cparams(*sem):
    return pltpu.CompilerParams(dimension_semantics=sem, vmem_limit_bytes=VMEM_LIMIT)


def _dot(a, b):
    return jnp.dot(a, b, preferred_element_type=F32)


def _dot_nt(a, b):
    return lax.dot_general(a, b, (((1,), (1,)), ((), ())), preferred_element_type=F32)


def _dot_tn(a, b):
    return lax.dot_general(a, b, (((0,), (0,)), ((), ())), preferred_element_type=F32)


def _split2(x):
    hi = x.astype(BF16)
    lo = (x - hi.astype(F32)).astype(BF16)
    return hi, lo


def _dot3(a, b):
    ah, al = _split2(a)
    bh, bl = _split2(b)
    return _dot(ah, bh) + (_dot(al, bh) + _dot(ah, bl))


def _dot_exact_lhs(a_bf16, x):
    x1 = x.astype(BF16)
    r1 = x - x1.astype(F32)
    x2 = r1.astype(BF16)
    x3 = (r1 - x2.astype(F32)).astype(BF16)
    return _dot(a_bf16, x1) + (_dot(a_bf16, x2) + _dot(a_bf16, x3))


def _seg_sum(x, e_ones):
    hi, lo = _split2(x)
    return _dot(hi, e_ones) + _dot(lo, e_ones)


def _layer_norm(x, g, b):
    mu = jnp.mean(x, axis=-1, keepdims=True)
    xc = x - mu
    var = jnp.mean(xc * xc, axis=-1, keepdims=True)
    return xc * lax.rsqrt(var + LN_EPS) * g + b


def _sigmoid(x):
    return 1.0 / (1.0 + jnp.exp(-x))


def _full(shape):
    nd = len(shape)
    return pl.BlockSpec(shape, lambda *_: (0,) * nd)


def _proj_kernel(x_ref, g_ref, b_ref, wqkv_ref, wrw_ref, wgate_ref, *out_refs, apply_ln):
    x = x_ref[...]
    if apply_ln:
        xn_ref, qkv_ref, rw_ref, gate_ref = out_refs
        x = _layer_norm(x, g_ref[...], b_ref[...])
        xn_ref[...] = x
    else:
        qkv_ref, rw_ref, gate_ref = out_refs
    xb = x.astype(BF16)
    qkv_ref[...] = _dot(xb, wqkv_ref[...]).astype(BF16)
    rw_ref[...] = _dot(xb, wrw_ref[...])
    gate_ref[...] = _sigmoid(_dot(xb, wgate_ref[...])).astype(BF16)


def _proj(x, g, b, wqkv, wrw, wgate, apply_ln):
    t = x.shape[0]
    tm = min(256, t)
    row = lambda w: pl.BlockSpec((tm, w), lambda i: (i, 0))
    out_shape = [jax.ShapeDtypeStruct((t, 3 * D_HEADS), BF16),
                 jax.ShapeDtypeStruct((t, RW_COLS_PAD), F32),
                 jax.ShapeDtypeStruct((t, 2 * D_MODEL), BF16)]
    out_specs = [row(3 * D_HEADS), row(RW_COLS_PAD), row(2 * D_MODEL)]
    if apply_ln:
        out_shape = [jax.ShapeDtypeStruct((t, D_MODEL), F32)] + out_shape
        out_specs = [row(D_MODEL)] + out_specs
    return pl.pallas_call(
        functools.partial(_proj_kernel, apply_ln=apply_ln),
        grid=(t // tm,),
        in_specs=[row(D_MODEL), _full(g.shape), _full(b.shape),
                  _full(wqkv.shape), _full(wrw.shape), _full(wgate.shape)],
        out_specs=out_specs,
        out_shape=out_shape,
        compiler_params=_cparams("parallel"),
        name="proj_ln" if apply_ln else "proj",
    )(x, g, b, wqkv, wrw, wgate)


def _attn_kernel(q_ref, k_ref, v_ref, bias_ref, o_ref):
    qb = pl.program_id(1)
    ws = pl.multiple_of(jnp.maximum(qb * ATTN_QBLK - N_PREV_CHUNKS * CHUNK, 0), ATTN_QBLK)
    q = q_ref[0]
    kw = k_ref[0, pl.ds(ws, ATTN_WIN), :]
    vw = v_ref[0, pl.ds(ws, ATTN_WIN), :]
    left = lax.broadcasted_iota(jnp.int32, (ATTN_QBLK, PAIR), 1) < HEAD_DIM
    zero = jnp.zeros((ATTN_QBLK, PAIR), BF16)
    heads = [(h // 2, h % 2) for h in range(N_HEADS)]
    spans = [slice(p * PAIR, (p + 1) * PAIR) for p in range(N_HEADS // 2)]
    qm = [jnp.where(left, q[:, spans[p]], zero) if side == 0 else jnp.where(left, zero, q[:, spans[p]])
          for p, side in heads]
    s = [_dot_nt(qm[h], kw[:, spans[p]]) + bias_ref[0, h] for h, (p, _) in enumerate(heads)]
    m = [jnp.max(x, axis=-1, keepdims=True) for x in s]
    e = [jnp.exp(x - mx) for x, mx in zip(s, m)]
    inv = [1.0 / jnp.sum(x, axis=-1, keepdims=True) for x in e]
    o = [_dot(x.astype(BF16), vw[:, spans[p]]) * r for x, r, (p, _) in zip(e, inv, heads)]
    pairs = [jnp.where(left, o[2 * p], o[2 * p + 1]) for p in range(N_HEADS // 2)]
    o_ref[0] = jnp.concatenate(pairs, axis=-1).astype(BF16)


def _attention(qkv, bias_tabs, batch, seq):
    qkv3 = qkv.reshape(batch, seq, 3 * D_HEADS)
    n_tabs = bias_tabs.shape[0]
    out = pl.pallas_call(
        _attn_kernel,
        grid=(batch, seq // ATTN_QBLK),
        in_specs=[
            pl.BlockSpec((1, ATTN_QBLK, D_HEADS), lambda b, i: (b, i, 0)),
            pl.BlockSpec((1, seq, D_HEADS), lambda b, i: (b, 0, 1)),
            pl.BlockSpec((1, seq, D_HEADS), lambda b, i: (b, 0, 2)),
            pl.BlockSpec((1, N_HEADS, ATTN_QBLK, ATTN_WIN),
                         lambda b, i: (jnp.minimum(i, n_tabs - 1), 0, 0, 0)),
        ],
        out_specs=pl.BlockSpec((1, ATTN_QBLK, D_HEADS), lambda b, i: (b, i, 0)),
        out_shape=jax.ShapeDtypeStruct((batch, seq, D_HEADS), BF16),
        compiler_params=_cparams("parallel", "arbitrary"),
        name="band_attn",
    )(qkv3, qkv3, qkv3, bias_tabs)
    return out.reshape(batch * seq, D_HEADS)


def _attn_bias_tables(rel_bias):
    n_tabs = N_PREV_CHUNKS * CHUNK // ATTN_QBLK + 1
    start = (jnp.arange(n_tabs) * ATTN_QBLK)[:, None, None]
    qpos = start + jnp.arange(ATTN_QBLK)[None, :, None]
    kpos = jnp.arange(ATTN_WIN)[None, None, :]
    qc, kc = qpos // CHUNK, kpos // CHUNK
    valid = (kc <= qc) & (kc >= qc - N_PREV_CHUNKS)
    n = jnp.arange(ATTN_QBLK - 1 + ATTN_WIN)
    dist = start[:, :, 0] + (ATTN_QBLK - 1) - n[None, :]
    vals = rel_bias.astype(F32)[:, jnp.clip(dist, -REL_CLIP, REL_CLIP) + REL_CLIP]
    rows = [vals[:, :, ATTN_QBLK - 1 - i:ATTN_QBLK - 1 - i + ATTN_WIN] for i in range(ATTN_QBLK)]
    bias = jnp.stack(rows, axis=2)
    bias = jnp.where(valid[None], bias, NEG_INF)
    return jnp.transpose(bias, (1, 0, 2, 3))


def _rwkv_chunk_kernel(*refs, seq, tm, has_vres):
    if has_vres:
        (rw_ref, prev_ref, mu_ref, dbase_ref, dup_ref, abase_ref, aup_ref, gup_ref, kk_ref, ka_ref,
         rk_ref, eones_ref, tri_ref, blk_ref, vfirst_ref, vbase_ref, vdown_ref, vup_ref,
         q1_ref, y0_ref, m_ref, n0_ref, g_ref, bonus_ref,
         rw_s, aw_s, bi_s, ki_s, be_s, ke_s, v_s, wc_s) = refs
    else:
        (rw_ref, prev_ref, mu_ref, dbase_ref, dup_ref, abase_ref, aup_ref, gup_ref, kk_ref, ka_ref,
         rk_ref, eones_ref, tri_ref, blk_ref,
         q1_ref, y0_ref, m_ref, n0_ref, g_ref, bonus_ref, vfirst_out_ref,
         rw_s, aw_s, bi_s, ki_s, be_s, ke_s, v_s, wc_s) = refs

    cols = rw_ref[...]
    first = (pl.program_id(0) % (seq // tm)) == 0
    prev_row = jnp.where(first, 0.0, prev_ref[7:8, :])
    rows = lax.broadcasted_iota(jnp.int32, cols.shape, 0)
    prev = jnp.where(rows == 0, prev_row, pltpu.roll(cols, 1, axis=0))
    xs = cols + (prev - cols) * mu_ref[...]

    d = D_HEADS
    r, k, v = xs[:, 0:d], xs[:, d:2 * d], xs[:, 2 * d:3 * d]
    wd = xs[:, 3 * d:3 * d + LANES]
    ad = xs[:, 3 * d + LANES:3 * d + 2 * LANES]
    gd = xs[:, 3 * d + 2 * LANES:3 * d + 4 * LANES]
    e_ones = eones_ref[...]

    u = dbase_ref[...] + _dot3(jnp.tanh(wd), dup_ref[...])
    w_log = jnp.minimum(u, 0.0) - jnp.log(1.0 + jnp.exp(-jnp.abs(u))) - 0.5
    logw = -jnp.exp(w_log)
    a_icl = _sigmoid(abase_ref[...] + _dot3(ad, aup_ref[...]))
    g_ref[...] = _dot3(_sigmoid(gd), gup_ref[...])
    kk = k * kk_ref[...]
    kk = kk * lax.rsqrt(jnp.maximum(_seg_sum(kk * kk, e_ones), 1e-24))
    k2 = k * (1.0 + (a_icl - 1.0) * ka_ref[...])
    if has_vres:
        mix = _sigmoid(vbase_ref[...] + _dot3(_dot3(v, vdown_ref[...]), vup_ref[...]))
        v2 = v + (vfirst_ref[...] - v) * mix
    else:
        v2 = v
        vfirst_out_ref[...] = v
    a_vec = -kk
    b_vec = kk * a_icl
    bonus_ref[...] = _seg_sum(r * k2 * rk_ref[...], e_ones) * v2

    cw = _dot_exact_lhs(tri_ref[...], logw)
    cw_tot = _dot_exact_lhs(blk_ref[...], logw)
    rw_s[...] = r * jnp.exp(cw)
    aw_s[...] = a_vec * jnp.exp(cw - logw)
    inv = jnp.exp(-cw)
    bi_s[...] = b_vec * inv
    ki_s[...] = k2 * inv
    rest = jnp.exp(cw_tot - cw)
    be_s[...] = b_vec * rest
    ke_s[...] = k2 * rest
    v_s[...] = v2
    wc_s[...] = jnp.exp(cw_tot)

    shape = (CHUNK, PAIR)
    ti = lax.broadcasted_iota(jnp.int32, shape, 0)
    li = lax.broadcasted_iota(jnp.int32, shape, 1)
    si = li & (HEAD_DIM - 1)
    left = li < HEAD_DIM
    strict = ti > si
    incl = ti >= si
    eye = ti == si

    def bd(x):
        z = jnp.zeros_like(x)
        return jnp.concatenate([jnp.where(left, x, z), jnp.where(left, z, x)], axis=0)

    def fold(x):
        return jnp.where(left, x[:CHUNK], x[CHUNK:])

    def b16(xs):
        return [x.astype(BF16) for x in xs]

    def summaries(insts):
        sl = [(slice(c * CHUNK, (c + 1) * CHUNK), slice(p * PAIR, (p + 1) * PAIR)) for c, p in insts]
        rw = [rw_s[s] for s in sl]
        awb = b16([aw_s[s] for s in sl])
        vb = b16([v_s[s] for s in sl])
        lhs = [jnp.concatenate([a, r.astype(BF16)], axis=0) for a, r in zip(awb, rw)]
        gb = [_dot_nt(l, bd(x)) for l, x in zip(lhs, b16([bi_s[s] for s in sl]))]
        gk = [_dot_nt(l, bd(x)) for l, x in zip(lhs, b16([ki_s[s] for s in sl]))]
        a_ab = [jnp.where(strict, g[:CHUNK], 0.0) for g in gb]
        a_rb = b16([jnp.where(incl, g[CHUNK:], 0.0) for g in gb])
        a_ak = b16([jnp.where(strict, g[:CHUNK], 0.0) for g in gk])
        a_rk = b16([jnp.where(incl, g[CHUNK:], 0.0) for g in gk])
        bdv = [bd(x) for x in vb]
        x0 = [_dot(a, v) for a, v in zip(a_ak, bdv)]
        lp = a_ab
        tinv = [jnp.where(eye, 1.0, a) for a in a_ab]
        for _ in range(5):
            lpb = b16(lp)
            lp = [_dot(l, bd(l)) for l in lpb]
            tinv = [t + _dot(t.astype(BF16), bd(l.astype(BF16))) for t, l in zip(tinv, lp)]
        tb = b16(tinv)
        p1b = b16([_dot(t, bd(a)) for t, a in zip(tb, awb)])
        u0b = b16([_dot(t, bd(x.astype(BF16))) for t, x in zip(tb, x0)])
        q1 = [r + _dot(a, bd(p)) for r, a, p in zip(rw, a_rb, p1b)]
        y0 = [_dot(a, bd(u)) + _dot(k, v) for a, u, k, v in zip(a_rb, u0b, a_rk, bdv)]
        beb = b16([be_s[s] for s in sl])
        keb = b16([ke_s[s] for s in sl])
        mm = [fold(_dot_tn(b, p)) for b, p in zip(beb, p1b)]
        nn = [fold(_dot_tn(b, u) + _dot_tn(k, v)) for b, u, k, v in zip(beb, u0b, keb, vb)]
        for i, (s, (c, p)) in enumerate(zip(sl, insts)):
            wc = wc_s[c * CHUNK:c * CHUNK + 1, s[1]]
            q1_ref[s] = q1[i]
            y0_ref[s] = y0[i]
            m_ref[s] = jnp.where(eye, wc, 0.0) + mm[i]
            n0_ref[s] = nn[i]

    n_pairs = D_HEADS // PAIR
    summaries([(c, p) for c in range(tm // CHUNK) for p in range(n_pairs)])


def _rwkv_chunk(rw, lp, seq, v_first):
    t = rw.shape[0]
    tm = 256
    has_vres = v_first is not None
    nct = tm // CHUNK
    row = lambda w: pl.BlockSpec((tm, w), lambda i: (i, 0))
    prev_spec = pl.BlockSpec((8, RW_COLS_PAD), lambda i: (jnp.maximum(i * (tm // 8) - 1, 0), 0))
    params = [lp["mu"], lp["decay_base"], lp["decay_up"], lp["aaa_base"], lp["aaa_up"], lp["gate_up"],
              lp["k_k"], lp["k_a"], lp["r_k"], lp["e_ones"], lp["tri"], lp["blk"]]
    inputs = [rw, rw] + params
    in_specs = [row(RW_COLS_PAD), prev_spec] + [_full(p.shape) for p in params]
    if has_vres:
        extra = [lp["vres_base"], lp["vres_down"], lp["vres_up"]]
        inputs += [v_first] + extra
        in_specs += [row(D_HEADS)] + [_full(p.shape) for p in extra]
    tok_shape = jax.ShapeDtypeStruct((t, D_HEADS), F32)
    out_shape = [tok_shape] * 6
    out_specs = [row(D_HEADS)] * 6
    if not has_vres:
        out_shape.append(tok_shape)
        out_specs.append(row(D_HEADS))
    scratch = [pltpu.VMEM((tm, D_HEADS), F32) for _ in range(8)]
    return pl.pallas_call(
        functools.partial(_rwkv_chunk_kernel, seq=seq, tm=tm, has_vres=has_vres),
        grid=(t // tm,),
        in_specs=in_specs,
        out_specs=out_specs,
        out_shape=out_shape,
        scratch_shapes=scratch,
        compiler_params=_cparams("parallel"),
        name="rwkv_chunk_vres" if has_vres else "rwkv_chunk",
    )(*inputs)


def _rwkv_scan_kernel(q1_ref, y0_ref, m_ref, n0_ref, y_ref, st_ref, *, n_chunks):
    @pl.when(pl.program_id(1) == 0)
    def _():
        st_ref[...] = jnp.zeros_like(st_ref)

    left = lax.broadcasted_iota(jnp.int32, (CHUNK, PAIR), 1) < HEAD_DIM

    def bd(x):
        z = jnp.zeros_like(x)
        return jnp.concatenate([jnp.where(left, x, z), jnp.where(left, z, x)], axis=0)

    n_pairs = D_HEADS // PAIR
    st = [st_ref[:, p * PAIR:(p + 1) * PAIR] for p in range(n_pairs)]
    for c in range(n_chunks):
        rs = slice(c * CHUNK, (c + 1) * CHUNK)
        for p in range(n_pairs):
            ls = slice(p * PAIR, (p + 1) * PAIR)
            lh, ll = _split2(jnp.concatenate([q1_ref[rs, ls], m_ref[rs, ls]], axis=0))
            sh, sl = _split2(st[p])
            bh = bd(sh)
            res = _dot(lh, bh) + (_dot(ll, bh) + _dot(lh, bd(sl)))
            y_ref[rs, ls] = res[:CHUNK] + y0_ref[rs, ls]
            st[p] = res[CHUNK:] + n0_ref[rs, ls]
    for p in range(n_pairs):
        st_ref[:, p * PAIR:(p + 1) * PAIR] = st[p]


def _rwkv_scan(q1, y0, m, n0, batch, seq):
    t = q1.shape[0]
    cb = 8
    steps = seq // (cb * CHUNK)
    tok_spec = pl.BlockSpec((cb * CHUNK, D_HEADS), lambda b, j: (b * steps + j, 0))
    return pl.pallas_call(
        functools.partial(_rwkv_scan_kernel, n_chunks=cb),
        grid=(batch, steps),
        in_specs=[tok_spec] * 4,
        out_specs=tok_spec,
        out_shape=jax.ShapeDtypeStruct((t, D_HEADS), F32),
        scratch_shapes=[pltpu.VMEM((HEAD_DIM, D_HEADS), F32)],
        compiler_params=_cparams("parallel", "arbitrary"),
        name="rwkv_scan",
    )(q1, y0, m, n0)


def _mix_kernel(y_ref, g_ref, bonus_ref, ya_ref, gate_ref, x_ref, wa_ref, wb_ref, wo_ref,
                eones_ref, gng_ref, gnb_ref, lng_ref, lnb_ref, o_ref):
    e_ones = eones_ref[...]
    y = y_ref[...]
    mean = _seg_sum(y, e_ones) * (1.0 / HEAD_DIM)
    yc = y - mean
    var = _seg_sum(yc * yc, e_ones) * (1.0 / HEAD_DIM)
    yn = yc * lax.rsqrt(var + GN_EPS) * gng_ref[...] + gnb_ref[...]
    yr = (yn + bonus_ref[...]) * g_ref[...]
    za = _dot(ya_ref[...], wa_ref[...])
    zb = _dot(yr.astype(BF16), wb_ref[...])
    gate = gate_ref[...]
    mixed_in = gate[:, :D_MODEL].astype(F32) * za + gate[:, D_MODEL:].astype(F32) * zb
    mixed = _dot(mixed_in.astype(BF16), wo_ref[...])
    o_ref[...] = _layer_norm(DEEPNORM_ALPHA * x_ref[...] + mixed, lng_ref[...], lnb_ref[...])


def _mix(y, g, bonus, y_attn, gate, x, lp):
    t = x.shape[0]
    tm = min(256, t)
    row = lambda w: pl.BlockSpec((tm, w), lambda i: (i, 0))
    params = [lp["w_a"], lp["w_b"], lp["w_o"], lp["e_ones"], lp["gn_g"], lp["gn_b"], lp["ln_g0"], lp["ln_b0"]]
    return pl.pallas_call(
        _mix_kernel,
        grid=(t // tm,),
        in_specs=[row(D_HEADS), row(D_HEADS), row(D_HEADS), row(D_HEADS), row(2 * D_MODEL), row(D_MODEL)]
        + [_full(p.shape) for p in params],
        out_specs=row(D_MODEL),
        out_shape=jax.ShapeDtypeStruct((t, D_MODEL), F32),
        compiler_params=_cparams("parallel"),
        name="branch_mix",
    )(y, g, bonus, y_attn, gate, x, *params)


MOE_SLAB = 2048
MOE_WIN = 128
MOE_ROWS = 2 * MOE_SLAB + MOE_WIN
SUBLANES = 8
RANK_BLK = 256


def _route_kernel(x_ref, wr_ref, br_ref, tri_ref, w_ref, pos_ref, seg_ref):
    x = x_ref[...]
    tn = x.shape[0]
    lane = lax.broadcasted_iota(jnp.int32, (tn, LANES), 1)
    lanef = lane.astype(F32)
    hp = lax.Precision.HIGHEST
    elog = _dot3(x, wr_ref[...]) + br_ref[...]
    big = float(LANES)
    glog = jnp.where((lane >= N_EXPERTS) & (lane < N_EXPERTS + N_GROUPS), elog, -jnp.inf)
    gmax = jnp.max(glog, axis=-1, keepdims=True)
    g_gate = 1.0 / jnp.sum(jnp.exp(glog - gmax), axis=-1, keepdims=True)
    grp = jnp.min(jnp.where(glog == gmax, lanef, big), axis=-1, keepdims=True) - N_EXPERTS
    in_grp = (lanef >= grp * EXPERTS_PER_GROUP) & (lanef < (grp + 1.0) * EXPERTS_PER_GROUP)
    sel = jnp.where(in_grp, elog, -jnp.inf)
    v1 = jnp.max(sel, axis=-1, keepdims=True)
    i1 = jnp.min(jnp.where(sel == v1, lanef, big), axis=-1, keepdims=True)
    sel2 = jnp.where(lanef == i1, -jnp.inf, sel)
    v2 = jnp.max(sel2, axis=-1, keepdims=True)
    i2 = jnp.min(jnp.where(sel2 == v2, lanef, big), axis=-1, keepdims=True)
    e2 = jnp.exp(v2 - v1)
    w1 = g_gate / (1.0 + e2)
    w2 = g_gate * e2 / (1.0 + e2)
    w_ref[...] = jnp.where(lane == 0, w1, jnp.where(lane == 1, w2, 0.0))

    hit1 = lanef == i1
    hit2 = lanef == i2
    onehot = jnp.where(hit1 | hit2, 1.0, 0.0)
    tri = tri_ref[...]
    carry = jnp.zeros((1, LANES), F32)
    ranks = []
    for j in range(tn // RANK_BLK):
        blk = onehot[j * RANK_BLK:(j + 1) * RANK_BLK]
        ranks.append(_dot(tri, blk.astype(BF16)) + carry)
        carry = carry + jnp.sum(blk, axis=0, keepdims=True)
    rank = jnp.concatenate(ranks, axis=0)
    cnt = carry
    li = lax.broadcasted_iota(jnp.int32, (LANES, LANES), 0)
    lj = lax.broadcasted_iota(jnp.int32, (LANES, LANES), 1)
    before = jnp.where(li < lj, 1.0, 0.0)
    off = jnp.dot(jnp.broadcast_to(cnt, (8, LANES)), before, precision=hp,
                  preferred_element_type=F32)[0:1]
    slot = off + rank
    pos1 = jnp.sum(jnp.where(hit1, slot, 0.0), axis=-1, keepdims=True)
    pos2 = jnp.sum(jnp.where(hit2, slot, 0.0), axis=-1, keepdims=True)
    pos_ref[...] = jnp.where(lane == 0, pos1, jnp.where(lane == 1, pos2, 0.0)).astype(jnp.int32)
    row8 = lax.broadcasted_iota(jnp.int32, (8, LANES), 0)
    seg_ref[0] = jnp.where(row8 == 0, off, jnp.where(row8 == 1, cnt, 0.0)).astype(jnp.int32)


def _route(x, lp):
    t = x.shape[0]
    tn = min(MOE_SLAB, t)
    n_slabs = t // tn
    small = [lp["w_route"], lp["b_route"], lp["tri_rank"]]
    row = lambda w: pl.BlockSpec((tn, w), lambda i: (i, 0))
    return pl.pallas_call(
        _route_kernel,
        grid=(n_slabs,),
        in_specs=[row(D_MODEL)] + [_full(p.shape) for p in small],
        out_specs=[row(LANES), row(LANES), pl.BlockSpec((1, 8, LANES), lambda i: (i, 0, 0))],
        out_shape=[jax.ShapeDtypeStruct((t, LANES), F32), jax.ShapeDtypeStruct((t, LANES), jnp.int32),
                   jax.ShapeDtypeStruct((n_slabs, 8, LANES), jnp.int32)],
        compiler_params=_cparams("parallel"),
        name="moe_route",
    )(x, *small)


def _moe_sparse_kernel(off_ref, cnt_ref, pos_ref, w_ref, x_ref, eg_ref, eu_ref, ed_ref, lng_ref, lnb_ref,
                       o_ref, xs_ref):
    s = pl.program_id(0)
    e = pl.program_id(1)
    tn = x_ref.shape[0]

    n_feat = D_MODEL // LANES

    def tile_rows(r):
        return pl.ds(pl.multiple_of(r * SUBLANES, SUBLANES), SUBLANES)

    @pl.when((s == 0) & (e == 0))
    def _():
        xs_ref[pl.ds(2 * tn * SUBLANES, MOE_WIN * SUBLANES), :] = jnp.zeros((MOE_WIN * SUBLANES, LANES), F32)

    @pl.when(e == 0)
    def _():
        def scatter(i, carry):
            t0 = pl.multiple_of(i * SUBLANES, SUBLANES)
            tiles = x_ref[pl.ds(t0, SUBLANES), :].reshape(SUBLANES, n_feat, LANES)
            for j in range(SUBLANES):
                xs_ref[tile_rows(pos_ref[0, 0, 2 * (t0 + j)]), :] = tiles[j]
                xs_ref[tile_rows(pos_ref[0, 0, 2 * (t0 + j) + 1]), :] = tiles[j]
            return carry
        lax.fori_loop(0, tn // SUBLANES, scatter, 0)

    off = off_ref[s * N_EXPERTS + e]
    cnt = cnt_ref[s * N_EXPERTS + e]
    ridx = lax.broadcasted_iota(jnp.int32, (MOE_WIN, 1), 0)

    def window(w, carry):
        r0 = off + w * MOE_WIN
        rows = pl.ds(pl.multiple_of(r0 * SUBLANES, SUBLANES), MOE_WIN * SUBLANES)
        xw = xs_ref[rows, :].reshape(MOE_WIN, n_feat, LANES).reshape(MOE_WIN, D_MODEL)
        xb = xw.astype(BF16)
        hg = _dot(xb, eg_ref[0])
        hu = _dot(xb, eu_ref[0])
        hh = hg * _sigmoid(hg) * hu
        yw = _dot(hh.astype(BF16), ed_ref[0])
        yw = jnp.where(ridx + w * MOE_WIN < cnt, yw, xw)
        xs_ref[rows, :] = yw.reshape(MOE_WIN, n_feat, LANES).reshape(MOE_WIN * SUBLANES, LANES)
        return carry

    lax.fori_loop(0, (cnt + MOE_WIN - 1) // MOE_WIN, window, 0)

    @pl.when(e == N_EXPERTS - 1)
    def _():
        def combine(i, carry):
            t0 = pl.multiple_of(i * SUBLANES, SUBLANES)
            tiles = []
            for j in range(SUBLANES):
                y1 = xs_ref[tile_rows(pos_ref[0, 0, 2 * (t0 + j)]), :]
                y2 = xs_ref[tile_rows(pos_ref[0, 0, 2 * (t0 + j) + 1]), :]
                tiles.append(w_ref[0, 0, 2 * (t0 + j)] * y1 + w_ref[0, 0, 2 * (t0 + j) + 1] * y2)
            o_ref[pl.ds(t0, SUBLANES), :] = jnp.stack(tiles, axis=0).reshape(SUBLANES, D_MODEL)
            return carry
        lax.fori_loop(0, tn // SUBLANES, combine, 0)
        for j in range(tn // RANK_BLK):
            rs = slice(j * RANK_BLK, (j + 1) * RANK_BLK)
            o_ref[rs, :] = _layer_norm(DEEPNORM_ALPHA * x_ref[rs, :] + o_ref[rs, :], lng_ref[...], lnb_ref[...])


def _moe_sparse(x, lp):
    t = x.shape[0]
    tn = min(MOE_SLAB, t)
    n_slabs = t // tn
    w, pos, seg = _route(x, lp)
    w_flat = w[:, :2].reshape(n_slabs, 1, 2 * tn)
    pos_flat = pos[:, :2].reshape(n_slabs, 1, 2 * tn)
    off = seg[:, 0, :N_EXPERTS].reshape(-1)
    cnt = seg[:, 1, :N_EXPERTS].reshape(-1)
    smem_row = pl.BlockSpec((1, 1, 2 * tn), lambda s, e, *_: (s, 0, 0), memory_space=pltpu.SMEM)
    slab = pl.BlockSpec((tn, D_MODEL), lambda s, e, *_: (s, 0), pipeline_mode=pl.Buffered(1))
    const = lambda p: pl.BlockSpec(p.shape, lambda s, e, *_: (0,) * p.ndim)
    grid_spec = pltpu.PrefetchScalarGridSpec(
        num_scalar_prefetch=2,
        grid=(n_slabs, N_EXPERTS),
        in_specs=[smem_row, smem_row, slab,
                  pl.BlockSpec((1, D_MODEL, D_EXPERT), lambda s, e, *_: (e, 0, 0)),
                  pl.BlockSpec((1, D_MODEL, D_EXPERT), lambda s, e, *_: (e, 0, 0)),
                  pl.BlockSpec((1, D_EXPERT, D_MODEL), lambda s, e, *_: (e, 0, 0)),
                  const(lp["ln_g1"]), const(lp["ln_b1"])],
        out_specs=slab,
        scratch_shapes=[pltpu.VMEM((MOE_ROWS * SUBLANES, LANES), F32)],
    )
    return pl.pallas_call(
        _moe_sparse_kernel,
        grid_spec=grid_spec,
        out_shape=jax.ShapeDtypeStruct((t, D_MODEL), F32),
        compiler_params=_cparams("arbitrary", "arbitrary"),
        name="moe_sparse",
    )(off, cnt, pos_flat, w_flat, x, lp["e_gate"], lp["e_up"], lp["e_down"], lp["ln_g1"], lp["ln_b1"])


def _ple_kernel(x_ref, p_ref, wp_ref, wg_ref, lng_ref, lnb_ref, o_ref):
    x = x_ref[...]
    emb = _dot(p_ref[...].astype(BF16), wp_ref[...])
    gate = _sigmoid(_dot(x.astype(BF16), wg_ref[...]))
    o_ref[...] = _layer_norm(DEEPNORM_ALPHA * x + emb * gate, lng_ref[...], lnb_ref[...])


def _ple(x, p, lp):
    t = x.shape[0]
    tm = min(512, t)
    d_ple = p.shape[1]
    row = lambda w: pl.BlockSpec((tm, w), lambda i: (i, 0))
    params = [lp["w_ple"], lp["w_pg"], lp["ln_g2"], lp["ln_b2"]]
    return pl.pallas_call(
        _ple_kernel,
        grid=(t // tm,),
        in_specs=[row(D_MODEL), row(d_ple)] + [_full(q.shape) for q in params],
        out_specs=row(D_MODEL),
        out_shape=jax.ShapeDtypeStruct((t, D_MODEL), F32),
        compiler_params=_cparams("parallel"),
        name="ple",
    )(x, p, *params)


def _pad_to(a, axis, size):
    pad = [(0, 0)] * a.ndim
    pad[axis] = (0, size - a.shape[axis])
    return jnp.pad(a, pad)


def _row(v):
    return v.reshape(1, -1).astype(F32)


def _layer_params(i, w_in, tok_mix, decay_base, decay_up, aaa_base, aaa_up, gate_up, k_k, k_a, r_k,
                  vres_base, vres_down, vres_up, gn_g, gn_b, w_branch_attn, w_branch_rwkv, w_out,
                  router_grp, router_grp_bias, router_exp, router_exp_bias, exp_gate, exp_up, exp_down,
                  ple_proj, ple_gate, ln_g, ln_b, consts):
    d = D_HEADS
    w = w_in[i]
    rw0 = 3 * d
    gate0 = rw0 + 3 * d + DECAY_LORA + AAA_LORA + GATE_LORA
    wqkv = jnp.concatenate([w[:, 0:d] * (HEAD_DIM ** -0.5), w[:, d:rw0]], axis=1).astype(BF16)
    wr = w[:, rw0:gate0]
    lora0 = 3 * d
    pieces = [wr[:, :lora0],
              _pad_to(wr[:, lora0:lora0 + DECAY_LORA], 1, LANES),
              _pad_to(wr[:, lora0 + DECAY_LORA:lora0 + DECAY_LORA + AAA_LORA], 1, LANES),
              _pad_to(wr[:, lora0 + DECAY_LORA + AAA_LORA:], 1, 2 * LANES)]
    wrw = jnp.concatenate(pieces, axis=1).astype(BF16)
    mu = tok_mix[i]
    mu_pieces = [mu[:lora0],
                 _pad_to(mu[lora0:lora0 + DECAY_LORA], 0, LANES),
                 _pad_to(mu[lora0 + DECAY_LORA:lora0 + DECAY_LORA + AAA_LORA], 0, LANES),
                 _pad_to(mu[lora0 + DECAY_LORA + AAA_LORA:], 0, 2 * LANES)]
    lp = dict(consts)
    lp.update(
        wqkv=wqkv, wrw=wrw, wgate=w[:, gate0:].astype(BF16),
        mu=_row(jnp.concatenate(mu_pieces)),
        decay_base=_row(decay_base[i]), decay_up=_pad_to(decay_up[i], 0, LANES).astype(F32),
        aaa_base=_row(aaa_base[i]), aaa_up=_pad_to(aaa_up[i], 0, LANES).astype(F32),
        gate_up=_pad_to(gate_up[i], 0, 2 * LANES).astype(F32),
        k_k=_row(k_k[i]), k_a=_row(k_a[i]), r_k=_row(r_k[i]),
        gn_g=_row(gn_g[i]), gn_b=_row(gn_b[i]),
        w_a=w_branch_attn[i].astype(BF16), w_b=w_branch_rwkv[i].astype(BF16), w_o=w_out[i].astype(BF16),
        w_route=_pad_to(jnp.concatenate([router_exp[i], router_grp[i]], axis=1), 1, LANES).astype(F32),
        b_route=_row(_pad_to(jnp.concatenate([router_exp_bias[i], router_grp_bias[i]]), 0, LANES)),
        e_gate=exp_gate[i].astype(BF16), e_up=exp_up[i].astype(BF16), e_down=exp_down[i].astype(BF16),
        w_ple=ple_proj[i].astype(BF16), w_pg=ple_gate[i].astype(BF16),
        ln_g0=_row(ln_g[i, 0]), ln_b0=_row(ln_b[i, 0]),
        ln_g1=_row(ln_g[i, 1]), ln_b1=_row(ln_b[i, 1]),
        ln_g2=_row(ln_g[i, 2]), ln_b2=_row(ln_b[i, 2]),
    )
    if i > 0:
        lp.update(vres_base=_row(vres_base[i - 1]),
                  vres_down=_pad_to(vres_down[i - 1], 1, LANES).astype(F32),
                  vres_up=_pad_to(vres_up[i - 1], 0, LANES).astype(F32))
    return lp


def _const_mats(tm):
    head = jnp.arange(D_HEADS) // HEAD_DIM
    e_ones = (head[:, None] == head[None, :]).astype(BF16)
    tok = jnp.arange(tm)
    same = (tok[:, None] // CHUNK) == (tok[None, :] // CHUNK)
    tri = (same & (tok[:, None] >= tok[None, :])).astype(BF16)
    blk = same.astype(BF16)
    rtok = jnp.arange(RANK_BLK)
    tri_rank = (rtok[None, :] < rtok[:, None]).astype(BF16)
    return dict(e_ones=e_ones, tri=tri, blk=blk, tri_rank=tri_rank)


def kernel(x, p, ln_in_g, ln_in_b, rel_bias, w_in, tok_mix, decay_base, decay_up, aaa_base, aaa_up,
           gate_up, k_k, k_a, r_k, vres_base, vres_down, vres_up, gn_g, gn_b, w_branch_attn,
           w_branch_rwkv, w_out, router_grp, router_grp_bias, router_exp, router_exp_bias, exp_gate,
           exp_up, exp_down, ple_proj, ple_gate, ln_g, ln_b):
    batch, seq, _ = x.shape
    t = batch * seq
    consts = _const_mats(256)
    bias_tabs = _attn_bias_tables(rel_bias)
    xt = x.reshape(t, D_MODEL)
    pt = p.reshape(p.shape[0], t, p.shape[-1])
    v_first = None
    for i in range(DEPTH):
        lp = _layer_params(i, w_in, tok_mix, decay_base, decay_up, aaa_base, aaa_up, gate_up, k_k, k_a,
                           r_k, vres_base, vres_down, vres_up, gn_g, gn_b, w_branch_attn, w_branch_rwkv,
                           w_out, router_grp, router_grp_bias, router_exp, router_exp_bias, exp_gate,
                           exp_up, exp_down, ple_proj, ple_gate, ln_g, ln_b, consts)
        if i == 0:
            xt, qkv, rw, gate = _proj(xt, _row(ln_in_g), _row(ln_in_b), lp["wqkv"], lp["wrw"],
                                      lp["wgate"], apply_ln=True)
        else:
            qkv, rw, gate = _proj(xt, _row(ln_in_g), _row(ln_in_b), lp["wqkv"], lp["wrw"],
                                  lp["wgate"], apply_ln=False)
        y_attn = _attention(qkv, bias_tabs, batch, seq)
        if i == 0:
            q1, y0, m, n0, g, bonus, v_first = _rwkv_chunk(rw, lp, seq, None)
        else:
            q1, y0, m, n0, g, bonus = _rwkv_chunk(rw, lp, seq, v_first)
        y = _rwkv_scan(q1, y0, m, n0, batch, seq)
        xt = _mix(y, g, bonus, y_attn, gate, xt, lp)
        xt = _moe_sparse(xt, lp)
        xt = _ple(xt, pt[i], lp)
    return xt.reshape(batch, seq, D_MODEL)
```

```python
import functools
import math

import jax
import jax.numpy as jnp
from jax import lax
from jax.experimental import pallas as pl
from jax.experimental.pallas import tpu as pltpu

F32 = jnp.float32
BF16 = jnp.bfloat16

D_MODEL = 1024
CHUNK = 64
N_PREV_CHUNKS = 8
HEAD_DIM = 64
N_HEADS = 8
D_HEADS = N_HEADS * HEAD_DIM
REL_CLIP = 128
DECAY_LORA = 64
AAA_LORA = 64
GATE_LORA = 160
MV_LORA = 32
N_GROUPS = 4
EXPERTS_PER_GROUP = 8
N_EXPERTS = N_GROUPS * EXPERTS_PER_GROUP
D_EXPERT = 256
DEPTH = 2
DEEPNORM_ALPHA = (2 * DEPTH) ** 0.25
LN_EPS = 1e-5
GN_EPS = 64e-5
NEG_INF = -1e30

LANES = 128
PAIR = 2 * HEAD_DIM
ATTN_QBLK = 2 * CHUNK
ATTN_WIN = (N_PREV_CHUNKS + 2) * CHUNK
RW_COLS_PAD = 2048
VMEM_LIMIT = 56 * 1024 * 1024


def _cparams(*sem):
    return pltpu.CompilerParams(dimension_semantics=sem, vmem_limit_bytes=VMEM_LIMIT)


def _dot(a, b):
    return jnp.dot(a, b, preferred_element_type=F32)


def _dot_nt(a, b):
    return lax.dot_general(a, b, (((1,), (1,)), ((), ())), preferred_element_type=F32)


def _dot_tn(a, b):
    return lax.dot_general(a, b, (((0,), (0,)), ((), ())), preferred_element_type=F32)


def _split2(x):
    hi = x.astype(BF16)
    lo = (x - hi.astype(F32)).astype(BF16)
    return hi, lo


def _dot3(a, b):
    ah, al = _split2(a)
    bh, bl = _split2(b)
    return _dot(ah, bh) + (_dot(al, bh) + _dot(ah, bl))


def _dot_exact_lhs(a_bf16, x):
    x1 = x.astype(BF16)
    r1 = x - x1.astype(F32)
    x2 = r1.astype(BF16)
    x3 = (r1 - x2.astype(F32)).astype(BF16)
    return _dot(a_bf16, x1) + (_dot(a_bf16, x2) + _dot(a_bf16, x3))


def _seg_sum(x, e_ones):
    hi, lo = _split2(x)
    return _dot(hi, e_ones) + _dot(lo, e_ones)


def _layer_norm(x, g, b):
    mu = jnp.mean(x, axis=-1, keepdims=True)
    xc = x - mu
    var = jnp.mean(xc * xc, axis=-1, keepdims=True)
    return xc * lax.rsqrt(var + LN_EPS) * g + b


def _sigmoid(x):
    return 1.0 / (1.0 + jnp.exp(-x))


def _full(shape):
    nd = len(shape)
    return pl.BlockSpec(shape, lambda *_: (0,) * nd)


def _proj_kernel(x_ref, g_ref, b_ref, wqkv_ref, wrw_ref, wgate_ref, *out_refs, apply_ln):
    x = x_ref[...]
    if apply_ln:
        xn_ref, qkv_ref, rw_ref, gate_ref = out_refs
        x = _layer_norm(x, g_ref[...], b_ref[...])
        xn_ref[...] = x
    else:
        qkv_ref, rw_ref, gate_ref = out_refs
    xb = x.astype(BF16)
    qkv_ref[...] = _dot(xb, wqkv_ref[...]).astype(BF16)
    rw_ref[...] = _dot(xb, wrw_ref[...])
    gate_ref[...] = _sigmoid(_dot(xb, wgate_ref[...])).astype(BF16)


def _proj(x, g, b, wqkv, wrw, wgate, apply_ln):
    t = x.shape[0]
    tm = min(256, t)
    row = lambda w: pl.BlockSpec((tm, w), lambda i: (i, 0))
    out_shape = [jax.ShapeDtypeStruct((t, 3 * D_HEADS), BF16),
                 jax.ShapeDtypeStruct((t, RW_COLS_PAD), F32),
                 jax.ShapeDtypeStruct((t, 2 * D_MODEL), BF16)]
    out_specs = [row(3 * D_HEADS), row(RW_COLS_PAD), row(2 * D_MODEL)]
    if apply_ln:
        out_shape = [jax.ShapeDtypeStruct((t, D_MODEL), F32)] + out_shape
        out_specs = [row(D_MODEL)] + out_specs
    return pl.pallas_call(
        functools.partial(_proj_kernel, apply_ln=apply_ln),
        grid=(t // tm,),
        in_specs=[row(D_MODEL), _full(g.shape), _full(b.shape),
                  _full(wqkv.shape), _full(wrw.shape), _full(wgate.shape)],
        out_specs=out_specs,
        out_shape=out_shape,
        compiler_params=_cparams("parallel"),
        name="proj_ln" if apply_ln else "proj",
    )(x, g, b, wqkv, wrw, wgate)


def _attn_kernel(q_ref, k_ref, v_ref, bias_ref, o_ref):
    qb = pl.program_id(1)
    ws = pl.multiple_of(jnp.maximum(qb * ATTN_QBLK - N_PREV_CHUNKS * CHUNK, 0), ATTN_QBLK)
    q = q_ref[0]
    kw = k_ref[0, pl.ds(ws, ATTN_WIN), :]
    vw = v_ref[0, pl.ds(ws, ATTN_WIN), :]
    left = lax.broadcasted_iota(jnp.int32, (ATTN_QBLK, PAIR), 1) < HEAD_DIM
    zero = jnp.zeros((ATTN_QBLK, PAIR), BF16)
    heads = [(h // 2, h % 2) for h in range(N_HEADS)]
    spans = [slice(p * PAIR, (p + 1) * PAIR) for p in range(N_HEADS // 2)]
    qm = [jnp.where(left, q[:, spans[p]], zero) if side == 0 else jnp.where(left, zero, q[:, spans[p]])
          for p, side in heads]
    s = [_dot_nt(qm[h], kw[:, spans[p]]) + bias_ref[0, h] for h, (p, _) in enumerate(heads)]
    m = [jnp.max(x, axis=-1, keepdims=True) for x in s]
    e = [jnp.exp(x - mx) for x, mx in zip(s, m)]
    inv = [1.0 / jnp.sum(x, axis=-1, keepdims=True) for x in e]
    o = [_dot(x.astype(BF16), vw[:, spans[p]]) * r for x, r, (p, _) in zip(e, inv, heads)]
    pairs = [jnp.where(left, o[2 * p], o[2 * p + 1]) for p in range(N_HEADS // 2)]
    o_ref[0] = jnp.concatenate(pairs, axis=-1).astype(BF16)


def _attention(qkv, bias_tabs, batch, seq):
    qkv3 = qkv.reshape(batch, seq, 3 * D_HEADS)
    n_tabs = bias_tabs.shape[0]
    out = pl.pallas_call(
        _attn_kernel,
        grid=(batch, seq // ATTN_QBLK),
        in_specs=[
            pl.BlockSpec((1, ATTN_QBLK, D_HEADS), lambda b, i: (b, i, 0)),
            pl.BlockSpec((1, seq, D_HEADS), lambda b, i: (b, 0, 1)),
            pl.BlockSpec((1, seq, D_HEADS), lambda b, i: (b, 0, 2)),
            pl.BlockSpec((1, N_HEADS, ATTN_QBLK, ATTN_WIN),
                         lambda b, i: (jnp.minimum(i, n_tabs - 1), 0, 0, 0)),
        ],
        out_specs=pl.BlockSpec((1, ATTN_QBLK, D_HEADS), lambda b, i: (b, i, 0)),
        out_shape=jax.ShapeDtypeStruct((batch, seq, D_HEADS), BF16),
        compiler_params=_cparams("parallel", "arbitrary"),
        name="band_attn",
    )(qkv3, qkv3, qkv3, bias_tabs)
    return out.reshape(batch * seq, D_HEADS)


def _attn_bias_tables(rel_bias):
    n_tabs = N_PREV_CHUNKS * CHUNK // ATTN_QBLK + 1
    start = (jnp.arange(n_tabs) * ATTN_QBLK)[:, None, None]
    qpos = start + jnp.arange(ATTN_QBLK)[None, :, None]
    kpos = jnp.arange(ATTN_WIN)[None, None, :]
    qc, kc = qpos // CHUNK, kpos // CHUNK
    valid = (kc <= qc) & (kc >= qc - N_PREV_CHUNKS)
    n = jnp.arange(ATTN_QBLK - 1 + ATTN_WIN)
    dist = start[:, :, 0] + (ATTN_QBLK - 1) - n[None, :]
    vals = rel_bias.astype(F32)[:, jnp.clip(dist, -REL_CLIP, REL_CLIP) + REL_CLIP]
    rows = [vals[:, :, ATTN_QBLK - 1 - i:ATTN_QBLK - 1 - i + ATTN_WIN] for i in range(ATTN_QBLK)]
    bias = jnp.stack(rows, axis=2)
    bias = jnp.where(valid[None], bias, NEG_INF)
    return jnp.transpose(bias, (1, 0, 2, 3))


def _rwkv_chunk_kernel(*refs, seq, tm, has_vres):
    if has_vres:
        (rw_ref, prev_ref, mu_ref, dbase_ref, dup_ref, abase_ref, aup_ref, gup_ref, kk_ref, ka_ref,
         rk_ref, eones_ref, tri_ref, blk_ref, vfirst_ref, vbase_ref, vdown_ref, vup_ref,
         q1_ref, y0_ref, m_ref, n0_ref, g_ref, bonus_ref,
         rw_s, aw_s, bi_s, ki_s, be_s, ke_s, v_s, wc_s) = refs
    else:
        (rw_ref, prev_ref, mu_ref, dbase_ref, dup_ref, abase_ref, aup_ref, gup_ref, kk_ref, ka_ref,
         rk_ref, eones_ref, tri_ref, blk_ref,
         q1_ref, y0_ref, m_ref, n0_ref, g_ref, bonus_ref, vfirst_out_ref,
         rw_s, aw_s, bi_s, ki_s, be_s, ke_s, v_s, wc_s) = refs

    cols = rw_ref[...]
    first = (pl.program_id(0) % (seq // tm)) == 0
    prev_row = jnp.where(first, 0.0, prev_ref[7:8, :])
    rows = lax.broadcasted_iota(jnp.int32, cols.shape, 0)
    prev = jnp.where(rows == 0, prev_row, pltpu.roll(cols, 1, axis=0))
    xs = cols + (prev - cols) * mu_ref[...]

    d = D_HEADS
    r, k, v = xs[:, 0:d], xs[:, d:2 * d], xs[:, 2 * d:3 * d]
    wd = xs[:, 3 * d:3 * d + LANES]
    ad = xs[:, 3 * d + LANES:3 * d + 2 * LANES]
    gd = xs[:, 3 * d + 2 * LANES:3 * d + 4 * LANES]
    e_ones = eones_ref[...]

    u = dbase_ref[...] + _dot3(jnp.tanh(wd), dup_ref[...])
    w_log = jnp.minimum(u, 0.0) - jnp.log(1.0 + jnp.exp(-jnp.abs(u))) - 0.5
    logw = -jnp.exp(w_log)
    a_icl = _sigmoid(abase_ref[...] + _dot3(ad, aup_ref[...]))
    g_ref[...] = _dot3(_sigmoid(gd), gup_ref[...])
    kk = k * kk_ref[...]
    kk = kk * lax.rsqrt(jnp.maximum(_seg_sum(kk * kk, e_ones), 1e-24))
    k2 = k * (1.0 + (a_icl - 1.0) * ka_ref[...])
    if has_vres:
        mix = _sigmoid(vbase_ref[...] + _dot3(_dot3(v, vdown_ref[...]), vup_ref[...]))
        v2 = v + (vfirst_ref[...] - v) * mix
    else:
        v2 = v
        vfirst_out_ref[...] = v
    a_vec = -kk
    b_vec = kk * a_icl
    bonus_ref[...] = _seg_sum(r * k2 * rk_ref[...], e_ones) * v2

    cw = _dot_exact_lhs(tri_ref[...], logw)
    cw_tot = _dot_exact_lhs(blk_ref[...], logw)
    rw_s[...] = r * jnp.exp(cw)
    aw_s[...] = a_vec * jnp.exp(cw - logw)
    inv = jnp.exp(-cw)
    bi_s[...] = b_vec * inv
    ki_s[...] = k2 * inv
    rest = jnp.exp(cw_tot - cw)
    be_s[...] = b_vec * rest
    ke_s[...] = k2 * rest
    v_s[...] = v2
    wc_s[...] = jnp.exp(cw_tot)

    shape = (CHUNK, PAIR)
    ti = lax.broadcasted_iota(jnp.int32, shape, 0)
    li = lax.broadcasted_iota(jnp.int32, shape, 1)
    si = li & (HEAD_DIM - 1)
    left = li < HEAD_DIM
    strict = ti > si
    incl = ti >= si
    eye = ti == si

    def bd(x):
        z = jnp.zeros_like(x)
        return jnp.concatenate([jnp.where(left, x, z), jnp.where(left, z, x)], axis=0)

    def fold(x):
        return jnp.where(left, x[:CHUNK], x[CHUNK:])

    def b16(xs):
        return [x.astype(BF16) for x in xs]

    def summaries(insts):
        sl = [(slice(c * CHUNK, (c + 1) * CHUNK), slice(p * PAIR, (p + 1) * PAIR)) for c, p in insts]
        rw = [rw_s[s] for s in sl]
        awb = b16([aw_s[s] for s in sl])
        vb = b16([v_s[s] for s in sl])
        lhs = [jnp.concatenate([a, r.astype(BF16)], axis=0) for a, r in zip(awb, rw)]
        gb = [_dot_nt(l, bd(x)) for l, x in zip(lhs, b16([bi_s[s] for s in sl]))]
        gk = [_dot_nt(l, bd(x)) for l, x in zip(lhs, b16([ki_s[s] for s in sl]))]
        a_ab = [jnp.where(strict, g[:CHUNK], 0.0) for g in gb]
        a_rb = b16([jnp.where(incl, g[CHUNK:], 0.0) for g in gb])
        a_ak = b16([jnp.where(strict, g[:CHUNK], 0.0) for g in gk])
        a_rk = b16([jnp.where(incl, g[CHUNK:], 0.0) for g in gk])
        bdv = [bd(x) for x in vb]
        x0 = [_dot(a, v) for a, v in zip(a_ak, bdv)]
        lp = a_ab
        tinv = [jnp.where(eye, 1.0, a) for a in a_ab]
        for _ in range(5):
            lpb = b16(lp)
            lp = [_dot(l, bd(l)) for l in lpb]
            tinv = [t + _dot(t.astype(BF16), bd(l.astype(BF16))) for t, l in zip(tinv, lp)]
        tb = b16(tinv)
        p1b = b16([_dot(t, bd(a)) for t, a in zip(tb, awb)])
        u0b = b16([_dot(t, bd(x.astype(BF16))) for t, x in zip(tb, x0)])
        q1 = [r + _dot(a, bd(p)) for r, a, p in zip(rw, a_rb, p1b)]
        y0 = [_dot(a, bd(u)) + _dot(k, v) for a, u, k, v in zip(a_rb, u0b, a_rk, bdv)]
        beb = b16([be_s[s] for s in sl])
        keb = b16([ke_s[s] for s in sl])
        mm = [fold(_dot_tn(b, p)) for b, p in zip(beb, p1b)]
        nn = [fold(_dot_tn(b, u) + _dot_tn(k, v)) for b, u, k, v in zip(beb, u0b, keb, vb)]
        for i, (s, (c, p)) in enumerate(zip(sl, insts)):
            wc = wc_s[c * CHUNK:c * CHUNK + 1, s[1]]
            q1_ref[s] = q1[i]
            y0_ref[s] = y0[i]
            m_ref[s] = jnp.where(eye, wc, 0.0) + mm[i]
            n0_ref[s] = nn[i]

    n_pairs = D_HEADS // PAIR
    summaries([(c, p) for c in range(tm // CHUNK) for p in range(n_pairs)])


def _rwkv_chunk(rw, lp, seq, v_first):
    t = rw.shape[0]
    tm = 256
    has_vres = v_first is not None
    nct = tm // CHUNK
    row = lambda w: pl.BlockSpec((tm, w), lambda i: (i, 0))
    prev_spec = pl.BlockSpec((8, RW_COLS_PAD), lambda i: (jnp.maximum(i * (tm // 8) - 1, 0), 0))
    params = [lp["mu"], lp["decay_base"], lp["decay_up"], lp["aaa_base"], lp["aaa_up"], lp["gate_up"],
              lp["k_k"], lp["k_a"], lp["r_k"], lp["e_ones"], lp["tri"], lp["blk"]]
    inputs = [rw, rw] + params
    in_specs = [row(RW_COLS_PAD), prev_spec] + [_full(p.shape) for p in params]
    if has_vres:
        extra = [lp["vres_base"], lp["vres_down"], lp["vres_up"]]
        inputs += [v_first] + extra
        in_specs += [row(D_HEADS)] + [_full(p.shape) for p in extra]
    tok_shape = jax.ShapeDtypeStruct((t, D_HEADS), F32)
    out_shape = [tok_shape] * 6
    out_specs = [row(D_HEADS)] * 6
    if not has_vres:
        out_shape.append(tok_shape)
        out_specs.append(row(D_HEADS))
    scratch = [pltpu.VMEM((tm, D_HEADS), F32) for _ in range(8)]
    return pl.pallas_call(
        functools.partial(_rwkv_chunk_kernel, seq=seq, tm=tm, has_vres=has_vres),
        grid=(t // tm,),
        in_specs=in_specs,
        out_specs=out_specs,
        out_shape=out_shape,
        scratch_shapes=scratch,
        compiler_params=_cparams("parallel"),
        name="rwkv_chunk_vres" if has_vres else "rwkv_chunk",
    )(*inputs)


def _rwkv_scan_kernel(q1_ref, y0_ref, m_ref, n0_ref, y_ref, st_ref, *, n_chunks):
    @pl.when(pl.program_id(1) == 0)
    def _():
        st_ref[...] = jnp.zeros_like(st_ref)

    left = lax.broadcasted_iota(jnp.int32, (CHUNK, PAIR), 1) < HEAD_DIM

    def bd(x):
        z = jnp.zeros_like(x)
        return jnp.concatenate([jnp.where(left, x, z), jnp.where(left, z, x)], axis=0)

    n_pairs = D_HEADS // PAIR
    st = [st_ref[:, p * PAIR:(p + 1) * PAIR] for p in range(n_pairs)]
    for c in range(n_chunks):
        rs = slice(c * CHUNK, (c + 1) * CHUNK)
        for p in range(n_pairs):
            ls = slice(p * PAIR, (p + 1) * PAIR)
            lh, ll = _split2(jnp.concatenate([q1_ref[rs, ls], m_ref[rs, ls]], axis=0))
            sh, sl = _split2(st[p])
            bh = bd(sh)
            res = _dot(lh, bh) + (_dot(ll, bh) + _dot(lh, bd(sl)))
            y_ref[rs, ls] = res[:CHUNK] + y0_ref[rs, ls]
            st[p] = res[CHUNK:] + n0_ref[rs, ls]
    for p in range(n_pairs):
        st_ref[:, p * PAIR:(p + 1) * PAIR] = st[p]


def _rwkv_scan(q1, y0, m, n0, batch, seq):
    t = q1.shape[0]
    cb = 8
    steps = seq // (cb * CHUNK)
    tok_spec = pl.BlockSpec((cb * CHUNK, D_HEADS), lambda b, j: (b * steps + j, 0))
    return pl.pallas_call(
        functools.partial(_rwkv_scan_kernel, n_chunks=cb),
        grid=(batch, steps),
        in_specs=[tok_spec] * 4,
        out_specs=tok_spec,
        out_shape=jax.ShapeDtypeStruct((t, D_HEADS), F32),
        scratch_shapes=[pltpu.VMEM((HEAD_DIM, D_HEADS), F32)],
        compiler_params=_cparams("parallel", "arbitrary"),
        name="rwkv_scan",
    )(q1, y0, m, n0)


def _mix_kernel(y_ref, g_ref, bonus_ref, ya_ref, gate_ref, x_ref, wa_ref, wb_ref, wo_ref,
                eones_ref, gng_ref, gnb_ref, lng_ref, lnb_ref, o_ref):
    e_ones = eones_ref[...]
    y = y_ref[...]
    mean = _seg_sum(y, e_ones) * (1.0 / HEAD_DIM)
    yc = y - mean
    var = _seg_sum(yc * yc, e_ones) * (1.0 / HEAD_DIM)
    yn = yc * lax.rsqrt(var + GN_EPS) * gng_ref[...] + gnb_ref[...]
    yr = (yn + bonus_ref[...]) * g_ref[...]
    za = _dot(ya_ref[...], wa_ref[...])
    zb = _dot(yr.astype(BF16), wb_ref[...])
    gate = gate_ref[...]
    mixed_in = gate[:, :D_MODEL].astype(F32) * za + gate[:, D_MODEL:].astype(F32) * zb
    mixed = _dot(mixed_in.astype(BF16), wo_ref[...])
    o_ref[...] = _layer_norm(DEEPNORM_ALPHA * x_ref[...] + mixed, lng_ref[...], lnb_ref[...])


def _mix(y, g, bonus, y_attn, gate, x, lp):
    t = x.shape[0]
    tm = min(256, t)
    row = lambda w: pl.BlockSpec((tm, w), lambda i: (i, 0))
    params = [lp["w_a"], lp["w_b"], lp["w_o"], lp["e_ones"], lp["gn_g"], lp["gn_b"], lp["ln_g0"], lp["ln_b0"]]
    return pl.pallas_call(
        _mix_kernel,
        grid=(t // tm,),
        in_specs=[row(D_HEADS), row(D_HEADS), row(D_HEADS), row(D_HEADS), row(2 * D_MODEL), row(D_MODEL)]
        + [_full(p.shape) for p in params],
        out_specs=row(D_MODEL),
        out_shape=jax.ShapeDtypeStruct((t, D_MODEL), F32),
        compiler_params=_cparams("parallel"),
        name="branch_mix",
    )(y, g, bonus, y_attn, gate, x, *params)


MOE_SLAB = 2048
MOE_WIN = 128
MOE_EBLK = 4
MOE_ROWS = 2 * MOE_SLAB + MOE_WIN
SUBLANES = 8
RANK_BLK = 256


def _route_kernel(x_ref, wr_ref, br_ref, tri_ref, w_ref, pos_ref, seg_ref):
    x = x_ref[...]
    tn = x.shape[0]
    lane = lax.broadcasted_iota(jnp.int32, (tn, LANES), 1)
    lanef = lane.astype(F32)
    hp = lax.Precision.HIGHEST
    elog = _dot3(x, wr_ref[...]) + br_ref[...]
    big = float(LANES)
    glog = jnp.where((lane >= N_EXPERTS) & (lane < N_EXPERTS + N_GROUPS), elog, -jnp.inf)
    gmax = jnp.max(glog, axis=-1, keepdims=True)
    g_gate = 1.0 / jnp.sum(jnp.exp(glog - gmax), axis=-1, keepdims=True)
    grp = jnp.min(jnp.where(glog == gmax, lanef, big), axis=-1, keepdims=True) - N_EXPERTS
    in_grp = (lanef >= grp * EXPERTS_PER_GROUP) & (lanef < (grp + 1.0) * EXPERTS_PER_GROUP)
    sel = jnp.where(in_grp, elog, -jnp.inf)
    v1 = jnp.max(sel, axis=-1, keepdims=True)
    i1 = jnp.min(jnp.where(sel == v1, lanef, big), axis=-1, keepdims=True)
    sel2 = jnp.where(lanef == i1, -jnp.inf, sel)
    v2 = jnp.max(sel2, axis=-1, keepdims=True)
    i2 = jnp.min(jnp.where(sel2 == v2, lanef, big), axis=-1, keepdims=True)
    e2 = jnp.exp(v2 - v1)
    w1 = g_gate / (1.0 + e2)
    w2 = g_gate * e2 / (1.0 + e2)
    w_ref[...] = jnp.where(lane == 0, w1, jnp.where(lane == 1, w2, 0.0))

    hit1 = lanef == i1
    hit2 = lanef == i2
    onehot = jnp.where(hit1 | hit2, 1.0, 0.0)
    tri = tri_ref[...]
    carry = jnp.zeros((1, LANES), F32)
    ranks = []
    for j in range(tn // RANK_BLK):
        blk = onehot[j * RANK_BLK:(j + 1) * RANK_BLK]
        ranks.append(_dot(tri, blk.astype(BF16)) + carry)
        carry = carry + jnp.sum(blk, axis=0, keepdims=True)
    rank = jnp.concatenate(ranks, axis=0)
    cnt = carry
    li = lax.broadcasted_iota(jnp.int32, (LANES, LANES), 0)
    lj = lax.broadcasted_iota(jnp.int32, (LANES, LANES), 1)
    before = jnp.where(li < lj, 1.0, 0.0)
    off = jnp.dot(jnp.broadcast_to(cnt, (8, LANES)), before, precision=hp,
                  preferred_element_type=F32)[0:1]
    slot = off + rank
    pos1 = jnp.sum(jnp.where(hit1, slot, 0.0), axis=-1, keepdims=True)
    pos2 = jnp.sum(jnp.where(hit2, slot, 0.0), axis=-1, keepdims=True)
    pos_ref[...] = jnp.where(lane == 0, pos1, jnp.where(lane == 1, pos2, 0.0)).astype(jnp.int32)
    row8 = lax.broadcasted_iota(jnp.int32, (8, LANES), 0)
    seg_ref[0] = jnp.where(row8 == 0, off, jnp.where(row8 == 1, cnt, 0.0)).astype(jnp.int32)


def _route(x, lp):
    t = x.shape[0]
    tn = min(MOE_SLAB, t)
    n_slabs = t // tn
    small = [lp["w_route"], lp["b_route"], lp["tri_rank"]]
    row = lambda w: pl.BlockSpec((tn, w), lambda i: (i, 0))
    return pl.pallas_call(
        _route_kernel,
        grid=(n_slabs,),
        in_specs=[row(D_MODEL)] + [_full(p.shape) for p in small],
        out_specs=[row(LANES), row(LANES), pl.BlockSpec((1, 8, LANES), lambda i: (i, 0, 0))],
        out_shape=[jax.ShapeDtypeStruct((t, LANES), F32), jax.ShapeDtypeStruct((t, LANES), jnp.int32),
                   jax.ShapeDtypeStruct((n_slabs, 8, LANES), jnp.int32)],
        compiler_params=_cparams("parallel"),
        name="moe_route",
    )(x, *small)


def _moe_sparse_kernel(off_ref, cnt_ref, pos_ref, w_ref, x_ref, eg_ref, eu_ref, ed_ref, lng_ref, lnb_ref,
                       o_ref, xs_ref):
    s = pl.program_id(0)
    e = pl.program_id(1)
    tn = x_ref.shape[0]

    n_feat = D_MODEL // LANES

    def tile_rows(r):
        return pl.ds(pl.multiple_of(r * SUBLANES, SUBLANES), SUBLANES)

    @pl.when((s == 0) & (e == 0))
    def _():
        xs_ref[pl.ds(2 * tn * SUBLANES, MOE_WIN * SUBLANES), :] = jnp.zeros((MOE_WIN * SUBLANES, LANES), F32)

    @pl.when(e == 0)
    def _():
        def scatter(i, carry):
            t0 = pl.multiple_of(i * SUBLANES, SUBLANES)
            tiles = x_ref[pl.ds(t0, SUBLANES), :].reshape(SUBLANES, n_feat, LANES)
            for j in range(SUBLANES):
                xs_ref[tile_rows(pos_ref[0, 0, 2 * (t0 + j)]), :] = tiles[j]
                xs_ref[tile_rows(pos_ref[0, 0, 2 * (t0 + j) + 1]), :] = tiles[j]
            return carry
        lax.fori_loop(0, tn // SUBLANES, scatter, 0)

    ridx = lax.broadcasted_iota(jnp.int32, (MOE_WIN, 1), 0)

    for j in range(MOE_EBLK):
        ei = s * N_EXPERTS + e * MOE_EBLK + j
        off = off_ref[ei]
        cnt = cnt_ref[ei]

        def window(w, carry, j=j, off=off, cnt=cnt):
            r0 = off + w * MOE_WIN
            rows = pl.ds(pl.multiple_of(r0 * SUBLANES, SUBLANES), MOE_WIN * SUBLANES)
            xw = xs_ref[rows, :].reshape(MOE_WIN, n_feat, LANES).reshape(MOE_WIN, D_MODEL)
            xb = xw.astype(BF16)
            hg = _dot(xb, eg_ref[j])
            hu = _dot(xb, eu_ref[j])
            hh = hg * _sigmoid(hg) * hu
            yw = _dot(hh.astype(BF16), ed_ref[j])
            yw = jnp.where(ridx + w * MOE_WIN < cnt, yw, xw)
            xs_ref[rows, :] = yw.reshape(MOE_WIN, n_feat, LANES).reshape(MOE_WIN * SUBLANES, LANES)
            return carry

        lax.fori_loop(0, (cnt + MOE_WIN - 1) // MOE_WIN, window, 0)

    @pl.when(e == N_EXPERTS // MOE_EBLK - 1)
    def _():
        def combine(i, carry):
            t0 = pl.multiple_of(i * SUBLANES, SUBLANES)
            tiles = []
            for j in range(SUBLANES):
                y1 = xs_ref[tile_rows(pos_ref[0, 0, 2 * (t0 + j)]), :]
                y2 = xs_ref[tile_rows(pos_ref[0, 0, 2 * (t0 + j) + 1]), :]
                tiles.append(w_ref[0, 0, 2 * (t0 + j)] * y1 + w_ref[0, 0, 2 * (t0 + j) + 1] * y2)
            o_ref[pl.ds(t0, SUBLANES), :] = jnp.stack(tiles, axis=0).reshape(SUBLANES, D_MODEL)
            return carry
        lax.fori_loop(0, tn // SUBLANES, combine, 0)
        for j in range(tn // RANK_BLK):
            rs = slice(j * RANK_BLK, (j + 1) * RANK_BLK)
            o_ref[rs, :] = _layer_norm(DEEPNORM_ALPHA * x_ref[rs, :] + o_ref[rs, :], lng_ref[...], lnb_ref[...])


def _moe_sparse(x, lp):
    t = x.shape[0]
    tn = min(MOE_SLAB, t)
    n_slabs = t // tn
    w, pos, seg = _route(x, lp)
    w_flat = w[:, :2].reshape(n_slabs, 1, 2 * tn)
    pos_flat = pos[:, :2].reshape(n_slabs, 1, 2 * tn)
    off = seg[:, 0, :N_EXPERTS].reshape(-1)
    cnt = seg[:, 1, :N_EXPERTS].reshape(-1)
    smem_row = pl.BlockSpec((1, 1, 2 * tn), lambda s, e, *_: (s, 0, 0), memory_space=pltpu.SMEM)
    slab = pl.BlockSpec((tn, D_MODEL), lambda s, e, *_: (s, 0), pipeline_mode=pl.Buffered(1))
    const = lambda p: pl.BlockSpec(p.shape, lambda s, e, *_: (0,) * p.ndim)
    grid_spec = pltpu.PrefetchScalarGridSpec(
        num_scalar_prefetch=2,
        grid=(n_slabs, N_EXPERTS // MOE_EBLK),
        in_specs=[smem_row, smem_row, slab,
                  pl.BlockSpec((MOE_EBLK, D_MODEL, D_EXPERT), lambda s, e, *_: (e, 0, 0)),
                  pl.BlockSpec((MOE_EBLK, D_MODEL, D_EXPERT), lambda s, e, *_: (e, 0, 0)),
                  pl.BlockSpec((MOE_EBLK, D_EXPERT, D_MODEL), lambda s, e, *_: (e, 0, 0)),
                  const(lp["ln_g1"]), const(lp["ln_b1"])],
        out_specs=slab,
        scratch_shapes=[pltpu.VMEM((MOE_ROWS * SUBLANES, LANES), F32)],
    )
    return pl.pallas_call(
        _moe_sparse_kernel,
        grid_spec=grid_spec,
        out_shape=jax.ShapeDtypeStruct((t, D_MODEL), F32),
        compiler_params=_cparams("arbitrary", "arbitrary"),
        name="moe_sparse",
    )(off, cnt, pos_flat, w_flat, x, lp["e_gate"], lp["e_up"], lp["e_down"], lp["ln_g1"], lp["ln_b1"])


def _ple_kernel(x_ref, p_ref, wp_ref, wg_ref, lng_ref, lnb_ref, o_ref):
    x = x_ref[...]
    emb = _dot(p_ref[...].astype(BF16), wp_ref[...])
    gate = _sigmoid(_dot(x.astype(BF16), wg_ref[...]))
    o_ref[...] = _layer_norm(DEEPNORM_ALPHA * x + emb * gate, lng_ref[...], lnb_ref[...])


def _ple(x, p, lp):
    t = x.shape[0]
    tm = min(512, t)
    d_ple = p.shape[1]
    row = lambda w: pl.BlockSpec((tm, w), lambda i: (i, 0))
    params = [lp["w_ple"], lp["w_pg"], lp["ln_g2"], lp["ln_b2"]]
    return pl.pallas_call(
        _ple_kernel,
        grid=(t // tm,),
        in_specs=[row(D_MODEL), row(d_ple)] + [_full(q.shape) for q in params],
        out_specs=row(D_MODEL),
        out_shape=jax.ShapeDtypeStruct((t, D_MODEL), F32),
        compiler_params=_cparams("parallel"),
        name="ple",
    )(x, p, *params)


def _pad_to(a, axis, size):
    pad = [(0, 0)] * a.ndim
    pad[axis] = (0, size - a.shape[axis])
    return jnp.pad(a, pad)


def _row(v):
    return v.reshape(1, -1).astype(F32)


def _layer_params(i, w_in, tok_mix, decay_base, decay_up, aaa_base, aaa_up, gate_up, k_k, k_a, r_k,
                  vres_base, vres_down, vres_up, gn_g, gn_b, w_branch_attn, w_branch_rwkv, w_out,
                  router_grp, router_grp_bias, router_exp, router_exp_bias, exp_gate, exp_up, exp_down,
                  ple_proj, ple_gate, ln_g, ln_b, consts):
    d = D_HEADS
    w = w_in[i]
    rw0 = 3 * d
    gate0 = rw0 + 3 * d + DECAY_LORA + AAA_LORA + GATE_LORA
    wqkv = jnp.concatenate([w[:, 0:d] * (HEAD_DIM ** -0.5), w[:, d:rw0]], axis=1).astype(BF16)
    wr = w[:, rw0:gate0]
    lora0 = 3 * d
    pieces = [wr[:, :lora0],
              _pad_to(wr[:, lora0:lora0 + DECAY_LORA], 1, LANES),
              _pad_to(wr[:, lora0 + DECAY_LORA:lora0 + DECAY_LORA + AAA_LORA], 1, LANES),
              _pad_to(wr[:, lora0 + DECAY_LORA + AAA_LORA:], 1, 2 * LANES)]
    wrw = jnp.concatenate(pieces, axis=1).astype(BF16)
    mu = tok_mix[i]
    mu_pieces = [mu[:lora0],
                 _pad_to(mu[lora0:lora0 + DECAY_LORA], 0, LANES),
                 _pad_to(mu[lora0 + DECAY_LORA:lora0 + DECAY_LORA + AAA_LORA], 0, LANES),
                 _pad_to(mu[lora0 + DECAY_LORA + AAA_LORA:], 0, 2 * LANES)]
    lp = dict(consts)
    lp.update(
        wqkv=wqkv, wrw=wrw, wgate=w[:, gate0:].astype(BF16),
        mu=_row(jnp.concatenate(mu_pieces)),
        decay_base=_row(decay_base[i]), decay_up=_pad_to(decay_up[i], 0, LANES).astype(F32),
        aaa_base=_row(aaa_base[i]), aaa_up=_pad_to(aaa_up[i], 0, LANES).astype(F32),
        gate_up=_pad_to(gate_up[i], 0, 2 * LANES).astype(F32),
        k_k=_row(k_k[i]), k_a=_row(k_a[i]), r_k=_row(r_k[i]),
        gn_g=_row(gn_g[i]), gn_b=_row(gn_b[i]),
        w_a=w_branch_attn[i].astype(BF16), w_b=w_branch_rwkv[i].astype(BF16), w_o=w_out[i].astype(BF16),
        w_route=_pad_to(jnp.concatenate([router_exp[i], router_grp[i]], axis=1), 1, LANES).astype(F32),
        b_route=_row(_pad_to(jnp.concatenate([router_exp_bias[i], router_grp_bias[i]]), 0, LANES)),
        e_gate=exp_gate[i].astype(BF16), e_up=exp_up[i].astype(BF16), e_down=exp_down[i].astype(BF16),
        w_ple=ple_proj[i].astype(BF16), w_pg=ple_gate[i].astype(BF16),
        ln_g0=_row(ln_g[i, 0]), ln_b0=_row(ln_b[i, 0]),
        ln_g1=_row(ln_g[i, 1]), ln_b1=_row(ln_b[i, 1]),
        ln_g2=_row(ln_g[i, 2]), ln_b2=_row(ln_b[i, 2]),
    )
    if i > 0:
        lp.update(vres_base=_row(vres_base[i - 1]),
                  vres_down=_pad_to(vres_down[i - 1], 1, LANES).astype(F32),
                  vres_up=_pad_to(vres_up[i - 1], 0, LANES).astype(F32))
    return lp


def _const_mats(tm):
    head = jnp.arange(D_HEADS) // HEAD_DIM
    e_ones = (head[:, None] == head[None, :]).astype(BF16)
    tok = jnp.arange(tm)
    same = (tok[:, None] // CHUNK) == (tok[None, :] // CHUNK)
    tri = (same & (tok[:, None] >= tok[None, :])).astype(BF16)
    blk = same.astype(BF16)
    rtok = jnp.arange(RANK_BLK)
    tri_rank = (rtok[None, :] < rtok[:, None]).astype(BF16)
    return dict(e_ones=e_ones, tri=tri, blk=blk, tri_rank=tri_rank)


def kernel(x, p, ln_in_g, ln_in_b, rel_bias, w_in, tok_mix, decay_base, decay_up, aaa_base, aaa_up,
           gate_up, k_k, k_a, r_k, vres_base, vres_down, vres_up, gn_g, gn_b, w_branch_attn,
           w_branch_rwkv, w_out, router_grp, router_grp_bias, router_exp, router_exp_bias, exp_gate,
           exp_up, exp_down, ple_proj, ple_gate, ln_g, ln_b):
    batch, seq, _ = x.shape
    t = batch * seq
    consts = _const_mats(256)
    bias_tabs = _attn_bias_tables(rel_bias)
    xt = x.reshape(t, D_MODEL)
    pt = p.reshape(p.shape[0], t, p.shape[-1])
    v_first = None
    for i in range(DEPTH):
        lp = _layer_params(i, w_in, tok_mix, decay_base, decay_up, aaa_base, aaa_up, gate_up, k_k, k_a,
                           r_k, vres_base, vres_down, vres_up, gn_g, gn_b, w_branch_attn, w_branch_rwkv,
                           w_out, router_grp, router_grp_bias, router_exp, router_exp_bias, exp_gate,
                           exp_up, exp_down, ple_proj, ple_gate, ln_g, ln_b, consts)
        if i == 0:
            xt, qkv, rw, gate = _proj(xt, _row(ln_in_g), _row(ln_in_b), lp["wqkv"], lp["wrw"],
                                      lp["wgate"], apply_ln=True)
        else:
            qkv, rw, gate = _proj(xt, _row(ln_in_g), _row(ln_in_b), lp["wqkv"], lp["wrw"],
                                  lp["wgate"], apply_ln=False)
        y_attn = _attention(qkv, bias_tabs, batch, seq)
        if i == 0:
            q1, y0, m, n0, g, bonus, v_first = _rwkv_chunk(rw, lp, seq, None)
        else:
            q1, y0, m, n0, g, bonus = _rwkv_chunk(rw, lp, seq, v_first)
        y = _rwkv_scan(q1, y0, m, n0, batch, seq)
        xt = _mix(y, g, bonus, y_attn, gate, xt, lp)
        xt = _moe_sparse(xt, lp)
        xt = _ple(xt, pt[i], lp)
    return xt.reshape(batch, seq, D_MODEL)
```

```python
import functools
import math

import jax
import jax.numpy as jnp
from jax import lax
from jax.experimental import pallas as pl
from jax.experimental.pallas import tpu as pltpu

F32 = jnp.float32
BF16 = jnp.bfloat16

D_MODEL = 1024
CHUNK = 64
N_PREV_CHUNKS = 8
HEAD_DIM = 64
N_HEADS = 8
D_HEADS = N_HEADS * HEAD_DIM
REL_CLIP = 128
DECAY_LORA = 64
AAA_LORA = 64
GATE_LORA = 160
MV_LORA = 32
N_GROUPS = 4
EXPERTS_PER_GROUP = 8
N_EXPERTS = N_GROUPS * EXPERTS_PER_GROUP
D_EXPERT = 256
DEPTH = 2
DEEPNORM_ALPHA = (2 * DEPTH) ** 0.25
LN_EPS = 1e-5
GN_EPS = 64e-5
NEG_INF = -1e30

LANES = 128
PAIR = 2 * HEAD_DIM
ATTN_QBLK = 2 * CHUNK
ATTN_WIN = (N_PREV_CHUNKS + 2) * CHUNK
RW_COLS_PAD = 2048
VMEM_LIMIT = 56 * 1024 * 1024


def _cparams(*sem):
    return pltpu.CompilerParams(dimension_semantics=sem, vmem_limit_bytes=VMEM_LIMIT)


def _dot(a, b):
    return jnp.dot(a, b, preferred_element_type=F32)


def _dot_nt(a, b):
    return lax.dot_general(a, b, (((1,), (1,)), ((), ())), preferred_element_type=F32)


def _dot_tn(a, b):
    return lax.dot_general(a, b, (((0,), (0,)), ((), ())), preferred_element_type=F32)


def _split2(x):
    hi = x.astype(BF16)
    lo = (x - hi.astype(F32)).astype(BF16)
    return hi, lo


def _dot3(a, b):
    ah, al = _split2(a)
    bh, bl = _split2(b)
    return _dot(ah, bh) + (_dot(al, bh) + _dot(ah, bl))


def _dot_exact_lhs(a_bf16, x):
    x1 = x.astype(BF16)
    r1 = x - x1.astype(F32)
    x2 = r1.astype(BF16)
    x3 = (r1 - x2.astype(F32)).astype(BF16)
    return _dot(a_bf16, x1) + (_dot(a_bf16, x2) + _dot(a_bf16, x3))


def _seg_sum(x, e_ones):
    hi, lo = _split2(x)
    return _dot(hi, e_ones) + _dot(lo, e_ones)


def _layer_norm(x, g, b):
    mu = jnp.mean(x, axis=-1, keepdims=True)
    xc = x - mu
    var = jnp.mean(xc * xc, axis=-1, keepdims=True)
    return xc * lax.rsqrt(var + LN_EPS) * g + b


def _sigmoid(x):
    return 1.0 / (1.0 + jnp.exp(-x))


def _full(shape):
    nd = len(shape)
    return pl.BlockSpec(shape, lambda *_: (0,) * nd)


def _proj_kernel(x_ref, g_ref, b_ref, wqkv_ref, wrw_ref, wgate_ref, *out_refs, apply_ln):
    x = x_ref[...]
    if apply_ln:
        xn_ref, qkv_ref, rw_ref, gate_ref = out_refs
        x = _layer_norm(x, g_ref[...], b_ref[...])
        xn_ref[...] = x
    else:
        qkv_ref, rw_ref, gate_ref = out_refs
    xb = x.astype(BF16)
    qkv_ref[...] = _dot(xb, wqkv_ref[...]).astype(BF16)
    rw_ref[...] = _dot(xb, wrw_ref[...])
    gate_ref[...] = _sigmoid(_dot(xb, wgate_ref[...])).astype(BF16)


def _proj(x, g, b, wqkv, wrw, wgate, apply_ln):
    t = x.shape[0]
    tm = min(256, t)
    row = lambda w: pl.BlockSpec((tm, w), lambda i: (i, 0))
    out_shape = [jax.ShapeDtypeStruct((t, 3 * D_HEADS), BF16),
                 jax.ShapeDtypeStruct((t, RW_COLS_PAD), F32),
                 jax.ShapeDtypeStruct((t, 2 * D_MODEL), BF16)]
    out_specs = [row(3 * D_HEADS), row(RW_COLS_PAD), row(2 * D_MODEL)]
    if apply_ln:
        out_shape = [jax.ShapeDtypeStruct((t, D_MODEL), F32)] + out_shape
        out_specs = [row(D_MODEL)] + out_specs
    return pl.pallas_call(
        functools.partial(_proj_kernel, apply_ln=apply_ln),
        grid=(t // tm,),
        in_specs=[row(D_MODEL), _full(g.shape), _full(b.shape),
                  _full(wqkv.shape), _full(wrw.shape), _full(wgate.shape)],
        out_specs=out_specs,
        out_shape=out_shape,
        compiler_params=_cparams("parallel"),
        name="proj_ln" if apply_ln else "proj",
    )(x, g, b, wqkv, wrw, wgate)


def _attn_kernel(q_ref, k_ref, v_ref, bias_ref, o_ref):
    qb = pl.program_id(1)
    ws = pl.multiple_of(jnp.maximum(qb * ATTN_QBLK - N_PREV_CHUNKS * CHUNK, 0), ATTN_QBLK)
    q = q_ref[0]
    kw = k_ref[0, pl.ds(ws, ATTN_WIN), :]
    vw = v_ref[0, pl.ds(ws, ATTN_WIN), :]
    left = lax.broadcasted_iota(jnp.int32, (ATTN_QBLK, PAIR), 1) < HEAD_DIM
    zero = jnp.zeros((ATTN_QBLK, PAIR), BF16)
    heads = [(h // 2, h % 2) for h in range(N_HEADS)]
    spans = [slice(p * PAIR, (p + 1) * PAIR) for p in range(N_HEADS // 2)]
    qm = [jnp.where(left, q[:, spans[p]], zero) if side == 0 else jnp.where(left, zero, q[:, spans[p]])
          for p, side in heads]
    s = [_dot_nt(qm[h], kw[:, spans[p]]) + bias_ref[0, h] for h, (p, _) in enumerate(heads)]
    m = [jnp.max(x, axis=-1, keepdims=True) for x in s]
    e = [jnp.exp(x - mx) for x, mx in zip(s, m)]
    inv = [1.0 / jnp.sum(x, axis=-1, keepdims=True) for x in e]
    o = [_dot(x.astype(BF16), vw[:, spans[p]]) * r for x, r, (p, _) in zip(e, inv, heads)]
    pairs = [jnp.where(left, o[2 * p], o[2 * p + 1]) for p in range(N_HEADS // 2)]
    o_ref[0] = jnp.concatenate(pairs, axis=-1).astype(BF16)


def _attention(qkv, bias_tabs, batch, seq):
    qkv3 = qkv.reshape(batch, seq, 3 * D_HEADS)
    n_tabs = bias_tabs.shape[0]
    out = pl.pallas_call(
        _attn_kernel,
        grid=(batch, seq // ATTN_QBLK),
        in_specs=[
            pl.BlockSpec((1, ATTN_QBLK, D_HEADS), lambda b, i: (b, i, 0)),
            pl.BlockSpec((1, seq, D_HEADS), lambda b, i: (b, 0, 1)),
            pl.BlockSpec((1, seq, D_HEADS), lambda b, i: (b, 0, 2)),
            pl.BlockSpec((1, N_HEADS, ATTN_QBLK, ATTN_WIN),
                         lambda b, i: (jnp.minimum(i, n_tabs - 1), 0, 0, 0)),
        ],
        out_specs=pl.BlockSpec((1, ATTN_QBLK, D_HEADS), lambda b, i: (b, i, 0)),
        out_shape=jax.ShapeDtypeStruct((batch, seq, D_HEADS), BF16),
        compiler_params=_cparams("parallel", "arbitrary"),
        name="band_attn",
    )(qkv3, qkv3, qkv3, bias_tabs)
    return out.reshape(batch * seq, D_HEADS)


def _attn_bias_tables(rel_bias):
    n_tabs = N_PREV_CHUNKS * CHUNK // ATTN_QBLK + 1
    start = (jnp.arange(n_tabs) * ATTN_QBLK)[:, None, None]
    qpos = start + jnp.arange(ATTN_QBLK)[None, :, None]
    kpos = jnp.arange(ATTN_WIN)[None, None, :]
    qc, kc = qpos // CHUNK, kpos // CHUNK
    valid = (kc <= qc) & (kc >= qc - N_PREV_CHUNKS)
    n = jnp.arange(ATTN_QBLK - 1 + ATTN_WIN)
    dist = start[:, :, 0] + (ATTN_QBLK - 1) - n[None, :]
    vals = rel_bias.astype(F32)[:, jnp.clip(dist, -REL_CLIP, REL_CLIP) + REL_CLIP]
    length = ATTN_QBLK - 1 + ATTN_WIN
    tiled = jnp.tile(_pad_to(vals, 2, length + 1), (1, 1, ATTN_QBLK))
    cut = tiled[:, :, :ATTN_QBLK * length].reshape(N_HEADS, n_tabs, ATTN_QBLK, length)
    bias = cut[:, :, :, ATTN_QBLK - 1:ATTN_QBLK - 1 + ATTN_WIN]
    bias = jnp.where(valid[None], bias, NEG_INF)
    return jnp.transpose(bias, (1, 0, 2, 3))


def _rwkv_chunk_kernel(*refs, seq, tm, has_vres):
    if has_vres:
        (rw_ref, prev_ref, mu_ref, dbase_ref, dup_ref, abase_ref, aup_ref, gup_ref, kk_ref, ka_ref,
         rk_ref, eones_ref, tri_ref, blk_ref, vfirst_ref, vbase_ref, vdown_ref, vup_ref,
         q1_ref, y0_ref, m_ref, n0_ref, g_ref, bonus_ref,
         rw_s, aw_s, bi_s, ki_s, be_s, ke_s, v_s, wc_s) = refs
    else:
        (rw_ref, prev_ref, mu_ref, dbase_ref, dup_ref, abase_ref, aup_ref, gup_ref, kk_ref, ka_ref,
         rk_ref, eones_ref, tri_ref, blk_ref,
         q1_ref, y0_ref, m_ref, n0_ref, g_ref, bonus_ref, vfirst_out_ref,
         rw_s, aw_s, bi_s, ki_s, be_s, ke_s, v_s, wc_s) = refs

    cols = rw_ref[...]
    first = (pl.program_id(0) % (seq // tm)) == 0
    prev_row = jnp.where(first, 0.0, prev_ref[7:8, :])
    rows = lax.broadcasted_iota(jnp.int32, cols.shape, 0)
    prev = jnp.where(rows == 0, prev_row, pltpu.roll(cols, 1, axis=0))
    xs = cols + (prev - cols) * mu_ref[...]

    d = D_HEADS
    r, k, v = xs[:, 0:d], xs[:, d:2 * d], xs[:, 2 * d:3 * d]
    wd = xs[:, 3 * d:3 * d + LANES]
    ad = xs[:, 3 * d + LANES:3 * d + 2 * LANES]
    gd = xs[:, 3 * d + 2 * LANES:3 * d + 4 * LANES]
    e_ones = eones_ref[...]

    u = dbase_ref[...] + _dot3(jnp.tanh(wd), dup_ref[...])
    w_log = jnp.minimum(u, 0.0) - jnp.log(1.0 + jnp.exp(-jnp.abs(u))) - 0.5
    logw = -jnp.exp(w_log)
    a_icl = _sigmoid(abase_ref[...] + _dot3(ad, aup_ref[...]))
    g_ref[...] = _dot3(_sigmoid(gd), gup_ref[...])
    kk = k * kk_ref[...]
    kk = kk * lax.rsqrt(jnp.maximum(_seg_sum(kk * kk, e_ones), 1e-24))
    k2 = k * (1.0 + (a_icl - 1.0) * ka_ref[...])
    if has_vres:
        mix = _sigmoid(vbase_ref[...] + _dot3(_dot3(v, vdown_ref[...]), vup_ref[...]))
        v2 = v + (vfirst_ref[...] - v) * mix
    else:
        v2 = v
        vfirst_out_ref[...] = v
    a_vec = -kk
    b_vec = kk * a_icl
    bonus_ref[...] = _seg_sum(r * k2 * rk_ref[...], e_ones) * v2

    cw = _dot_exact_lhs(tri_ref[...], logw)
    cw_tot = _dot_exact_lhs(blk_ref[...], logw)
    rw_s[...] = r * jnp.exp(cw)
    aw_s[...] = a_vec * jnp.exp(cw - logw)
    inv = jnp.exp(-cw)
    bi_s[...] = b_vec * inv
    ki_s[...] = k2 * inv
    rest = jnp.exp(cw_tot - cw)
    be_s[...] = b_vec * rest
    ke_s[...] = k2 * rest
    v_s[...] = v2
    wc_s[...] = jnp.exp(cw_tot)

    shape = (CHUNK, PAIR)
    ti = lax.broadcasted_iota(jnp.int32, shape, 0)
    li = lax.broadcasted_iota(jnp.int32, shape, 1)
    si = li & (HEAD_DIM - 1)
    left = li < HEAD_DIM
    strict = ti > si
    incl = ti >= si
    eye = ti == si

    def bd(x):
        z = jnp.zeros_like(x)
        return jnp.concatenate([jnp.where(left, x, z), jnp.where(left, z, x)], axis=0)

    def fold(x):
        return jnp.where(left, x[:CHUNK], x[CHUNK:])

    def b16(xs):
        return [x.astype(BF16) for x in xs]

    def summaries(insts):
        sl = [(slice(c * CHUNK, (c + 1) * CHUNK), slice(p * PAIR, (p + 1) * PAIR)) for c, p in insts]
        rw = [rw_s[s] for s in sl]
        awb = b16([aw_s[s] for s in sl])
        vb = b16([v_s[s] for s in sl])
        lhs = [jnp.concatenate([a, r.astype(BF16)], axis=0) for a, r in zip(awb, rw)]
        bik = [jnp.concatenate([bd(b), bd(k)], axis=0)
               for b, k in zip(b16([bi_s[s] for s in sl]), b16([ki_s[s] for s in sl]))]
        g = [_dot_nt(l, x) for l, x in zip(lhs, bik)]
        a_ab = [jnp.where(strict, x[:CHUNK, :PAIR], 0.0) for x in g]
        a_rb = b16([jnp.where(incl, x[CHUNK:, :PAIR], 0.0) for x in g])
        a_ak = b16([jnp.where(strict, x[:CHUNK, PAIR:], 0.0) for x in g])
        a_rk = b16([jnp.where(incl, x[CHUNK:, PAIR:], 0.0) for x in g])
        bdv = [bd(x) for x in vb]
        x0 = [_dot(a, v) for a, v in zip(a_ak, bdv)]
        lp = a_ab
        tinv = [jnp.where(eye, 1.0, a) for a in a_ab]
        for _ in range(5):
            lpb = b16(lp)
            lp = [_dot(l, bd(l)) for l in lpb]
            tinv = [t + _dot(t.astype(BF16), bd(l.astype(BF16))) for t, l in zip(tinv, lp)]
        tb = b16(tinv)
        pu = b16([_dot(t, jnp.concatenate([bd(a), bd(x.astype(BF16))], axis=1))
                  for t, a, x in zip(tb, awb, x0)])
        qy = [_dot(a, jnp.concatenate([bd(x[:, :PAIR]), bd(x[:, PAIR:])], axis=1)) for a, x in zip(a_rb, pu)]
        q1 = [r + x[:, :PAIR] for r, x in zip(rw, qy)]
        y0 = [x[:, PAIR:] + _dot(k, v) for x, k, v in zip(qy, a_rk, bdv)]
        beb = b16([be_s[s] for s in sl])
        keb = b16([ke_s[s] for s in sl])
        mn = [_dot_tn(b, x) for b, x in zip(beb, pu)]
        mm = [fold(x[:, :PAIR]) for x in mn]
        nn = [fold(x[:, PAIR:] + _dot_tn(k, v)) for x, k, v in zip(mn, keb, vb)]
        for i, (s, (c, p)) in enumerate(zip(sl, insts)):
            wc = wc_s[c * CHUNK:c * CHUNK + 1, s[1]]
            q1_ref[s] = q1[i]
            y0_ref[s] = y0[i]
            m_ref[s] = jnp.where(eye, wc, 0.0) + mm[i]
            n0_ref[s] = nn[i]

    n_pairs = D_HEADS // PAIR
    summaries([(c, p) for c in range(tm // CHUNK) for p in range(n_pairs)])


def _rwkv_chunk(rw, lp, seq, v_first):
    t = rw.shape[0]
    tm = 256
    has_vres = v_first is not None
    nct = tm // CHUNK
    row = lambda w: pl.BlockSpec((tm, w), lambda i: (i, 0))
    prev_spec = pl.BlockSpec((8, RW_COLS_PAD), lambda i: (jnp.maximum(i * (tm // 8) - 1, 0), 0))
    params = [lp["mu"], lp["decay_base"], lp["decay_up"], lp["aaa_base"], lp["aaa_up"], lp["gate_up"],
              lp["k_k"], lp["k_a"], lp["r_k"], lp["e_ones"], lp["tri"], lp["blk"]]
    inputs = [rw, rw] + params
    in_specs = [row(RW_COLS_PAD), prev_spec] + [_full(p.shape) for p in params]
    if has_vres:
        extra = [lp["vres_base"], lp["vres_down"], lp["vres_up"]]
        inputs += [v_first] + extra
        in_specs += [row(D_HEADS)] + [_full(p.shape) for p in extra]
    tok_shape = jax.ShapeDtypeStruct((t, D_HEADS), F32)
    out_shape = [tok_shape] * 6
    out_specs = [row(D_HEADS)] * 6
    if not has_vres:
        out_shape.append(tok_shape)
        out_specs.append(row(D_HEADS))
    scratch = [pltpu.VMEM((tm, D_HEADS), F32) for _ in range(8)]
    return pl.pallas_call(
        functools.partial(_rwkv_chunk_kernel, seq=seq, tm=tm, has_vres=has_vres),
        grid=(t // tm,),
        in_specs=in_specs,
        out_specs=out_specs,
        out_shape=out_shape,
        scratch_shapes=scratch,
        compiler_params=_cparams("parallel"),
        name="rwkv_chunk_vres" if has_vres else "rwkv_chunk",
    )(*inputs)


def _rwkv_scan_kernel(q1_ref, y0_ref, m_ref, n0_ref, y_ref, st_ref, *, n_chunks):
    @pl.when(pl.program_id(1) == 0)
    def _():
        st_ref[...] = jnp.zeros_like(st_ref)

    left = lax.broadcasted_iota(jnp.int32, (CHUNK, PAIR), 1) < HEAD_DIM

    def bd(x):
        z = jnp.zeros_like(x)
        return jnp.concatenate([jnp.where(left, x, z), jnp.where(left, z, x)], axis=0)

    n_pairs = D_HEADS // PAIR
    st = [st_ref[:, p * PAIR:(p + 1) * PAIR] for p in range(n_pairs)]
    for c in range(n_chunks):
        rs = slice(c * CHUNK, (c + 1) * CHUNK)
        for p in range(n_pairs):
            ls = slice(p * PAIR, (p + 1) * PAIR)
            lh, ll = _split2(jnp.concatenate([q1_ref[rs, ls], m_ref[rs, ls]], axis=0))
            sh, sl = _split2(st[p])
            bh = bd(sh)
            res = _dot(lh, bh) + (_dot(ll, bh) + _dot(lh, bd(sl)))
            y_ref[rs, ls] = res[:CHUNK] + y0_ref[rs, ls]
            st[p] = res[CHUNK:] + n0_ref[rs, ls]
    for p in range(n_pairs):
        st_ref[:, p * PAIR:(p + 1) * PAIR] = st[p]


def _rwkv_scan(q1, y0, m, n0, batch, seq):
    t = q1.shape[0]
    cb = 8
    steps = seq // (cb * CHUNK)
    tok_spec = pl.BlockSpec((cb * CHUNK, D_HEADS), lambda b, j: (b * steps + j, 0))
    return pl.pallas_call(
        functools.partial(_rwkv_scan_kernel, n_chunks=cb),
        grid=(batch, steps),
        in_specs=[tok_spec] * 4,
        out_specs=tok_spec,
        out_shape=jax.ShapeDtypeStruct((t, D_HEADS), F32),
        scratch_shapes=[pltpu.VMEM((HEAD_DIM, D_HEADS), F32)],
        compiler_params=_cparams("parallel", "arbitrary"),
        name="rwkv_scan",
    )(q1, y0, m, n0)


def _mix_kernel(y_ref, g_ref, bonus_ref, ya_ref, gate_ref, x_ref, wa_ref, wb_ref, wo_ref,
                eones_ref, gng_ref, gnb_ref, lng_ref, lnb_ref, o_ref):
    e_ones = eones_ref[...]
    y = y_ref[...]
    mean = _seg_sum(y, e_ones) * (1.0 / HEAD_DIM)
    yc = y - mean
    var = _seg_sum(yc * yc, e_ones) * (1.0 / HEAD_DIM)
    yn = yc * lax.rsqrt(var + GN_EPS) * gng_ref[...] + gnb_ref[...]
    yr = (yn + bonus_ref[...]) * g_ref[...]
    za = _dot(ya_ref[...], wa_ref[...])
    zb = _dot(yr.astype(BF16), wb_ref[...])
    gate = gate_ref[...]
    mixed_in = gate[:, :D_MODEL].astype(F32) * za + gate[:, D_MODEL:].astype(F32) * zb
    mixed = _dot(mixed_in.astype(BF16), wo_ref[...])
    o_ref[...] = _layer_norm(DEEPNORM_ALPHA * x_ref[...] + mixed, lng_ref[...], lnb_ref[...])


def _mix(y, g, bonus, y_attn, gate, x, lp):
    t = x.shape[0]
    tm = min(256, t)
    row = lambda w: pl.BlockSpec((tm, w), lambda i: (i, 0))
    params = [lp["w_a"], lp["w_b"], lp["w_o"], lp["e_ones"], lp["gn_g"], lp["gn_b"], lp["ln_g0"], lp["ln_b0"]]
    return pl.pallas_call(
        _mix_kernel,
        grid=(t // tm,),
        in_specs=[row(D_HEADS), row(D_HEADS), row(D_HEADS), row(D_HEADS), row(2 * D_MODEL), row(D_MODEL)]
        + [_full(p.shape) for p in params],
        out_specs=row(D_MODEL),
        out_shape=jax.ShapeDtypeStruct((t, D_MODEL), F32),
        compiler_params=_cparams("parallel"),
        name="branch_mix",
    )(y, g, bonus, y_attn, gate, x, *params)


MOE_SLAB = 2048
MOE_WIN = 128
MOE_EBLK = 4
MOE_ROWS = 2 * MOE_SLAB + MOE_WIN
SUBLANES = 8
RANK_BLK = 256
ROUTE_COLS = 8


def _route_kernel(x_ref, wr_ref, br_ref, tri_ref, w_ref, pos_ref, seg_ref):
    x = x_ref[...]
    tn = x.shape[0]
    lane = lax.broadcasted_iota(jnp.int32, (tn, LANES), 1)
    lanef = lane.astype(F32)
    hp = lax.Precision.HIGHEST
    elog = _dot3(x, wr_ref[...]) + br_ref[...]
    big = float(LANES)
    glog = jnp.where((lane >= N_EXPERTS) & (lane < N_EXPERTS + N_GROUPS), elog, -jnp.inf)
    gmax = jnp.max(glog, axis=-1, keepdims=True)
    g_gate = 1.0 / jnp.sum(jnp.exp(glog - gmax), axis=-1, keepdims=True)
    grp = jnp.min(jnp.where(glog == gmax, lanef, big), axis=-1, keepdims=True) - N_EXPERTS
    in_grp = (lanef >= grp * EXPERTS_PER_GROUP) & (lanef < (grp + 1.0) * EXPERTS_PER_GROUP)
    sel = jnp.where(in_grp, elog, -jnp.inf)
    v1 = jnp.max(sel, axis=-1, keepdims=True)
    i1 = jnp.min(jnp.where(sel == v1, lanef, big), axis=-1, keepdims=True)
    sel2 = jnp.where(lanef == i1, -jnp.inf, sel)
    v2 = jnp.max(sel2, axis=-1, keepdims=True)
    i2 = jnp.min(jnp.where(sel2 == v2, lanef, big), axis=-1, keepdims=True)
    e2 = jnp.exp(v2 - v1)
    w1 = g_gate / (1.0 + e2)
    w2 = g_gate * e2 / (1.0 + e2)
    w_ref[...] = jnp.where(lane == 0, w1, jnp.where(lane == 1, w2, 0.0))[:, :ROUTE_COLS]

    hit1 = lanef == i1
    hit2 = lanef == i2
    onehot = jnp.where(hit1 | hit2, 1.0, 0.0)
    tri = tri_ref[...]
    carry = jnp.zeros((1, LANES), F32)
    ranks = []
    for j in range(tn // RANK_BLK):
        blk = onehot[j * RANK_BLK:(j + 1) * RANK_BLK]
        ranks.append(_dot(tri, blk.astype(BF16)) + carry)
        carry = carry + jnp.sum(blk, axis=0, keepdims=True)
    rank = jnp.concatenate(ranks, axis=0)
    cnt = carry
    li = lax.broadcasted_iota(jnp.int32, (LANES, LANES), 0)
    lj = lax.broadcasted_iota(jnp.int32, (LANES, LANES), 1)
    before = jnp.where(li < lj, 1.0, 0.0)
    off = jnp.dot(jnp.broadcast_to(cnt, (8, LANES)), before, precision=hp,
                  preferred_element_type=F32)[0:1]
    slot = off + rank
    pos1 = jnp.sum(jnp.where(hit1, slot, 0.0), axis=-1, keepdims=True)
    pos2 = jnp.sum(jnp.where(hit2, slot, 0.0), axis=-1, keepdims=True)
    pos_ref[...] = jnp.where(lane == 0, pos1, jnp.where(lane == 1, pos2, 0.0)).astype(jnp.int32)[:, :ROUTE_COLS]
    row8 = lax.broadcasted_iota(jnp.int32, (8, LANES), 0)
    seg_ref[0] = jnp.where(row8 == 0, off, jnp.where(row8 == 1, cnt, 0.0)).astype(jnp.int32)


def _route(x, lp):
    t = x.shape[0]
    tn = min(MOE_SLAB, t)
    n_slabs = t // tn
    small = [lp["w_route"], lp["b_route"], lp["tri_rank"]]
    row = lambda w: pl.BlockSpec((tn, w), lambda i: (i, 0))
    return pl.pallas_call(
        _route_kernel,
        grid=(n_slabs,),
        in_specs=[row(D_MODEL)] + [_full(p.shape) for p in small],
        out_specs=[row(ROUTE_COLS), row(ROUTE_COLS), pl.BlockSpec((1, 8, LANES), lambda i: (i, 0, 0))],
        out_shape=[jax.ShapeDtypeStruct((t, ROUTE_COLS), F32), jax.ShapeDtypeStruct((t, ROUTE_COLS), jnp.int32),
                   jax.ShapeDtypeStruct((n_slabs, 8, LANES), jnp.int32)],
        compiler_params=_cparams("parallel"),
        name="moe_route",
    )(x, *small)


def _moe_sparse_kernel(off_ref, cnt_ref, pos_ref, w_ref, x_ref, eg_ref, eu_ref, ed_ref, lng_ref, lnb_ref,
                       o_ref, xs_ref):
    s = pl.program_id(0)
    e = pl.program_id(1)
    tn = x_ref.shape[0]

    n_feat = D_MODEL // LANES

    def tile_rows(r):
        return pl.ds(pl.multiple_of(r * SUBLANES, SUBLANES), SUBLANES)

    @pl.when((s == 0) & (e == 0))
    def _():
        xs_ref[pl.ds(2 * tn * SUBLANES, MOE_WIN * SUBLANES), :] = jnp.zeros((MOE_WIN * SUBLANES, LANES), F32)

    @pl.when(e == 0)
    def _():
        def scatter(i, carry):
            t0 = pl.multiple_of(i * SUBLANES, SUBLANES)
            tiles = x_ref[pl.ds(t0, SUBLANES), :].reshape(SUBLANES, n_feat, LANES)
            for j in range(SUBLANES):
                xs_ref[tile_rows(pos_ref[0, 0, 2 * (t0 + j)]), :] = tiles[j]
                xs_ref[tile_rows(pos_ref[0, 0, 2 * (t0 + j) + 1]), :] = tiles[j]
            return carry
        lax.fori_loop(0, tn // SUBLANES, scatter, 0)

    ridx = lax.broadcasted_iota(jnp.int32, (MOE_WIN, 1), 0)

    base = s * N_EXPERTS + e * MOE_EBLK
    offs = [off_ref[base + j] for j in range(MOE_EBLK)]
    cnts = [cnt_ref[base + j] for j in range(MOE_EBLK)]

    def feat_rows(j, w, f):
        return pl.ds((offs[j] + w * MOE_WIN) * SUBLANES + f, MOE_WIN, stride=SUBLANES)

    def load_win(j, w):
        return jnp.concatenate([xs_ref[feat_rows(j, w, f), :] for f in range(n_feat)], axis=1)

    def expert(j, w, xw):
        xb = xw.astype(BF16)
        hg = _dot(xb, eg_ref[j])
        hu = _dot(xb, eu_ref[j])
        hh = hg * _sigmoid(hg) * hu
        yw = _dot(hh.astype(BF16), ed_ref[j])
        return jnp.where(ridx + w * MOE_WIN < cnts[j], yw, xw)

    def store_win(j, w, yw):
        for f in range(n_feat):
            xs_ref[feat_rows(j, w, f), :] = yw[:, f * LANES:(f + 1) * LANES]

    xw = load_win(0, 0)
    for j in range(MOE_EBLK):
        nxt = load_win(j + 1, 0) if j + 1 < MOE_EBLK else None
        store_win(j, 0, expert(j, 0, xw))
        xw = nxt

    for j in range(MOE_EBLK):
        def window(w, carry, j=j):
            store_win(j, w, expert(j, w, load_win(j, w)))
            return carry
        lax.fori_loop(1, (cnts[j] + MOE_WIN - 1) // MOE_WIN, window, 0)

    @pl.when(e == N_EXPERTS // MOE_EBLK - 1)
    def _():
        def combine(i, carry):
            t0 = pl.multiple_of(i * SUBLANES, SUBLANES)
            tiles = []
            for j in range(SUBLANES):
                y1 = xs_ref[tile_rows(pos_ref[0, 0, 2 * (t0 + j)]), :]
                y2 = xs_ref[tile_rows(pos_ref[0, 0, 2 * (t0 + j) + 1]), :]
                tiles.append(w_ref[0, 0, 2 * (t0 + j)] * y1 + w_ref[0, 0, 2 * (t0 + j) + 1] * y2)
            o_ref[pl.ds(t0, SUBLANES), :] = jnp.stack(tiles, axis=0).reshape(SUBLANES, D_MODEL)
            return carry
        lax.fori_loop(0, tn // SUBLANES, combine, 0)
        for j in range(tn // RANK_BLK):
            rs = slice(j * RANK_BLK, (j + 1) * RANK_BLK)
            o_ref[rs, :] = _layer_norm(DEEPNORM_ALPHA * x_ref[rs, :] + o_ref[rs, :], lng_ref[...], lnb_ref[...])


def _moe_sparse(x, lp):
    t = x.shape[0]
    tn = min(MOE_SLAB, t)
    n_slabs = t // tn
    w, pos, seg = _route(x, lp)
    w_flat = w[:, :2].reshape(n_slabs, 1, 2 * tn)
    pos_flat = pos[:, :2].reshape(n_slabs, 1, 2 * tn)
    off = seg[:, 0, :N_EXPERTS].reshape(-1)
    cnt = seg[:, 1, :N_EXPERTS].reshape(-1)
    smem_row = pl.BlockSpec((1, 1, 2 * tn), lambda s, e, *_: (s, 0, 0), memory_space=pltpu.SMEM)
    slab = pl.BlockSpec((tn, D_MODEL), lambda s, e, *_: (s, 0), pipeline_mode=pl.Buffered(1))
    const = lambda p: pl.BlockSpec(p.shape, lambda s, e, *_: (0,) * p.ndim)
    grid_spec = pltpu.PrefetchScalarGridSpec(
        num_scalar_prefetch=2,
        grid=(n_slabs, N_EXPERTS // MOE_EBLK),
        in_specs=[smem_row, smem_row, slab,
                  pl.BlockSpec((MOE_EBLK, D_MODEL, D_EXPERT), lambda s, e, *_: (e, 0, 0)),
                  pl.BlockSpec((MOE_EBLK, D_MODEL, D_EXPERT), lambda s, e, *_: (e, 0, 0)),
                  pl.BlockSpec((MOE_EBLK, D_EXPERT, D_MODEL), lambda s, e, *_: (e, 0, 0)),
                  const(lp["ln_g1"]), const(lp["ln_b1"])],
        out_specs=slab,
        scratch_shapes=[pltpu.VMEM((MOE_ROWS * SUBLANES, LANES), F32)],
    )
    return pl.pallas_call(
        _moe_sparse_kernel,
        grid_spec=grid_spec,
        out_shape=jax.ShapeDtypeStruct((t, D_MODEL), F32),
        compiler_params=_cparams("arbitrary", "arbitrary"),
        name="moe_sparse",
    )(off, cnt, pos_flat, w_flat, x, lp["e_gate"], lp["e_up"], lp["e_down"], lp["ln_g1"], lp["ln_b1"])


def _ple_kernel(x_ref, p_ref, wp_ref, wg_ref, lng_ref, lnb_ref, o_ref):
    x = x_ref[...]
    emb = _dot(p_ref[...].astype(BF16), wp_ref[...])
    gate = _sigmoid(_dot(x.astype(BF16), wg_ref[...]))
    o_ref[...] = _layer_norm(DEEPNORM_ALPHA * x + emb * gate, lng_ref[...], lnb_ref[...])


def _ple(x, p, lp):
    t = x.shape[0]
    tm = min(512, t)
    d_ple = p.shape[1]
    row = lambda w: pl.BlockSpec((tm, w), lambda i: (i, 0))
    params = [lp["w_ple"], lp["w_pg"], lp["ln_g2"], lp["ln_b2"]]
    return pl.pallas_call(
        _ple_kernel,
        grid=(t // tm,),
        in_specs=[row(D_MODEL), row(d_ple)] + [_full(q.shape) for q in params],
        out_specs=row(D_MODEL),
        out_shape=jax.ShapeDtypeStruct((t, D_MODEL), F32),
        compiler_params=_cparams("parallel"),
        name="ple",
    )(x, p, *params)


def _pad_to(a, axis, size):
    pad = [(0, 0)] * a.ndim
    pad[axis] = (0, size - a.shape[axis])
    return jnp.pad(a, pad)


def _row(v):
    return v.reshape(1, -1).astype(F32)


def _layer_params(i, w_in, tok_mix, decay_base, decay_up, aaa_base, aaa_up, gate_up, k_k, k_a, r_k,
                  vres_base, vres_down, vres_up, gn_g, gn_b, w_branch_attn, w_branch_rwkv, w_out,
                  router_grp, router_grp_bias, router_exp, router_exp_bias, exp_gate, exp_up, exp_down,
                  ple_proj, ple_gate, ln_g, ln_b, consts):
    d = D_HEADS
    w = w_in[i]
    rw0 = 3 * d
    gate0 = rw0 + 3 * d + DECAY_LORA + AAA_LORA + GATE_LORA
    wqkv = jnp.concatenate([w[:, 0:d] * (HEAD_DIM ** -0.5), w[:, d:rw0]], axis=1).astype(BF16)
    wr = w[:, rw0:gate0]
    lora0 = 3 * d
    pieces = [wr[:, :lora0],
              _pad_to(wr[:, lora0:lora0 + DECAY_LORA], 1, LANES),
              _pad_to(wr[:, lora0 + DECAY_LORA:lora0 + DECAY_LORA + AAA_LORA], 1, LANES),
              _pad_to(wr[:, lora0 + DECAY_LORA + AAA_LORA:], 1, 2 * LANES)]
    wrw = jnp.concatenate(pieces, axis=1).astype(BF16)
    mu = tok_mix[i]
    mu_pieces = [mu[:lora0],
                 _pad_to(mu[lora0:lora0 + DECAY_LORA], 0, LANES),
                 _pad_to(mu[lora0 + DECAY_LORA:lora0 + DECAY_LORA + AAA_LORA], 0, LANES),
                 _pad_to(mu[lora0 + DECAY_LORA + AAA_LORA:], 0, 2 * LANES)]
    lp = dict(consts)
    lp.update(
        wqkv=wqkv, wrw=wrw, wgate=w[:, gate0:].astype(BF16),
        mu=_row(jnp.concatenate(mu_pieces)),
        decay_base=_row(decay_base[i]), decay_up=_pad_to(decay_up[i], 0, LANES).astype(F32),
        aaa_base=_row(aaa_base[i]), aaa_up=_pad_to(aaa_up[i], 0, LANES).astype(F32),
        gate_up=_pad_to(gate_up[i], 0, 2 * LANES).astype(F32),
        k_k=_row(k_k[i]), k_a=_row(k_a[i]), r_k=_row(r_k[i]),
        gn_g=_row(gn_g[i]), gn_b=_row(gn_b[i]),
        w_a=w_branch_attn[i].astype(BF16), w_b=w_branch_rwkv[i].astype(BF16), w_o=w_out[i].astype(BF16),
        w_route=_pad_to(jnp.concatenate([router_exp[i], router_grp[i]], axis=1), 1, LANES).astype(F32),
        b_route=_row(_pad_to(jnp.concatenate([router_exp_bias[i], router_grp_bias[i]]), 0, LANES)),
        e_gate=exp_gate[i].astype(BF16), e_up=exp_up[i].astype(BF16), e_down=exp_down[i].astype(BF16),
        w_ple=ple_proj[i].astype(BF16), w_pg=ple_gate[i].astype(BF16),
        ln_g0=_row(ln_g[i, 0]), ln_b0=_row(ln_b[i, 0]),
        ln_g1=_row(ln_g[i, 1]), ln_b1=_row(ln_b[i, 1]),
        ln_g2=_row(ln_g[i, 2]), ln_b2=_row(ln_b[i, 2]),
    )
    if i > 0:
        lp.update(vres_base=_row(vres_base[i - 1]),
                  vres_down=_pad_to(vres_down[i - 1], 1, LANES).astype(F32),
                  vres_up=_pad_to(vres_up[i - 1], 0, LANES).astype(F32))
    return lp


def _const_mats(tm):
    head = jnp.arange(D_HEADS) // HEAD_DIM
    e_ones = (head[:, None] == head[None, :]).astype(BF16)
    tok = jnp.arange(tm)
    same = (tok[:, None] // CHUNK) == (tok[None, :] // CHUNK)
    tri = (same & (tok[:, None] >= tok[None, :])).astype(BF16)
    blk = same.astype(BF16)
    rtok = jnp.arange(RANK_BLK)
    tri_rank = (rtok[None, :] < rtok[:, None]).astype(BF16)
    return dict(e_ones=e_ones, tri=tri, blk=blk, tri_rank=tri_rank)


def kernel(x, p, ln_in_g, ln_in_b, rel_bias, w_in, tok_mix, decay_base, decay_up, aaa_base, aaa_up,
           gate_up, k_k, k_a, r_k, vres_base, vres_down, vres_up, gn_g, gn_b, w_branch_attn,
           w_branch_rwkv, w_out, router_grp, router_grp_bias, router_exp, router_exp_bias, exp_gate,
           exp_up, exp_down, ple_proj, ple_gate, ln_g, ln_b):
    batch, seq, _ = x.shape
    t = batch * seq
    consts = _const_mats(256)
    bias_tabs = _attn_bias_tables(rel_bias)
    xt = x.reshape(t, D_MODEL)
    pt = p.reshape(p.shape[0], t, p.shape[-1])
    v_first = None
    for i in range(DEPTH):
        lp = _layer_params(i, w_in, tok_mix, decay_base, decay_up, aaa_base, aaa_up, gate_up, k_k, k_a,
                           r_k, vres_base, vres_down, vres_up, gn_g, gn_b, w_branch_attn, w_branch_rwkv,
                           w_out, router_grp, router_grp_bias, router_exp, router_exp_bias, exp_gate,
                           exp_up, exp_down, ple_proj, ple_gate, ln_g, ln_b, consts)
        if i == 0:
            xt, qkv, rw, gate = _proj(xt, _row(ln_in_g), _row(ln_in_b), lp["wqkv"], lp["wrw"],
                                      lp["wgate"], apply_ln=True)
        else:
            qkv, rw, gate = _proj(xt, _row(ln_in_g), _row(ln_in_b), lp["wqkv"], lp["wrw"],
                                  lp["wgate"], apply_ln=False)
        y_attn = _attention(qkv, bias_tabs, batch, seq)
        if i == 0:
            q1, y0, m, n0, g, bonus, v_first = _rwkv_chunk(rw, lp, seq, None)
        else:
            q1, y0, m, n0, g, bonus = _rwkv_chunk(rw, lp, seq, v_first)
        y = _rwkv_scan(q1, y0, m, n0, batch, seq)
        xt = _mix(y, g, bonus, y_attn, gate, xt, lp)
        xt = _moe_sparse(xt, lp)
        xt = _ple(xt, pt[i], lp)
    return xt.reshape(batch, seq, D_MODEL)
```

```python
import functools
import math

import jax
import jax.numpy as jnp
from jax import lax
from jax.experimental import pallas as pl
from jax.experimental.pallas import tpu as pltpu

F32 = jnp.float32
BF16 = jnp.bfloat16

D_MODEL = 1024
CHUNK = 64
N_PREV_CHUNKS = 8
HEAD_DIM = 64
N_HEADS = 8
D_HEADS = N_HEADS * HEAD_DIM
REL_CLIP = 128
DECAY_LORA = 64
AAA_LORA = 64
GATE_LORA = 160
MV_LORA = 32
N_GROUPS = 4
EXPERTS_PER_GROUP = 8
N_EXPERTS = N_GROUPS * EXPERTS_PER_GROUP
D_EXPERT = 256
DEPTH = 2
DEEPNORM_ALPHA = (2 * DEPTH) ** 0.25
LN_EPS = 1e-5
GN_EPS = 64e-5
NEG_INF = -1e30

LANES = 128
PAIR = 2 * HEAD_DIM
ATTN_QBLK = 2 * CHUNK
ATTN_WIN = (N_PREV_CHUNKS + 2) * CHUNK
ATTN_GROUP = 4
RW_COLS_PAD = 2048
VMEM_LIMIT = 56 * 1024 * 1024


def _cparams(*sem):
    return pltpu.CompilerParams(dimension_semantics=sem, vmem_limit_bytes=VMEM_LIMIT)


def _dot(a, b):
    return jnp.dot(a, b, preferred_element_type=F32)


def _dot_nt(a, b):
    return lax.dot_general(a, b, (((1,), (1,)), ((), ())), preferred_element_type=F32)


def _dot_tn(a, b):
    return lax.dot_general(a, b, (((0,), (0,)), ((), ())), preferred_element_type=F32)


def _split2(x):
    hi = x.astype(BF16)
    lo = (x - hi.astype(F32)).astype(BF16)
    return hi, lo


def _dot3(a, b):
    ah, al = _split2(a)
    bh, bl = _split2(b)
    return _dot(ah, bh) + (_dot(al, bh) + _dot(ah, bl))


def _dot_exact_lhs(a_bf16, x):
    x1 = x.astype(BF16)
    r1 = x - x1.astype(F32)
    x2 = r1.astype(BF16)
    x3 = (r1 - x2.astype(F32)).astype(BF16)
    return _dot(a_bf16, x1) + (_dot(a_bf16, x2) + _dot(a_bf16, x3))


def _seg_sum(x, e_ones):
    hi, lo = _split2(x)
    return _dot(hi, e_ones) + _dot(lo, e_ones)


def _layer_norm(x, g, b):
    mu = jnp.mean(x, axis=-1, keepdims=True)
    xc = x - mu
    var = jnp.mean(xc * xc, axis=-1, keepdims=True)
    return xc * lax.rsqrt(var + LN_EPS) * g + b


def _sigmoid(x):
    return 1.0 / (1.0 + jnp.exp(-x))


def _full(shape):
    nd = len(shape)
    return pl.BlockSpec(shape, lambda *_: (0,) * nd)


def _proj_kernel(x_ref, g_ref, b_ref, wqkv_ref, wrw_ref, wgate_ref, *out_refs, apply_ln):
    x = x_ref[...]
    if apply_ln:
        xn_ref, qkv_ref, rw_ref, gate_ref = out_refs
        x = _layer_norm(x, g_ref[...], b_ref[...])
        xn_ref[...] = x
    else:
        qkv_ref, rw_ref, gate_ref = out_refs
    xb = x.astype(BF16)
    qkv_ref[...] = _dot(xb, wqkv_ref[...]).astype(BF16)
    rw_ref[...] = _dot(xb, wrw_ref[...])
    gate_ref[...] = _sigmoid(_dot(xb, wgate_ref[...])).astype(BF16)


def _proj(x, g, b, wqkv, wrw, wgate, apply_ln):
    t = x.shape[0]
    tm = min(256, t)
    row = lambda w: pl.BlockSpec((tm, w), lambda i: (i, 0))
    out_shape = [jax.ShapeDtypeStruct((t, 3 * D_HEADS), BF16),
                 jax.ShapeDtypeStruct((t, RW_COLS_PAD), F32),
                 jax.ShapeDtypeStruct((t, 2 * D_MODEL), BF16)]
    out_specs = [row(3 * D_HEADS), row(RW_COLS_PAD), row(2 * D_MODEL)]
    if apply_ln:
        out_shape = [jax.ShapeDtypeStruct((t, D_MODEL), F32)] + out_shape
        out_specs = [row(D_MODEL)] + out_specs
    return pl.pallas_call(
        functools.partial(_proj_kernel, apply_ln=apply_ln),
        grid=(t // tm,),
        in_specs=[row(D_MODEL), _full(g.shape), _full(b.shape),
                  _full(wqkv.shape), _full(wrw.shape), _full(wgate.shape)],
        out_specs=out_specs,
        out_shape=out_shape,
        compiler_params=_cparams("parallel"),
        name="proj_ln" if apply_ln else "proj",
    )(x, g, b, wqkv, wrw, wgate)


def _attn_kernel(q_ref, k_ref, v_ref, bias_ref, o_ref):
    qb = pl.program_id(1)
    ws = pl.multiple_of(jnp.maximum(qb * ATTN_QBLK - N_PREV_CHUNKS * CHUNK, 0), ATTN_QBLK)
    q = q_ref[0]
    kw = k_ref[0, pl.ds(ws, ATTN_WIN), :]
    vw = v_ref[0, pl.ds(ws, ATTN_WIN), :]
    left = lax.broadcasted_iota(jnp.int32, (ATTN_QBLK, PAIR), 1) < HEAD_DIM
    zero = jnp.zeros((ATTN_QBLK, PAIR), BF16)
    heads = [(h // 2, h % 2) for h in range(N_HEADS)]
    spans = [slice(p * PAIR, (p + 1) * PAIR) for p in range(N_HEADS // 2)]
    o = []
    for g0 in range(0, N_HEADS, ATTN_GROUP):
        grp = list(enumerate(heads))[g0:g0 + ATTN_GROUP]
        qm = [jnp.where(left, q[:, spans[p]], zero) if side == 0 else jnp.where(left, zero, q[:, spans[p]])
              for _, (p, side) in grp]
        s = [_dot_nt(x, kw[:, spans[p]]) + bias_ref[0, h] for x, (h, (p, _)) in zip(qm, grp)]
        m = [jnp.max(x, axis=-1, keepdims=True) for x in s]
        e = [jnp.exp(x - mx) for x, mx in zip(s, m)]
        inv = [1.0 / jnp.sum(x, axis=-1, keepdims=True) for x in e]
        o += [_dot(x.astype(BF16), vw[:, spans[p]]) * r for x, r, (_, (p, _)) in zip(e, inv, grp)]
    pairs = [jnp.where(left, o[2 * p], o[2 * p + 1]) for p in range(N_HEADS // 2)]
    o_ref[0] = jnp.concatenate(pairs, axis=-1).astype(BF16)


def _attention(qkv, bias_tabs, batch, seq):
    qkv3 = qkv.reshape(batch, seq, 3 * D_HEADS)
    n_tabs = bias_tabs.shape[0]
    out = pl.pallas_call(
        _attn_kernel,
        grid=(batch, seq // ATTN_QBLK),
        in_specs=[
            pl.BlockSpec((1, ATTN_QBLK, D_HEADS), lambda b, i: (b, i, 0)),
            pl.BlockSpec((1, seq, D_HEADS), lambda b, i: (b, 0, 1)),
            pl.BlockSpec((1, seq, D_HEADS), lambda b, i: (b, 0, 2)),
            pl.BlockSpec((1, N_HEADS, ATTN_QBLK, ATTN_WIN),
                         lambda b, i: (jnp.minimum(i, n_tabs - 1), 0, 0, 0)),
        ],
        out_specs=pl.BlockSpec((1, ATTN_QBLK, D_HEADS), lambda b, i: (b, i, 0)),
        out_shape=jax.ShapeDtypeStruct((batch, seq, D_HEADS), BF16),
        compiler_params=_cparams("parallel", "arbitrary"),
        name="band_attn",
    )(qkv3, qkv3, qkv3, bias_tabs)
    return out.reshape(batch * seq, D_HEADS)


def _attn_bias_tables(rel_bias):
    n_tabs = N_PREV_CHUNKS * CHUNK // ATTN_QBLK + 1
    start = (jnp.arange(n_tabs) * ATTN_QBLK)[:, None, None]
    qpos = start + jnp.arange(ATTN_QBLK)[None, :, None]
    kpos = jnp.arange(ATTN_WIN)[None, None, :]
    qc, kc = qpos // CHUNK, kpos // CHUNK
    valid = (kc <= qc) & (kc >= qc - N_PREV_CHUNKS)
    n = jnp.arange(ATTN_QBLK - 1 + ATTN_WIN)
    dist = start[:, :, 0] + (ATTN_QBLK - 1) - n[None, :]
    vals = rel_bias.astype(F32)[:, jnp.clip(dist, -REL_CLIP, REL_CLIP) + REL_CLIP]
    length = ATTN_QBLK - 1 + ATTN_WIN
    tiled = jnp.tile(_pad_to(vals, 2, length + 1), (1, 1, ATTN_QBLK))
    cut = tiled[:, :, :ATTN_QBLK * length].reshape(N_HEADS, n_tabs, ATTN_QBLK, length)
    bias = cut[:, :, :, ATTN_QBLK - 1:ATTN_QBLK - 1 + ATTN_WIN]
    bias = jnp.where(valid[None], bias, NEG_INF)
    return jnp.transpose(bias, (1, 0, 2, 3))


def _rwkv_chunk_kernel(*refs, seq, tm, has_vres):
    if has_vres:
        (rw_ref, prev_ref, mu_ref, dbase_ref, dup_ref, abase_ref, aup_ref, gup_ref, kk_ref, ka_ref,
         rk_ref, eones_ref, tri_ref, blk_ref, vfirst_ref, vbase_ref, vdown_ref, vup_ref,
         q1_ref, y0_ref, m_ref, n0_ref, g_ref, bonus_ref,
         rw_s, aw_s, bi_s, ki_s, be_s, ke_s, v_s, wc_s) = refs
    else:
        (rw_ref, prev_ref, mu_ref, dbase_ref, dup_ref, abase_ref, aup_ref, gup_ref, kk_ref, ka_ref,
         rk_ref, eones_ref, tri_ref, blk_ref,
         q1_ref, y0_ref, m_ref, n0_ref, g_ref, bonus_ref, vfirst_out_ref,
         rw_s, aw_s, bi_s, ki_s, be_s, ke_s, v_s, wc_s) = refs

    cols = rw_ref[...]
    first = (pl.program_id(0) % (seq // tm)) == 0
    prev_row = jnp.where(first, 0.0, prev_ref[7:8, :])
    rows = lax.broadcasted_iota(jnp.int32, cols.shape, 0)
    prev = jnp.where(rows == 0, prev_row, pltpu.roll(cols, 1, axis=0))
    xs = cols + (prev - cols) * mu_ref[...]

    d = D_HEADS
    r, k, v = xs[:, 0:d], xs[:, d:2 * d], xs[:, 2 * d:3 * d]
    wd = xs[:, 3 * d:3 * d + LANES]
    ad = xs[:, 3 * d + LANES:3 * d + 2 * LANES]
    gd = xs[:, 3 * d + 2 * LANES:3 * d + 4 * LANES]
    e_ones = eones_ref[...]

    u = dbase_ref[...] + _dot3(jnp.tanh(wd), dup_ref[...])
    w_log = jnp.minimum(u, 0.0) - jnp.log(1.0 + jnp.exp(-jnp.abs(u))) - 0.5
    logw = -jnp.exp(w_log)
    a_icl = _sigmoid(abase_ref[...] + _dot3(ad, aup_ref[...]))
    g_ref[...] = _dot3(_sigmoid(gd), gup_ref[...])
    kk = k * kk_ref[...]
    kk = kk * lax.rsqrt(jnp.maximum(_seg_sum(kk * kk, e_ones), 1e-24))
    k2 = k * (1.0 + (a_icl - 1.0) * ka_ref[...])
    if has_vres:
        mix = _sigmoid(vbase_ref[...] + _dot3(_dot3(v, vdown_ref[...]), vup_ref[...]))
        v2 = v + (vfirst_ref[...] - v) * mix
    else:
        v2 = v
        vfirst_out_ref[...] = v
    a_vec = -kk
    b_vec = kk * a_icl
    bonus_ref[...] = _seg_sum(r * k2 * rk_ref[...], e_ones) * v2

    cw = _dot_exact_lhs(tri_ref[...], logw)
    cw_tot = _dot_exact_lhs(blk_ref[...], logw)
    rw_s[...] = r * jnp.exp(cw)
    aw_s[...] = a_vec * jnp.exp(cw - logw)
    inv = jnp.exp(-cw)
    bi_s[...] = b_vec * inv
    ki_s[...] = k2 * inv
    rest = jnp.exp(cw_tot - cw)
    be_s[...] = b_vec * rest
    ke_s[...] = k2 * rest
    v_s[...] = v2
    wc_s[...] = jnp.exp(cw_tot)

    shape = (CHUNK, PAIR)
    ti = lax.broadcasted_iota(jnp.int32, shape, 0)
    li = lax.broadcasted_iota(jnp.int32, shape, 1)
    si = li & (HEAD_DIM - 1)
    left = li < HEAD_DIM
    strict = ti > si
    incl = ti >= si
    eye = ti == si

    def bd(x):
        z = jnp.zeros_like(x)
        return jnp.concatenate([jnp.where(left, x, z), jnp.where(left, z, x)], axis=0)

    def fold(x):
        return jnp.where(left, x[:CHUNK], x[CHUNK:])

    def b16(xs):
        return [x.astype(BF16) for x in xs]

    def summaries(insts):
        sl = [(slice(c * CHUNK, (c + 1) * CHUNK), slice(p * PAIR, (p + 1) * PAIR)) for c, p in insts]
        rw = [rw_s[s] for s in sl]
        awb = b16([aw_s[s] for s in sl])
        vb = b16([v_s[s] for s in sl])
        lhs = [jnp.concatenate([a, r.astype(BF16)], axis=0) for a, r in zip(awb, rw)]
        bik = [jnp.concatenate([bd(b), bd(k)], axis=0)
               for b, k in zip(b16([bi_s[s] for s in sl]), b16([ki_s[s] for s in sl]))]
        g = [_dot_nt(l, x) for l, x in zip(lhs, bik)]
        a_ab = [jnp.where(strict, x[:CHUNK, :PAIR], 0.0) for x in g]
        a_rb = b16([jnp.where(incl, x[CHUNK:, :PAIR], 0.0) for x in g])
        a_ak = b16([jnp.where(strict, x[:CHUNK, PAIR:], 0.0) for x in g])
        a_rk = b16([jnp.where(incl, x[CHUNK:, PAIR:], 0.0) for x in g])
        bdv = [bd(x) for x in vb]
        x0 = [_dot(a, v) for a, v in zip(a_ak, bdv)]
        lp = a_ab
        tinv = [jnp.where(eye, 1.0, a) for a in a_ab]
        for _ in range(5):
            lpb = b16(lp)
            lp = [_dot(l, bd(l)) for l in lpb]
            tinv = [t + _dot(t.astype(BF16), bd(l.astype(BF16))) for t, l in zip(tinv, lp)]
        tb = b16(tinv)
        pu = b16([_dot(t, jnp.concatenate([bd(a), bd(x.astype(BF16))], axis=1))
                  for t, a, x in zip(tb, awb, x0)])
        qy = [_dot(a, jnp.concatenate([bd(x[:, :PAIR]), bd(x[:, PAIR:])], axis=1)) for a, x in zip(a_rb, pu)]
        q1 = [r + x[:, :PAIR] for r, x in zip(rw, qy)]
        y0 = [x[:, PAIR:] + _dot(k, v) for x, k, v in zip(qy, a_rk, bdv)]
        beb = b16([be_s[s] for s in sl])
        keb = b16([ke_s[s] for s in sl])
        mn = [_dot_tn(b, x) for b, x in zip(beb, pu)]
        mm = [fold(x[:, :PAIR]) for x in mn]
        nn = [fold(x[:, PAIR:] + _dot_tn(k, v)) for x, k, v in zip(mn, keb, vb)]
        for i, (s, (c, p)) in enumerate(zip(sl, insts)):
            wc = wc_s[c * CHUNK:c * CHUNK + 1, s[1]]
            q1_ref[s] = q1[i]
            y0_ref[s] = y0[i]
            m_ref[s] = jnp.where(eye, wc, 0.0) + mm[i]
            n0_ref[s] = nn[i]

    n_pairs = D_HEADS // PAIR
    summaries([(c, p) for c in range(tm // CHUNK) for p in range(n_pairs)])


def _rwkv_chunk(rw, lp, seq, v_first):
    t = rw.shape[0]
    tm = 256
    has_vres = v_first is not None
    nct = tm // CHUNK
    row = lambda w: pl.BlockSpec((tm, w), lambda i: (i, 0))
    prev_spec = pl.BlockSpec((8, RW_COLS_PAD), lambda i: (jnp.maximum(i * (tm // 8) - 1, 0), 0))
    params = [lp["mu"], lp["decay_base"], lp["decay_up"], lp["aaa_base"], lp["aaa_up"], lp["gate_up"],
              lp["k_k"], lp["k_a"], lp["r_k"], lp["e_ones"], lp["tri"], lp["blk"]]
    inputs = [rw, rw] + params
    in_specs = [row(RW_COLS_PAD), prev_spec] + [_full(p.shape) for p in params]
    if has_vres:
        extra = [lp["vres_base"], lp["vres_down"], lp["vres_up"]]
        inputs += [v_first] + extra
        in_specs += [row(D_HEADS)] + [_full(p.shape) for p in extra]
    tok_shape = jax.ShapeDtypeStruct((t, D_HEADS), F32)
    out_shape = [tok_shape] * 6
    out_specs = [row(D_HEADS)] * 6
    if not has_vres:
        out_shape.append(tok_shape)
        out_specs.append(row(D_HEADS))
    scratch = [pltpu.VMEM((tm, D_HEADS), F32) for _ in range(8)]
    return pl.pallas_call(
        functools.partial(_rwkv_chunk_kernel, seq=seq, tm=tm, has_vres=has_vres),
        grid=(t // tm,),
        in_specs=in_specs,
        out_specs=out_specs,
        out_shape=out_shape,
        scratch_shapes=scratch,
        compiler_params=_cparams("parallel"),
        name="rwkv_chunk_vres" if has_vres else "rwkv_chunk",
    )(*inputs)


def _rwkv_scan_kernel(q1_ref, y0_ref, m_ref, n0_ref, y_ref, st_ref, *, n_chunks):
    @pl.when(pl.program_id(1) == 0)
    def _():
        st_ref[...] = jnp.zeros_like(st_ref)

    left = lax.broadcasted_iota(jnp.int32, (CHUNK, PAIR), 1) < HEAD_DIM

    def bd(x):
        z = jnp.zeros_like(x)
        return jnp.concatenate([jnp.where(left, x, z), jnp.where(left, z, x)], axis=0)

    n_pairs = D_HEADS // PAIR
    st = [st_ref[:, p * PAIR:(p + 1) * PAIR] for p in range(n_pairs)]
    for c in range(n_chunks):
        rs = slice(c * CHUNK, (c + 1) * CHUNK)
        for p in range(n_pairs):
            ls = slice(p * PAIR, (p + 1) * PAIR)
            lh, ll = _split2(jnp.concatenate([q1_ref[rs, ls], m_ref[rs, ls]], axis=0))
            sh, sl = _split2(st[p])
            bh = bd(sh)
            res = _dot(lh, bh) + (_dot(ll, bh) + _dot(lh, bd(sl)))
            y_ref[rs, ls] = res[:CHUNK] + y0_ref[rs, ls]
            st[p] = res[CHUNK:] + n0_ref[rs, ls]
    for p in range(n_pairs):
        st_ref[:, p * PAIR:(p + 1) * PAIR] = st[p]


def _rwkv_scan(q1, y0, m, n0, batch, seq):
    t = q1.shape[0]
    cb = 8
    steps = seq // (cb * CHUNK)
    tok_spec = pl.BlockSpec((cb * CHUNK, D_HEADS), lambda b, j: (b * steps + j, 0))
    return pl.pallas_call(
        functools.partial(_rwkv_scan_kernel, n_chunks=cb),
        grid=(batch, steps),
        in_specs=[tok_spec] * 4,
        out_specs=tok_spec,
        out_shape=jax.ShapeDtypeStruct((t, D_HEADS), F32),
        scratch_shapes=[pltpu.VMEM((HEAD_DIM, D_HEADS), F32)],
        compiler_params=_cparams("parallel", "arbitrary"),
        name="rwkv_scan",
    )(q1, y0, m, n0)


def _mix_kernel(y_ref, g_ref, bonus_ref, ya_ref, gate_ref, x_ref, wa_ref, wb_ref, wo_ref,
                eones_ref, gng_ref, gnb_ref, lng_ref, lnb_ref, o_ref):
    e_ones = eones_ref[...]
    y = y_ref[...]
    mean = _seg_sum(y, e_ones) * (1.0 / HEAD_DIM)
    yc = y - mean
    var = _seg_sum(yc * yc, e_ones) * (1.0 / HEAD_DIM)
    yn = yc * lax.rsqrt(var + GN_EPS) * gng_ref[...] + gnb_ref[...]
    yr = (yn + bonus_ref[...]) * g_ref[...]
    za = _dot(ya_ref[...], wa_ref[...])
    zb = _dot(yr.astype(BF16), wb_ref[...])
    gate = gate_ref[...]
    mixed_in = gate[:, :D_MODEL].astype(F32) * za + gate[:, D_MODEL:].astype(F32) * zb
    mixed = _dot(mixed_in.astype(BF16), wo_ref[...])
    o_ref[...] = _layer_norm(DEEPNORM_ALPHA * x_ref[...] + mixed, lng_ref[...], lnb_ref[...])


def _mix(y, g, bonus, y_attn, gate, x, lp):
    t = x.shape[0]
    tm = min(256, t)
    row = lambda w: pl.BlockSpec((tm, w), lambda i: (i, 0))
    params = [lp["w_a"], lp["w_b"], lp["w_o"], lp["e_ones"], lp["gn_g"], lp["gn_b"], lp["ln_g0"], lp["ln_b0"]]
    return pl.pallas_call(
        _mix_kernel,
        grid=(t // tm,),
        in_specs=[row(D_HEADS), row(D_HEADS), row(D_HEADS), row(D_HEADS), row(2 * D_MODEL), row(D_MODEL)]
        + [_full(p.shape) for p in params],
        out_specs=row(D_MODEL),
        out_shape=jax.ShapeDtypeStruct((t, D_MODEL), F32),
        compiler_params=_cparams("parallel"),
        name="branch_mix",
    )(y, g, bonus, y_attn, gate, x, *params)


MOE_SLAB = 2048
MOE_WIN = 128
MOE_EBLK = 4
MOE_ROWS = 2 * MOE_SLAB + MOE_WIN
SUBLANES = 8
RANK_BLK = 256
ROUTE_COLS = 8


def _route_kernel(x_ref, wr_ref, br_ref, tri_ref, w_ref, pos_ref, seg_ref):
    x = x_ref[...]
    tn = x.shape[0]
    lane = lax.broadcasted_iota(jnp.int32, (tn, LANES), 1)
    lanef = lane.astype(F32)
    hp = lax.Precision.HIGHEST
    elog = _dot3(x, wr_ref[...]) + br_ref[...]
    big = float(LANES)
    glog = jnp.where((lane >= N_EXPERTS) & (lane < N_EXPERTS + N_GROUPS), elog, -jnp.inf)
    gmax = jnp.max(glog, axis=-1, keepdims=True)
    g_gate = 1.0 / jnp.sum(jnp.exp(glog - gmax), axis=-1, keepdims=True)
    grp = jnp.min(jnp.where(glog == gmax, lanef, big), axis=-1, keepdims=True) - N_EXPERTS
    in_grp = (lanef >= grp * EXPERTS_PER_GROUP) & (lanef < (grp + 1.0) * EXPERTS_PER_GROUP)
    sel = jnp.where(in_grp, elog, -jnp.inf)
    v1 = jnp.max(sel, axis=-1, keepdims=True)
    i1 = jnp.min(jnp.where(sel == v1, lanef, big), axis=-1, keepdims=True)
    sel2 = jnp.where(lanef == i1, -jnp.inf, sel)
    v2 = jnp.max(sel2, axis=-1, keepdims=True)
    i2 = jnp.min(jnp.where(sel2 == v2, lanef, big), axis=-1, keepdims=True)
    e2 = jnp.exp(v2 - v1)
    w1 = g_gate / (1.0 + e2)
    w2 = g_gate * e2 / (1.0 + e2)
    w_ref[...] = jnp.where(lane == 0, w1, jnp.where(lane == 1, w2, 0.0))[:, :ROUTE_COLS]

    hit1 = lanef == i1
    hit2 = lanef == i2
    onehot = jnp.where(hit1 | hit2, 1.0, 0.0)
    tri = tri_ref[...]
    carry = jnp.zeros((1, LANES), F32)
    ranks = []
    for j in range(tn // RANK_BLK):
        blk = onehot[j * RANK_BLK:(j + 1) * RANK_BLK]
        ranks.append(_dot(tri, blk.astype(BF16)) + carry)
        carry = carry + jnp.sum(blk, axis=0, keepdims=True)
    rank = jnp.concatenate(ranks, axis=0)
    cnt = carry
    li = lax.broadcasted_iota(jnp.int32, (LANES, LANES), 0)
    lj = lax.broadcasted_iota(jnp.int32, (LANES, LANES), 1)
    before = jnp.where(li < lj, 1.0, 0.0)
    off = jnp.dot(jnp.broadcast_to(cnt, (8, LANES)), before, precision=hp,
                  preferred_element_type=F32)[0:1]
    slot = (off + rank) * SUBLANES
    pos1 = jnp.sum(jnp.where(hit1, slot, 0.0), axis=-1, keepdims=True)
    pos2 = jnp.sum(jnp.where(hit2, slot, 0.0), axis=-1, keepdims=True)
    pos_ref[...] = jnp.where(lane == 0, pos1, jnp.where(lane == 1, pos2, 0.0)).astype(jnp.int32)[:, :ROUTE_COLS]
    row8 = lax.broadcasted_iota(jnp.int32, (8, LANES), 0)
    seg_ref[0] = jnp.where(row8 == 0, off, jnp.where(row8 == 1, cnt, 0.0)).astype(jnp.int32)


def _route(x, lp):
    t = x.shape[0]
    tn = min(MOE_SLAB, t)
    n_slabs = t // tn
    small = [lp["w_route"], lp["b_route"], lp["tri_rank"]]
    row = lambda w: pl.BlockSpec((tn, w), lambda i: (i, 0))
    return pl.pallas_call(
        _route_kernel,
        grid=(n_slabs,),
        in_specs=[row(D_MODEL)] + [_full(p.shape) for p in small],
        out_specs=[row(ROUTE_COLS), row(ROUTE_COLS), pl.BlockSpec((1, 8, LANES), lambda i: (i, 0, 0))],
        out_shape=[jax.ShapeDtypeStruct((t, ROUTE_COLS), F32), jax.ShapeDtypeStruct((t, ROUTE_COLS), jnp.int32),
                   jax.ShapeDtypeStruct((n_slabs, 8, LANES), jnp.int32)],
        compiler_params=_cparams("parallel"),
        name="moe_route",
    )(x, *small)


def _moe_sparse_kernel(off_ref, cnt_ref, pos_ref, w_ref, x_ref, eg_ref, eu_ref, ed_ref, lng_ref, lnb_ref,
                       o_ref, xs_ref):
    s = pl.program_id(0)
    e = pl.program_id(1)
    tn = x_ref.shape[0]

    n_feat = D_MODEL // LANES

    def tile_rows(first_row):
        return pl.ds(pl.multiple_of(first_row, SUBLANES), SUBLANES)

    @pl.when((s == 0) & (e == 0))
    def _():
        xs_ref[pl.ds(2 * tn * SUBLANES, MOE_WIN * SUBLANES), :] = jnp.zeros((MOE_WIN * SUBLANES, LANES), F32)

    @pl.when(e == 0)
    def _():
        def scatter(i, carry):
            t0 = pl.multiple_of(i * SUBLANES, SUBLANES)
            tiles = x_ref[pl.ds(t0, SUBLANES), :].reshape(SUBLANES, n_feat, LANES)
            for j in range(SUBLANES):
                xs_ref[tile_rows(pos_ref[0, 0, 2 * (t0 + j)]), :] = tiles[j]
                xs_ref[tile_rows(pos_ref[0, 0, 2 * (t0 + j) + 1]), :] = tiles[j]
            return carry
        lax.fori_loop(0, tn // SUBLANES, scatter, 0)

    ridx = lax.broadcasted_iota(jnp.int32, (MOE_WIN, 1), 0)

    base = s * N_EXPERTS + e * MOE_EBLK
    offs = [off_ref[base + j] for j in range(MOE_EBLK)]
    cnts = [cnt_ref[base + j] for j in range(MOE_EBLK)]

    def feat_rows(j, w, f):
        return pl.ds((offs[j] + w * MOE_WIN) * SUBLANES + f, MOE_WIN, stride=SUBLANES)

    def load_win(j, w):
        return jnp.concatenate([xs_ref[feat_rows(j, w, f), :] for f in range(n_feat)], axis=1)

    def hidden(j, xw):
        xb = xw.astype(BF16)
        hg = _dot(xb, eg_ref[j])
        hu = _dot(xb, eu_ref[j])
        return (hg * _sigmoid(hg) * hu).astype(BF16)

    def project(j, w, xw, hh):
        yw = _dot(hh, ed_ref[j])
        return jnp.where(ridx + w * MOE_WIN < cnts[j], yw, xw)

    def expert(j, w, xw):
        return project(j, w, xw, hidden(j, xw))

    def store_win(j, w, yw):
        for f in range(n_feat):
            xs_ref[feat_rows(j, w, f), :] = yw[:, f * LANES:(f + 1) * LANES]

    xw = [None] * MOE_EBLK
    hh = [None] * MOE_EBLK
    for j in range(MOE_EBLK + 1):
        if j < MOE_EBLK:
            xw[j] = load_win(j, 0)
            hh[j] = hidden(j, xw[j])
        if j > 0:
            store_win(j - 1, 0, project(j - 1, 0, xw[j - 1], hh[j - 1]))

    for j in range(MOE_EBLK):
        def window(w, carry, j=j):
            store_win(j, w, expert(j, w, load_win(j, w)))
            return carry
        lax.fori_loop(1, (cnts[j] + MOE_WIN - 1) // MOE_WIN, window, 0)

    @pl.when(e == N_EXPERTS // MOE_EBLK - 1)
    def _():
        def combine(i, carry):
            t0 = pl.multiple_of(i * SUBLANES, SUBLANES)
            tiles = []
            for j in range(SUBLANES):
                y1 = xs_ref[tile_rows(pos_ref[0, 0, 2 * (t0 + j)]), :]
                y2 = xs_ref[tile_rows(pos_ref[0, 0, 2 * (t0 + j) + 1]), :]
                tiles.append(w_ref[0, 0, 2 * (t0 + j)] * y1 + w_ref[0, 0, 2 * (t0 + j) + 1] * y2)
            o_ref[pl.ds(t0, SUBLANES), :] = jnp.stack(tiles, axis=0).reshape(SUBLANES, D_MODEL)
            return carry
        lax.fori_loop(0, tn // SUBLANES, combine, 0)
        for j in range(tn // RANK_BLK):
            rs = slice(j * RANK_BLK, (j + 1) * RANK_BLK)
            o_ref[rs, :] = _layer_norm(DEEPNORM_ALPHA * x_ref[rs, :] + o_ref[rs, :], lng_ref[...], lnb_ref[...])


def _moe_sparse(x, lp):
    t = x.shape[0]
    tn = min(MOE_SLAB, t)
    n_slabs = t // tn
    w, pos, seg = _route(x, lp)
    w_flat = w[:, :2].reshape(n_slabs, 1, 2 * tn)
    pos_flat = pos[:, :2].reshape(n_slabs, 1, 2 * tn)
    off = seg[:, 0, :N_EXPERTS].reshape(-1)
    cnt = seg[:, 1, :N_EXPERTS].reshape(-1)
    smem_row = pl.BlockSpec((1, 1, 2 * tn), lambda s, e, *_: (s, 0, 0), memory_space=pltpu.SMEM)
    slab = pl.BlockSpec((tn, D_MODEL), lambda s, e, *_: (s, 0), pipeline_mode=pl.Buffered(1))
    const = lambda p: pl.BlockSpec(p.shape, lambda s, e, *_: (0,) * p.ndim)
    grid_spec = pltpu.PrefetchScalarGridSpec(
        num_scalar_prefetch=2,
        grid=(n_slabs, N_EXPERTS // MOE_EBLK),
        in_specs=[smem_row, smem_row, slab,
                  pl.BlockSpec((MOE_EBLK, D_MODEL, D_EXPERT), lambda s, e, *_: (e, 0, 0)),
                  pl.BlockSpec((MOE_EBLK, D_MODEL, D_EXPERT), lambda s, e, *_: (e, 0, 0)),
                  pl.BlockSpec((MOE_EBLK, D_EXPERT, D_MODEL), lambda s, e, *_: (e, 0, 0)),
                  const(lp["ln_g1"]), const(lp["ln_b1"])],
        out_specs=slab,
        scratch_shapes=[pltpu.VMEM((MOE_ROWS * SUBLANES, LANES), F32)],
    )
    return pl.pallas_call(
        _moe_sparse_kernel,
        grid_spec=grid_spec,
        out_shape=jax.ShapeDtypeStruct((t, D_MODEL), F32),
        compiler_params=_cparams("arbitrary", "arbitrary"),
        name="moe_sparse",
    )(off, cnt, pos_flat, w_flat, x, lp["e_gate"], lp["e_up"], lp["e_down"], lp["ln_g1"], lp["ln_b1"])


def _ple_kernel(x_ref, p_ref, wp_ref, wg_ref, lng_ref, lnb_ref, o_ref):
    x = x_ref[...]
    emb = _dot(p_ref[...].astype(BF16), wp_ref[...])
    gate = _sigmoid(_dot(x.astype(BF16), wg_ref[...]))
    o_ref[...] = _layer_norm(DEEPNORM_ALPHA * x + emb * gate, lng_ref[...], lnb_ref[...])


def _ple(x, p, lp):
    t = x.shape[0]
    tm = min(512, t)
    d_ple = p.shape[1]
    row = lambda w: pl.BlockSpec((tm, w), lambda i: (i, 0))
    params = [lp["w_ple"], lp["w_pg"], lp["ln_g2"], lp["ln_b2"]]
    return pl.pallas_call(
        _ple_kernel,
        grid=(t // tm,),
        in_specs=[row(D_MODEL), row(d_ple)] + [_full(q.shape) for q in params],
        out_specs=row(D_MODEL),
        out_shape=jax.ShapeDtypeStruct((t, D_MODEL), F32),
        compiler_params=_cparams("parallel"),
        name="ple",
    )(x, p, *params)


def _pad_to(a, axis, size):
    pad = [(0, 0)] * a.ndim
    pad[axis] = (0, size - a.shape[axis])
    return jnp.pad(a, pad)


def _row(v):
    return v.reshape(1, -1).astype(F32)


def _layer_params(i, w_in, tok_mix, decay_base, decay_up, aaa_base, aaa_up, gate_up, k_k, k_a, r_k,
                  vres_base, vres_down, vres_up, gn_g, gn_b, w_branch_attn, w_branch_rwkv, w_out,
                  router_grp, router_grp_bias, router_exp, router_exp_bias, exp_gate, exp_up, exp_down,
                  ple_proj, ple_gate, ln_g, ln_b, consts):
    d = D_HEADS
    w = w_in[i]
    rw0 = 3 * d
    gate0 = rw0 + 3 * d + DECAY_LORA + AAA_LORA + GATE_LORA
    wqkv = jnp.concatenate([w[:, 0:d] * (HEAD_DIM ** -0.5), w[:, d:rw0]], axis=1).astype(BF16)
    wr = w[:, rw0:gate0]
    lora0 = 3 * d
    pieces = [wr[:, :lora0],
              _pad_to(wr[:, lora0:lora0 + DECAY_LORA], 1, LANES),
              _pad_to(wr[:, lora0 + DECAY_LORA:lora0 + DECAY_LORA + AAA_LORA], 1, LANES),
              _pad_to(wr[:, lora0 + DECAY_LORA + AAA_LORA:], 1, 2 * LANES)]
    wrw = jnp.concatenate(pieces, axis=1).astype(BF16)
    mu = tok_mix[i]
    mu_pieces = [mu[:lora0],
                 _pad_to(mu[lora0:lora0 + DECAY_LORA], 0, LANES),
                 _pad_to(mu[lora0 + DECAY_LORA:lora0 + DECAY_LORA + AAA_LORA], 0, LANES),
                 _pad_to(mu[lora0 + DECAY_LORA + AAA_LORA:], 0, 2 * LANES)]
    lp = dict(consts)
    lp.update(
        wqkv=wqkv, wrw=wrw, wgate=w[:, gate0:].astype(BF16),
        mu=_row(jnp.concatenate(mu_pieces)),
        decay_base=_row(decay_base[i]), decay_up=_pad_to(decay_up[i], 0, LANES).astype(F32),
        aaa_base=_row(aaa_base[i]), aaa_up=_pad_to(aaa_up[i], 0, LANES).astype(F32),
        gate_up=_pad_to(gate_up[i], 0, 2 * LANES).astype(F32),
        k_k=_row(k_k[i]), k_a=_row(k_a[i]), r_k=_row(r_k[i]),
        gn_g=_row(gn_g[i]), gn_b=_row(gn_b[i]),
        w_a=w_branch_attn[i].astype(BF16), w_b=w_branch_rwkv[i].astype(BF16), w_o=w_out[i].astype(BF16),
        w_route=_pad_to(jnp.concatenate([router_exp[i], router_grp[i]], axis=1), 1, LANES).astype(F32),
        b_route=_row(_pad_to(jnp.concatenate([router_exp_bias[i], router_grp_bias[i]]), 0, LANES)),
        e_gate=exp_gate[i].astype(BF16), e_up=exp_up[i].astype(BF16), e_down=exp_down[i].astype(BF16),
        w_ple=ple_proj[i].astype(BF16), w_pg=ple_gate[i].astype(BF16),
        ln_g0=_row(ln_g[i, 0]), ln_b0=_row(ln_b[i, 0]),
        ln_g1=_row(ln_g[i, 1]), ln_b1=_row(ln_b[i, 1]),
        ln_g2=_row(ln_g[i, 2]), ln_b2=_row(ln_b[i, 2]),
    )
    if i > 0:
        lp.update(vres_base=_row(vres_base[i - 1]),
                  vres_down=_pad_to(vres_down[i - 1], 1, LANES).astype(F32),
                  vres_up=_pad_to(vres_up[i - 1], 0, LANES).astype(F32))
    return lp


def _const_mats(tm):
    head = jnp.arange(D_HEADS) // HEAD_DIM
    e_ones = (head[:, None] == head[None, :]).astype(BF16)
    tok = jnp.arange(tm)
    same = (tok[:, None] // CHUNK) == (tok[None, :] // CHUNK)
    tri = (same & (tok[:, None] >= tok[None, :])).astype(BF16)
    blk = same.astype(BF16)
    rtok = jnp.arange(RANK_BLK)
    tri_rank = (rtok[None, :] < rtok[:, None]).astype(BF16)
    return dict(e_ones=e_ones, tri=tri, blk=blk, tri_rank=tri_rank)


def kernel(x, p, ln_in_g, ln_in_b, rel_bias, w_in, tok_mix, decay_base, decay_up, aaa_base, aaa_up,
           gate_up, k_k, k_a, r_k, vres_base, vres_down, vres_up, gn_g, gn_b, w_branch_attn,
           w_branch_rwkv, w_out, router_grp, router_grp_bias, router_exp, router_exp_bias, exp_gate,
           exp_up, exp_down, ple_proj, ple_gate, ln_g, ln_b):
    batch, seq, _ = x.shape
    t = batch * seq
    consts = _const_mats(256)
    bias_tabs = _attn_bias_tables(rel_bias)
    xt = x.reshape(t, D_MODEL)
    pt = p.reshape(p.shape[0], t, p.shape[-1])
    v_first = None
    for i in range(DEPTH):
        lp = _layer_params(i, w_in, tok_mix, decay_base, decay_up, aaa_base, aaa_up, gate_up, k_k, k_a,
                           r_k, vres_base, vres_down, vres_up, gn_g, gn_b, w_branch_attn, w_branch_rwkv,
                           w_out, router_grp, router_grp_bias, router_exp, router_exp_bias, exp_gate,
                           exp_up, exp_down, ple_proj, ple_gate, ln_g, ln_b, consts)
        if i == 0:
            xt, qkv, rw, gate = _proj(xt, _row(ln_in_g), _row(ln_in_b), lp["wqkv"], lp["wrw"],
                                      lp["wgate"], apply_ln=True)
        else:
            qkv, rw, gate = _proj(xt, _row(ln_in_g), _row(ln_in_b), lp["wqkv"], lp["wrw"],
                                  lp["wgate"], apply_ln=False)
        y_attn = _attention(qkv, bias_tabs, batch, seq)
        if i == 0:
            q1, y0, m, n0, g, bonus, v_first = _rwkv_chunk(rw, lp, seq, None)
        else:
            q1, y0, m, n0, g, bonus = _rwkv_chunk(rw, lp, seq, v_first)
        y = _rwkv_scan(q1, y0, m, n0, batch, seq)
        xt = _mix(y, g, bonus, y_attn, gate, xt, lp)
        xt = _moe_sparse(xt, lp)
        xt = _ple(xt, pt[i], lp)
    return xt.reshape(batch, seq, D_MODEL)
```

```python
import functools
import math

import jax
import jax.numpy as jnp
from jax import lax
from jax.experimental import pallas as pl
from jax.experimental.pallas import tpu as pltpu

F32 = jnp.float32
BF16 = jnp.bfloat16

D_MODEL = 1024
CHUNK = 64
N_PREV_CHUNKS = 8
HEAD_DIM = 64
N_HEADS = 8
D_HEADS = N_HEADS * HEAD_DIM
REL_CLIP = 128
DECAY_LORA = 64
AAA_LORA = 64
GATE_LORA = 160
MV_LORA = 32
N_GROUPS = 4
EXPERTS_PER_GROUP = 8
N_EXPERTS = N_GROUPS * EXPERTS_PER_GROUP
D_EXPERT = 256
DEPTH = 2
DEEPNORM_ALPHA = (2 * DEPTH) ** 0.25
LN_EPS = 1e-5
GN_EPS = 64e-5
NEG_INF = -1e30

LANES = 128
PAIR = 2 * HEAD_DIM
ATTN_QBLK = 2 * CHUNK
ATTN_WIN = (N_PREV_CHUNKS + 2) * CHUNK
ATTN_GROUP = 4
RW_COLS_PAD = 2048
VMEM_LIMIT = 56 * 1024 * 1024


def _cparams(*sem):
    return pltpu.CompilerParams(dimension_semantics=sem, vmem_limit_bytes=VMEM_LIMIT)


def _dot(a, b):
    return jnp.dot(a, b, preferred_element_type=F32)


def _dot_nt(a, b):
    return lax.dot_general(a, b, (((1,), (1,)), ((), ())), preferred_element_type=F32)


def _dot_tn(a, b):
    return lax.dot_general(a, b, (((0,), (0,)), ((), ())), preferred_element_type=F32)


def _split2(x):
    hi = x.astype(BF16)
    lo = (x - hi.astype(F32)).astype(BF16)
    return hi, lo


def _dot3(a, b):
    ah, al = _split2(a)
    bh, bl = _split2(b)
    return _dot(ah, bh) + (_dot(al, bh) + _dot(ah, bl))


def _dot_exact_lhs(a_bf16, x):
    x1 = x.astype(BF16)
    r1 = x - x1.astype(F32)
    x2 = r1.astype(BF16)
    x3 = (r1 - x2.astype(F32)).astype(BF16)
    return _dot(a_bf16, x1) + (_dot(a_bf16, x2) + _dot(a_bf16, x3))


def _seg_sum(x, e_ones):
    hi, lo = _split2(x)
    return _dot(hi, e_ones) + _dot(lo, e_ones)


def _layer_norm(x, g, b):
    mu = jnp.mean(x, axis=-1, keepdims=True)
    xc = x - mu
    var = jnp.mean(xc * xc, axis=-1, keepdims=True)
    return xc * lax.rsqrt(var + LN_EPS) * g + b


def _sigmoid(x):
    return 1.0 / (1.0 + jnp.exp(-x))


def _full(shape):
    nd = len(shape)
    return pl.BlockSpec(shape, lambda *_: (0,) * nd)


def _proj_kernel(x_ref, g_ref, b_ref, wqkv_ref, wrw_ref, wgate_ref, *out_refs, apply_ln):
    x = x_ref[...]
    if apply_ln:
        xn_ref, qkv_ref, rw_ref, gate_ref = out_refs
        x = _layer_norm(x, g_ref[...], b_ref[...])
        xn_ref[...] = x
    else:
        qkv_ref, rw_ref, gate_ref = out_refs
    xb = x.astype(BF16)
    qkv_ref[...] = _dot(xb, wqkv_ref[...]).astype(BF16)
    rw_ref[...] = _dot(xb, wrw_ref[...])
    gate_ref[...] = _sigmoid(_dot(xb, wgate_ref[...])).astype(BF16)


def _proj(x, g, b, wqkv, wrw, wgate, apply_ln):
    t = x.shape[0]
    tm = min(256, t)
    row = lambda w: pl.BlockSpec((tm, w), lambda i: (i, 0))
    out_shape = [jax.ShapeDtypeStruct((t, 3 * D_HEADS), BF16),
                 jax.ShapeDtypeStruct((t, RW_COLS_PAD), F32),
                 jax.ShapeDtypeStruct((t, 2 * D_MODEL), BF16)]
    out_specs = [row(3 * D_HEADS), row(RW_COLS_PAD), row(2 * D_MODEL)]
    if apply_ln:
        out_shape = [jax.ShapeDtypeStruct((t, D_MODEL), F32)] + out_shape
        out_specs = [row(D_MODEL)] + out_specs
    return pl.pallas_call(
        functools.partial(_proj_kernel, apply_ln=apply_ln),
        grid=(t // tm,),
        in_specs=[row(D_MODEL), _full(g.shape), _full(b.shape),
                  _full(wqkv.shape), _full(wrw.shape), _full(wgate.shape)],
        out_specs=out_specs,
        out_shape=out_shape,
        compiler_params=_cparams("parallel"),
        name="proj_ln" if apply_ln else "proj",
    )(x, g, b, wqkv, wrw, wgate)


def _attn_kernel(q_ref, k_ref, v_ref, bias_ref, o_ref):
    qb = pl.program_id(1)
    ws = pl.multiple_of(jnp.maximum(qb * ATTN_QBLK - N_PREV_CHUNKS * CHUNK, 0), ATTN_QBLK)
    q = q_ref[0]
    kw = k_ref[0, pl.ds(ws, ATTN_WIN), :]
    vw = v_ref[0, pl.ds(ws, ATTN_WIN), :]
    left = lax.broadcasted_iota(jnp.int32, (ATTN_QBLK, PAIR), 1) < HEAD_DIM
    zero = jnp.zeros((ATTN_QBLK, PAIR), BF16)
    heads = [(h // 2, h % 2) for h in range(N_HEADS)]
    spans = [slice(p * PAIR, (p + 1) * PAIR) for p in range(N_HEADS // 2)]
    o = []
    for g0 in range(0, N_HEADS, ATTN_GROUP):
        grp = list(enumerate(heads))[g0:g0 + ATTN_GROUP]
        qm = [jnp.where(left, q[:, spans[p]], zero) if side == 0 else jnp.where(left, zero, q[:, spans[p]])
              for _, (p, side) in grp]
        s = [_dot_nt(x, kw[:, spans[p]]) + bias_ref[0, h] for x, (h, (p, _)) in zip(qm, grp)]
        m = [jnp.max(x, axis=-1, keepdims=True) for x in s]
        e = [jnp.exp(x - mx) for x, mx in zip(s, m)]
        inv = [1.0 / jnp.sum(x, axis=-1, keepdims=True) for x in e]
        o += [_dot(x.astype(BF16), vw[:, spans[p]]) * r for x, r, (_, (p, _)) in zip(e, inv, grp)]
    pairs = [jnp.where(left, o[2 * p], o[2 * p + 1]) for p in range(N_HEADS // 2)]
    o_ref[0] = jnp.concatenate(pairs, axis=-1).astype(BF16)


def _attention(qkv, bias_tabs, batch, seq):
    qkv3 = qkv.reshape(batch, seq, 3 * D_HEADS)
    n_tabs = bias_tabs.shape[0]
    out = pl.pallas_call(
        _attn_kernel,
        grid=(batch, seq // ATTN_QBLK),
        in_specs=[
            pl.BlockSpec((1, ATTN_QBLK, D_HEADS), lambda b, i: (b, i, 0)),
            pl.BlockSpec((1, seq, D_HEADS), lambda b, i: (b, 0, 1)),
            pl.BlockSpec((1, seq, D_HEADS), lambda b, i: (b, 0, 2)),
            pl.BlockSpec((1, N_HEADS, ATTN_QBLK, ATTN_WIN),
                         lambda b, i: (jnp.minimum(i, n_tabs - 1), 0, 0, 0)),
        ],
        out_specs=pl.BlockSpec((1, ATTN_QBLK, D_HEADS), lambda b, i: (b, i, 0)),
        out_shape=jax.ShapeDtypeStruct((batch, seq, D_HEADS), BF16),
        compiler_params=_cparams("parallel", "arbitrary"),
        name="band_attn",
    )(qkv3, qkv3, qkv3, bias_tabs)
    return out.reshape(batch * seq, D_HEADS)


def _attn_bias_tables(rel_bias):
    n_tabs = N_PREV_CHUNKS * CHUNK // ATTN_QBLK + 1
    start = (jnp.arange(n_tabs) * ATTN_QBLK)[:, None, None]
    qpos = start + jnp.arange(ATTN_QBLK)[None, :, None]
    kpos = jnp.arange(ATTN_WIN)[None, None, :]
    qc, kc = qpos // CHUNK, kpos // CHUNK
    valid = (kc <= qc) & (kc >= qc - N_PREV_CHUNKS)
    n = jnp.arange(ATTN_QBLK - 1 + ATTN_WIN)
    dist = start[:, :, 0] + (ATTN_QBLK - 1) - n[None, :]
    vals = rel_bias.astype(F32)[:, jnp.clip(dist, -REL_CLIP, REL_CLIP) + REL_CLIP]
    length = ATTN_QBLK - 1 + ATTN_WIN
    tiled = jnp.tile(_pad_to(vals, 2, length + 1), (1, 1, ATTN_QBLK))
    cut = tiled[:, :, :ATTN_QBLK * length].reshape(N_HEADS, n_tabs, ATTN_QBLK, length)
    bias = cut[:, :, :, ATTN_QBLK - 1:ATTN_QBLK - 1 + ATTN_WIN]
    bias = jnp.where(valid[None], bias, NEG_INF)
    return jnp.transpose(bias, (1, 0, 2, 3))


def _rwkv_chunk_kernel(*refs, seq, tm, has_vres):
    if has_vres:
        (rw_ref, prev_ref, mu_ref, dbase_ref, dup_ref, abase_ref, aup_ref, gup_ref, kk_ref, ka_ref,
         rk_ref, eones_ref, tri_ref, blk_ref, vfirst_ref, vbase_ref, vdown_ref, vup_ref,
         q1_ref, y0_ref, m_ref, n0_ref, g_ref, bonus_ref,
         rw_s, aw_s, bi_s, ki_s, be_s, ke_s, v_s, wc_s) = refs
    else:
        (rw_ref, prev_ref, mu_ref, dbase_ref, dup_ref, abase_ref, aup_ref, gup_ref, kk_ref, ka_ref,
         rk_ref, eones_ref, tri_ref, blk_ref,
         q1_ref, y0_ref, m_ref, n0_ref, g_ref, bonus_ref, vfirst_out_ref,
         rw_s, aw_s, bi_s, ki_s, be_s, ke_s, v_s, wc_s) = refs

    cols = rw_ref[...]
    first = (pl.program_id(0) % (seq // tm)) == 0
    prev_row = jnp.where(first, 0.0, prev_ref[7:8, :])
    rows = lax.broadcasted_iota(jnp.int32, cols.shape, 0)
    prev = jnp.where(rows == 0, prev_row, pltpu.roll(cols, 1, axis=0))
    xs = cols + (prev - cols) * mu_ref[...]

    d = D_HEADS
    r, k, v = xs[:, 0:d], xs[:, d:2 * d], xs[:, 2 * d:3 * d]
    wd = xs[:, 3 * d:3 * d + LANES]
    ad = xs[:, 3 * d + LANES:3 * d + 2 * LANES]
    gd = xs[:, 3 * d + 2 * LANES:3 * d + 4 * LANES]
    e_ones = eones_ref[...]

    u = dbase_ref[...] + _dot3(jnp.tanh(wd), dup_ref[...])
    w_log = jnp.minimum(u, 0.0) - jnp.log(1.0 + jnp.exp(-jnp.abs(u))) - 0.5
    logw = -jnp.exp(w_log)
    a_icl = _sigmoid(abase_ref[...] + _dot3(ad, aup_ref[...]))
    g_ref[...] = _dot3(_sigmoid(gd), gup_ref[...]).astype(BF16)
    kk = k * kk_ref[...]
    kk = kk * lax.rsqrt(jnp.maximum(_seg_sum(kk * kk, e_ones), 1e-24))
    k2 = k * (1.0 + (a_icl - 1.0) * ka_ref[...])
    if has_vres:
        mix = _sigmoid(vbase_ref[...] + _dot3(_dot3(v, vdown_ref[...]), vup_ref[...]))
        v2 = v + (vfirst_ref[...] - v) * mix
    else:
        v2 = v
        vfirst_out_ref[...] = v
    a_vec = -kk
    b_vec = kk * a_icl
    bonus_ref[...] = (_seg_sum(r * k2 * rk_ref[...], e_ones) * v2).astype(BF16)

    cw = _dot_exact_lhs(tri_ref[...], logw)
    cw_tot = _dot_exact_lhs(blk_ref[...], logw)
    rw_s[...] = r * jnp.exp(cw)
    aw_s[...] = a_vec * jnp.exp(cw - logw)
    inv = jnp.exp(-cw)
    bi_s[...] = b_vec * inv
    ki_s[...] = k2 * inv
    rest = jnp.exp(cw_tot - cw)
    be_s[...] = b_vec * rest
    ke_s[...] = k2 * rest
    v_s[...] = v2
    wc_s[...] = jnp.exp(cw_tot)

    shape = (CHUNK, PAIR)
    ti = lax.broadcasted_iota(jnp.int32, shape, 0)
    li = lax.broadcasted_iota(jnp.int32, shape, 1)
    si = li & (HEAD_DIM - 1)
    left = li < HEAD_DIM
    strict = ti > si
    incl = ti >= si
    eye = ti == si

    def bd(x):
        z = jnp.zeros_like(x)
        return jnp.concatenate([jnp.where(left, x, z), jnp.where(left, z, x)], axis=0)

    def tpair(x):
        return jnp.concatenate([x[:, :HEAD_DIM].T, x[:, HEAD_DIM:].T], axis=1)

    def b16(xs):
        return [x.astype(BF16) for x in xs]

    def summaries(insts):
        sl = [(slice(c * CHUNK, (c + 1) * CHUNK), slice(p * PAIR, (p + 1) * PAIR)) for c, p in insts]
        rw = [rw_s[s] for s in sl]
        awb = b16([aw_s[s] for s in sl])
        vb = b16([v_s[s] for s in sl])
        lhs = [jnp.concatenate([a, r.astype(BF16)], axis=0) for a, r in zip(awb, rw)]
        bik = [jnp.concatenate([bd(b), bd(k)], axis=0)
               for b, k in zip(b16([bi_s[s] for s in sl]), b16([ki_s[s] for s in sl]))]
        g = [_dot_nt(l, x) for l, x in zip(lhs, bik)]
        a_ab = [jnp.where(strict, x[:CHUNK, :PAIR], 0.0) for x in g]
        a_rb = b16([jnp.where(incl, x[CHUNK:, :PAIR], 0.0) for x in g])
        a_ak = b16([jnp.where(strict, x[:CHUNK, PAIR:], 0.0) for x in g])
        a_rk = b16([jnp.where(incl, x[CHUNK:, PAIR:], 0.0) for x in g])
        bdv = [bd(x) for x in vb]
        x0 = [_dot(a, v) for a, v in zip(a_ak, bdv)]
        tinv = [jnp.where(eye, 1.0, a) for a in a_ab]
        lpb = b16(a_ab)
        lpb = b16([_dot(l, bd(l)) for l in lpb])
        for _ in range(4):
            both = [_dot(jnp.concatenate([t.astype(BF16), l], axis=0), bd(l)) for t, l in zip(tinv, lpb)]
            tinv = [t + x[:CHUNK] for t, x in zip(tinv, both)]
            lpb = b16([x[CHUNK:] for x in both])
        tinv = [t + _dot(t.astype(BF16), bd(l)) for t, l in zip(tinv, lpb)]
        tb = b16(tinv)
        pu = b16([_dot(t, jnp.concatenate([bd(a), bd(x.astype(BF16))], axis=1))
                  for t, a, x in zip(tb, awb, x0)])
        pu_bd = [jnp.concatenate([bd(x[:, :PAIR]), bd(x[:, PAIR:])], axis=1) for x in pu]
        qy = [_dot(a, r) for a, r in zip(a_rb, pu_bd)]
        q1 = [r + x[:, :PAIR] for r, x in zip(rw, qy)]
        y0 = [x[:, PAIR:] + _dot(k, v) for x, k, v in zip(qy, a_rk, bdv)]
        bet = b16([tpair(be_s[s]) for s in sl])
        ket = b16([tpair(ke_s[s]) for s in sl])
        mn = [_dot(b, r) for b, r in zip(bet, pu_bd)]
        mm = [x[:, :PAIR] for x in mn]
        nn = [x[:, PAIR:] + _dot(k, v) for x, k, v in zip(mn, ket, bdv)]
        for i, (s, (c, p)) in enumerate(zip(sl, insts)):
            wc = wc_s[c * CHUNK:c * CHUNK + 1, s[1]]
            q1_ref[s] = q1[i]
            y0_ref[s] = y0[i]
            m_ref[s] = jnp.where(eye, wc, 0.0) + mm[i]
            n0_ref[s] = nn[i]

    n_pairs = D_HEADS // PAIR
    summaries([(c, p) for c in range(tm // CHUNK) for p in range(n_pairs)])


def _rwkv_chunk(rw, lp, seq, v_first):
    t = rw.shape[0]
    tm = 256
    has_vres = v_first is not None
    nct = tm // CHUNK
    row = lambda w: pl.BlockSpec((tm, w), lambda i: (i, 0))
    prev_spec = pl.BlockSpec((8, RW_COLS_PAD), lambda i: (jnp.maximum(i * (tm // 8) - 1, 0), 0))
    params = [lp["mu"], lp["decay_base"], lp["decay_up"], lp["aaa_base"], lp["aaa_up"], lp["gate_up"],
              lp["k_k"], lp["k_a"], lp["r_k"], lp["e_ones"], lp["tri"], lp["blk"]]
    inputs = [rw, rw] + params
    in_specs = [row(RW_COLS_PAD), prev_spec] + [_full(p.shape) for p in params]
    if has_vres:
        extra = [lp["vres_base"], lp["vres_down"], lp["vres_up"]]
        inputs += [v_first] + extra
        in_specs += [row(D_HEADS)] + [_full(p.shape) for p in extra]
    tok_shape = jax.ShapeDtypeStruct((t, D_HEADS), F32)
    half_shape = jax.ShapeDtypeStruct((t, D_HEADS), BF16)
    out_shape = [tok_shape] * 4 + [half_shape] * 2
    out_specs = [row(D_HEADS)] * 6
    if not has_vres:
        out_shape.append(tok_shape)
        out_specs.append(row(D_HEADS))
    scratch = [pltpu.VMEM((tm, D_HEADS), F32) for _ in range(8)]
    return pl.pallas_call(
        functools.partial(_rwkv_chunk_kernel, seq=seq, tm=tm, has_vres=has_vres),
        grid=(t // tm,),
        in_specs=in_specs,
        out_specs=out_specs,
        out_shape=out_shape,
        scratch_shapes=scratch,
        compiler_params=_cparams("parallel"),
        name="rwkv_chunk_vres" if has_vres else "rwkv_chunk",
    )(*inputs)


def _rwkv_scan_kernel(q1_ref, y0_ref, m_ref, n0_ref, y_ref, st_ref, *, n_chunks):
    @pl.when(pl.program_id(1) == 0)
    def _():
        st_ref[...] = jnp.zeros_like(st_ref)

    left = lax.broadcasted_iota(jnp.int32, (CHUNK, PAIR), 1) < HEAD_DIM

    def bd(x):
        z = jnp.zeros_like(x)
        return jnp.concatenate([jnp.where(left, x, z), jnp.where(left, z, x)], axis=0)

    n_pairs = D_HEADS // PAIR
    st = [st_ref[:, p * PAIR:(p + 1) * PAIR] for p in range(n_pairs)]
    for c in range(n_chunks):
        rs = slice(c * CHUNK, (c + 1) * CHUNK)
        for p in range(n_pairs):
            ls = slice(p * PAIR, (p + 1) * PAIR)
            lh, ll = _split2(jnp.concatenate([q1_ref[rs, ls], m_ref[rs, ls]], axis=0))
            sh, sl = _split2(st[p])
            bh = bd(sh)
            res = _dot(lh, bh) + (_dot(ll, bh) + _dot(lh, bd(sl)))
            y_ref[rs, ls] = res[:CHUNK] + y0_ref[rs, ls]
            st[p] = res[CHUNK:] + n0_ref[rs, ls]
    for p in range(n_pairs):
        st_ref[:, p * PAIR:(p + 1) * PAIR] = st[p]


def _rwkv_scan(q1, y0, m, n0, batch, seq):
    t = q1.shape[0]
    cb = 8
    steps = seq // (cb * CHUNK)
    tok_spec = pl.BlockSpec((cb * CHUNK, D_HEADS), lambda b, j: (b * steps + j, 0))
    return pl.pallas_call(
        functools.partial(_rwkv_scan_kernel, n_chunks=cb),
        grid=(batch, steps),
        in_specs=[tok_spec] * 4,
        out_specs=tok_spec,
        out_shape=jax.ShapeDtypeStruct((t, D_HEADS), F32),
        scratch_shapes=[pltpu.VMEM((HEAD_DIM, D_HEADS), F32)],
        compiler_params=_cparams("parallel", "arbitrary"),
        name="rwkv_scan",
    )(q1, y0, m, n0)


def _mix_kernel(y_ref, g_ref, bonus_ref, ya_ref, gate_ref, x_ref, wa_ref, wb_ref, wo_ref,
                eones_ref, gng_ref, gnb_ref, lng_ref, lnb_ref, o_ref):
    e_ones = eones_ref[...]
    y = y_ref[...]
    mean = _seg_sum(y, e_ones) * (1.0 / HEAD_DIM)
    yc = y - mean
    var = _seg_sum(yc * yc, e_ones) * (1.0 / HEAD_DIM)
    yn = yc * lax.rsqrt(var + GN_EPS) * gng_ref[...] + gnb_ref[...]
    yr = (yn + bonus_ref[...].astype(F32)) * g_ref[...].astype(F32)
    za = _dot(ya_ref[...], wa_ref[...])
    zb = _dot(yr.astype(BF16), wb_ref[...])
    gate = gate_ref[...]
    mixed_in = gate[:, :D_MODEL].astype(F32) * za + gate[:, D_MODEL:].astype(F32) * zb
    mixed = _dot(mixed_in.astype(BF16), wo_ref[...])
    o_ref[...] = _layer_norm(DEEPNORM_ALPHA * x_ref[...] + mixed, lng_ref[...], lnb_ref[...])


def _mix(y, g, bonus, y_attn, gate, x, lp):
    t = x.shape[0]
    tm = min(256, t)
    row = lambda w: pl.BlockSpec((tm, w), lambda i: (i, 0))
    params = [lp["w_a"], lp["w_b"], lp["w_o"], lp["e_ones"], lp["gn_g"], lp["gn_b"], lp["ln_g0"], lp["ln_b0"]]
    return pl.pallas_call(
        _mix_kernel,
        grid=(t // tm,),
        in_specs=[row(D_HEADS), row(D_HEADS), row(D_HEADS), row(D_HEADS), row(2 * D_MODEL), row(D_MODEL)]
        + [_full(p.shape) for p in params],
        out_specs=row(D_MODEL),
        out_shape=jax.ShapeDtypeStruct((t, D_MODEL), F32),
        compiler_params=_cparams("parallel"),
        name="branch_mix",
    )(y, g, bonus, y_attn, gate, x, *params)


MOE_SLAB = 2048
MOE_WIN = 128
MOE_EBLK = 4
MOE_ROWS = 2 * MOE_SLAB + MOE_WIN
SUBLANES = 8
RANK_BLK = 256
ROUTE_COLS = 8


def _route_kernel(x_ref, wr_ref, br_ref, tri_ref, w_ref, pos_ref, seg_ref):
    x = x_ref[...]
    tn = x.shape[0]
    lane = lax.broadcasted_iota(jnp.int32, (tn, LANES), 1)
    lanef = lane.astype(F32)
    hp = lax.Precision.HIGHEST
    elog = _dot3(x, wr_ref[...]) + br_ref[...]
    big = float(LANES)
    glog = jnp.where((lane >= N_EXPERTS) & (lane < N_EXPERTS + N_GROUPS), elog, -jnp.inf)
    gmax = jnp.max(glog, axis=-1, keepdims=True)
    g_gate = 1.0 / jnp.sum(jnp.exp(glog - gmax), axis=-1, keepdims=True)
    grp = jnp.min(jnp.where(glog == gmax, lanef, big), axis=-1, keepdims=True) - N_EXPERTS
    in_grp = (lanef >= grp * EXPERTS_PER_GROUP) & (lanef < (grp + 1.0) * EXPERTS_PER_GROUP)
    sel = jnp.where(in_grp, elog, -jnp.inf)
    v1 = jnp.max(sel, axis=-1, keepdims=True)
    i1 = jnp.min(jnp.where(sel == v1, lanef, big), axis=-1, keepdims=True)
    sel2 = jnp.where(lanef == i1, -jnp.inf, sel)
    v2 = jnp.max(sel2, axis=-1, keepdims=True)
    i2 = jnp.min(jnp.where(sel2 == v2, lanef, big), axis=-1, keepdims=True)
    e2 = jnp.exp(v2 - v1)
    w1 = g_gate / (1.0 + e2)
    w2 = g_gate * e2 / (1.0 + e2)
    w_ref[...] = jnp.where(lane == 0, w1, jnp.where(lane == 1, w2, 0.0))[:, :ROUTE_COLS]

    hit1 = lanef == i1
    hit2 = lanef == i2
    onehot = jnp.where(hit1 | hit2, 1.0, 0.0)
    tri = tri_ref[...]
    carry = jnp.zeros((1, LANES), F32)
    ranks = []
    for j in range(tn // RANK_BLK):
        blk = onehot[j * RANK_BLK:(j + 1) * RANK_BLK]
        ranks.append(_dot(tri, blk.astype(BF16)) + carry)
        carry = carry + jnp.sum(blk, axis=0, keepdims=True)
    rank = jnp.concatenate(ranks, axis=0)
    cnt = carry
    li = lax.broadcasted_iota(jnp.int32, (LANES, LANES), 0)
    lj = lax.broadcasted_iota(jnp.int32, (LANES, LANES), 1)
    before = jnp.where(li < lj, 1.0, 0.0)
    off = jnp.dot(jnp.broadcast_to(cnt, (8, LANES)), before, precision=hp,
                  preferred_element_type=F32)[0:1]
    slot = (off + rank) * SUBLANES
    pos1 = jnp.sum(jnp.where(hit1, slot, 0.0), axis=-1, keepdims=True)
    pos2 = jnp.sum(jnp.where(hit2, slot, 0.0), axis=-1, keepdims=True)
    pos_ref[...] = jnp.where(lane == 0, pos1, jnp.where(lane == 1, pos2, 0.0)).astype(jnp.int32)[:, :ROUTE_COLS]
    row8 = lax.broadcasted_iota(jnp.int32, (8, LANES), 0)
    seg_ref[0] = jnp.where(row8 == 0, off, jnp.where(row8 == 1, cnt, 0.0)).astype(jnp.int32)


def _route(x, lp):
    t = x.shape[0]
    tn = min(MOE_SLAB, t)
    n_slabs = t // tn
    small = [lp["w_route"], lp["b_route"], lp["tri_rank"]]
    row = lambda w: pl.BlockSpec((tn, w), lambda i: (i, 0))
    return pl.pallas_call(
        _route_kernel,
        grid=(n_slabs,),
        in_specs=[row(D_MODEL)] + [_full(p.shape) for p in small],
        out_specs=[row(ROUTE_COLS), row(ROUTE_COLS), pl.BlockSpec((1, 8, LANES), lambda i: (i, 0, 0))],
        out_shape=[jax.ShapeDtypeStruct((t, ROUTE_COLS), F32), jax.ShapeDtypeStruct((t, ROUTE_COLS), jnp.int32),
                   jax.ShapeDtypeStruct((n_slabs, 8, LANES), jnp.int32)],
        compiler_params=_cparams("parallel"),
        name="moe_route",
    )(x, *small)


def _moe_sparse_kernel(off_ref, cnt_ref, pos_ref, w_ref, x_ref, eg_ref, eu_ref, ed_ref, lng_ref, lnb_ref,
                       o_ref, xs_ref):
    s = pl.program_id(0)
    e = pl.program_id(1)
    tn = x_ref.shape[0]

    n_feat = D_MODEL // LANES

    def tile_rows(first_row):
        return pl.ds(pl.multiple_of(first_row, SUBLANES), SUBLANES)

    @pl.when((s == 0) & (e == 0))
    def _():
        xs_ref[pl.ds(2 * tn * SUBLANES, MOE_WIN * SUBLANES), :] = jnp.zeros((MOE_WIN * SUBLANES, LANES), F32)

    @pl.when(e == 0)
    def _():
        def scatter(i, carry):
            t0 = pl.multiple_of(i * SUBLANES, SUBLANES)
            tiles = x_ref[pl.ds(t0, SUBLANES), :].reshape(SUBLANES, n_feat, LANES)
            for j in range(SUBLANES):
                xs_ref[tile_rows(pos_ref[0, 0, 2 * (t0 + j)]), :] = tiles[j]
                xs_ref[tile_rows(pos_ref[0, 0, 2 * (t0 + j) + 1]), :] = tiles[j]
            return carry
        lax.fori_loop(0, tn // SUBLANES, scatter, 0)

    ridx = lax.broadcasted_iota(jnp.int32, (MOE_WIN, 1), 0)

    base = s * N_EXPERTS + e * MOE_EBLK
    offs = [off_ref[base + j] for j in range(MOE_EBLK)]
    cnts = [cnt_ref[base + j] for j in range(MOE_EBLK)]

    def feat_rows(j, w, f):
        return pl.ds((offs[j] + w * MOE_WIN) * SUBLANES + f, MOE_WIN, stride=SUBLANES)

    def load_win(j, w):
        return jnp.concatenate([xs_ref[feat_rows(j, w, f), :] for f in range(n_feat)], axis=1)

    def hidden(j, xw):
        xb = xw.astype(BF16)
        hg = _dot(xb, eg_ref[j])
        hu = _dot(xb, eu_ref[j])
        return (hg * _sigmoid(hg) * hu).astype(BF16)

    def project(j, w, xw, hh):
        yw = _dot(hh, ed_ref[j])
        return jnp.where(ridx + w * MOE_WIN < cnts[j], yw, xw)

    def expert(j, w, xw):
        return project(j, w, xw, hidden(j, xw))

    def store_win(j, w, yw):
        for f in range(n_feat):
            xs_ref[feat_rows(j, w, f), :] = yw[:, f * LANES:(f + 1) * LANES]

    xw = [None] * MOE_EBLK
    hh = [None] * MOE_EBLK
    for j in range(MOE_EBLK + 1):
        if j < MOE_EBLK:
            xw[j] = load_win(j, 0)
            hh[j] = hidden(j, xw[j])
        if j > 0:
            store_win(j - 1, 0, project(j - 1, 0, xw[j - 1], hh[j - 1]))

    for j in range(MOE_EBLK):
        def window(w, carry, j=j):
            store_win(j, w, expert(j, w, load_win(j, w)))
            return carry
        lax.fori_loop(1, (cnts[j] + MOE_WIN - 1) // MOE_WIN, window, 0)

    @pl.when(e == N_EXPERTS // MOE_EBLK - 1)
    def _():
        def combine(i, carry):
            t0 = pl.multiple_of(i * SUBLANES, SUBLANES)
            tiles = []
            for j in range(SUBLANES):
                y1 = xs_ref[tile_rows(pos_ref[0, 0, 2 * (t0 + j)]), :]
                y2 = xs_ref[tile_rows(pos_ref[0, 0, 2 * (t0 + j) + 1]), :]
                tiles.append(w_ref[0, 0, 2 * (t0 + j)] * y1 + w_ref[0, 0, 2 * (t0 + j) + 1] * y2)
            o_ref[pl.ds(t0, SUBLANES), :] = jnp.stack(tiles, axis=0).reshape(SUBLANES, D_MODEL)
            return carry
        lax.fori_loop(0, tn // SUBLANES, combine, 0)
        for j in range(tn // RANK_BLK):
            rs = slice(j * RANK_BLK, (j + 1) * RANK_BLK)
            o_ref[rs, :] = _layer_norm(DEEPNORM_ALPHA * x_ref[rs, :] + o_ref[rs, :], lng_ref[...], lnb_ref[...])


def _moe_sparse(x, lp):
    t = x.shape[0]
    tn = min(MOE_SLAB, t)
    n_slabs = t // tn
    w, pos, seg = _route(x, lp)
    w_flat = w[:, :2].reshape(n_slabs, 1, 2 * tn)
    pos_flat = pos[:, :2].reshape(n_slabs, 1, 2 * tn)
    off = seg[:, 0, :N_EXPERTS].reshape(-1)
    cnt = seg[:, 1, :N_EXPERTS].reshape(-1)
    smem_row = pl.BlockSpec((1, 1, 2 * tn), lambda s, e, *_: (s, 0, 0), memory_space=pltpu.SMEM)
    slab = pl.BlockSpec((tn, D_MODEL), lambda s, e, *_: (s, 0), pipeline_mode=pl.Buffered(1))
    const = lambda p: pl.BlockSpec(p.shape, lambda s, e, *_: (0,) * p.ndim)
    grid_spec = pltpu.PrefetchScalarGridSpec(
        num_scalar_prefetch=2,
        grid=(n_slabs, N_EXPERTS // MOE_EBLK),
        in_specs=[smem_row, smem_row, slab,
                  pl.BlockSpec((MOE_EBLK, D_MODEL, D_EXPERT), lambda s, e, *_: (e, 0, 0)),
                  pl.BlockSpec((MOE_EBLK, D_MODEL, D_EXPERT), lambda s, e, *_: (e, 0, 0)),
                  pl.BlockSpec((MOE_EBLK, D_EXPERT, D_MODEL), lambda s, e, *_: (e, 0, 0)),
                  const(lp["ln_g1"]), const(lp["ln_b1"])],
        out_specs=slab,
        scratch_shapes=[pltpu.VMEM((MOE_ROWS * SUBLANES, LANES), F32)],
    )
    return pl.pallas_call(
        _moe_sparse_kernel,
        grid_spec=grid_spec,
        out_shape=jax.ShapeDtypeStruct((t, D_MODEL), F32),
        compiler_params=_cparams("arbitrary", "arbitrary"),
        name="moe_sparse",
    )(off, cnt, pos_flat, w_flat, x, lp["e_gate"], lp["e_up"], lp["e_down"], lp["ln_g1"], lp["ln_b1"])


def _ple_kernel(x_ref, p_ref, wp_ref, wg_ref, lng_ref, lnb_ref, o_ref):
    x = x_ref[...]
    emb = _dot(p_ref[...].astype(BF16), wp_ref[...])
    gate = _sigmoid(_dot(x.astype(BF16), wg_ref[...]))
    o_ref[...] = _layer_norm(DEEPNORM_ALPHA * x + emb * gate, lng_ref[...], lnb_ref[...])


def _ple(x, p, lp):
    t = x.shape[0]
    tm = min(512, t)
    d_ple = p.shape[1]
    row = lambda w: pl.BlockSpec((tm, w), lambda i: (i, 0))
    params = [lp["w_ple"], lp["w_pg"], lp["ln_g2"], lp["ln_b2"]]
    return pl.pallas_call(
        _ple_kernel,
        grid=(t // tm,),
        in_specs=[row(D_MODEL), row(d_ple)] + [_full(q.shape) for q in params],
        out_specs=row(D_MODEL),
        out_shape=jax.ShapeDtypeStruct((t, D_MODEL), F32),
        compiler_params=_cparams("parallel"),
        name="ple",
    )(x, p, *params)


def _pad_to(a, axis, size):
    pad = [(0, 0)] * a.ndim
    pad[axis] = (0, size - a.shape[axis])
    return jnp.pad(a, pad)


def _row(v):
    return v.reshape(1, -1).astype(F32)


def _layer_params(i, w_in, tok_mix, decay_base, decay_up, aaa_base, aaa_up, gate_up, k_k, k_a, r_k,
                  vres_base, vres_down, vres_up, gn_g, gn_b, w_branch_attn, w_branch_rwkv, w_out,
                  router_grp, router_grp_bias, router_exp, router_exp_bias, exp_gate, exp_up, exp_down,
                  ple_proj, ple_gate, ln_g, ln_b, consts):
    d = D_HEADS
    w = w_in[i]
    rw0 = 3 * d
    gate0 = rw0 + 3 * d + DECAY_LORA + AAA_LORA + GATE_LORA
    wqkv = jnp.concatenate([w[:, 0:d] * (HEAD_DIM ** -0.5), w[:, d:rw0]], axis=1).astype(BF16)
    wr = w[:, rw0:gate0]
    lora0 = 3 * d
    pieces = [wr[:, :lora0],
              _pad_to(wr[:, lora0:lora0 + DECAY_LORA], 1, LANES),
              _pad_to(wr[:, lora0 + DECAY_LORA:lora0 + DECAY_LORA + AAA_LORA], 1, LANES),
              _pad_to(wr[:, lora0 + DECAY_LORA + AAA_LORA:], 1, 2 * LANES)]
    wrw = jnp.concatenate(pieces, axis=1).astype(BF16)
    mu = tok_mix[i]
    mu_pieces = [mu[:lora0],
                 _pad_to(mu[lora0:lora0 + DECAY_LORA], 0, LANES),
                 _pad_to(mu[lora0 + DECAY_LORA:lora0 + DECAY_LORA + AAA_LORA], 0, LANES),
                 _pad_to(mu[lora0 + DECAY_LORA + AAA_LORA:], 0, 2 * LANES)]
    lp = dict(consts)
    lp.update(
        wqkv=wqkv, wrw=wrw, wgate=w[:, gate0:].astype(BF16),
        mu=_row(jnp.concatenate(mu_pieces)),
        decay_base=_row(decay_base[i]), decay_up=_pad_to(decay_up[i], 0, LANES).astype(F32),
        aaa_base=_row(aaa_base[i]), aaa_up=_pad_to(aaa_up[i], 0, LANES).astype(F32),
        gate_up=_pad_to(gate_up[i], 0, 2 * LANES).astype(F32),
        k_k=_row(k_k[i]), k_a=_row(k_a[i]), r_k=_row(r_k[i]),
        gn_g=_row(gn_g[i]), gn_b=_row(gn_b[i]),
        w_a=w_branch_attn[i].astype(BF16), w_b=w_branch_rwkv[i].astype(BF16), w_o=w_out[i].astype(BF16),
        w_route=_pad_to(jnp.concatenate([router_exp[i], router_grp[i]], axis=1), 1, LANES).astype(F32),
        b_route=_row(_pad_to(jnp.concatenate([router_exp_bias[i], router_grp_bias[i]]), 0, LANES)),
        e_gate=exp_gate[i].astype(BF16), e_up=exp_up[i].astype(BF16), e_down=exp_down[i].astype(BF16),
        w_ple=ple_proj[i].astype(BF16), w_pg=ple_gate[i].astype(BF16),
        ln_g0=_row(ln_g[i, 0]), ln_b0=_row(ln_b[i, 0]),
        ln_g1=_row(ln_g[i, 1]), ln_b1=_row(ln_b[i, 1]),
        ln_g2=_row(ln_g[i, 2]), ln_b2=_row(ln_b[i, 2]),
    )
    if i > 0:
        lp.update(vres_base=_row(vres_base[i - 1]),
                  vres_down=_pad_to(vres_down[i - 1], 1, LANES).astype(F32),
                  vres_up=_pad_to(vres_up[i - 1], 0, LANES).astype(F32))
    return lp


def _const_mats(tm):
    head = jnp.arange(D_HEADS) // HEAD_DIM
    e_ones = (head[:, None] == head[None, :]).astype(BF16)
    tok = jnp.arange(tm)
    same = (tok[:, None] // CHUNK) == (tok[None, :] // CHUNK)
    tri = (same & (tok[:, None] >= tok[None, :])).astype(BF16)
    blk = same.astype(BF16)
    rtok = jnp.arange(RANK_BLK)
    tri_rank = (rtok[None, :] < rtok[:, None]).astype(BF16)
    return dict(e_ones=e_ones, tri=tri, blk=blk, tri_rank=tri_rank)


def kernel(x, p, ln_in_g, ln_in_b, rel_bias, w_in, tok_mix, decay_base, decay_up, aaa_base, aaa_up,
           gate_up, k_k, k_a, r_k, vres_base, vres_down, vres_up, gn_g, gn_b, w_branch_attn,
           w_branch_rwkv, w_out, router_grp, router_grp_bias, router_exp, router_exp_bias, exp_gate,
           exp_up, exp_down, ple_proj, ple_gate, ln_g, ln_b):
    batch, seq, _ = x.shape
    t = batch * seq
    consts = _const_mats(256)
    bias_tabs = _attn_bias_tables(rel_bias)
    xt = x.reshape(t, D_MODEL)
    pt = p.reshape(p.shape[0], t, p.shape[-1])
    v_first = None
    for i in range(DEPTH):
        lp = _layer_params(i, w_in, tok_mix, decay_base, decay_up, aaa_base, aaa_up, gate_up, k_k, k_a,
                           r_k, vres_base, vres_down, vres_up, gn_g, gn_b, w_branch_attn, w_branch_rwkv,
                           w_out, router_grp, router_grp_bias, router_exp, router_exp_bias, exp_gate,
                           exp_up, exp_down, ple_proj, ple_gate, ln_g, ln_b, consts)
        if i == 0:
            xt, qkv, rw, gate = _proj(xt, _row(ln_in_g), _row(ln_in_b), lp["wqkv"], lp["wrw"],
                                      lp["wgate"], apply_ln=True)
        else:
            qkv, rw, gate = _proj(xt, _row(ln_in_g), _row(ln_in_b), lp["wqkv"], lp["wrw"],
                                  lp["wgate"], apply_ln=False)
        y_attn = _attention(qkv, bias_tabs, batch, seq)
        if i == 0:
            q1, y0, m, n0, g, bonus, v_first = _rwkv_chunk(rw, lp, seq, None)
        else:
            q1, y0, m, n0, g, bonus = _rwkv_chunk(rw, lp, seq, v_first)
        y = _rwkv_scan(q1, y0, m, n0, batch, seq)
        xt = _mix(y, g, bonus, y_attn, gate, xt, lp)
        xt = _moe_sparse(xt, lp)
        xt = _ple(xt, pt[i], lp)
    return xt.reshape(batch, seq, D_MODEL)
```

```python
import functools
import math

import jax
import jax.numpy as jnp
from jax import lax
from jax.experimental import pallas as pl
from jax.experimental.pallas import tpu as pltpu

F32 = jnp.float32
BF16 = jnp.bfloat16

D_MODEL = 1024
CHUNK = 64
N_PREV_CHUNKS = 8
HEAD_DIM = 64
N_HEADS = 8
D_HEADS = N_HEADS * HEAD_DIM
REL_CLIP = 128
DECAY_LORA = 64
AAA_LORA = 64
GATE_LORA = 160
MV_LORA = 32
N_GROUPS = 4
EXPERTS_PER_GROUP = 8
N_EXPERTS = N_GROUPS * EXPERTS_PER_GROUP
D_EXPERT = 256
DEPTH = 2
DEEPNORM_ALPHA = (2 * DEPTH) ** 0.25
LN_EPS = 1e-5
GN_EPS = 64e-5
NEG_INF = -1e30

LANES = 128
PAIR = 2 * HEAD_DIM
GRP_HEADS = 2
GRP = GRP_HEADS * HEAD_DIM
ATTN_QBLK = 2 * CHUNK
ATTN_WIN = (N_PREV_CHUNKS + 2) * CHUNK
ATTN_GROUP = 4
RW_COLS_PAD = 2048
VMEM_LIMIT = 56 * 1024 * 1024


def _cparams(*sem):
    return pltpu.CompilerParams(dimension_semantics=sem, vmem_limit_bytes=VMEM_LIMIT)


def _dot(a, b):
    return jnp.dot(a, b, preferred_element_type=F32)


def _dot_nt(a, b):
    return lax.dot_general(a, b, (((1,), (1,)), ((), ())), preferred_element_type=F32)


def _dot_tn(a, b):
    return lax.dot_general(a, b, (((0,), (0,)), ((), ())), preferred_element_type=F32)


def _split2(x):
    hi = x.astype(BF16)
    lo = (x - hi.astype(F32)).astype(BF16)
    return hi, lo


def _dot3(a, b):
    ah, al = _split2(a)
    bh, bl = _split2(b)
    return _dot(ah, bh) + (_dot(al, bh) + _dot(ah, bl))


def _dot_exact_lhs(a_bf16, x):
    x1 = x.astype(BF16)
    r1 = x - x1.astype(F32)
    x2 = r1.astype(BF16)
    x3 = (r1 - x2.astype(F32)).astype(BF16)
    return _dot(a_bf16, x1) + (_dot(a_bf16, x2) + _dot(a_bf16, x3))


def _seg_sum(x):
    left = lax.broadcasted_iota(jnp.int32, (x.shape[0], PAIR), 1) < HEAD_DIM
    outs = []
    for p in range(x.shape[1] // PAIR):
        xp = x[:, p * PAIR:(p + 1) * PAIR]
        s0 = jnp.sum(jnp.where(left, xp, 0.0), axis=-1, keepdims=True)
        s1 = jnp.sum(jnp.where(left, 0.0, xp), axis=-1, keepdims=True)
        outs.append(jnp.where(left, s0, s1))
    return jnp.concatenate(outs, axis=-1)


def _layer_norm(x, g, b):
    mu = jnp.mean(x, axis=-1, keepdims=True)
    xc = x - mu
    var = jnp.mean(xc * xc, axis=-1, keepdims=True)
    return xc * lax.rsqrt(var + LN_EPS) * g + b


def _sigmoid(x):
    return 1.0 / (1.0 + jnp.exp(-x))


def _full(shape):
    nd = len(shape)
    return pl.BlockSpec(shape, lambda *_: (0,) * nd)


def _proj_kernel(x_ref, g_ref, b_ref, wqkv_ref, wrw_ref, wgate_ref, *out_refs, apply_ln):
    x = x_ref[...]
    if apply_ln:
        xn_ref, qkv_ref, rw_ref, gate_ref = out_refs
        x = _layer_norm(x, g_ref[...], b_ref[...])
        xn_ref[...] = x
    else:
        qkv_ref, rw_ref, gate_ref = out_refs
    xb = x.astype(BF16)
    qkv_ref[...] = _dot(xb, wqkv_ref[...]).astype(BF16)
    rw_ref[...] = _dot(xb, wrw_ref[...])
    gate_ref[...] = _sigmoid(_dot(xb, wgate_ref[...])).astype(BF16)


def _proj(x, g, b, wqkv, wrw, wgate, apply_ln):
    t = x.shape[0]
    tm = min(256, t)
    row = lambda w: pl.BlockSpec((tm, w), lambda i: (i, 0))
    out_shape = [jax.ShapeDtypeStruct((t, 3 * D_HEADS), BF16),
                 jax.ShapeDtypeStruct((t, RW_COLS_PAD), F32),
                 jax.ShapeDtypeStruct((t, 2 * D_MODEL), BF16)]
    out_specs = [row(3 * D_HEADS), row(RW_COLS_PAD), row(2 * D_MODEL)]
    if apply_ln:
        out_shape = [jax.ShapeDtypeStruct((t, D_MODEL), F32)] + out_shape
        out_specs = [row(D_MODEL)] + out_specs
    return pl.pallas_call(
        functools.partial(_proj_kernel, apply_ln=apply_ln),
        grid=(t // tm,),
        in_specs=[row(D_MODEL), _full(g.shape), _full(b.shape),
                  _full(wqkv.shape), _full(wrw.shape), _full(wgate.shape)],
        out_specs=out_specs,
        out_shape=out_shape,
        compiler_params=_cparams("parallel"),
        name="proj_ln" if apply_ln else "proj",
    )(x, g, b, wqkv, wrw, wgate)


def _attn_kernel(q_ref, k_ref, v_ref, bias_ref, o_ref):
    qb = pl.program_id(1)
    ws = pl.multiple_of(jnp.maximum(qb * ATTN_QBLK - N_PREV_CHUNKS * CHUNK, 0), ATTN_QBLK)
    q = q_ref[0]
    kw = k_ref[0, pl.ds(ws, ATTN_WIN), :]
    vw = v_ref[0, pl.ds(ws, ATTN_WIN), :]
    left = lax.broadcasted_iota(jnp.int32, (ATTN_QBLK, PAIR), 1) < HEAD_DIM
    zero = jnp.zeros((ATTN_QBLK, PAIR), BF16)
    heads = [(h // 2, h % 2) for h in range(N_HEADS)]
    spans = [slice(p * PAIR, (p + 1) * PAIR) for p in range(N_HEADS // 2)]
    o = []
    for g0 in range(0, N_HEADS, ATTN_GROUP):
        grp = list(enumerate(heads))[g0:g0 + ATTN_GROUP]
        qm = [jnp.where(left, q[:, spans[p]], zero) if side == 0 else jnp.where(left, zero, q[:, spans[p]])
              for _, (p, side) in grp]
        s = [_dot_nt(x, kw[:, spans[p]]) + bias_ref[0, h] for x, (h, (p, _)) in zip(qm, grp)]
        m = [jnp.max(x, axis=-1, keepdims=True) for x in s]
        e = [jnp.exp(x - mx) for x, mx in zip(s, m)]
        inv = [1.0 / jnp.sum(x, axis=-1, keepdims=True) for x in e]
        o += [_dot(x.astype(BF16), vw[:, spans[p]]) * r for x, r, (_, (p, _)) in zip(e, inv, grp)]
    pairs = [jnp.where(left, o[2 * p], o[2 * p + 1]) for p in range(N_HEADS // 2)]
    o_ref[0] = jnp.concatenate(pairs, axis=-1).astype(BF16)


def _attention(qkv, bias_tabs, batch, seq):
    qkv3 = qkv.reshape(batch, seq, 3 * D_HEADS)
    n_tabs = bias_tabs.shape[0]
    out = pl.pallas_call(
        _attn_kernel,
        grid=(batch, seq // ATTN_QBLK),
        in_specs=[
            pl.BlockSpec((1, ATTN_QBLK, D_HEADS), lambda b, i: (b, i, 0)),
            pl.BlockSpec((1, seq, D_HEADS), lambda b, i: (b, 0, 1)),
            pl.BlockSpec((1, seq, D_HEADS), lambda b, i: (b, 0, 2)),
            pl.BlockSpec((1, N_HEADS, ATTN_QBLK, ATTN_WIN),
                         lambda b, i: (jnp.minimum(i, n_tabs - 1), 0, 0, 0)),
        ],
        out_specs=pl.BlockSpec((1, ATTN_QBLK, D_HEADS), lambda b, i: (b, i, 0)),
        out_shape=jax.ShapeDtypeStruct((batch, seq, D_HEADS), BF16),
        compiler_params=_cparams("parallel", "arbitrary"),
        name="band_attn",
    )(qkv3, qkv3, qkv3, bias_tabs)
    return out.reshape(batch * seq, D_HEADS)


def _attn_bias_tables(rel_bias):
    n_tabs = N_PREV_CHUNKS * CHUNK // ATTN_QBLK + 1
    start = (jnp.arange(n_tabs) * ATTN_QBLK)[:, None, None]
    qpos = start + jnp.arange(ATTN_QBLK)[None, :, None]
    kpos = jnp.arange(ATTN_WIN)[None, None, :]
    qc, kc = qpos // CHUNK, kpos // CHUNK
    valid = (kc <= qc) & (kc >= qc - N_PREV_CHUNKS)
    n = jnp.arange(ATTN_QBLK - 1 + ATTN_WIN)
    dist = start[:, :, 0] + (ATTN_QBLK - 1) - n[None, :]
    vals = rel_bias.astype(F32)[:, jnp.clip(dist, -REL_CLIP, REL_CLIP) + REL_CLIP]
    length = ATTN_QBLK - 1 + ATTN_WIN
    tiled = jnp.tile(_pad_to(vals, 2, length + 1), (1, 1, ATTN_QBLK))
    cut = tiled[:, :, :ATTN_QBLK * length].reshape(N_HEADS, n_tabs, ATTN_QBLK, length)
    bias = cut[:, :, :, ATTN_QBLK - 1:ATTN_QBLK - 1 + ATTN_WIN]
    bias = jnp.where(valid[None], bias, NEG_INF)
    return jnp.transpose(bias, (1, 0, 2, 3))


def _rwkv_chunk_kernel(*refs, seq, tm, has_vres):
    if has_vres:
        (rw_ref, prev_ref, mu_ref, dbase_ref, dup_ref, abase_ref, aup_ref, gup_ref, kk_ref, ka_ref,
         rk_ref, tri_ref, blk_ref, vfirst_ref, vbase_ref, vdown_ref, vup_ref,
         q1_ref, y0_ref, m_ref, n0_ref, g_ref, bonus_ref,
         rw_s, aw_s, bi_s, ki_s, be_s, ke_s, v_s, wc_s) = refs
    else:
        (rw_ref, prev_ref, mu_ref, dbase_ref, dup_ref, abase_ref, aup_ref, gup_ref, kk_ref, ka_ref,
         rk_ref, tri_ref, blk_ref,
         q1_ref, y0_ref, m_ref, n0_ref, g_ref, bonus_ref, vfirst_out_ref,
         rw_s, aw_s, bi_s, ki_s, be_s, ke_s, v_s, wc_s) = refs

    cols = rw_ref[...]
    first = (pl.program_id(0) % (seq // tm)) == 0
    prev_row = jnp.where(first, 0.0, prev_ref[7:8, :])
    rows = lax.broadcasted_iota(jnp.int32, cols.shape, 0)
    prev = jnp.where(rows == 0, prev_row, pltpu.roll(cols, 1, axis=0))
    xs = cols + (prev - cols) * mu_ref[...]

    d = D_HEADS
    r, k, v = xs[:, 0:d], xs[:, d:2 * d], xs[:, 2 * d:3 * d]
    wd = xs[:, 3 * d:3 * d + LANES]
    ad = xs[:, 3 * d + LANES:3 * d + 2 * LANES]
    gd = xs[:, 3 * d + 2 * LANES:3 * d + 4 * LANES]

    u = dbase_ref[...] + _dot3(jnp.tanh(wd), dup_ref[...])
    w_log = jnp.minimum(u, 0.0) - jnp.log(1.0 + jnp.exp(-jnp.abs(u))) - 0.5
    logw = -jnp.exp(w_log)
    a_icl = _sigmoid(abase_ref[...] + _dot3(ad, aup_ref[...]))
    g_ref[...] = _dot3(_sigmoid(gd), gup_ref[...]).astype(BF16)
    kk = k * kk_ref[...]
    kk = kk * lax.rsqrt(jnp.maximum(_seg_sum(kk * kk), 1e-24))
    k2 = k * (1.0 + (a_icl - 1.0) * ka_ref[...])
    if has_vres:
        mix = _sigmoid(vbase_ref[...] + _dot3(_dot3(v, vdown_ref[...]), vup_ref[...]))
        v2 = v + (vfirst_ref[...] - v) * mix
    else:
        v2 = v
        vfirst_out_ref[...] = v
    a_vec = -kk
    b_vec = kk * a_icl
    bonus_ref[...] = (_seg_sum(r * k2 * rk_ref[...]) * v2).astype(BF16)

    cw = _dot_exact_lhs(tri_ref[...], logw)
    cw_tot = _dot_exact_lhs(blk_ref[...], logw)
    rw_s[...] = r * jnp.exp(cw)
    aw_s[...] = a_vec * jnp.exp(cw - logw)
    inv = jnp.exp(-cw)
    bi_s[...] = b_vec * inv
    ki_s[...] = k2 * inv
    rest = jnp.exp(cw_tot - cw)
    be_s[...] = b_vec * rest
    ke_s[...] = k2 * rest
    v_s[...] = v2
    wc_s[...] = jnp.exp(cw_tot)

    shape = (CHUNK, GRP)
    ti = lax.broadcasted_iota(jnp.int32, shape, 0)
    li = lax.broadcasted_iota(jnp.int32, shape, 1)
    si = li & (HEAD_DIM - 1)
    hid = li // HEAD_DIM
    strict = ti > si
    incl = ti >= si
    eye = ti == si

    def bd(x):
        z = jnp.zeros_like(x)
        return jnp.concatenate([jnp.where(hid == h, x, z) for h in range(GRP_HEADS)], axis=0)

    def tpair(x):
        return jnp.concatenate([x[:, h * HEAD_DIM:(h + 1) * HEAD_DIM].T for h in range(GRP_HEADS)], axis=1)

    def b16(xs):
        return [x.astype(BF16) for x in xs]

    def summaries(insts):
        sl = [(slice(c * CHUNK, (c + 1) * CHUNK), slice(p * GRP, (p + 1) * GRP)) for c, p in insts]
        rw = [rw_s[s] for s in sl]
        awb = b16([aw_s[s] for s in sl])
        vb = b16([v_s[s] for s in sl])
        lhs = [jnp.concatenate([a, r.astype(BF16)], axis=0) for a, r in zip(awb, rw)]
        bik = [jnp.concatenate([bd(b), bd(k)], axis=0)
               for b, k in zip(b16([bi_s[s] for s in sl]), b16([ki_s[s] for s in sl]))]
        g = [_dot_nt(l, x) for l, x in zip(lhs, bik)]
        a_ab = [jnp.where(strict, x[:CHUNK, :GRP], 0.0) for x in g]
        a_rb = b16([jnp.where(incl, x[CHUNK:, :GRP], 0.0) for x in g])
        a_ak = b16([jnp.where(strict, x[:CHUNK, GRP:], 0.0) for x in g])
        a_rk = b16([jnp.where(incl, x[CHUNK:, GRP:], 0.0) for x in g])
        bdv = [bd(x) for x in vb]
        x0 = [_dot(a, v) for a, v in zip(a_ak, bdv)]
        tinv = [jnp.where(eye, 1.0, a) for a in a_ab]
        lpb = b16(a_ab)
        lpb = b16([_dot(l, bd(l)) for l in lpb])
        for _ in range(4):
            both = [_dot(jnp.concatenate([t.astype(BF16), l], axis=0), bd(l)) for t, l in zip(tinv, lpb)]
            tinv = [t + x[:CHUNK] for t, x in zip(tinv, both)]
            lpb = b16([x[CHUNK:] for x in both])
        tinv = [t + _dot(t.astype(BF16), bd(l)) for t, l in zip(tinv, lpb)]
        tb = b16(tinv)
        pu = b16([_dot(t, jnp.concatenate([bd(a), bd(x.astype(BF16))], axis=1))
                  for t, a, x in zip(tb, awb, x0)])
        pu_bd = [jnp.concatenate([bd(x[:, :GRP]), bd(x[:, GRP:])], axis=1) for x in pu]
        qy = [_dot(a, r) for a, r in zip(a_rb, pu_bd)]
        q1 = [r + x[:, :GRP] for r, x in zip(rw, qy)]
        y0 = [x[:, GRP:] + _dot(k, v) for x, k, v in zip(qy, a_rk, bdv)]
        bet = b16([tpair(be_s[s]) for s in sl])
        ket = b16([tpair(ke_s[s]) for s in sl])
        mn = [_dot(b, r) for b, r in zip(bet, pu_bd)]
        mm = [x[:, :GRP] for x in mn]
        nn = [x[:, GRP:] + _dot(k, v) for x, k, v in zip(mn, ket, bdv)]
        for i, (s, (c, p)) in enumerate(zip(sl, insts)):
            wc = wc_s[c * CHUNK:c * CHUNK + 1, s[1]]
            q1_ref[s] = q1[i]
            y0_ref[s] = y0[i]
            m_ref[s] = jnp.where(eye, wc, 0.0) + mm[i]
            n0_ref[s] = nn[i]

    summaries([(c, p) for c in range(tm // CHUNK) for p in range(D_HEADS // GRP)])


def _rwkv_chunk(rw, lp, seq, v_first):
    t = rw.shape[0]
    tm = 256
    has_vres = v_first is not None
    nct = tm // CHUNK
    row = lambda w: pl.BlockSpec((tm, w), lambda i: (i, 0))
    prev_spec = pl.BlockSpec((8, RW_COLS_PAD), lambda i: (jnp.maximum(i * (tm // 8) - 1, 0), 0))
    params = [lp["mu"], lp["decay_base"], lp["decay_up"], lp["aaa_base"], lp["aaa_up"], lp["gate_up"],
              lp["k_k"], lp["k_a"], lp["r_k"], lp["tri"], lp["blk"]]
    inputs = [rw, rw] + params
    in_specs = [row(RW_COLS_PAD), prev_spec] + [_full(p.shape) for p in params]
    if has_vres:
        extra = [lp["vres_base"], lp["vres_down"], lp["vres_up"]]
        inputs += [v_first] + extra
        in_specs += [row(D_HEADS)] + [_full(p.shape) for p in extra]
    tok_shape = jax.ShapeDtypeStruct((t, D_HEADS), F32)
    half_shape = jax.ShapeDtypeStruct((t, D_HEADS), BF16)
    out_shape = [tok_shape] * 4 + [half_shape] * 2
    out_specs = [row(D_HEADS)] * 6
    if not has_vres:
        out_shape.append(tok_shape)
        out_specs.append(row(D_HEADS))
    scratch = [pltpu.VMEM((tm, D_HEADS), F32) for _ in range(8)]
    return pl.pallas_call(
        functools.partial(_rwkv_chunk_kernel, seq=seq, tm=tm, has_vres=has_vres),
        grid=(t // tm,),
        in_specs=in_specs,
        out_specs=out_specs,
        out_shape=out_shape,
        scratch_shapes=scratch,
        compiler_params=_cparams("parallel"),
        name="rwkv_chunk_vres" if has_vres else "rwkv_chunk",
    )(*inputs)


def _rwkv_scan_kernel(q1_ref, y0_ref, m_ref, n0_ref, y_ref, st_ref, *, n_chunks):
    @pl.when(pl.program_id(1) == 0)
    def _():
        st_ref[...] = jnp.zeros_like(st_ref)

    left = lax.broadcasted_iota(jnp.int32, (CHUNK, PAIR), 1) < HEAD_DIM

    def bd(x):
        z = jnp.zeros_like(x)
        return jnp.concatenate([jnp.where(left, x, z), jnp.where(left, z, x)], axis=0)

    n_pairs = D_HEADS // PAIR
    st = [st_ref[:, p * PAIR:(p + 1) * PAIR] for p in range(n_pairs)]
    for c in range(n_chunks):
        rs = slice(c * CHUNK, (c + 1) * CHUNK)
        for p in range(n_pairs):
            ls = slice(p * PAIR, (p + 1) * PAIR)
            lh, ll = _split2(jnp.concatenate([q1_ref[rs, ls], m_ref[rs, ls]], axis=0))
            sh, sl = _split2(st[p])
            bh = bd(sh)
            res = _dot(lh, bh) + (_dot(ll, bh) + _dot(lh, bd(sl)))
            y_ref[rs, ls] = res[:CHUNK] + y0_ref[rs, ls]
            st[p] = res[CHUNK:] + n0_ref[rs, ls]
    for p in range(n_pairs):
        st_ref[:, p * PAIR:(p + 1) * PAIR] = st[p]


def _rwkv_scan(q1, y0, m, n0, batch, seq):
    t = q1.shape[0]
    cb = 8
    steps = seq // (cb * CHUNK)
    tok_spec = pl.BlockSpec((cb * CHUNK, D_HEADS), lambda b, j: (b * steps + j, 0))
    return pl.pallas_call(
        functools.partial(_rwkv_scan_kernel, n_chunks=cb),
        grid=(batch, steps),
        in_specs=[tok_spec] * 4,
        out_specs=tok_spec,
        out_shape=jax.ShapeDtypeStruct((t, D_HEADS), F32),
        scratch_shapes=[pltpu.VMEM((HEAD_DIM, D_HEADS), F32)],
        compiler_params=_cparams("parallel", "arbitrary"),
        name="rwkv_scan",
    )(q1, y0, m, n0)


def _mix_kernel(y_ref, g_ref, bonus_ref, ya_ref, gate_ref, x_ref, wa_ref, wb_ref, wo_ref,
                gng_ref, gnb_ref, lng_ref, lnb_ref, o_ref):
    y = y_ref[...]
    mean = _seg_sum(y) * (1.0 / HEAD_DIM)
    yc = y - mean
    var = _seg_sum(yc * yc) * (1.0 / HEAD_DIM)
    yn = yc * lax.rsqrt(var + GN_EPS) * gng_ref[...] + gnb_ref[...]
    yr = (yn + bonus_ref[...].astype(F32)) * g_ref[...].astype(F32)
    za = _dot(ya_ref[...], wa_ref[...])
    zb = _dot(yr.astype(BF16), wb_ref[...])
    gate = gate_ref[...]
    mixed_in = gate[:, :D_MODEL].astype(F32) * za + gate[:, D_MODEL:].astype(F32) * zb
    mixed = _dot(mixed_in.astype(BF16), wo_ref[...])
    o_ref[...] = _layer_norm(DEEPNORM_ALPHA * x_ref[...] + mixed, lng_ref[...], lnb_ref[...])


def _mix(y, g, bonus, y_attn, gate, x, lp):
    t = x.shape[0]
    tm = min(256, t)
    row = lambda w: pl.BlockSpec((tm, w), lambda i: (i, 0))
    params = [lp["w_a"], lp["w_b"], lp["w_o"], lp["gn_g"], lp["gn_b"], lp["ln_g0"], lp["ln_b0"]]
    return pl.pallas_call(
        _mix_kernel,
        grid=(t // tm,),
        in_specs=[row(D_HEADS), row(D_HEADS), row(D_HEADS), row(D_HEADS), row(2 * D_MODEL), row(D_MODEL)]
        + [_full(p.shape) for p in params],
        out_specs=row(D_MODEL),
        out_shape=jax.ShapeDtypeStruct((t, D_MODEL), F32),
        compiler_params=_cparams("parallel"),
        name="branch_mix",
    )(y, g, bonus, y_attn, gate, x, *params)


MOE_SLAB = 2048
MOE_WIN = 128
MOE_EBLK = 4
MOE_ROWS = 2 * MOE_SLAB + MOE_WIN
SUBLANES = 8
RANK_BLK = 256
ROUTE_COLS = 8


def _route_kernel(x_ref, wr_ref, br_ref, tri_ref, w_ref, pos_ref, seg_ref):
    x = x_ref[...]
    tn = x.shape[0]
    lane = lax.broadcasted_iota(jnp.int32, (tn, LANES), 1)
    lanef = lane.astype(F32)
    hp = lax.Precision.HIGHEST
    elog = _dot3(x, wr_ref[...]) + br_ref[...]
    big = float(LANES)
    glog = jnp.where((lane >= N_EXPERTS) & (lane < N_EXPERTS + N_GROUPS), elog, -jnp.inf)
    gmax = jnp.max(glog, axis=-1, keepdims=True)
    g_gate = 1.0 / jnp.sum(jnp.exp(glog - gmax), axis=-1, keepdims=True)
    grp = jnp.min(jnp.where(glog == gmax, lanef, big), axis=-1, keepdims=True) - N_EXPERTS
    in_grp = (lanef >= grp * EXPERTS_PER_GROUP) & (lanef < (grp + 1.0) * EXPERTS_PER_GROUP)
    sel = jnp.where(in_grp, elog, -jnp.inf)
    v1 = jnp.max(sel, axis=-1, keepdims=True)
    i1 = jnp.min(jnp.where(sel == v1, lanef, big), axis=-1, keepdims=True)
    sel2 = jnp.where(lanef == i1, -jnp.inf, sel)
    v2 = jnp.max(sel2, axis=-1, keepdims=True)
    i2 = jnp.min(jnp.where(sel2 == v2, lanef, big), axis=-1, keepdims=True)
    e2 = jnp.exp(v2 - v1)
    w1 = g_gate / (1.0 + e2)
    w2 = g_gate * e2 / (1.0 + e2)
    w_ref[...] = jnp.where(lane == 0, w1, jnp.where(lane == 1, w2, 0.0))[:, :ROUTE_COLS]

    hit1 = lanef == i1
    hit2 = lanef == i2
    onehot = jnp.where(hit1 | hit2, 1.0, 0.0)
    tri = tri_ref[...]
    carry = jnp.zeros((1, LANES), F32)
    ranks = []
    for j in range(tn // RANK_BLK):
        blk = onehot[j * RANK_BLK:(j + 1) * RANK_BLK]
        ranks.append(_dot(tri, blk.astype(BF16)) + carry)
        carry = carry + jnp.sum(blk, axis=0, keepdims=True)
    rank = jnp.concatenate(ranks, axis=0)
    cnt = carry
    li = lax.broadcasted_iota(jnp.int32, (LANES, LANES), 0)
    lj = lax.broadcasted_iota(jnp.int32, (LANES, LANES), 1)
    before = jnp.where(li < lj, 1.0, 0.0)
    off = jnp.dot(jnp.broadcast_to(cnt, (8, LANES)), before, precision=hp,
                  preferred_element_type=F32)[0:1]
    slot = (off + rank) * SUBLANES
    pos1 = jnp.sum(jnp.where(hit1, slot, 0.0), axis=-1, keepdims=True)
    pos2 = jnp.sum(jnp.where(hit2, slot, 0.0), axis=-1, keepdims=True)
    pos_ref[...] = jnp.where(lane == 0, pos1, jnp.where(lane == 1, pos2, 0.0)).astype(jnp.int32)[:, :ROUTE_COLS]
    row8 = lax.broadcasted_iota(jnp.int32, (8, LANES), 0)
    seg_ref[0] = jnp.where(row8 == 0, off, jnp.where(row8 == 1, cnt, 0.0)).astype(jnp.int32)


def _route(x, lp):
    t = x.shape[0]
    tn = min(MOE_SLAB, t)
    n_slabs = t // tn
    small = [lp["w_route"], lp["b_route"], lp["tri_rank"]]
    row = lambda w: pl.BlockSpec((tn, w), lambda i: (i, 0))
    return pl.pallas_call(
        _route_kernel,
        grid=(n_slabs,),
        in_specs=[row(D_MODEL)] + [_full(p.shape) for p in small],
        out_specs=[row(ROUTE_COLS), row(ROUTE_COLS), pl.BlockSpec((1, 8, LANES), lambda i: (i, 0, 0))],
        out_shape=[jax.ShapeDtypeStruct((t, ROUTE_COLS), F32), jax.ShapeDtypeStruct((t, ROUTE_COLS), jnp.int32),
                   jax.ShapeDtypeStruct((n_slabs, 8, LANES), jnp.int32)],
        compiler_params=_cparams("parallel"),
        name="moe_route",
    )(x, *small)


def _moe_sparse_kernel(off_ref, cnt_ref, pos_ref, w_ref, x_ref, eg_ref, eu_ref, ed_ref, lng_ref, lnb_ref,
                       o_ref, xs_ref):
    s = pl.program_id(0)
    e = pl.program_id(1)
    tn = x_ref.shape[0]

    n_feat = D_MODEL // LANES

    def tile_rows(first_row):
        return pl.ds(pl.multiple_of(first_row, SUBLANES), SUBLANES)

    @pl.when((s == 0) & (e == 0))
    def _():
        xs_ref[pl.ds(2 * tn * SUBLANES, MOE_WIN * SUBLANES), :] = jnp.zeros((MOE_WIN * SUBLANES, LANES), F32)

    @pl.when(e == 0)
    def _():
        def scatter(i, carry):
            t0 = pl.multiple_of(i * SUBLANES, SUBLANES)
            tiles = x_ref[pl.ds(t0, SUBLANES), :].reshape(SUBLANES, n_feat, LANES)
            for j in range(SUBLANES):
                xs_ref[tile_rows(pos_ref[0, 0, 2 * (t0 + j)]), :] = tiles[j]
                xs_ref[tile_rows(pos_ref[0, 0, 2 * (t0 + j) + 1]), :] = tiles[j]
            return carry
        lax.fori_loop(0, tn // SUBLANES, scatter, 0)

    ridx = lax.broadcasted_iota(jnp.int32, (MOE_WIN, 1), 0)

    base = s * N_EXPERTS + e * MOE_EBLK
    offs = [off_ref[base + j] for j in range(MOE_EBLK)]
    cnts = [cnt_ref[base + j] for j in range(MOE_EBLK)]

    def feat_rows(j, w, f):
        return pl.ds((offs[j] + w * MOE_WIN) * SUBLANES + f, MOE_WIN, stride=SUBLANES)

    def load_win(j, w):
        return jnp.concatenate([xs_ref[feat_rows(j, w, f), :] for f in range(n_feat)], axis=1)

    def hidden(j, xw):
        xb = xw.astype(BF16)
        hg = _dot(xb, eg_ref[j])
        hu = _dot(xb, eu_ref[j])
        return (hg * _sigmoid(hg) * hu).astype(BF16)

    def project(j, w, xw, hh):
        yw = _dot(hh, ed_ref[j])
        return jnp.where(ridx + w * MOE_WIN < cnts[j], yw, xw)

    def expert(j, w, xw):
        return project(j, w, xw, hidden(j, xw))

    def store_win(j, w, yw):
        for f in range(n_feat):
            xs_ref[feat_rows(j, w, f), :] = yw[:, f * LANES:(f + 1) * LANES]

    xw = [None] * MOE_EBLK
    hh = [None] * MOE_EBLK
    for j in range(MOE_EBLK + 1):
        if j < MOE_EBLK:
            xw[j] = load_win(j, 0)
            hh[j] = hidden(j, xw[j])
        if j > 0:
            store_win(j - 1, 0, project(j - 1, 0, xw[j - 1], hh[j - 1]))

    for j in range(MOE_EBLK):
        def window(w, carry, j=j):
            store_win(j, w, expert(j, w, load_win(j, w)))
            return carry
        lax.fori_loop(1, (cnts[j] + MOE_WIN - 1) // MOE_WIN, window, 0)

    @pl.when(e == N_EXPERTS // MOE_EBLK - 1)
    def _():
        def combine(i, carry):
            t0 = pl.multiple_of(i * SUBLANES, SUBLANES)
            tiles = []
            for j in range(SUBLANES):
                y1 = xs_ref[tile_rows(pos_ref[0, 0, 2 * (t0 + j)]), :]
                y2 = xs_ref[tile_rows(pos_ref[0, 0, 2 * (t0 + j) + 1]), :]
                tiles.append(w_ref[0, 0, 2 * (t0 + j)] * y1 + w_ref[0, 0, 2 * (t0 + j) + 1] * y2)
            o_ref[pl.ds(t0, SUBLANES), :] = jnp.stack(tiles, axis=0).reshape(SUBLANES, D_MODEL)
            return carry
        lax.fori_loop(0, tn // SUBLANES, combine, 0)
        for j in range(tn // RANK_BLK):
            rs = slice(j * RANK_BLK, (j + 1) * RANK_BLK)
            o_ref[rs, :] = _layer_norm(DEEPNORM_ALPHA * x_ref[rs, :] + o_ref[rs, :], lng_ref[...], lnb_ref[...])


def _moe_sparse(x, lp):
    t = x.shape[0]
    tn = min(MOE_SLAB, t)
    n_slabs = t // tn
    w, pos, seg = _route(x, lp)
    w_flat = w[:, :2].reshape(n_slabs, 1, 2 * tn)
    pos_flat = pos[:, :2].reshape(n_slabs, 1, 2 * tn)
    off = seg[:, 0, :N_EXPERTS].reshape(-1)
    cnt = seg[:, 1, :N_EXPERTS].reshape(-1)
    smem_row = pl.BlockSpec((1, 1, 2 * tn), lambda s, e, *_: (s, 0, 0), memory_space=pltpu.SMEM)
    slab = pl.BlockSpec((tn, D_MODEL), lambda s, e, *_: (s, 0), pipeline_mode=pl.Buffered(1))
    const = lambda p: pl.BlockSpec(p.shape, lambda s, e, *_: (0,) * p.ndim)
    grid_spec = pltpu.PrefetchScalarGridSpec(
        num_scalar_prefetch=2,
        grid=(n_slabs, N_EXPERTS // MOE_EBLK),
        in_specs=[smem_row, smem_row, slab,
                  pl.BlockSpec((MOE_EBLK, D_MODEL, D_EXPERT), lambda s, e, *_: (e, 0, 0)),
                  pl.BlockSpec((MOE_EBLK, D_MODEL, D_EXPERT), lambda s, e, *_: (e, 0, 0)),
                  pl.BlockSpec((MOE_EBLK, D_EXPERT, D_MODEL), lambda s, e, *_: (e, 0, 0)),
                  const(lp["ln_g1"]), const(lp["ln_b1"])],
        out_specs=slab,
        scratch_shapes=[pltpu.VMEM((MOE_ROWS * SUBLANES, LANES), F32)],
    )
    return pl.pallas_call(
        _moe_sparse_kernel,
        grid_spec=grid_spec,
        out_shape=jax.ShapeDtypeStruct((t, D_MODEL), F32),
        compiler_params=_cparams("arbitrary", "arbitrary"),
        name="moe_sparse",
    )(off, cnt, pos_flat, w_flat, x, lp["e_gate"], lp["e_up"], lp["e_down"], lp["ln_g1"], lp["ln_b1"])


def _ple_kernel(x_ref, p_ref, wp_ref, wg_ref, lng_ref, lnb_ref, o_ref):
    x = x_ref[...]
    emb = _dot(p_ref[...].astype(BF16), wp_ref[...])
    gate = _sigmoid(_dot(x.astype(BF16), wg_ref[...]))
    o_ref[...] = _layer_norm(DEEPNORM_ALPHA * x + emb * gate, lng_ref[...], lnb_ref[...])


def _ple(x, p, lp):
    t = x.shape[0]
    tm = min(512, t)
    d_ple = p.shape[1]
    row = lambda w: pl.BlockSpec((tm, w), lambda i: (i, 0))
    params = [lp["w_ple"], lp["w_pg"], lp["ln_g2"], lp["ln_b2"]]
    return pl.pallas_call(
        _ple_kernel,
        grid=(t // tm,),
        in_specs=[row(D_MODEL), row(d_ple)] + [_full(q.shape) for q in params],
        out_specs=row(D_MODEL),
        out_shape=jax.ShapeDtypeStruct((t, D_MODEL), F32),
        compiler_params=_cparams("parallel"),
        name="ple",
    )(x, p, *params)


def _pad_to(a, axis, size):
    pad = [(0, 0)] * a.ndim
    pad[axis] = (0, size - a.shape[axis])
    return jnp.pad(a, pad)


def _row(v):
    return v.reshape(1, -1).astype(F32)


def _layer_params(i, w_in, tok_mix, decay_base, decay_up, aaa_base, aaa_up, gate_up, k_k, k_a, r_k,
                  vres_base, vres_down, vres_up, gn_g, gn_b, w_branch_attn, w_branch_rwkv, w_out,
                  router_grp, router_grp_bias, router_exp, router_exp_bias, exp_gate, exp_up, exp_down,
                  ple_proj, ple_gate, ln_g, ln_b, consts):
    d = D_HEADS
    w = w_in[i]
    rw0 = 3 * d
    gate0 = rw0 + 3 * d + DECAY_LORA + AAA_LORA + GATE_LORA
    wqkv = jnp.concatenate([w[:, 0:d] * (HEAD_DIM ** -0.5), w[:, d:rw0]], axis=1).astype(BF16)
    wr = w[:, rw0:gate0]
    lora0 = 3 * d
    pieces = [wr[:, :lora0],
              _pad_to(wr[:, lora0:lora0 + DECAY_LORA], 1, LANES),
              _pad_to(wr[:, lora0 + DECAY_LORA:lora0 + DECAY_LORA + AAA_LORA], 1, LANES),
              _pad_to(wr[:, lora0 + DECAY_LORA + AAA_LORA:], 1, 2 * LANES)]
    wrw = jnp.concatenate(pieces, axis=1).astype(BF16)
    mu = tok_mix[i]
    mu_pieces = [mu[:lora0],
                 _pad_to(mu[lora0:lora0 + DECAY_LORA], 0, LANES),
                 _pad_to(mu[lora0 + DECAY_LORA:lora0 + DECAY_LORA + AAA_LORA], 0, LANES),
                 _pad_to(mu[lora0 + DECAY_LORA + AAA_LORA:], 0, 2 * LANES)]
    lp = dict(consts)
    lp.update(
        wqkv=wqkv, wrw=wrw, wgate=w[:, gate0:].astype(BF16),
        mu=_row(jnp.concatenate(mu_pieces)),
        decay_base=_row(decay_base[i]), decay_up=_pad_to(decay_up[i], 0, LANES).astype(F32),
        aaa_base=_row(aaa_base[i]), aaa_up=_pad_to(aaa_up[i], 0, LANES).astype(F32),
        gate_up=_pad_to(gate_up[i], 0, 2 * LANES).astype(F32),
        k_k=_row(k_k[i]), k_a=_row(k_a[i]), r_k=_row(r_k[i]),
        gn_g=_row(gn_g[i]), gn_b=_row(gn_b[i]),
        w_a=w_branch_attn[i].astype(BF16), w_b=w_branch_rwkv[i].astype(BF16), w_o=w_out[i].astype(BF16),
        w_route=_pad_to(jnp.concatenate([router_exp[i], router_grp[i]], axis=1), 1, LANES).astype(F32),
        b_route=_row(_pad_to(jnp.concatenate([router_exp_bias[i], router_grp_bias[i]]), 0, LANES)),
        e_gate=exp_gate[i].astype(BF16), e_up=exp_up[i].astype(BF16), e_down=exp_down[i].astype(BF16),
        w_ple=ple_proj[i].astype(BF16), w_pg=ple_gate[i].astype(BF16),
        ln_g0=_row(ln_g[i, 0]), ln_b0=_row(ln_b[i, 0]),
        ln_g1=_row(ln_g[i, 1]), ln_b1=_row(ln_b[i, 1]),
        ln_g2=_row(ln_g[i, 2]), ln_b2=_row(ln_b[i, 2]),
    )
    if i > 0:
        lp.update(vres_base=_row(vres_base[i - 1]),
                  vres_down=_pad_to(vres_down[i - 1], 1, LANES).astype(F32),
                  vres_up=_pad_to(vres_up[i - 1], 0, LANES).astype(F32))
    return lp


def _const_mats(tm):
    tok = jnp.arange(tm)
    same = (tok[:, None] // CHUNK) == (tok[None, :] // CHUNK)
    tri = (same & (tok[:, None] >= tok[None, :])).astype(BF16)
    blk = same.astype(BF16)
    rtok = jnp.arange(RANK_BLK)
    tri_rank = (rtok[None, :] < rtok[:, None]).astype(BF16)
    return dict(tri=tri, blk=blk, tri_rank=tri_rank)


def kernel(x, p, ln_in_g, ln_in_b, rel_bias, w_in, tok_mix, decay_base, decay_up, aaa_base, aaa_up,
           gate_up, k_k, k_a, r_k, vres_base, vres_down, vres_up, gn_g, gn_b, w_branch_attn,
           w_branch_rwkv, w_out, router_grp, router_grp_bias, router_exp, router_exp_bias, exp_gate,
           exp_up, exp_down, ple_proj, ple_gate, ln_g, ln_b):
    batch, seq, _ = x.shape
    t = batch * seq
    consts = _const_mats(256)
    bias_tabs = _attn_bias_tables(rel_bias)
    xt = x.reshape(t, D_MODEL)
    pt = p.reshape(p.shape[0], t, p.shape[-1])
    v_first = None
    for i in range(DEPTH):
        lp = _layer_params(i, w_in, tok_mix, decay_base, decay_up, aaa_base, aaa_up, gate_up, k_k, k_a,
                           r_k, vres_base, vres_down, vres_up, gn_g, gn_b, w_branch_attn, w_branch_rwkv,
                           w_out, router_grp, router_grp_bias, router_exp, router_exp_bias, exp_gate,
                           exp_up, exp_down, ple_proj, ple_gate, ln_g, ln_b, consts)
        if i == 0:
            xt, qkv, rw, gate = _proj(xt, _row(ln_in_g), _row(ln_in_b), lp["wqkv"], lp["wrw"],
                                      lp["wgate"], apply_ln=True)
        else:
            qkv, rw, gate = _proj(xt, _row(ln_in_g), _row(ln_in_b), lp["wqkv"], lp["wrw"],
                                  lp["wgate"], apply_ln=False)
        y_attn = _attention(qkv, bias_tabs, batch, seq)
        if i == 0:
            q1, y0, m, n0, g, bonus, v_first = _rwkv_chunk(rw, lp, seq, None)
        else:
            q1, y0, m, n0, g, bonus = _rwkv_chunk(rw, lp, seq, v_first)
        y = _rwkv_scan(q1, y0, m, n0, batch, seq)
        xt = _mix(y, g, bonus, y_attn, gate, xt, lp)
        xt = _moe_sparse(xt, lp)
        xt = _ple(xt, pt[i], lp)
    return xt.reshape(batch, seq, D_MODEL)
```

```python
import functools
import math

import jax
import jax.numpy as jnp
from jax import lax
from jax.experimental import pallas as pl
from jax.experimental.pallas import tpu as pltpu

F32 = jnp.float32
BF16 = jnp.bfloat16

D_MODEL = 1024
CHUNK = 64
N_PREV_CHUNKS = 8
HEAD_DIM = 64
N_HEADS = 8
D_HEADS = N_HEADS * HEAD_DIM
REL_CLIP = 128
DECAY_LORA = 64
AAA_LORA = 64
GATE_LORA = 160
MV_LORA = 32
N_GROUPS = 4
EXPERTS_PER_GROUP = 8
N_EXPERTS = N_GROUPS * EXPERTS_PER_GROUP
D_EXPERT = 256
DEPTH = 2
DEEPNORM_ALPHA = (2 * DEPTH) ** 0.25
LN_EPS = 1e-5
GN_EPS = 64e-5
NEG_INF = -1e30

LANES = 128
PAIR = 2 * HEAD_DIM
GRP_HEADS = 2
GRP = GRP_HEADS * HEAD_DIM
SUMMARY_BATCH = 16
ATTN_QBLK = 2 * CHUNK
ATTN_WIN = (N_PREV_CHUNKS + 2) * CHUNK
ATTN_GROUP = 4
RW_COLS_PAD = 2048
VMEM_LIMIT = 56 * 1024 * 1024


def _cparams(*sem):
    return pltpu.CompilerParams(dimension_semantics=sem, vmem_limit_bytes=VMEM_LIMIT)


def _dot(a, b):
    return jnp.dot(a, b, preferred_element_type=F32)


def _dot_nt(a, b):
    return lax.dot_general(a, b, (((1,), (1,)), ((), ())), preferred_element_type=F32)


def _dot_tn(a, b):
    return lax.dot_general(a, b, (((0,), (0,)), ((), ())), preferred_element_type=F32)


def _split2(x):
    hi = x.astype(BF16)
    lo = (x - hi.astype(F32)).astype(BF16)
    return hi, lo


def _dot3(a, b):
    ah, al = _split2(a)
    bh, bl = _split2(b)
    return _dot(ah, bh) + (_dot(al, bh) + _dot(ah, bl))


def _dot_exact_lhs(a_bf16, x):
    x1 = x.astype(BF16)
    r1 = x - x1.astype(F32)
    x2 = r1.astype(BF16)
    x3 = (r1 - x2.astype(F32)).astype(BF16)
    return _dot(a_bf16, x1) + (_dot(a_bf16, x2) + _dot(a_bf16, x3))


def _seg_sum(x):
    left = lax.broadcasted_iota(jnp.int32, (x.shape[0], PAIR), 1) < HEAD_DIM
    outs = []
    for p in range(x.shape[1] // PAIR):
        xp = x[:, p * PAIR:(p + 1) * PAIR]
        s0 = jnp.sum(jnp.where(left, xp, 0.0), axis=-1, keepdims=True)
        s1 = jnp.sum(jnp.where(left, 0.0, xp), axis=-1, keepdims=True)
        outs.append(jnp.where(left, s0, s1))
    return jnp.concatenate(outs, axis=-1)


def _layer_norm(x, g, b):
    mu = jnp.mean(x, axis=-1, keepdims=True)
    xc = x - mu
    var = jnp.mean(xc * xc, axis=-1, keepdims=True)
    return xc * lax.rsqrt(var + LN_EPS) * g + b


def _sigmoid(x):
    return 1.0 / (1.0 + jnp.exp(-x))


def _full(shape):
    nd = len(shape)
    return pl.BlockSpec(shape, lambda *_: (0,) * nd)


def _proj_kernel(x_ref, g_ref, b_ref, wqkv_ref, wrw_ref, wgate_ref, *out_refs, apply_ln):
    x = x_ref[...]
    if apply_ln:
        xn_ref, qkv_ref, rw_ref, gate_ref = out_refs
        x = _layer_norm(x, g_ref[...], b_ref[...])
        xn_ref[...] = x
    else:
        qkv_ref, rw_ref, gate_ref = out_refs
    xb = x.astype(BF16)
    qkv_ref[...] = _dot(xb, wqkv_ref[...]).astype(BF16)
    rw_ref[...] = _dot(xb, wrw_ref[...])
    gate_ref[...] = _sigmoid(_dot(xb, wgate_ref[...])).astype(BF16)


def _proj(x, g, b, wqkv, wrw, wgate, apply_ln):
    t = x.shape[0]
    tm = min(256, t)
    row = lambda w: pl.BlockSpec((tm, w), lambda i: (i, 0))
    out_shape = [jax.ShapeDtypeStruct((t, 3 * D_HEADS), BF16),
                 jax.ShapeDtypeStruct((t, RW_COLS_PAD), F32),
                 jax.ShapeDtypeStruct((t, 2 * D_MODEL), BF16)]
    out_specs = [row(3 * D_HEADS), row(RW_COLS_PAD), row(2 * D_MODEL)]
    if apply_ln:
        out_shape = [jax.ShapeDtypeStruct((t, D_MODEL), F32)] + out_shape
        out_specs = [row(D_MODEL)] + out_specs
    return pl.pallas_call(
        functools.partial(_proj_kernel, apply_ln=apply_ln),
        grid=(t // tm,),
        in_specs=[row(D_MODEL), _full(g.shape), _full(b.shape),
                  _full(wqkv.shape), _full(wrw.shape), _full(wgate.shape)],
        out_specs=out_specs,
        out_shape=out_shape,
        compiler_params=_cparams("parallel"),
        name="proj_ln" if apply_ln else "proj",
    )(x, g, b, wqkv, wrw, wgate)


def _attn_kernel(q_ref, k_ref, v_ref, bias_ref, o_ref):
    qb = pl.program_id(1)
    ws = pl.multiple_of(jnp.maximum(qb * ATTN_QBLK - N_PREV_CHUNKS * CHUNK, 0), ATTN_QBLK)
    q = q_ref[0]
    kw = k_ref[0, pl.ds(ws, ATTN_WIN), :]
    vw = v_ref[0, pl.ds(ws, ATTN_WIN), :]
    left = lax.broadcasted_iota(jnp.int32, (ATTN_QBLK, PAIR), 1) < HEAD_DIM
    zero = jnp.zeros((ATTN_QBLK, PAIR), BF16)
    heads = [(h // 2, h % 2) for h in range(N_HEADS)]
    spans = [slice(p * PAIR, (p + 1) * PAIR) for p in range(N_HEADS // 2)]
    o = []
    for g0 in range(0, N_HEADS, ATTN_GROUP):
        grp = list(enumerate(heads))[g0:g0 + ATTN_GROUP]
        qm = [jnp.where(left, q[:, spans[p]], zero) if side == 0 else jnp.where(left, zero, q[:, spans[p]])
              for _, (p, side) in grp]
        s = [_dot_nt(x, kw[:, spans[p]]) + bias_ref[0, h] for x, (h, (p, _)) in zip(qm, grp)]
        m = [jnp.max(x, axis=-1, keepdims=True) for x in s]
        e = [jnp.exp(x - mx) for x, mx in zip(s, m)]
        inv = [1.0 / jnp.sum(x, axis=-1, keepdims=True) for x in e]
        o += [_dot(x.astype(BF16), vw[:, spans[p]]) * r for x, r, (_, (p, _)) in zip(e, inv, grp)]
    pairs = [jnp.where(left, o[2 * p], o[2 * p + 1]) for p in range(N_HEADS // 2)]
    o_ref[0] = jnp.concatenate(pairs, axis=-1).astype(BF16)


def _attention(qkv, bias_tabs, batch, seq):
    qkv3 = qkv.reshape(batch, seq, 3 * D_HEADS)
    n_tabs = bias_tabs.shape[0]
    out = pl.pallas_call(
        _attn_kernel,
        grid=(batch, seq // ATTN_QBLK),
        in_specs=[
            pl.BlockSpec((1, ATTN_QBLK, D_HEADS), lambda b, i: (b, i, 0)),
            pl.BlockSpec((1, seq, D_HEADS), lambda b, i: (b, 0, 1)),
            pl.BlockSpec((1, seq, D_HEADS), lambda b, i: (b, 0, 2)),
            pl.BlockSpec((1, N_HEADS, ATTN_QBLK, ATTN_WIN),
                         lambda b, i: (jnp.minimum(i, n_tabs - 1), 0, 0, 0)),
        ],
        out_specs=pl.BlockSpec((1, ATTN_QBLK, D_HEADS), lambda b, i: (b, i, 0)),
        out_shape=jax.ShapeDtypeStruct((batch, seq, D_HEADS), BF16),
        compiler_params=_cparams("parallel", "arbitrary"),
        name="band_attn",
    )(qkv3, qkv3, qkv3, bias_tabs)
    return out.reshape(batch * seq, D_HEADS)


def _attn_bias_tables(rel_bias):
    n_tabs = N_PREV_CHUNKS * CHUNK // ATTN_QBLK + 1
    start = (jnp.arange(n_tabs) * ATTN_QBLK)[:, None, None]
    qpos = start + jnp.arange(ATTN_QBLK)[None, :, None]
    kpos = jnp.arange(ATTN_WIN)[None, None, :]
    qc, kc = qpos // CHUNK, kpos // CHUNK
    valid = (kc <= qc) & (kc >= qc - N_PREV_CHUNKS)
    n = jnp.arange(ATTN_QBLK - 1 + ATTN_WIN)
    dist = start[:, :, 0] + (ATTN_QBLK - 1) - n[None, :]
    vals = rel_bias.astype(F32)[:, jnp.clip(dist, -REL_CLIP, REL_CLIP) + REL_CLIP]
    length = ATTN_QBLK - 1 + ATTN_WIN
    tiled = jnp.tile(_pad_to(vals, 2, length + 1), (1, 1, ATTN_QBLK))
    cut = tiled[:, :, :ATTN_QBLK * length].reshape(N_HEADS, n_tabs, ATTN_QBLK, length)
    bias = cut[:, :, :, ATTN_QBLK - 1:ATTN_QBLK - 1 + ATTN_WIN]
    bias = jnp.where(valid[None], bias, NEG_INF)
    return jnp.transpose(bias, (1, 0, 2, 3))


def _rwkv_chunk_kernel(*refs, seq, tm, has_vres):
    if has_vres:
        (rw_ref, prev_ref, mu_ref, dbase_ref, dup_ref, abase_ref, aup_ref, gup_ref, kk_ref, ka_ref,
         rk_ref, tri_ref, vfirst_ref, vbase_ref, vdown_ref, vup_ref,
         q1_ref, y0_ref, m_ref, n0_ref, g_ref, bonus_ref,
         rw_s, aw_s, bi_s, ki_s, be_s, ke_s, v_s, wc_s) = refs
    else:
        (rw_ref, prev_ref, mu_ref, dbase_ref, dup_ref, abase_ref, aup_ref, gup_ref, kk_ref, ka_ref,
         rk_ref, tri_ref,
         q1_ref, y0_ref, m_ref, n0_ref, g_ref, bonus_ref, vfirst_out_ref,
         rw_s, aw_s, bi_s, ki_s, be_s, ke_s, v_s, wc_s) = refs

    cols = rw_ref[...]
    first = (pl.program_id(0) % (seq // tm)) == 0
    prev_row = jnp.where(first, 0.0, prev_ref[7:8, :])
    rows = lax.broadcasted_iota(jnp.int32, cols.shape, 0)
    prev = jnp.where(rows == 0, prev_row, pltpu.roll(cols, 1, axis=0))
    xs = cols + (prev - cols) * mu_ref[...]

    d = D_HEADS
    r, k, v = xs[:, 0:d], xs[:, d:2 * d], xs[:, 2 * d:3 * d]
    wd = xs[:, 3 * d:3 * d + LANES]
    ad = xs[:, 3 * d + LANES:3 * d + 2 * LANES]
    gd = xs[:, 3 * d + 2 * LANES:3 * d + 4 * LANES]

    u = dbase_ref[...] + _dot3(jnp.tanh(wd), dup_ref[...])
    w_log = jnp.minimum(u, 0.0) - jnp.log(1.0 + jnp.exp(-jnp.abs(u))) - 0.5
    logw = -jnp.exp(w_log)
    a_icl = _sigmoid(abase_ref[...] + _dot3(ad, aup_ref[...]))
    g_ref[...] = _dot3(_sigmoid(gd), gup_ref[...]).astype(BF16)
    kk = k * kk_ref[...]
    kk = kk * lax.rsqrt(jnp.maximum(_seg_sum(kk * kk), 1e-24))
    k2 = k * (1.0 + (a_icl - 1.0) * ka_ref[...])
    if has_vres:
        low = _dot(v.astype(BF16), vdown_ref[...])
        mix = _sigmoid(vbase_ref[...] + _dot(low.astype(BF16), vup_ref[...]))
        v2 = v + (vfirst_ref[...] - v) * mix
    else:
        v2 = v
        vfirst_out_ref[...] = v
    a_vec = -kk
    b_vec = kk * a_icl
    bonus_ref[...] = (_seg_sum(r * k2 * rk_ref[...]) * v2).astype(BF16)

    cw = _dot_exact_lhs(tri_ref[...], logw)
    bi_ = lax.broadcasted_iota(jnp.int32, (tm, tm), 0) // CHUNK
    bj_ = lax.broadcasted_iota(jnp.int32, (tm, tm), 1) // CHUNK
    cw_tot = _dot_exact_lhs(jnp.where(bi_ == bj_, 1.0, 0.0).astype(BF16), logw)
    rw_s[...] = r * jnp.exp(cw)
    aw_s[...] = a_vec * jnp.exp(cw - logw)
    inv = jnp.exp(-cw)
    bi_s[...] = b_vec * inv
    ki_s[...] = k2 * inv
    rest = jnp.exp(cw_tot - cw)
    be_s[...] = b_vec * rest
    ke_s[...] = k2 * rest
    v_s[...] = v2
    wc_s[...] = jnp.exp(cw_tot)

    shape = (CHUNK, GRP)
    ti = lax.broadcasted_iota(jnp.int32, shape, 0)
    li = lax.broadcasted_iota(jnp.int32, shape, 1)
    si = li & (HEAD_DIM - 1)
    hid = li // HEAD_DIM
    strict = ti > si
    incl = ti >= si
    eye = ti == si

    def bd(x):
        z = jnp.zeros_like(x)
        return jnp.concatenate([jnp.where(hid == h, x, z) for h in range(GRP_HEADS)], axis=0)

    def tpair(x):
        return jnp.concatenate([x[:, h * HEAD_DIM:(h + 1) * HEAD_DIM].T for h in range(GRP_HEADS)], axis=1)

    def b16(xs):
        return [x.astype(BF16) for x in xs]

    def summaries(insts):
        sl = [(slice(c * CHUNK, (c + 1) * CHUNK), slice(p * GRP, (p + 1) * GRP)) for c, p in insts]
        rw = [rw_s[s] for s in sl]
        awb = b16([aw_s[s] for s in sl])
        vb = b16([v_s[s] for s in sl])
        lhs = [jnp.concatenate([a, r.astype(BF16)], axis=0) for a, r in zip(awb, rw)]
        bik = [jnp.concatenate([bd(b), bd(k)], axis=0)
               for b, k in zip(b16([bi_s[s] for s in sl]), b16([ki_s[s] for s in sl]))]
        g = [_dot_nt(l, x) for l, x in zip(lhs, bik)]
        a_ab = [jnp.where(strict, x[:CHUNK, :GRP], 0.0) for x in g]
        a_rb = b16([jnp.where(incl, x[CHUNK:, :GRP], 0.0) for x in g])
        a_ak = b16([jnp.where(strict, x[:CHUNK, GRP:], 0.0) for x in g])
        a_rk = b16([jnp.where(incl, x[CHUNK:, GRP:], 0.0) for x in g])
        bdv = [bd(x) for x in vb]
        x0 = [_dot(a, v) for a, v in zip(a_ak, bdv)]
        tinv = [jnp.where(eye, 1.0, a) for a in a_ab]
        lpb = b16(a_ab)
        lpb = b16([_dot(l, bd(l)) for l in lpb])
        for _ in range(4):
            both = [_dot(jnp.concatenate([t.astype(BF16), l], axis=0), bd(l)) for t, l in zip(tinv, lpb)]
            tinv = [t + x[:CHUNK] for t, x in zip(tinv, both)]
            lpb = b16([x[CHUNK:] for x in both])
        tinv = [t + _dot(t.astype(BF16), bd(l)) for t, l in zip(tinv, lpb)]
        tb = b16(tinv)
        pu = b16([_dot(t, jnp.concatenate([bd(a), bd(x.astype(BF16))], axis=1))
                  for t, a, x in zip(tb, awb, x0)])
        pu_bd = [jnp.concatenate([bd(x[:, :GRP]), bd(x[:, GRP:])], axis=1) for x in pu]
        qy = [_dot(a, r) for a, r in zip(a_rb, pu_bd)]
        q1 = [r + x[:, :GRP] for r, x in zip(rw, qy)]
        y0 = [x[:, GRP:] + _dot(k, v) for x, k, v in zip(qy, a_rk, bdv)]
        bet = b16([tpair(be_s[s]) for s in sl])
        ket = b16([tpair(ke_s[s]) for s in sl])
        mn = [_dot(b, r) for b, r in zip(bet, pu_bd)]
        mm = [x[:, :GRP] for x in mn]
        nn = [x[:, GRP:] + _dot(k, v) for x, k, v in zip(mn, ket, bdv)]
        for i, (s, (c, p)) in enumerate(zip(sl, insts)):
            wc = wc_s[c * CHUNK:c * CHUNK + 1, s[1]]
            q1_ref[s] = q1[i]
            y0_ref[s] = y0[i]
            m_ref[s] = jnp.where(eye, wc, 0.0) + mm[i]
            n0_ref[s] = nn[i]

    insts = [(c, p) for c in range(tm // CHUNK) for p in range(D_HEADS // GRP)]
    for i0 in range(0, len(insts), SUMMARY_BATCH):
        summaries(insts[i0:i0 + SUMMARY_BATCH])


def _rwkv_chunk(rw, lp, seq, v_first):
    t = rw.shape[0]
    tm = 256
    has_vres = v_first is not None
    nct = tm // CHUNK
    row = lambda w: pl.BlockSpec((tm, w), lambda i: (i, 0))
    prev_spec = pl.BlockSpec((8, RW_COLS_PAD), lambda i: (jnp.maximum(i * (tm // 8) - 1, 0), 0))
    params = [lp["mu"], lp["decay_base"], lp["decay_up"], lp["aaa_base"], lp["aaa_up"], lp["gate_up"],
              lp["k_k"], lp["k_a"], lp["r_k"], lp["tri"]]
    inputs = [rw, rw] + params
    in_specs = [row(RW_COLS_PAD), prev_spec] + [_full(p.shape) for p in params]
    if has_vres:
        extra = [lp["vres_base"], lp["vres_down"], lp["vres_up"]]
        inputs += [v_first] + extra
        in_specs += [row(D_HEADS)] + [_full(p.shape) for p in extra]
    tok_shape = jax.ShapeDtypeStruct((t, D_HEADS), F32)
    half_shape = jax.ShapeDtypeStruct((t, D_HEADS), BF16)
    out_shape = [tok_shape] * 4 + [half_shape] * 2
    out_specs = [row(D_HEADS)] * 6
    if not has_vres:
        out_shape.append(tok_shape)
        out_specs.append(row(D_HEADS))
    scratch = [pltpu.VMEM((tm, D_HEADS), F32) for _ in range(8)]
    return pl.pallas_call(
        functools.partial(_rwkv_chunk_kernel, seq=seq, tm=tm, has_vres=has_vres),
        grid=(t // tm,),
        in_specs=in_specs,
        out_specs=out_specs,
        out_shape=out_shape,
        scratch_shapes=scratch,
        compiler_params=_cparams("parallel"),
        name="rwkv_chunk_vres" if has_vres else "rwkv_chunk",
    )(*inputs)


def _rwkv_scan_kernel(q1_ref, y0_ref, m_ref, n0_ref, y_ref, st_ref, *, n_chunks):
    @pl.when(pl.program_id(1) == 0)
    def _():
        st_ref[...] = jnp.zeros_like(st_ref)

    left = lax.broadcasted_iota(jnp.int32, (CHUNK, PAIR), 1) < HEAD_DIM

    def bd(x):
        z = jnp.zeros_like(x)
        return jnp.concatenate([jnp.where(left, x, z), jnp.where(left, z, x)], axis=0)

    n_pairs = D_HEADS // PAIR
    st = [st_ref[:, p * PAIR:(p + 1) * PAIR] for p in range(n_pairs)]
    for c in range(n_chunks):
        rs = slice(c * CHUNK, (c + 1) * CHUNK)
        for p in range(n_pairs):
            ls = slice(p * PAIR, (p + 1) * PAIR)
            lh, ll = _split2(jnp.concatenate([q1_ref[rs, ls], m_ref[rs, ls]], axis=0))
            sh, sl = _split2(st[p])
            bh = bd(sh)
            res = _dot(lh, bh) + (_dot(ll, bh) + _dot(lh, bd(sl)))
            y_ref[rs, ls] = res[:CHUNK] + y0_ref[rs, ls]
            st[p] = res[CHUNK:] + n0_ref[rs, ls]
    for p in range(n_pairs):
        st_ref[:, p * PAIR:(p + 1) * PAIR] = st[p]


def _rwkv_scan(q1, y0, m, n0, batch, seq):
    t = q1.shape[0]
    cb = 8
    steps = seq // (cb * CHUNK)
    tok_spec = pl.BlockSpec((cb * CHUNK, D_HEADS), lambda b, j: (b * steps + j, 0))
    return pl.pallas_call(
        functools.partial(_rwkv_scan_kernel, n_chunks=cb),
        grid=(batch, steps),
        in_specs=[tok_spec] * 4,
        out_specs=tok_spec,
        out_shape=jax.ShapeDtypeStruct((t, D_HEADS), F32),
        scratch_shapes=[pltpu.VMEM((HEAD_DIM, D_HEADS), F32)],
        compiler_params=_cparams("parallel", "arbitrary"),
        name="rwkv_scan",
    )(q1, y0, m, n0)


def _mix_kernel(y_ref, g_ref, bonus_ref, ya_ref, gate_ref, x_ref, wa_ref, wb_ref, wo_ref,
                gng_ref, gnb_ref, lng_ref, lnb_ref, o_ref):
    y = y_ref[...]
    mean = _seg_sum(y) * (1.0 / HEAD_DIM)
    yc = y - mean
    var = _seg_sum(yc * yc) * (1.0 / HEAD_DIM)
    yn = yc * lax.rsqrt(var + GN_EPS) * gng_ref[...] + gnb_ref[...]
    yr = (yn + bonus_ref[...].astype(F32)) * g_ref[...].astype(F32)
    za = _dot(ya_ref[...], wa_ref[...])
    zb = _dot(yr.astype(BF16), wb_ref[...])
    gate = gate_ref[...]
    mixed_in = gate[:, :D_MODEL].astype(F32) * za + gate[:, D_MODEL:].astype(F32) * zb
    mixed = _dot(mixed_in.astype(BF16), wo_ref[...])
    o_ref[...] = _layer_norm(DEEPNORM_ALPHA * x_ref[...] + mixed, lng_ref[...], lnb_ref[...])


def _mix(y, g, bonus, y_attn, gate, x, lp):
    t = x.shape[0]
    tm = min(256, t)
    row = lambda w: pl.BlockSpec((tm, w), lambda i: (i, 0))
    params = [lp["w_a"], lp["w_b"], lp["w_o"], lp["gn_g"], lp["gn_b"], lp["ln_g0"], lp["ln_b0"]]
    return pl.pallas_call(
        _mix_kernel,
        grid=(t // tm,),
        in_specs=[row(D_HEADS), row(D_HEADS), row(D_HEADS), row(D_HEADS), row(2 * D_MODEL), row(D_MODEL)]
        + [_full(p.shape) for p in params],
        out_specs=row(D_MODEL),
        out_shape=jax.ShapeDtypeStruct((t, D_MODEL), F32),
        compiler_params=_cparams("parallel"),
        name="branch_mix",
    )(y, g, bonus, y_attn, gate, x, *params)


MOE_SLAB = 2048
MOE_WIN = 128
MOE_EBLK = 4
MOE_ROWS = 2 * MOE_SLAB + MOE_WIN
SUBLANES = 8
RANK_BLK = 256
ROUTE_COLS = 8


def _route_kernel(x_ref, wr_ref, br_ref, tri_ref, w_ref, pos_ref, seg_ref):
    x = x_ref[...]
    tn = x.shape[0]
    lane = lax.broadcasted_iota(jnp.int32, (tn, LANES), 1)
    lanef = lane.astype(F32)
    hp = lax.Precision.HIGHEST
    elog = _dot3(x, wr_ref[...]) + br_ref[...]
    big = float(LANES)
    glog = jnp.where((lane >= N_EXPERTS) & (lane < N_EXPERTS + N_GROUPS), elog, -jnp.inf)
    gmax = jnp.max(glog, axis=-1, keepdims=True)
    g_gate = 1.0 / jnp.sum(jnp.exp(glog - gmax), axis=-1, keepdims=True)
    grp = jnp.min(jnp.where(glog == gmax, lanef, big), axis=-1, keepdims=True) - N_EXPERTS
    in_grp = (lanef >= grp * EXPERTS_PER_GROUP) & (lanef < (grp + 1.0) * EXPERTS_PER_GROUP)
    sel = jnp.where(in_grp, elog, -jnp.inf)
    v1 = jnp.max(sel, axis=-1, keepdims=True)
    i1 = jnp.min(jnp.where(sel == v1, lanef, big), axis=-1, keepdims=True)
    sel2 = jnp.where(lanef == i1, -jnp.inf, sel)
    v2 = jnp.max(sel2, axis=-1, keepdims=True)
    i2 = jnp.min(jnp.where(sel2 == v2, lanef, big), axis=-1, keepdims=True)
    e2 = jnp.exp(v2 - v1)
    w1 = g_gate / (1.0 + e2)
    w2 = g_gate * e2 / (1.0 + e2)
    w_ref[...] = jnp.where(lane == 0, w1, jnp.where(lane == 1, w2, 0.0))[:, :ROUTE_COLS]

    hit1 = lanef == i1
    hit2 = lanef == i2
    onehot = jnp.where(hit1 | hit2, 1.0, 0.0)
    tri = tri_ref[...]
    carry = jnp.zeros((1, LANES), F32)
    ranks = []
    for j in range(tn // RANK_BLK):
        blk = onehot[j * RANK_BLK:(j + 1) * RANK_BLK]
        ranks.append(_dot(tri, blk.astype(BF16)) + carry)
        carry = carry + jnp.sum(blk, axis=0, keepdims=True)
    rank = jnp.concatenate(ranks, axis=0)
    cnt = carry
    li = lax.broadcasted_iota(jnp.int32, (LANES, LANES), 0)
    lj = lax.broadcasted_iota(jnp.int32, (LANES, LANES), 1)
    before = jnp.where(li < lj, 1.0, 0.0)
    off = jnp.dot(jnp.broadcast_to(cnt, (8, LANES)), before, precision=hp,
                  preferred_element_type=F32)[0:1]
    slot = (off + rank) * SUBLANES
    pos1 = jnp.sum(jnp.where(hit1, slot, 0.0), axis=-1, keepdims=True)
    pos2 = jnp.sum(jnp.where(hit2, slot, 0.0), axis=-1, keepdims=True)
    pos_ref[...] = jnp.where(lane == 0, pos1, jnp.where(lane == 1, pos2, 0.0)).astype(jnp.int32)[:, :ROUTE_COLS]
    row8 = lax.broadcasted_iota(jnp.int32, (8, LANES), 0)
    seg_ref[0] = jnp.where(row8 == 0, off, jnp.where(row8 == 1, cnt, 0.0)).astype(jnp.int32)


def _route(x, lp):
    t = x.shape[0]
    tn = min(MOE_SLAB, t)
    n_slabs = t // tn
    small = [lp["w_route"], lp["b_route"], lp["tri_rank"]]
    row = lambda w: pl.BlockSpec((tn, w), lambda i: (i, 0))
    return pl.pallas_call(
        _route_kernel,
        grid=(n_slabs,),
        in_specs=[row(D_MODEL)] + [_full(p.shape) for p in small],
        out_specs=[row(ROUTE_COLS), row(ROUTE_COLS), pl.BlockSpec((1, 8, LANES), lambda i: (i, 0, 0))],
        out_shape=[jax.ShapeDtypeStruct((t, ROUTE_COLS), F32), jax.ShapeDtypeStruct((t, ROUTE_COLS), jnp.int32),
                   jax.ShapeDtypeStruct((n_slabs, 8, LANES), jnp.int32)],
        compiler_params=_cparams("parallel"),
        name="moe_route",
    )(x, *small)


def _moe_sparse_kernel(off_ref, cnt_ref, pos_ref, w_ref, x_ref, eg_ref, eu_ref, ed_ref, lng_ref, lnb_ref,
                       p_ref, wp_ref, wg_ref, lng2_ref, lnb2_ref, o_ref, xs_ref):
    s = pl.program_id(0)
    e = pl.program_id(1)
    tn = x_ref.shape[0]

    n_feat = D_MODEL // LANES

    def tile_rows(first_row):
        return pl.ds(pl.multiple_of(first_row, SUBLANES), SUBLANES)

    @pl.when((s == 0) & (e == 0))
    def _():
        xs_ref[pl.ds(2 * tn * SUBLANES, MOE_WIN * SUBLANES), :] = jnp.zeros((MOE_WIN * SUBLANES, LANES), F32)

    @pl.when(e == 0)
    def _():
        def scatter(i, carry):
            t0 = pl.multiple_of(i * SUBLANES, SUBLANES)
            tiles = x_ref[pl.ds(t0, SUBLANES), :].reshape(SUBLANES, n_feat, LANES)
            for j in range(SUBLANES):
                xs_ref[tile_rows(pos_ref[0, 0, 2 * (t0 + j)]), :] = tiles[j]
                xs_ref[tile_rows(pos_ref[0, 0, 2 * (t0 + j) + 1]), :] = tiles[j]
            return carry
        lax.fori_loop(0, tn // SUBLANES, scatter, 0)

    ridx = lax.broadcasted_iota(jnp.int32, (MOE_WIN, 1), 0)

    base = s * N_EXPERTS + e * MOE_EBLK
    offs = [off_ref[base + j] for j in range(MOE_EBLK)]
    cnts = [cnt_ref[base + j] for j in range(MOE_EBLK)]

    def feat_rows(j, w, f):
        return pl.ds((offs[j] + w * MOE_WIN) * SUBLANES + f, MOE_WIN, stride=SUBLANES)

    def load_win(j, w):
        return jnp.concatenate([xs_ref[feat_rows(j, w, f), :] for f in range(n_feat)], axis=1)

    def hidden(j, xw):
        xb = xw.astype(BF16)
        hg = _dot(xb, eg_ref[j])
        hu = _dot(xb, eu_ref[j])
        return (hg * _sigmoid(hg) * hu).astype(BF16)

    def project(j, w, xw, hh):
        yw = _dot(hh, ed_ref[j])
        return jnp.where(ridx + w * MOE_WIN < cnts[j], yw, xw)

    def expert(j, w, xw):
        return project(j, w, xw, hidden(j, xw))

    def store_win(j, w, yw):
        for f in range(n_feat):
            xs_ref[feat_rows(j, w, f), :] = yw[:, f * LANES:(f + 1) * LANES]

    xw = [None] * MOE_EBLK
    hh = [None] * MOE_EBLK
    for j in range(MOE_EBLK + 1):
        if j < MOE_EBLK:
            xw[j] = load_win(j, 0)
            hh[j] = hidden(j, xw[j])
        if j > 0:
            store_win(j - 1, 0, project(j - 1, 0, xw[j - 1], hh[j - 1]))

    for j in range(MOE_EBLK):
        def window(w, carry, j=j):
            store_win(j, w, expert(j, w, load_win(j, w)))
            return carry
        lax.fori_loop(1, (cnts[j] + MOE_WIN - 1) // MOE_WIN, window, 0)

    @pl.when(e == N_EXPERTS // MOE_EBLK - 1)
    def _():
        def combine(i, carry):
            t0 = pl.multiple_of(i * SUBLANES, SUBLANES)
            tiles = []
            for j in range(SUBLANES):
                y1 = xs_ref[tile_rows(pos_ref[0, 0, 2 * (t0 + j)]), :]
                y2 = xs_ref[tile_rows(pos_ref[0, 0, 2 * (t0 + j) + 1]), :]
                tiles.append(w_ref[0, 0, 2 * (t0 + j)] * y1 + w_ref[0, 0, 2 * (t0 + j) + 1] * y2)
            o_ref[pl.ds(t0, SUBLANES), :] = jnp.stack(tiles, axis=0).reshape(SUBLANES, D_MODEL)
            return carry
        lax.fori_loop(0, tn // SUBLANES, combine, 0)
        def finish(j, carry):
            rs = pl.ds(pl.multiple_of(j * RANK_BLK, RANK_BLK), RANK_BLK)
            x2 = _layer_norm(DEEPNORM_ALPHA * x_ref[rs, :] + o_ref[rs, :], lng_ref[...], lnb_ref[...])
            emb = _dot(p_ref[rs, :].astype(BF16), wp_ref[...])
            gate = _sigmoid(_dot(x2.astype(BF16), wg_ref[...]))
            o_ref[rs, :] = _layer_norm(DEEPNORM_ALPHA * x2 + emb * gate, lng2_ref[...], lnb2_ref[...])
            return carry
        lax.fori_loop(0, tn // RANK_BLK, finish, 0)


def _moe_sparse(x, p, lp):
    t = x.shape[0]
    tn = min(MOE_SLAB, t)
    n_slabs = t // tn
    w, pos, seg = _route(x, lp)
    w_flat = w[:, :2].reshape(n_slabs, 1, 2 * tn)
    pos_flat = pos[:, :2].reshape(n_slabs, 1, 2 * tn)
    off = seg[:, 0, :N_EXPERTS].reshape(-1)
    cnt = seg[:, 1, :N_EXPERTS].reshape(-1)
    smem_row = pl.BlockSpec((1, 1, 2 * tn), lambda s, e, *_: (s, 0, 0), memory_space=pltpu.SMEM)
    slab = pl.BlockSpec((tn, D_MODEL), lambda s, e, *_: (s, 0), pipeline_mode=pl.Buffered(1))
    const = lambda a: pl.BlockSpec(a.shape, lambda s, e, *_: (0,) * a.ndim, pipeline_mode=pl.Buffered(1))
    p_slab = pl.BlockSpec((tn, p.shape[1]), lambda s, e, *_: (s, 0), pipeline_mode=pl.Buffered(1))
    tail = [lp["w_ple"], lp["w_pg"], lp["ln_g2"], lp["ln_b2"]]
    grid_spec = pltpu.PrefetchScalarGridSpec(
        num_scalar_prefetch=2,
        grid=(n_slabs, N_EXPERTS // MOE_EBLK),
        in_specs=[smem_row, smem_row, slab,
                  pl.BlockSpec((MOE_EBLK, D_MODEL, D_EXPERT), lambda s, e, *_: (e, 0, 0)),
                  pl.BlockSpec((MOE_EBLK, D_MODEL, D_EXPERT), lambda s, e, *_: (e, 0, 0)),
                  pl.BlockSpec((MOE_EBLK, D_EXPERT, D_MODEL), lambda s, e, *_: (e, 0, 0)),
                  const(lp["ln_g1"]), const(lp["ln_b1"]), p_slab] + [const(a) for a in tail],
        out_specs=slab,
        scratch_shapes=[pltpu.VMEM((MOE_ROWS * SUBLANES, LANES), F32)],
    )
    return pl.pallas_call(
        _moe_sparse_kernel,
        grid_spec=grid_spec,
        out_shape=jax.ShapeDtypeStruct((t, D_MODEL), F32),
        compiler_params=_cparams("arbitrary", "arbitrary"),
        name="moe_sparse",
    )(off, cnt, pos_flat, w_flat, x, lp["e_gate"], lp["e_up"], lp["e_down"], lp["ln_g1"], lp["ln_b1"],
      p, *tail)


def _pad_to(a, axis, size):
    pad = [(0, 0)] * a.ndim
    pad[axis] = (0, size - a.shape[axis])
    return jnp.pad(a, pad)


def _row(v):
    return v.reshape(1, -1).astype(F32)


def _layer_params(i, w_in, tok_mix, decay_base, decay_up, aaa_base, aaa_up, gate_up, k_k, k_a, r_k,
                  vres_base, vres_down, vres_up, gn_g, gn_b, w_branch_attn, w_branch_rwkv, w_out,
                  router_grp, router_grp_bias, router_exp, router_exp_bias, exp_gate, exp_up, exp_down,
                  ple_proj, ple_gate, ln_g, ln_b, consts):
    d = D_HEADS
    w = w_in[i]
    rw0 = 3 * d
    gate0 = rw0 + 3 * d + DECAY_LORA + AAA_LORA + GATE_LORA
    wqkv = jnp.concatenate([w[:, 0:d] * (HEAD_DIM ** -0.5), w[:, d:rw0]], axis=1).astype(BF16)
    wr = w[:, rw0:gate0]
    lora0 = 3 * d
    pieces = [wr[:, :lora0],
              _pad_to(wr[:, lora0:lora0 + DECAY_LORA], 1, LANES),
              _pad_to(wr[:, lora0 + DECAY_LORA:lora0 + DECAY_LORA + AAA_LORA], 1, LANES),
              _pad_to(wr[:, lora0 + DECAY_LORA + AAA_LORA:], 1, 2 * LANES)]
    wrw = jnp.concatenate(pieces, axis=1).astype(BF16)
    mu = tok_mix[i]
    mu_pieces = [mu[:lora0],
                 _pad_to(mu[lora0:lora0 + DECAY_LORA], 0, LANES),
                 _pad_to(mu[lora0 + DECAY_LORA:lora0 + DECAY_LORA + AAA_LORA], 0, LANES),
                 _pad_to(mu[lora0 + DECAY_LORA + AAA_LORA:], 0, 2 * LANES)]
    lp = dict(consts)
    lp.update(
        wqkv=wqkv, wrw=wrw, wgate=w[:, gate0:].astype(BF16),
        mu=_row(jnp.concatenate(mu_pieces)),
        decay_base=_row(decay_base[i]), decay_up=_pad_to(decay_up[i], 0, LANES).astype(F32),
        aaa_base=_row(aaa_base[i]), aaa_up=_pad_to(aaa_up[i], 0, LANES).astype(F32),
        gate_up=_pad_to(gate_up[i], 0, 2 * LANES).astype(F32),
        k_k=_row(k_k[i]), k_a=_row(k_a[i]), r_k=_row(r_k[i]),
        gn_g=_row(gn_g[i]), gn_b=_row(gn_b[i]),
        w_a=w_branch_attn[i].astype(BF16), w_b=w_branch_rwkv[i].astype(BF16), w_o=w_out[i].astype(BF16),
        w_route=_pad_to(jnp.concatenate([router_exp[i], router_grp[i]], axis=1), 1, LANES).astype(F32),
        b_route=_row(_pad_to(jnp.concatenate([router_exp_bias[i], router_grp_bias[i]]), 0, LANES)),
        e_gate=exp_gate[i].astype(BF16), e_up=exp_up[i].astype(BF16), e_down=exp_down[i].astype(BF16),
        w_ple=ple_proj[i].astype(BF16), w_pg=ple_gate[i].astype(BF16),
        ln_g0=_row(ln_g[i, 0]), ln_b0=_row(ln_b[i, 0]),
        ln_g1=_row(ln_g[i, 1]), ln_b1=_row(ln_b[i, 1]),
        ln_g2=_row(ln_g[i, 2]), ln_b2=_row(ln_b[i, 2]),
    )
    if i > 0:
        lp.update(vres_base=_row(vres_base[i - 1]),
                  vres_down=_pad_to(vres_down[i - 1], 1, LANES).astype(BF16),
                  vres_up=_pad_to(vres_up[i - 1], 0, LANES).astype(BF16))
    return lp


def _const_mats(tm):
    tok = jnp.arange(tm)
    same = (tok[:, None] // CHUNK) == (tok[None, :] // CHUNK)
    tri = (same & (tok[:, None] >= tok[None, :])).astype(BF16)
    rtok = jnp.arange(RANK_BLK)
    tri_rank = (rtok[None, :] < rtok[:, None]).astype(BF16)
    return dict(tri=tri, tri_rank=tri_rank)


def kernel(x, p, ln_in_g, ln_in_b, rel_bias, w_in, tok_mix, decay_base, decay_up, aaa_base, aaa_up,
           gate_up, k_k, k_a, r_k, vres_base, vres_down, vres_up, gn_g, gn_b, w_branch_attn,
           w_branch_rwkv, w_out, router_grp, router_grp_bias, router_exp, router_exp_bias, exp_gate,
           exp_up, exp_down, ple_proj, ple_gate, ln_g, ln_b):
    batch, seq, _ = x.shape
    t = batch * seq
    consts = _const_mats(256)
    bias_tabs = _attn_bias_tables(rel_bias)
    xt = x.reshape(t, D_MODEL)
    pt = p.reshape(p.shape[0], t, p.shape[-1])
    v_first = None
    for i in range(DEPTH):
        lp = _layer_params(i, w_in, tok_mix, decay_base, decay_up, aaa_base, aaa_up, gate_up, k_k, k_a,
                           r_k, vres_base, vres_down, vres_up, gn_g, gn_b, w_branch_attn, w_branch_rwkv,
                           w_out, router_grp, router_grp_bias, router_exp, router_exp_bias, exp_gate,
                           exp_up, exp_down, ple_proj, ple_gate, ln_g, ln_b, consts)
        if i == 0:
            xt, qkv, rw, gate = _proj(xt, _row(ln_in_g), _row(ln_in_b), lp["wqkv"], lp["wrw"],
                                      lp["wgate"], apply_ln=True)
        else:
            qkv, rw, gate = _proj(xt, _row(ln_in_g), _row(ln_in_b), lp["wqkv"], lp["wrw"],
                                  lp["wgate"], apply_ln=False)
        y_attn = _attention(qkv, bias_tabs, batch, seq)
        if i == 0:
            q1, y0, m, n0, g, bonus, v_first = _rwkv_chunk(rw, lp, seq, None)
        else:
            q1, y0, m, n0, g, bonus = _rwkv_chunk(rw, lp, seq, v_first)
        y = _rwkv_scan(q1, y0, m, n0, batch, seq)
        xt = _mix(y, g, bonus, y_attn, gate, xt, lp)
        xt = _moe_sparse(xt, pt[i], lp)
    return xt.reshape(batch, seq, D_MODEL)
```

```python
import functools

import jax
import jax.numpy as jnp
from jax import lax
from jax.experimental import pallas as pl
from jax.experimental.pallas import tpu as pltpu

F32 = jnp.float32
BF16 = jnp.bfloat16

D_MODEL = 1024
CHUNK = 64
N_PREV_CHUNKS = 8
HEAD_DIM = 64
N_HEADS = 8
D_HEADS = N_HEADS * HEAD_DIM
REL_CLIP = 128
DECAY_LORA = 64
AAA_LORA = 64
GATE_LORA = 160
N_GROUPS = 4
EXPERTS_PER_GROUP = 8
N_EXPERTS = N_GROUPS * EXPERTS_PER_GROUP
D_EXPERT = 256
DEPTH = 2
DEEPNORM_ALPHA = (2 * DEPTH) ** 0.25
LN_EPS = 1e-5
GN_EPS = 64e-5
NEG_INF = -1e30

LANES = 128
PAIR = 2 * HEAD_DIM
GRP_HEADS = 2
GRP = GRP_HEADS * HEAD_DIM
SUMMARY_BATCH = 16
ATTN_QBLK = 2 * CHUNK
ATTN_WIN = (N_PREV_CHUNKS + 2) * CHUNK
ATTN_GROUP = 4
RW_COLS_PAD = 2048
VMEM_LIMIT = 56 * 1024 * 1024


def _cparams(*sem):
    return pltpu.CompilerParams(dimension_semantics=sem, vmem_limit_bytes=VMEM_LIMIT)


def _dot(a, b):
    return jnp.dot(a, b, preferred_element_type=F32)


def _dot_nt(a, b):
    return lax.dot_general(a, b, (((1,), (1,)), ((), ())), preferred_element_type=F32)


def _split2(x):
    hi = x.astype(BF16)
    lo = (x - hi.astype(F32)).astype(BF16)
    return hi, lo


def _dot3(a, b):
    ah, al = _split2(a)
    bh, bl = _split2(b)
    return _dot(ah, bh) + (_dot(al, bh) + _dot(ah, bl))


def _dot_exact_lhs(a_bf16, x):
    x1 = x.astype(BF16)
    r1 = x - x1.astype(F32)
    x2 = r1.astype(BF16)
    x3 = (r1 - x2.astype(F32)).astype(BF16)
    return _dot(a_bf16, x1) + (_dot(a_bf16, x2) + _dot(a_bf16, x3))


def _seg_sum(x):
    left = lax.broadcasted_iota(jnp.int32, (x.shape[0], PAIR), 1) < HEAD_DIM
    outs = []
    for p in range(x.shape[1] // PAIR):
        xp = x[:, p * PAIR:(p + 1) * PAIR]
        s0 = jnp.sum(jnp.where(left, xp, 0.0), axis=-1, keepdims=True)
        s1 = jnp.sum(jnp.where(left, 0.0, xp), axis=-1, keepdims=True)
        outs.append(jnp.where(left, s0, s1))
    return jnp.concatenate(outs, axis=-1)


def _layer_norm(x, g, b):
    mu = jnp.mean(x, axis=-1, keepdims=True)
    xc = x - mu
    var = jnp.mean(xc * xc, axis=-1, keepdims=True)
    return xc * lax.rsqrt(var + LN_EPS) * g + b


def _sigmoid(x):
    return 1.0 / (1.0 + jnp.exp(-x))


def _full(shape):
    nd = len(shape)
    return pl.BlockSpec(shape, lambda *_: (0,) * nd)


def _proj_kernel(x_ref, g_ref, b_ref, wqkv_ref, wrw_ref, wgate_ref, *out_refs, apply_ln):
    x = x_ref[...]
    if apply_ln:
        xn_ref, qkv_ref, rw_ref, gate_ref = out_refs
        x = _layer_norm(x, g_ref[...], b_ref[...])
        xn_ref[...] = x
    else:
        qkv_ref, rw_ref, gate_ref = out_refs
    xb = x.astype(BF16)
    qkv_ref[...] = _dot(xb, wqkv_ref[...]).astype(BF16)
    rw_ref[...] = _dot(xb, wrw_ref[...])
    gate_ref[...] = _sigmoid(_dot(xb, wgate_ref[...])).astype(BF16)


def _proj(x, g, b, wqkv, wrw, wgate, apply_ln):
    t = x.shape[0]
    tm = min(256, t)
    row = lambda w: pl.BlockSpec((tm, w), lambda i: (i, 0))
    out_shape = [jax.ShapeDtypeStruct((t, 3 * D_HEADS), BF16),
                 jax.ShapeDtypeStruct((t, RW_COLS_PAD), F32),
                 jax.ShapeDtypeStruct((t, 2 * D_MODEL), BF16)]
    out_specs = [row(3 * D_HEADS), row(RW_COLS_PAD), row(2 * D_MODEL)]
    if apply_ln:
        out_shape = [jax.ShapeDtypeStruct((t, D_MODEL), F32)] + out_shape
        out_specs = [row(D_MODEL)] + out_specs
    return pl.pallas_call(
        functools.partial(_proj_kernel, apply_ln=apply_ln),
        grid=(t // tm,),
        in_specs=[row(D_MODEL), _full(g.shape), _full(b.shape),
                  _full(wqkv.shape), _full(wrw.shape), _full(wgate.shape)],
        out_specs=out_specs,
        out_shape=out_shape,
        compiler_params=_cparams("parallel"),
        name="proj_ln" if apply_ln else "proj",
    )(x, g, b, wqkv, wrw, wgate)


def _attn_kernel(q_ref, k_ref, v_ref, bias_ref, o_ref):
    qb = pl.program_id(1)
    ws = pl.multiple_of(jnp.maximum(qb * ATTN_QBLK - N_PREV_CHUNKS * CHUNK, 0), ATTN_QBLK)
    q = q_ref[0]
    kw = k_ref[0, pl.ds(ws, ATTN_WIN), :]
    vw = v_ref[0, pl.ds(ws, ATTN_WIN), :]
    left = lax.broadcasted_iota(jnp.int32, (ATTN_QBLK, PAIR), 1) < HEAD_DIM
    zero = jnp.zeros((ATTN_QBLK, PAIR), BF16)
    heads = [(h // 2, h % 2) for h in range(N_HEADS)]
    spans = [slice(p * PAIR, (p + 1) * PAIR) for p in range(N_HEADS // 2)]
    o = []
    for g0 in range(0, N_HEADS, ATTN_GROUP):
        grp = list(enumerate(heads))[g0:g0 + ATTN_GROUP]
        qm = [jnp.where(left, q[:, spans[p]], zero) if side == 0 else jnp.where(left, zero, q[:, spans[p]])
              for _, (p, side) in grp]
        s = [_dot_nt(x, kw[:, spans[p]]) + bias_ref[0, h] for x, (h, (p, _)) in zip(qm, grp)]
        m = [jnp.max(x, axis=-1, keepdims=True) for x in s]
        e = [jnp.exp(x - mx) for x, mx in zip(s, m)]
        inv = [1.0 / jnp.sum(x, axis=-1, keepdims=True) for x in e]
        o += [_dot(x.astype(BF16), vw[:, spans[p]]) * r for x, r, (_, (p, _)) in zip(e, inv, grp)]
    pairs = [jnp.where(left, o[2 * p], o[2 * p + 1]) for p in range(N_HEADS // 2)]
    o_ref[0] = jnp.concatenate(pairs, axis=-1).astype(BF16)


def _attention(qkv, bias_tabs, batch, seq):
    qkv3 = qkv.reshape(batch, seq, 3 * D_HEADS)
    n_tabs = bias_tabs.shape[0]
    out = pl.pallas_call(
        _attn_kernel,
        grid=(batch, seq // ATTN_QBLK),
        in_specs=[
            pl.BlockSpec((1, ATTN_QBLK, D_HEADS), lambda b, i: (b, i, 0)),
            pl.BlockSpec((1, seq, D_HEADS), lambda b, i: (b, 0, 1)),
            pl.BlockSpec((1, seq, D_HEADS), lambda b, i: (b, 0, 2)),
            pl.BlockSpec((1, N_HEADS, ATTN_QBLK, ATTN_WIN),
                         lambda b, i: (jnp.minimum(i, n_tabs - 1), 0, 0, 0)),
        ],
        out_specs=pl.BlockSpec((1, ATTN_QBLK, D_HEADS), lambda b, i: (b, i, 0)),
        out_shape=jax.ShapeDtypeStruct((batch, seq, D_HEADS), BF16),
        compiler_params=_cparams("parallel", "arbitrary"),
        name="band_attn",
    )(qkv3, qkv3, qkv3, bias_tabs)
    return out.reshape(batch * seq, D_HEADS)


def _attn_bias_tables(rel_bias):
    n_tabs = N_PREV_CHUNKS * CHUNK // ATTN_QBLK + 1
    start = (jnp.arange(n_tabs) * ATTN_QBLK)[:, None, None]
    qpos = start + jnp.arange(ATTN_QBLK)[None, :, None]
    kpos = jnp.arange(ATTN_WIN)[None, None, :]
    qc, kc = qpos // CHUNK, kpos // CHUNK
    valid = (kc <= qc) & (kc >= qc - N_PREV_CHUNKS)
    n = jnp.arange(ATTN_QBLK - 1 + ATTN_WIN)
    dist = start[:, :, 0] + (ATTN_QBLK - 1) - n[None, :]
    vals = rel_bias.astype(F32)[:, jnp.clip(dist, -REL_CLIP, REL_CLIP) + REL_CLIP]
    length = ATTN_QBLK - 1 + ATTN_WIN
    tiled = jnp.tile(_pad_to(vals, 2, length + 1), (1, 1, ATTN_QBLK))
    cut = tiled[:, :, :ATTN_QBLK * length].reshape(N_HEADS, n_tabs, ATTN_QBLK, length)
    bias = cut[:, :, :, ATTN_QBLK - 1:ATTN_QBLK - 1 + ATTN_WIN]
    bias = jnp.where(valid[None], bias, NEG_INF)
    return jnp.transpose(bias, (1, 0, 2, 3))


def _rwkv_chunk_kernel(*refs, seq, tm, has_vres):
    if has_vres:
        (rw_ref, prev_ref, mu_ref, dbase_ref, dup_ref, abase_ref, aup_ref, gup_ref, kk_ref, ka_ref,
         rk_ref, tri_ref, vfirst_ref, vbase_ref, vdown_ref, vup_ref,
         q1_ref, y0_ref, m_ref, n0_ref, g_ref, bonus_ref,
         rw_s, aw_s, bi_s, ki_s, be_s, ke_s, v_s, wc_s) = refs
    else:
        (rw_ref, prev_ref, mu_ref, dbase_ref, dup_ref, abase_ref, aup_ref, gup_ref, kk_ref, ka_ref,
         rk_ref, tri_ref,
         q1_ref, y0_ref, m_ref, n0_ref, g_ref, bonus_ref, vfirst_out_ref,
         rw_s, aw_s, bi_s, ki_s, be_s, ke_s, v_s, wc_s) = refs

    cols = rw_ref[...]
    first = (pl.program_id(0) % (seq // tm)) == 0
    prev_row = jnp.where(first, 0.0, prev_ref[7:8, :])
    rows = lax.broadcasted_iota(jnp.int32, cols.shape, 0)
    prev = jnp.where(rows == 0, prev_row, pltpu.roll(cols, 1, axis=0))
    xs = cols + (prev - cols) * mu_ref[...]

    d = D_HEADS
    r, k, v = xs[:, 0:d], xs[:, d:2 * d], xs[:, 2 * d:3 * d]
    wd = xs[:, 3 * d:3 * d + LANES]
    ad = xs[:, 3 * d + LANES:3 * d + 2 * LANES]
    gd = xs[:, 3 * d + 2 * LANES:3 * d + 4 * LANES]

    u = dbase_ref[...] + _dot3(jnp.tanh(wd), dup_ref[...])
    w_log = jnp.minimum(u, 0.0) - jnp.log(1.0 + jnp.exp(-jnp.abs(u))) - 0.5
    logw = -jnp.exp(w_log)
    a_icl = _sigmoid(abase_ref[...] + _dot3(ad, aup_ref[...]))
    g_ref[...] = _dot3(_sigmoid(gd), gup_ref[...]).astype(BF16)
    kk = k * kk_ref[...]
    kk = kk * lax.rsqrt(jnp.maximum(_seg_sum(kk * kk), 1e-24))
    k2 = k * (1.0 + (a_icl - 1.0) * ka_ref[...])
    if has_vres:
        low = _dot(v.astype(BF16), vdown_ref[...])
        mix = _sigmoid(vbase_ref[...] + _dot(low.astype(BF16), vup_ref[...]))
        v2 = v + (vfirst_ref[...] - v) * mix
    else:
        v2 = v
        vfirst_out_ref[...] = v
    a_vec = -kk
    b_vec = kk * a_icl
    bonus_ref[...] = (_seg_sum(r * k2 * rk_ref[...]) * v2).astype(BF16)

    cw = _dot_exact_lhs(tri_ref[...], logw)
    bi_ = lax.broadcasted_iota(jnp.int32, (tm, tm), 0) // CHUNK
    bj_ = lax.broadcasted_iota(jnp.int32, (tm, tm), 1) // CHUNK
    cw_tot = _dot_exact_lhs(jnp.where(bi_ == bj_, 1.0, 0.0).astype(BF16), logw)
    rw_s[...] = r * jnp.exp(cw)
    aw_s[...] = a_vec * jnp.exp(cw - logw)
    inv = jnp.exp(-cw)
    bi_s[...] = b_vec * inv
    ki_s[...] = k2 * inv
    rest = jnp.exp(cw_tot - cw)
    be_s[...] = b_vec * rest
    ke_s[...] = k2 * rest
    v_s[...] = v2
    wc_s[...] = jnp.exp(cw_tot)

    shape = (CHUNK, GRP)
    ti = lax.broadcasted_iota(jnp.int32, shape, 0)
    li = lax.broadcasted_iota(jnp.int32, shape, 1)
    si = li & (HEAD_DIM - 1)
    hid = li // HEAD_DIM
    strict = ti > si
    incl = ti >= si
    eye = ti == si

    def bd(x):
        z = jnp.zeros_like(x)
        return jnp.concatenate([jnp.where(hid == h, x, z) for h in range(GRP_HEADS)], axis=0)

    def tpair(x):
        return jnp.concatenate([x[:, h * HEAD_DIM:(h + 1) * HEAD_DIM].T for h in range(GRP_HEADS)], axis=1)

    def b16(xs):
        return [x.astype(BF16) for x in xs]

    def summaries(insts):
        sl = [(slice(c * CHUNK, (c + 1) * CHUNK), slice(p * GRP, (p + 1) * GRP)) for c, p in insts]
        rw = [rw_s[s] for s in sl]
        awb = b16([aw_s[s] for s in sl])
        vb = b16([v_s[s] for s in sl])
        lhs = [jnp.concatenate([a, r.astype(BF16)], axis=0) for a, r in zip(awb, rw)]
        bik = [jnp.concatenate([bd(b), bd(k)], axis=0)
               for b, k in zip(b16([bi_s[s] for s in sl]), b16([ki_s[s] for s in sl]))]
        g = [_dot_nt(l, x) for l, x in zip(lhs, bik)]
        a_ab = [jnp.where(strict, x[:CHUNK, :GRP], 0.0) for x in g]
        a_rb = b16([jnp.where(incl, x[CHUNK:, :GRP], 0.0) for x in g])
        a_ak = b16([jnp.where(strict, x[:CHUNK, GRP:], 0.0) for x in g])
        a_rk = b16([jnp.where(incl, x[CHUNK:, GRP:], 0.0) for x in g])
        bdv = [bd(x) for x in vb]
        x0 = [_dot(a, v) for a, v in zip(a_ak, bdv)]
        tinv = [jnp.where(eye, 1.0, a) for a in a_ab]
        lpb = b16(a_ab)
        lpb = b16([_dot(l, bd(l)) for l in lpb])
        for _ in range(4):
            both = [_dot(jnp.concatenate([t.astype(BF16), l], axis=0), bd(l)) for t, l in zip(tinv, lpb)]
            tinv = [t + x[:CHUNK] for t, x in zip(tinv, both)]
            lpb = b16([x[CHUNK:] for x in both])
        tinv = [t + _dot(t.astype(BF16), bd(l)) for t, l in zip(tinv, lpb)]
        tb = b16(tinv)
        pu = b16([_dot(t, jnp.concatenate([bd(a), bd(x.astype(BF16))], axis=1))
                  for t, a, x in zip(tb, awb, x0)])
        pu_bd = [jnp.concatenate([bd(x[:, :GRP]), bd(x[:, GRP:])], axis=1) for x in pu]
        qy = [_dot(a, r) for a, r in zip(a_rb, pu_bd)]
        q1 = [r + x[:, :GRP] for r, x in zip(rw, qy)]
        y0 = [x[:, GRP:] + _dot(k, v) for x, k, v in zip(qy, a_rk, bdv)]
        bet = b16([tpair(be_s[s]) for s in sl])
        ket = b16([tpair(ke_s[s]) for s in sl])
        mn = [_dot(b, r) for b, r in zip(bet, pu_bd)]
        mm = [x[:, :GRP] for x in mn]
        nn = [x[:, GRP:] + _dot(k, v) for x, k, v in zip(mn, ket, bdv)]
        for i, (s, (c, p)) in enumerate(zip(sl, insts)):
            wc = wc_s[c * CHUNK:c * CHUNK + 1, s[1]]
            q1_ref[s] = q1[i]
            y0_ref[s] = y0[i]
            m_ref[s] = jnp.where(eye, wc, 0.0) + mm[i]
            n0_ref[s] = nn[i]

    insts = [(c, p) for c in range(tm // CHUNK) for p in range(D_HEADS // GRP)]
    for i0 in range(0, len(insts), SUMMARY_BATCH):
        summaries(insts[i0:i0 + SUMMARY_BATCH])


def _rwkv_chunk(rw, lp, seq, v_first):
    t = rw.shape[0]
    tm = 256
    has_vres = v_first is not None
    nct = tm // CHUNK
    row = lambda w: pl.BlockSpec((tm, w), lambda i: (i, 0))
    prev_spec = pl.BlockSpec((8, RW_COLS_PAD), lambda i: (jnp.maximum(i * (tm // 8) - 1, 0), 0))
    params = [lp["mu"], lp["decay_base"], lp["decay_up"], lp["aaa_base"], lp["aaa_up"], lp["gate_up"],
              lp["k_k"], lp["k_a"], lp["r_k"], lp["tri"]]
    inputs = [rw, rw] + params
    in_specs = [row(RW_COLS_PAD), prev_spec] + [_full(p.shape) for p in params]
    if has_vres:
        extra = [lp["vres_base"], lp["vres_down"], lp["vres_up"]]
        inputs += [v_first] + extra
        in_specs += [row(D_HEADS)] + [_full(p.shape) for p in extra]
    tok_shape = jax.ShapeDtypeStruct((t, D_HEADS), F32)
    half_shape = jax.ShapeDtypeStruct((t, D_HEADS), BF16)
    out_shape = [tok_shape] * 4 + [half_shape] * 2
    out_specs = [row(D_HEADS)] * 6
    if not has_vres:
        out_shape.append(tok_shape)
        out_specs.append(row(D_HEADS))
    scratch = [pltpu.VMEM((tm, D_HEADS), F32) for _ in range(8)]
    return pl.pallas_call(
        functools.partial(_rwkv_chunk_kernel, seq=seq, tm=tm, has_vres=has_vres),
        grid=(t // tm,),
        in_specs=in_specs,
        out_specs=out_specs,
        out_shape=out_shape,
        scratch_shapes=scratch,
        compiler_params=_cparams("parallel"),
        name="rwkv_chunk_vres" if has_vres else "rwkv_chunk",
    )(*inputs)


def _rwkv_scan_kernel(q1_ref, y0_ref, m_ref, n0_ref, y_ref, st_ref, *, n_chunks):
    @pl.when(pl.program_id(1) == 0)
    def _():
        st_ref[...] = jnp.zeros_like(st_ref)

    left = lax.broadcasted_iota(jnp.int32, (CHUNK, PAIR), 1) < HEAD_DIM

    def bd(x):
        z = jnp.zeros_like(x)
        return jnp.concatenate([jnp.where(left, x, z), jnp.where(left, z, x)], axis=0)

    n_pairs = D_HEADS // PAIR
    st = [st_ref[:, p * PAIR:(p + 1) * PAIR] for p in range(n_pairs)]
    for c in range(n_chunks):
        rs = slice(c * CHUNK, (c + 1) * CHUNK)
        for p in range(n_pairs):
            ls = slice(p * PAIR, (p + 1) * PAIR)
            lh, ll = _split2(jnp.concatenate([q1_ref[rs, ls], m_ref[rs, ls]], axis=0))
            sh, sl = _split2(st[p])
            bh = bd(sh)
            res = _dot(lh, bh) + (_dot(ll, bh) + _dot(lh, bd(sl)))
            y_ref[rs, ls] = res[:CHUNK] + y0_ref[rs, ls]
            st[p] = res[CHUNK:] + n0_ref[rs, ls]
    for p in range(n_pairs):
        st_ref[:, p * PAIR:(p + 1) * PAIR] = st[p]


def _rwkv_scan(q1, y0, m, n0, batch, seq):
    t = q1.shape[0]
    cb = 8
    steps = seq // (cb * CHUNK)
    tok_spec = pl.BlockSpec((cb * CHUNK, D_HEADS), lambda b, j: (b * steps + j, 0))
    return pl.pallas_call(
        functools.partial(_rwkv_scan_kernel, n_chunks=cb),
        grid=(batch, steps),
        in_specs=[tok_spec] * 4,
        out_specs=tok_spec,
        out_shape=jax.ShapeDtypeStruct((t, D_HEADS), F32),
        scratch_shapes=[pltpu.VMEM((HEAD_DIM, D_HEADS), F32)],
        compiler_params=_cparams("parallel", "arbitrary"),
        name="rwkv_scan",
    )(q1, y0, m, n0)


def _mix_kernel(y_ref, g_ref, bonus_ref, ya_ref, gate_ref, x_ref, wa_ref, wb_ref, wo_ref,
                gng_ref, gnb_ref, lng_ref, lnb_ref, o_ref):
    y = y_ref[...]
    mean = _seg_sum(y) * (1.0 / HEAD_DIM)
    yc = y - mean
    var = _seg_sum(yc * yc) * (1.0 / HEAD_DIM)
    yn = yc * lax.rsqrt(var + GN_EPS) * gng_ref[...] + gnb_ref[...]
    yr = (yn + bonus_ref[...].astype(F32)) * g_ref[...].astype(F32)
    za = _dot(ya_ref[...], wa_ref[...])
    zb = _dot(yr.astype(BF16), wb_ref[...])
    gate = gate_ref[...]
    mixed_in = gate[:, :D_MODEL].astype(F32) * za + gate[:, D_MODEL:].astype(F32) * zb
    mixed = _dot(mixed_in.astype(BF16), wo_ref[...])
    o_ref[...] = _layer_norm(DEEPNORM_ALPHA * x_ref[...] + mixed, lng_ref[...], lnb_ref[...])


def _mix(y, g, bonus, y_attn, gate, x, lp):
    t = x.shape[0]
    tm = min(256, t)
    row = lambda w: pl.BlockSpec((tm, w), lambda i: (i, 0))
    params = [lp["w_a"], lp["w_b"], lp["w_o"], lp["gn_g"], lp["gn_b"], lp["ln_g0"], lp["ln_b0"]]
    return pl.pallas_call(
        _mix_kernel,
        grid=(t // tm,),
        in_specs=[row(D_HEADS), row(D_HEADS), row(D_HEADS), row(D_HEADS), row(2 * D_MODEL), row(D_MODEL)]
        + [_full(p.shape) for p in params],
        out_specs=row(D_MODEL),
        out_shape=jax.ShapeDtypeStruct((t, D_MODEL), F32),
        compiler_params=_cparams("parallel"),
        name="branch_mix",
    )(y, g, bonus, y_attn, gate, x, *params)


MOE_SLAB = 2048
MOE_WIN = 128
MOE_EBLK = 4
MOE_ROWS = 2 * MOE_SLAB + MOE_WIN
SUBLANES = 8
RANK_BLK = 256
ROUTE_SLOTS = 2


def _route_kernel(x_ref, wr_ref, br_ref, tri_ref, w_ref, pos_ref, seg_ref):
    x = x_ref[...]
    tn = x.shape[0]
    lane = lax.broadcasted_iota(jnp.int32, (tn, LANES), 1)
    lanef = lane.astype(F32)
    hp = lax.Precision.HIGHEST
    elog = _dot3(x, wr_ref[...]) + br_ref[...]
    big = float(LANES)
    glog = jnp.where((lane >= N_EXPERTS) & (lane < N_EXPERTS + N_GROUPS), elog, -jnp.inf)
    gmax = jnp.max(glog, axis=-1, keepdims=True)
    g_gate = 1.0 / jnp.sum(jnp.exp(glog - gmax), axis=-1, keepdims=True)
    grp = jnp.min(jnp.where(glog == gmax, lanef, big), axis=-1, keepdims=True) - N_EXPERTS
    in_grp = (lanef >= grp * EXPERTS_PER_GROUP) & (lanef < (grp + 1.0) * EXPERTS_PER_GROUP)
    sel = jnp.where(in_grp, elog, -jnp.inf)
    v1 = jnp.max(sel, axis=-1, keepdims=True)
    i1 = jnp.min(jnp.where(sel == v1, lanef, big), axis=-1, keepdims=True)
    sel2 = jnp.where(lanef == i1, -jnp.inf, sel)
    v2 = jnp.max(sel2, axis=-1, keepdims=True)
    i2 = jnp.min(jnp.where(sel2 == v2, lanef, big), axis=-1, keepdims=True)
    e2 = jnp.exp(v2 - v1)
    w1 = g_gate / (1.0 + e2)
    w2 = g_gate * e2 / (1.0 + e2)
    pick = jnp.where(lax.broadcasted_iota(jnp.int32, (8, LANES), 0) == lax.broadcasted_iota(jnp.int32, (8, LANES), 1),
                     1.0, 0.0)

    def token_rows(c0, c1):
        cols = jnp.where(lane == 0, c0, jnp.where(lane == 1, c1, 0.0))
        return lax.dot_general(pick, cols, (((1,), (1,)), ((), ())), precision=hp,
                               preferred_element_type=F32)[0:ROUTE_SLOTS]

    w_ref[0] = token_rows(w1, w2)

    hit1 = lanef == i1
    hit2 = lanef == i2
    onehot = jnp.where(hit1 | hit2, 1.0, 0.0)
    tri = tri_ref[...]
    carry = jnp.zeros((1, LANES), F32)
    ranks = []
    for j in range(tn // RANK_BLK):
        blk = onehot[j * RANK_BLK:(j + 1) * RANK_BLK]
        ranks.append(_dot(tri, blk.astype(BF16)) + carry)
        carry = carry + jnp.sum(blk, axis=0, keepdims=True)
    rank = jnp.concatenate(ranks, axis=0)
    cnt = carry
    li = lax.broadcasted_iota(jnp.int32, (LANES, LANES), 0)
    lj = lax.broadcasted_iota(jnp.int32, (LANES, LANES), 1)
    before = jnp.where(li < lj, 1.0, 0.0)
    off = jnp.dot(jnp.broadcast_to(cnt, (8, LANES)), before, precision=hp,
                  preferred_element_type=F32)[0:1]
    slot = (off + rank) * SUBLANES
    pos1 = jnp.sum(jnp.where(hit1, slot, 0.0), axis=-1, keepdims=True)
    pos2 = jnp.sum(jnp.where(hit2, slot, 0.0), axis=-1, keepdims=True)
    pos_ref[0] = token_rows(pos1, pos2).astype(jnp.int32)
    row8 = lax.broadcasted_iota(jnp.int32, (8, LANES), 0)
    seg_ref[0] = jnp.where(row8 == 0, off, jnp.where(row8 == 1, cnt, 0.0)).astype(jnp.int32)


def _route(x, lp):
    t = x.shape[0]
    tn = min(MOE_SLAB, t)
    n_slabs = t // tn
    small = [lp["w_route"], lp["b_route"], lp["tri_rank"]]
    row = lambda w: pl.BlockSpec((tn, w), lambda i: (i, 0))
    return pl.pallas_call(
        _route_kernel,
        grid=(n_slabs,),
        in_specs=[row(D_MODEL)] + [_full(p.shape) for p in small],
        out_specs=[pl.BlockSpec((1, ROUTE_SLOTS, tn), lambda i: (i, 0, 0)),
                   pl.BlockSpec((1, ROUTE_SLOTS, tn), lambda i: (i, 0, 0)),
                   pl.BlockSpec((1, 8, LANES), lambda i: (i, 0, 0))],
        out_shape=[jax.ShapeDtypeStruct((n_slabs, ROUTE_SLOTS, tn), F32),
                   jax.ShapeDtypeStruct((n_slabs, ROUTE_SLOTS, tn), jnp.int32),
                   jax.ShapeDtypeStruct((n_slabs, 8, LANES), jnp.int32)],
        compiler_params=_cparams("parallel"),
        name="moe_route",
    )(x, *small)


def _moe_sparse_kernel(off_ref, cnt_ref, pos_ref, w_ref, x_ref, eg_ref, eu_ref, ed_ref, lng_ref, lnb_ref,
                       p_ref, wp_ref, wg_ref, lng2_ref, lnb2_ref, o_ref, xs_ref):
    s = pl.program_id(0)
    e = pl.program_id(1)
    tn = x_ref.shape[0]

    n_feat = D_MODEL // LANES

    def tile_rows(first_row):
        return pl.ds(pl.multiple_of(first_row, SUBLANES), SUBLANES)

    @pl.when((s == 0) & (e == 0))
    def _():
        xs_ref[pl.ds(2 * tn * SUBLANES, MOE_WIN * SUBLANES), :] = jnp.zeros((MOE_WIN * SUBLANES, LANES), F32)

    @pl.when(e == 0)
    def _():
        def scatter(i, carry):
            t0 = pl.multiple_of(i * SUBLANES, SUBLANES)
            tiles = x_ref[pl.ds(t0, SUBLANES), :].reshape(SUBLANES, n_feat, LANES)
            for j in range(SUBLANES):
                xs_ref[tile_rows(pos_ref[0, 0, t0 + j]), :] = tiles[j]
                xs_ref[tile_rows(pos_ref[0, 1, t0 + j]), :] = tiles[j]
            return carry
        lax.fori_loop(0, tn // SUBLANES, scatter, 0)

    ridx = lax.broadcasted_iota(jnp.int32, (MOE_WIN, 1), 0)

    base = s * N_EXPERTS + e * MOE_EBLK
    offs = [off_ref[base + j] for j in range(MOE_EBLK)]
    cnts = [cnt_ref[base + j] for j in range(MOE_EBLK)]

    def feat_rows(j, w, f):
        return pl.ds((offs[j] + w * MOE_WIN) * SUBLANES + f, MOE_WIN, stride=SUBLANES)

    def load_win(j, w):
        return jnp.concatenate([xs_ref[feat_rows(j, w, f), :] for f in range(n_feat)], axis=1)

    def hidden(j, xw):
        xb = xw.astype(BF16)
        hg = _dot(xb, eg_ref[j])
        hu = _dot(xb, eu_ref[j])
        return (hg * _sigmoid(hg) * hu).astype(BF16)

    def project(j, w, xw, hh):
        yw = _dot(hh, ed_ref[j])
        return jnp.where(ridx + w * MOE_WIN < cnts[j], yw, xw)

    def expert(j, w, xw):
        return project(j, w, xw, hidden(j, xw))

    def store_win(j, w, yw):
        for f in range(n_feat):
            xs_ref[feat_rows(j, w, f), :] = yw[:, f * LANES:(f + 1) * LANES]

    xw = [None] * MOE_EBLK
    hh = [None] * MOE_EBLK
    for j in range(MOE_EBLK + 1):
        if j < MOE_EBLK:
            xw[j] = load_win(j, 0)
            hh[j] = hidden(j, xw[j])
        if j > 0:
            store_win(j - 1, 0, project(j - 1, 0, xw[j - 1], hh[j - 1]))

    for j in range(MOE_EBLK):
        def window(w, carry, j=j):
            store_win(j, w, expert(j, w, load_win(j, w)))
            return carry
        lax.fori_loop(1, (cnts[j] + MOE_WIN - 1) // MOE_WIN, window, 0)

    @pl.when(e == N_EXPERTS // MOE_EBLK - 1)
    def _():
        def combine(i, carry):
            t0 = pl.multiple_of(i * SUBLANES, SUBLANES)
            tiles = []
            for j in range(SUBLANES):
                y1 = xs_ref[tile_rows(pos_ref[0, 0, t0 + j]), :]
                y2 = xs_ref[tile_rows(pos_ref[0, 1, t0 + j]), :]
                tiles.append(w_ref[0, 0, t0 + j] * y1 + w_ref[0, 1, t0 + j] * y2)
            o_ref[pl.ds(t0, SUBLANES), :] = jnp.stack(tiles, axis=0).reshape(SUBLANES, D_MODEL)
            return carry
        lax.fori_loop(0, tn // SUBLANES, combine, 0)
        def finish(j, carry):
            rs = pl.ds(pl.multiple_of(j * RANK_BLK, RANK_BLK), RANK_BLK)
            x2 = _layer_norm(DEEPNORM_ALPHA * x_ref[rs, :] + o_ref[rs, :], lng_ref[...], lnb_ref[...])
            emb = _dot(p_ref[rs, :].astype(BF16), wp_ref[...])
            gate = _sigmoid(_dot(x2.astype(BF16), wg_ref[...]))
            o_ref[rs, :] = _layer_norm(DEEPNORM_ALPHA * x2 + emb * gate, lng2_ref[...], lnb2_ref[...])
            return carry
        lax.fori_loop(0, tn // RANK_BLK, finish, 0)


def _moe_sparse(x, p, layer, lp):
    t = x.shape[0]
    tn = min(MOE_SLAB, t)
    n_slabs = t // tn
    w, pos, seg = _route(x, lp)
    off = seg[:, 0, :N_EXPERTS].reshape(-1)
    cnt = seg[:, 1, :N_EXPERTS].reshape(-1)
    smem_row = pl.BlockSpec((1, ROUTE_SLOTS, tn), lambda s, e, *_: (s, 0, 0), memory_space=pltpu.SMEM)
    slab = pl.BlockSpec((tn, D_MODEL), lambda s, e, *_: (s, 0), pipeline_mode=pl.Buffered(1))
    const = lambda a: pl.BlockSpec(a.shape, lambda s, e, *_: (0,) * a.ndim, pipeline_mode=pl.Buffered(1))
    p_slab = pl.BlockSpec((None, tn, p.shape[2]), lambda s, e, *_: (layer, s, 0), pipeline_mode=pl.Buffered(1))
    tail = [lp["w_ple"], lp["w_pg"], lp["ln_g2"], lp["ln_b2"]]
    grid_spec = pltpu.PrefetchScalarGridSpec(
        num_scalar_prefetch=2,
        grid=(n_slabs, N_EXPERTS // MOE_EBLK),
        in_specs=[smem_row, smem_row, slab,
                  pl.BlockSpec((MOE_EBLK, D_MODEL, D_EXPERT), lambda s, e, *_: (e, 0, 0)),
                  pl.BlockSpec((MOE_EBLK, D_MODEL, D_EXPERT), lambda s, e, *_: (e, 0, 0)),
                  pl.BlockSpec((MOE_EBLK, D_EXPERT, D_MODEL), lambda s, e, *_: (e, 0, 0)),
                  const(lp["ln_g1"]), const(lp["ln_b1"]), p_slab] + [const(a) for a in tail],
        out_specs=slab,
        scratch_shapes=[pltpu.VMEM((MOE_ROWS * SUBLANES, LANES), F32)],
    )
    return pl.pallas_call(
        _moe_sparse_kernel,
        grid_spec=grid_spec,
        out_shape=jax.ShapeDtypeStruct((t, D_MODEL), F32),
        compiler_params=_cparams("arbitrary", "arbitrary"),
        name="moe_sparse",
    )(off, cnt, pos, w, x, lp["e_gate"], lp["e_up"], lp["e_down"], lp["ln_g1"], lp["ln_b1"],
      p, *tail)


def _pad_to(a, axis, size):
    pad = [(0, 0)] * a.ndim
    pad[axis] = (0, size - a.shape[axis])
    return jnp.pad(a, pad)


def _row(v):
    return v.reshape(1, -1).astype(F32)


def _layer_params(i, w_in, tok_mix, decay_base, decay_up, aaa_base, aaa_up, gate_up, k_k, k_a, r_k,
                  vres_base, vres_down, vres_up, gn_g, gn_b, w_branch_attn, w_branch_rwkv, w_out,
                  router_grp, router_grp_bias, router_exp, router_exp_bias, exp_gate, exp_up, exp_down,
                  ple_proj, ple_gate, ln_g, ln_b, consts):
    d = D_HEADS
    w = w_in[i]
    rw0 = 3 * d
    gate0 = rw0 + 3 * d + DECAY_LORA + AAA_LORA + GATE_LORA
    wqkv = jnp.concatenate([w[:, 0:d] * (HEAD_DIM ** -0.5), w[:, d:rw0]], axis=1).astype(BF16)
    wr = w[:, rw0:gate0]
    lora0 = 3 * d
    pieces = [wr[:, :lora0],
              _pad_to(wr[:, lora0:lora0 + DECAY_LORA], 1, LANES),
              _pad_to(wr[:, lora0 + DECAY_LORA:lora0 + DECAY_LORA + AAA_LORA], 1, LANES),
              _pad_to(wr[:, lora0 + DECAY_LORA + AAA_LORA:], 1, 2 * LANES)]
    wrw = jnp.concatenate(pieces, axis=1).astype(BF16)
    mu = tok_mix[i]
    mu_pieces = [mu[:lora0],
                 _pad_to(mu[lora0:lora0 + DECAY_LORA], 0, LANES),
                 _pad_to(mu[lora0 + DECAY_LORA:lora0 + DECAY_LORA + AAA_LORA], 0, LANES),
                 _pad_to(mu[lora0 + DECAY_LORA + AAA_LORA:], 0, 2 * LANES)]
    lp = dict(consts)
    lp.update(
        wqkv=wqkv, wrw=wrw, wgate=w[:, gate0:].astype(BF16),
        mu=_row(jnp.concatenate(mu_pieces)),
        decay_base=_row(decay_base[i]), decay_up=_pad_to(decay_up[i], 0, LANES).astype(F32),
        aaa_base=_row(aaa_base[i]), aaa_up=_pad_to(aaa_up[i], 0, LANES).astype(F32),
        gate_up=_pad_to(gate_up[i], 0, 2 * LANES).astype(F32),
        k_k=_row(k_k[i]), k_a=_row(k_a[i]), r_k=_row(r_k[i]),
        gn_g=_row(gn_g[i]), gn_b=_row(gn_b[i]),
        w_a=w_branch_attn[i].astype(BF16), w_b=w_branch_rwkv[i].astype(BF16), w_o=w_out[i].astype(BF16),
        w_route=_pad_to(jnp.concatenate([router_exp[i], router_grp[i]], axis=1), 1, LANES).astype(F32),
        b_route=_row(_pad_to(jnp.concatenate([router_exp_bias[i], router_grp_bias[i]]), 0, LANES)),
        e_gate=exp_gate[i].astype(BF16), e_up=exp_up[i].astype(BF16), e_down=exp_down[i].astype(BF16),
        w_ple=ple_proj[i].astype(BF16), w_pg=ple_gate[i].astype(BF16),
        ln_g0=_row(ln_g[i, 0]), ln_b0=_row(ln_b[i, 0]),
        ln_g1=_row(ln_g[i, 1]), ln_b1=_row(ln_b[i, 1]),
        ln_g2=_row(ln_g[i, 2]), ln_b2=_row(ln_b[i, 2]),
    )
    if i > 0:
        lp.update(vres_base=_row(vres_base[i - 1]),
                  vres_down=_pad_to(vres_down[i - 1], 1, LANES).astype(BF16),
                  vres_up=_pad_to(vres_up[i - 1], 0, LANES).astype(BF16))
    return lp


def _const_mats(tm):
    tok = jnp.arange(tm)
    same = (tok[:, None] // CHUNK) == (tok[None, :] // CHUNK)
    tri = (same & (tok[:, None] >= tok[None, :])).astype(BF16)
    rtok = jnp.arange(RANK_BLK)
    tri_rank = (rtok[None, :] < rtok[:, None]).astype(BF16)
    return dict(tri=tri, tri_rank=tri_rank)


def kernel(x, p, ln_in_g, ln_in_b, rel_bias, w_in, tok_mix, decay_base, decay_up, aaa_base, aaa_up,
           gate_up, k_k, k_a, r_k, vres_base, vres_down, vres_up, gn_g, gn_b, w_branch_attn,
           w_branch_rwkv, w_out, router_grp, router_grp_bias, router_exp, router_exp_bias, exp_gate,
           exp_up, exp_down, ple_proj, ple_gate, ln_g, ln_b):
    batch, seq, _ = x.shape
    t = batch * seq
    consts = _const_mats(256)
    bias_tabs = _attn_bias_tables(rel_bias)
    xt = x.reshape(t, D_MODEL)
    pt = p.reshape(p.shape[0], t, p.shape[-1])
    v_first = None
    for i in range(DEPTH):
        lp = _layer_params(i, w_in, tok_mix, decay_base, decay_up, aaa_base, aaa_up, gate_up, k_k, k_a,
                           r_k, vres_base, vres_down, vres_up, gn_g, gn_b, w_branch_attn, w_branch_rwkv,
                           w_out, router_grp, router_grp_bias, router_exp, router_exp_bias, exp_gate,
                           exp_up, exp_down, ple_proj, ple_gate, ln_g, ln_b, consts)
        if i == 0:
            xt, qkv, rw, gate = _proj(xt, _row(ln_in_g), _row(ln_in_b), lp["wqkv"], lp["wrw"],
                                      lp["wgate"], apply_ln=True)
        else:
            qkv, rw, gate = _proj(xt, _row(ln_in_g), _row(ln_in_b), lp["wqkv"], lp["wrw"],
                                  lp["wgate"], apply_ln=False)
        y_attn = _attention(qkv, bias_tabs, batch, seq)
        if i == 0:
            q1, y0, m, n0, g, bonus, v_first = _rwkv_chunk(rw, lp, seq, None)
        else:
            q1, y0, m, n0, g, bonus = _rwkv_chunk(rw, lp, seq, v_first)
        y = _rwkv_scan(q1, y0, m, n0, batch, seq)
        xt = _mix(y, g, bonus, y_attn, gate, xt, lp)
        xt = _moe_sparse(xt, pt, i, lp)
    return xt.reshape(batch, seq, D_MODEL)
```

```python
import functools

import jax
import jax.numpy as jnp
from jax import lax
from jax.experimental import pallas as pl
from jax.experimental.pallas import tpu as pltpu

F32 = jnp.float32
BF16 = jnp.bfloat16

D_MODEL = 1024
CHUNK = 64
N_PREV_CHUNKS = 8
HEAD_DIM = 64
N_HEADS = 8
D_HEADS = N_HEADS * HEAD_DIM
REL_CLIP = 128
DECAY_LORA = 64
AAA_LORA = 64
GATE_LORA = 160
N_GROUPS = 4
EXPERTS_PER_GROUP = 8
N_EXPERTS = N_GROUPS * EXPERTS_PER_GROUP
D_EXPERT = 256
DEPTH = 2
DEEPNORM_ALPHA = (2 * DEPTH) ** 0.25
LN_EPS = 1e-5
GN_EPS = 64e-5
NEG_INF = -1e30

LANES = 128
PAIR = 2 * HEAD_DIM
GRP_HEADS = 2
GRP = GRP_HEADS * HEAD_DIM
SUMMARY_BATCH = 16
ATTN_QBLK = 2 * CHUNK
ATTN_WIN = (N_PREV_CHUNKS + 2) * CHUNK
ATTN_GROUP = 4
RW_COLS_PAD = 2048
VMEM_LIMIT = 56 * 1024 * 1024


def _cparams(*sem):
    return pltpu.CompilerParams(dimension_semantics=sem, vmem_limit_bytes=VMEM_LIMIT)


def _dot(a, b):
    return jnp.dot(a, b, preferred_element_type=F32)


def _dot_nt(a, b):
    return lax.dot_general(a, b, (((1,), (1,)), ((), ())), preferred_element_type=F32)


def _split2(x):
    hi = x.astype(BF16)
    lo = (x - hi.astype(F32)).astype(BF16)
    return hi, lo


def _dot3(a, b):
    ah, al = _split2(a)
    bh, bl = _split2(b)
    return _dot(ah, bh) + (_dot(al, bh) + _dot(ah, bl))


def _dot_exact_lhs(a_bf16, x):
    x1 = x.astype(BF16)
    r1 = x - x1.astype(F32)
    x2 = r1.astype(BF16)
    x3 = (r1 - x2.astype(F32)).astype(BF16)
    return _dot(a_bf16, x1) + (_dot(a_bf16, x2) + _dot(a_bf16, x3))


def _seg_sum(x):
    left = lax.broadcasted_iota(jnp.int32, (x.shape[0], PAIR), 1) < HEAD_DIM
    outs = []
    for p in range(x.shape[1] // PAIR):
        xp = x[:, p * PAIR:(p + 1) * PAIR]
        s0 = jnp.sum(jnp.where(left, xp, 0.0), axis=-1, keepdims=True)
        s1 = jnp.sum(jnp.where(left, 0.0, xp), axis=-1, keepdims=True)
        outs.append(jnp.where(left, s0, s1))
    return jnp.concatenate(outs, axis=-1)


def _layer_norm(x, g, b):
    mu = jnp.mean(x, axis=-1, keepdims=True)
    xc = x - mu
    var = jnp.mean(xc * xc, axis=-1, keepdims=True)
    return xc * lax.rsqrt(var + LN_EPS) * g + b


def _sigmoid(x):
    return 1.0 / (1.0 + jnp.exp(-x))


def _full(shape):
    nd = len(shape)
    return pl.BlockSpec(shape, lambda *_: (0,) * nd)


def _proj_kernel(x_ref, g_ref, b_ref, wqkv_ref, wrw_ref, wgate_ref, *out_refs, apply_ln):
    x = x_ref[...]
    if apply_ln:
        xn_ref, qkv_ref, rw_ref, gate_ref = out_refs
        x = _layer_norm(x, g_ref[...], b_ref[...])
        xn_ref[...] = x
    else:
        qkv_ref, rw_ref, gate_ref = out_refs
    xb = x.astype(BF16)
    qkv_ref[...] = _dot(xb, wqkv_ref[...]).astype(BF16)
    rw_ref[...] = _dot(xb, wrw_ref[...])
    gate_ref[...] = _sigmoid(_dot(xb, wgate_ref[...])).astype(BF16)


def _proj(x, g, b, wqkv, wrw, wgate, apply_ln):
    t = x.shape[0]
    tm = min(256, t)
    row = lambda w: pl.BlockSpec((tm, w), lambda i: (i, 0))
    out_shape = [jax.ShapeDtypeStruct((t, 3 * D_HEADS), BF16),
                 jax.ShapeDtypeStruct((t, RW_COLS_PAD), F32),
                 jax.ShapeDtypeStruct((t, 2 * D_MODEL), BF16)]
    out_specs = [row(3 * D_HEADS), row(RW_COLS_PAD), row(2 * D_MODEL)]
    if apply_ln:
        out_shape = [jax.ShapeDtypeStruct((t, D_MODEL), F32)] + out_shape
        out_specs = [row(D_MODEL)] + out_specs
    return pl.pallas_call(
        functools.partial(_proj_kernel, apply_ln=apply_ln),
        grid=(t // tm,),
        in_specs=[row(D_MODEL), _full(g.shape), _full(b.shape),
                  _full(wqkv.shape), _full(wrw.shape), _full(wgate.shape)],
        out_specs=out_specs,
        out_shape=out_shape,
        compiler_params=_cparams("parallel"),
        name="proj_ln" if apply_ln else "proj",
    )(x, g, b, wqkv, wrw, wgate)


def _attn_kernel(q_ref, k_ref, v_ref, bias_ref, o_ref):
    qb = pl.program_id(1)
    ws = pl.multiple_of(jnp.maximum(qb * ATTN_QBLK - N_PREV_CHUNKS * CHUNK, 0), ATTN_QBLK)
    q = q_ref[0]
    kw = k_ref[0, pl.ds(ws, ATTN_WIN), :]
    vw = v_ref[0, pl.ds(ws, ATTN_WIN), :]
    left = lax.broadcasted_iota(jnp.int32, (ATTN_QBLK, PAIR), 1) < HEAD_DIM
    zero = jnp.zeros((ATTN_QBLK, PAIR), BF16)
    heads = [(h // 2, h % 2) for h in range(N_HEADS)]
    spans = [slice(p * PAIR, (p + 1) * PAIR) for p in range(N_HEADS // 2)]
    o = []
    for g0 in range(0, N_HEADS, ATTN_GROUP):
        grp = list(enumerate(heads))[g0:g0 + ATTN_GROUP]
        qm = [jnp.where(left, q[:, spans[p]], zero) if side == 0 else jnp.where(left, zero, q[:, spans[p]])
              for _, (p, side) in grp]
        s = [_dot_nt(x, kw[:, spans[p]]) + bias_ref[0, h] for x, (h, (p, _)) in zip(qm, grp)]
        m = [jnp.max(x, axis=-1, keepdims=True) for x in s]
        e = [jnp.exp(x - mx) for x, mx in zip(s, m)]
        inv = [1.0 / jnp.sum(x, axis=-1, keepdims=True) for x in e]
        o += [_dot(x.astype(BF16), vw[:, spans[p]]) * r for x, r, (_, (p, _)) in zip(e, inv, grp)]
    pairs = [jnp.where(left, o[2 * p], o[2 * p + 1]) for p in range(N_HEADS // 2)]
    o_ref[0] = jnp.concatenate(pairs, axis=-1).astype(BF16)


def _attention(qkv, bias_tabs, batch, seq):
    qkv3 = qkv.reshape(batch, seq, 3 * D_HEADS)
    n_tabs = bias_tabs.shape[0]
    out = pl.pallas_call(
        _attn_kernel,
        grid=(batch, seq // ATTN_QBLK),
        in_specs=[
            pl.BlockSpec((1, ATTN_QBLK, D_HEADS), lambda b, i: (b, i, 0)),
            pl.BlockSpec((1, seq, D_HEADS), lambda b, i: (b, 0, 1)),
            pl.BlockSpec((1, seq, D_HEADS), lambda b, i: (b, 0, 2)),
            pl.BlockSpec((1, N_HEADS, ATTN_QBLK, ATTN_WIN),
                         lambda b, i: (jnp.minimum(i, n_tabs - 1), 0, 0, 0)),
        ],
        out_specs=pl.BlockSpec((1, ATTN_QBLK, D_HEADS), lambda b, i: (b, i, 0)),
        out_shape=jax.ShapeDtypeStruct((batch, seq, D_HEADS), BF16),
        compiler_params=_cparams("parallel", "arbitrary"),
        name="band_attn",
    )(qkv3, qkv3, qkv3, bias_tabs)
    return out.reshape(batch * seq, D_HEADS)


def _attn_bias_tables(rel_bias):
    n_tabs = N_PREV_CHUNKS * CHUNK // ATTN_QBLK + 1
    start = (jnp.arange(n_tabs) * ATTN_QBLK)[:, None, None]
    qpos = start + jnp.arange(ATTN_QBLK)[None, :, None]
    kpos = jnp.arange(ATTN_WIN)[None, None, :]
    qc, kc = qpos // CHUNK, kpos // CHUNK
    valid = (kc <= qc) & (kc >= qc - N_PREV_CHUNKS)
    n = jnp.arange(ATTN_QBLK - 1 + ATTN_WIN)
    dist = start[:, :, 0] + (ATTN_QBLK - 1) - n[None, :]
    vals = rel_bias.astype(F32)[:, jnp.clip(dist, -REL_CLIP, REL_CLIP) + REL_CLIP]
    length = ATTN_QBLK - 1 + ATTN_WIN
    tiled = jnp.tile(_pad_to(vals, 2, length + 1), (1, 1, ATTN_QBLK))
    cut = tiled[:, :, :ATTN_QBLK * length].reshape(N_HEADS, n_tabs, ATTN_QBLK, length)
    bias = cut[:, :, :, ATTN_QBLK - 1:ATTN_QBLK - 1 + ATTN_WIN]
    bias = jnp.where(valid[None], bias, NEG_INF)
    return jnp.transpose(bias, (1, 0, 2, 3))


def _rwkv_chunk_kernel(*refs, seq, tm, has_vres):
    if has_vres:
        (rw_ref, prev_ref, mu_ref, dbase_ref, dup_ref, abase_ref, aup_ref, gup_ref, kk_ref, ka_ref,
         rk_ref, tri_ref, vfirst_ref, vbase_ref, vdown_ref, vup_ref,
         q1_ref, y0_ref, m_ref, n0_ref, g_ref, bonus_ref,
         rw_s, aw_s, bi_s, ki_s, be_s, ke_s, v_s, wc_s) = refs
    else:
        (rw_ref, prev_ref, mu_ref, dbase_ref, dup_ref, abase_ref, aup_ref, gup_ref, kk_ref, ka_ref,
         rk_ref, tri_ref,
         q1_ref, y0_ref, m_ref, n0_ref, g_ref, bonus_ref, vfirst_out_ref,
         rw_s, aw_s, bi_s, ki_s, be_s, ke_s, v_s, wc_s) = refs

    cols = rw_ref[...]
    first = (pl.program_id(0) % (seq // tm)) == 0
    prev_row = jnp.where(first, 0.0, prev_ref[7:8, :])
    rows = lax.broadcasted_iota(jnp.int32, cols.shape, 0)
    prev = jnp.where(rows == 0, prev_row, pltpu.roll(cols, 1, axis=0))
    xs = cols + (prev - cols) * mu_ref[...]

    d = D_HEADS
    r, k, v = xs[:, 0:d], xs[:, d:2 * d], xs[:, 2 * d:3 * d]
    wd = xs[:, 3 * d:3 * d + LANES]
    ad = xs[:, 3 * d + LANES:3 * d + 2 * LANES]
    gd = xs[:, 3 * d + 2 * LANES:3 * d + 4 * LANES]

    u = dbase_ref[...] + _dot3(jnp.tanh(wd), dup_ref[...])
    w_log = jnp.minimum(u, 0.0) - jnp.log(1.0 + jnp.exp(-jnp.abs(u))) - 0.5
    logw = -jnp.exp(w_log)
    a_icl = _sigmoid(abase_ref[...] + _dot3(ad, aup_ref[...]))
    g_ref[...] = _dot3(_sigmoid(gd), gup_ref[...]).astype(BF16)
    kk = k * kk_ref[...]
    kk = kk * lax.rsqrt(jnp.maximum(_seg_sum(kk * kk), 1e-24))
    k2 = k * (1.0 + (a_icl - 1.0) * ka_ref[...])
    if has_vres:
        low = _dot(v.astype(BF16), vdown_ref[...])
        mix = _sigmoid(vbase_ref[...] + _dot(low.astype(BF16), vup_ref[...]))
        v2 = v + (vfirst_ref[...] - v) * mix
    else:
        v2 = v
        vfirst_out_ref[...] = v
    a_vec = -kk
    b_vec = kk * a_icl
    bonus_ref[...] = (_seg_sum(r * k2 * rk_ref[...]) * v2).astype(BF16)

    cw = _dot_exact_lhs(tri_ref[...], logw)
    bi_ = lax.broadcasted_iota(jnp.int32, (tm, tm), 0) // CHUNK
    bj_ = lax.broadcasted_iota(jnp.int32, (tm, tm), 1) // CHUNK
    cw_tot = _dot_exact_lhs(jnp.where(bi_ == bj_, 1.0, 0.0).astype(BF16), logw)
    rw_s[...] = r * jnp.exp(cw)
    aw_s[...] = a_vec * jnp.exp(cw - logw)
    inv = jnp.exp(-cw)
    bi_s[...] = b_vec * inv
    ki_s[...] = k2 * inv
    rest = jnp.exp(cw_tot - cw)
    be_s[...] = b_vec * rest
    ke_s[...] = k2 * rest
    v_s[...] = v2
    wc_s[...] = jnp.exp(cw_tot)

    shape = (CHUNK, GRP)
    ti = lax.broadcasted_iota(jnp.int32, shape, 0)
    li = lax.broadcasted_iota(jnp.int32, shape, 1)
    si = li & (HEAD_DIM - 1)
    hid = li // HEAD_DIM
    strict = ti > si
    incl = ti >= si
    eye = ti == si

    def bd(x):
        z = jnp.zeros_like(x)
        return jnp.concatenate([jnp.where(hid == h, x, z) for h in range(GRP_HEADS)], axis=0)

    def tpair(x):
        return jnp.concatenate([x[:, h * HEAD_DIM:(h + 1) * HEAD_DIM].T for h in range(GRP_HEADS)], axis=1)

    def b16(xs):
        return [x.astype(BF16) for x in xs]

    def summaries(insts):
        sl = [(slice(c * CHUNK, (c + 1) * CHUNK), slice(p * GRP, (p + 1) * GRP)) for c, p in insts]
        rw = [rw_s[s] for s in sl]
        awb = b16([aw_s[s] for s in sl])
        vb = b16([v_s[s] for s in sl])
        lhs = [jnp.concatenate([a, r.astype(BF16)], axis=0) for a, r in zip(awb, rw)]
        bik = [jnp.concatenate([bd(b), bd(k)], axis=0)
               for b, k in zip(b16([bi_s[s] for s in sl]), b16([ki_s[s] for s in sl]))]
        g = [_dot_nt(l, x) for l, x in zip(lhs, bik)]
        a_ab = [jnp.where(strict, x[:CHUNK, :GRP], 0.0) for x in g]
        a_rb = b16([jnp.where(incl, x[CHUNK:, :GRP], 0.0) for x in g])
        a_ak = b16([jnp.where(strict, x[:CHUNK, GRP:], 0.0) for x in g])
        a_rk = b16([jnp.where(incl, x[CHUNK:, GRP:], 0.0) for x in g])
        bdv = [bd(x) for x in vb]
        x0 = [_dot(a, v) for a, v in zip(a_ak, bdv)]
        tinv = [jnp.where(eye, 1.0, a) for a in a_ab]
        lpb = b16(a_ab)
        lpb = b16([_dot(l, bd(l)) for l in lpb])
        for _ in range(4):
            both = [_dot(jnp.concatenate([t.astype(BF16), l], axis=0), bd(l)) for t, l in zip(tinv, lpb)]
            tinv = [t + x[:CHUNK] for t, x in zip(tinv, both)]
            lpb = b16([x[CHUNK:] for x in both])
        tinv = [t + _dot(t.astype(BF16), bd(l)) for t, l in zip(tinv, lpb)]
        tb = b16(tinv)
        pu = b16([_dot(t, jnp.concatenate([bd(a), bd(x.astype(BF16))], axis=1))
                  for t, a, x in zip(tb, awb, x0)])
        pu_bd = [jnp.concatenate([bd(x[:, :GRP]), bd(x[:, GRP:])], axis=1) for x in pu]
        qy = [_dot(a, r) for a, r in zip(a_rb, pu_bd)]
        q1 = [r + x[:, :GRP] for r, x in zip(rw, qy)]
        y0 = [x[:, GRP:] + _dot(k, v) for x, k, v in zip(qy, a_rk, bdv)]
        bet = b16([tpair(be_s[s]) for s in sl])
        ket = b16([tpair(ke_s[s]) for s in sl])
        mn = [_dot(b, r) for b, r in zip(bet, pu_bd)]
        mm = [x[:, :GRP] for x in mn]
        nn = [x[:, GRP:] + _dot(k, v) for x, k, v in zip(mn, ket, bdv)]
        for i, (s, (c, p)) in enumerate(zip(sl, insts)):
            wc = wc_s[c * CHUNK:c * CHUNK + 1, s[1]]
            q1_ref[s] = q1[i]
            y0_ref[s] = y0[i]
            m_ref[s] = jnp.where(eye, wc, 0.0) + mm[i]
            n0_ref[s] = nn[i]

    insts = [(c, p) for c in range(tm // CHUNK) for p in range(D_HEADS // GRP)]
    for i0 in range(0, len(insts), SUMMARY_BATCH):
        summaries(insts[i0:i0 + SUMMARY_BATCH])


def _rwkv_chunk(rw, lp, seq, v_first):
    t = rw.shape[0]
    tm = 256
    has_vres = v_first is not None
    nct = tm // CHUNK
    row = lambda w: pl.BlockSpec((tm, w), lambda i: (i, 0))
    prev_spec = pl.BlockSpec((8, RW_COLS_PAD), lambda i: (jnp.maximum(i * (tm // 8) - 1, 0), 0))
    params = [lp["mu"], lp["decay_base"], lp["decay_up"], lp["aaa_base"], lp["aaa_up"], lp["gate_up"],
              lp["k_k"], lp["k_a"], lp["r_k"], lp["tri"]]
    inputs = [rw, rw] + params
    in_specs = [row(RW_COLS_PAD), prev_spec] + [_full(p.shape) for p in params]
    if has_vres:
        extra = [lp["vres_base"], lp["vres_down"], lp["vres_up"]]
        inputs += [v_first] + extra
        in_specs += [row(D_HEADS)] + [_full(p.shape) for p in extra]
    tok_shape = jax.ShapeDtypeStruct((t, D_HEADS), F32)
    half_shape = jax.ShapeDtypeStruct((t, D_HEADS), BF16)
    out_shape = [tok_shape] * 4 + [half_shape] * 2
    out_specs = [row(D_HEADS)] * 6
    if not has_vres:
        out_shape.append(tok_shape)
        out_specs.append(row(D_HEADS))
    scratch = [pltpu.VMEM((tm, D_HEADS), F32) for _ in range(8)]
    return pl.pallas_call(
        functools.partial(_rwkv_chunk_kernel, seq=seq, tm=tm, has_vres=has_vres),
        grid=(t // tm,),
        in_specs=in_specs,
        out_specs=out_specs,
        out_shape=out_shape,
        scratch_shapes=scratch,
        compiler_params=_cparams("parallel"),
        name="rwkv_chunk_vres" if has_vres else "rwkv_chunk",
    )(*inputs)


def _rwkv_scan_kernel(q1_ref, y0_ref, m_ref, n0_ref, y_ref, st_ref, *, n_chunks):
    @pl.when(pl.program_id(1) == 0)
    def _():
        st_ref[...] = jnp.zeros_like(st_ref)

    left = lax.broadcasted_iota(jnp.int32, (CHUNK, PAIR), 1) < HEAD_DIM

    def bd(x):
        z = jnp.zeros_like(x)
        return jnp.concatenate([jnp.where(left, x, z), jnp.where(left, z, x)], axis=0)

    n_pairs = D_HEADS // PAIR
    st = [st_ref[:, p * PAIR:(p + 1) * PAIR] for p in range(n_pairs)]
    for c in range(n_chunks):
        rs = slice(c * CHUNK, (c + 1) * CHUNK)
        for p in range(n_pairs):
            ls = slice(p * PAIR, (p + 1) * PAIR)
            lh, ll = _split2(jnp.concatenate([q1_ref[rs, ls], m_ref[rs, ls]], axis=0))
            sh, sl = _split2(st[p])
            bh = bd(sh)
            res = _dot(lh, bh) + (_dot(ll, bh) + _dot(lh, bd(sl)))
            y_ref[rs, ls] = res[:CHUNK] + y0_ref[rs, ls]
            st[p] = res[CHUNK:] + n0_ref[rs, ls]
    for p in range(n_pairs):
        st_ref[:, p * PAIR:(p + 1) * PAIR] = st[p]


def _rwkv_scan(q1, y0, m, n0, batch, seq):
    t = q1.shape[0]
    cb = 8
    steps = seq // (cb * CHUNK)
    tok_spec = pl.BlockSpec((cb * CHUNK, D_HEADS), lambda b, j: (b * steps + j, 0))
    return pl.pallas_call(
        functools.partial(_rwkv_scan_kernel, n_chunks=cb),
        grid=(batch, steps),
        in_specs=[tok_spec] * 4,
        out_specs=tok_spec,
        out_shape=jax.ShapeDtypeStruct((t, D_HEADS), F32),
        scratch_shapes=[pltpu.VMEM((HEAD_DIM, D_HEADS), F32)],
        compiler_params=_cparams("parallel", "arbitrary"),
        name="rwkv_scan",
    )(q1, y0, m, n0)


def _mix_kernel(y_ref, g_ref, bonus_ref, ya_ref, gate_ref, x_ref, wa_ref, wb_ref, wo_ref,
                gng_ref, gnb_ref, lng_ref, lnb_ref, o_ref):
    y = y_ref[...]
    mean = _seg_sum(y) * (1.0 / HEAD_DIM)
    yc = y - mean
    var = _seg_sum(yc * yc) * (1.0 / HEAD_DIM)
    yn = yc * lax.rsqrt(var + GN_EPS) * gng_ref[...] + gnb_ref[...]
    yr = (yn + bonus_ref[...].astype(F32)) * g_ref[...].astype(F32)
    za = _dot(ya_ref[...], wa_ref[...])
    zb = _dot(yr.astype(BF16), wb_ref[...])
    gate = gate_ref[...]
    mixed_in = gate[:, :D_MODEL].astype(F32) * za + gate[:, D_MODEL:].astype(F32) * zb
    mixed = _dot(mixed_in.astype(BF16), wo_ref[...])
    o_ref[...] = _layer_norm(DEEPNORM_ALPHA * x_ref[...] + mixed, lng_ref[...], lnb_ref[...])


def _mix(y, g, bonus, y_attn, gate, x, lp):
    t = x.shape[0]
    tm = min(256, t)
    row = lambda w: pl.BlockSpec((tm, w), lambda i: (i, 0))
    params = [lp["w_a"], lp["w_b"], lp["w_o"], lp["gn_g"], lp["gn_b"], lp["ln_g0"], lp["ln_b0"]]
    return pl.pallas_call(
        _mix_kernel,
        grid=(t // tm,),
        in_specs=[row(D_HEADS), row(D_HEADS), row(D_HEADS), row(D_HEADS), row(2 * D_MODEL), row(D_MODEL)]
        + [_full(p.shape) for p in params],
        out_specs=row(D_MODEL),
        out_shape=jax.ShapeDtypeStruct((t, D_MODEL), F32),
        compiler_params=_cparams("parallel"),
        name="branch_mix",
    )(y, g, bonus, y_attn, gate, x, *params)


MOE_SLAB = 2048
MOE_WIN = 128
MOE_EBLK = 4
MOE_ROWS = 2 * MOE_SLAB + MOE_WIN
SUBLANES = 8
RANK_BLK = 256
ROUTE_SLOTS = 2


def _route_kernel(x_ref, wr_ref, br_ref, tri_ref, w_ref, pos_ref, seg_ref):
    x = x_ref[...]
    tn = x.shape[0]
    lane = lax.broadcasted_iota(jnp.int32, (tn, LANES), 1)
    lanef = lane.astype(F32)
    hp = lax.Precision.HIGHEST
    elog = _dot3(x, wr_ref[...]) + br_ref[...]
    big = float(LANES)
    glog = jnp.where((lane >= N_EXPERTS) & (lane < N_EXPERTS + N_GROUPS), elog, -jnp.inf)
    gmax = jnp.max(glog, axis=-1, keepdims=True)
    g_gate = 1.0 / jnp.sum(jnp.exp(glog - gmax), axis=-1, keepdims=True)
    grp = jnp.min(jnp.where(glog == gmax, lanef, big), axis=-1, keepdims=True) - N_EXPERTS
    in_grp = (lanef >= grp * EXPERTS_PER_GROUP) & (lanef < (grp + 1.0) * EXPERTS_PER_GROUP)
    sel = jnp.where(in_grp, elog, -jnp.inf)
    v1 = jnp.max(sel, axis=-1, keepdims=True)
    i1 = jnp.min(jnp.where(sel == v1, lanef, big), axis=-1, keepdims=True)
    sel2 = jnp.where(lanef == i1, -jnp.inf, sel)
    v2 = jnp.max(sel2, axis=-1, keepdims=True)
    i2 = jnp.min(jnp.where(sel2 == v2, lanef, big), axis=-1, keepdims=True)
    e2 = jnp.exp(v2 - v1)
    w1 = g_gate / (1.0 + e2)
    w2 = g_gate * e2 / (1.0 + e2)
    pick = jnp.where(lax.broadcasted_iota(jnp.int32, (8, LANES), 0) == lax.broadcasted_iota(jnp.int32, (8, LANES), 1),
                     1.0, 0.0)

    def token_rows(c0, c1):
        cols = jnp.where(lane == 0, c0, jnp.where(lane == 1, c1, 0.0))
        rows = lax.dot_general(pick, cols, (((1,), (1,)), ((), ())), precision=hp, preferred_element_type=F32)
        return jnp.concatenate([rows[0:1], rows[1:2]], axis=1)

    w_ref[0] = token_rows(w1, w2)

    hit1 = lanef == i1
    hit2 = lanef == i2
    onehot = jnp.where(hit1 | hit2, 1.0, 0.0)
    tri = tri_ref[...]
    carry = jnp.zeros((1, LANES), F32)
    ranks = []
    for j in range(tn // RANK_BLK):
        blk = onehot[j * RANK_BLK:(j + 1) * RANK_BLK]
        ranks.append(_dot(tri, blk.astype(BF16)) + carry)
        carry = carry + jnp.sum(blk, axis=0, keepdims=True)
    rank = jnp.concatenate(ranks, axis=0)
    cnt = carry
    li = lax.broadcasted_iota(jnp.int32, (LANES, LANES), 0)
    lj = lax.broadcasted_iota(jnp.int32, (LANES, LANES), 1)
    before = jnp.where(li < lj, 1.0, 0.0)
    off = jnp.dot(jnp.broadcast_to(cnt, (8, LANES)), before, precision=hp,
                  preferred_element_type=F32)[0:1]
    slot = (off + rank) * SUBLANES
    pos1 = jnp.sum(jnp.where(hit1, slot, 0.0), axis=-1, keepdims=True)
    pos2 = jnp.sum(jnp.where(hit2, slot, 0.0), axis=-1, keepdims=True)
    pos_ref[0] = token_rows(pos1, pos2).astype(jnp.int32)
    row8 = lax.broadcasted_iota(jnp.int32, (8, LANES), 0)
    seg_ref[0] = jnp.where(row8 == 0, off, jnp.where(row8 == 1, cnt, 0.0)).astype(jnp.int32)


def _route(x, lp):
    t = x.shape[0]
    tn = min(MOE_SLAB, t)
    n_slabs = t // tn
    small = [lp["w_route"], lp["b_route"], lp["tri_rank"]]
    row = lambda w: pl.BlockSpec((tn, w), lambda i: (i, 0))
    return pl.pallas_call(
        _route_kernel,
        grid=(n_slabs,),
        in_specs=[row(D_MODEL)] + [_full(p.shape) for p in small],
        out_specs=[pl.BlockSpec((1, 1, ROUTE_SLOTS * tn), lambda i: (i, 0, 0)),
                   pl.BlockSpec((1, 1, ROUTE_SLOTS * tn), lambda i: (i, 0, 0)),
                   pl.BlockSpec((1, 8, LANES), lambda i: (i, 0, 0))],
        out_shape=[jax.ShapeDtypeStruct((n_slabs, 1, ROUTE_SLOTS * tn), F32),
                   jax.ShapeDtypeStruct((n_slabs, 1, ROUTE_SLOTS * tn), jnp.int32),
                   jax.ShapeDtypeStruct((n_slabs, 8, LANES), jnp.int32)],
        compiler_params=_cparams("parallel"),
        name="moe_route",
    )(x, *small)


def _moe_sparse_kernel(off_ref, cnt_ref, pos_ref, w_ref, x_ref, eg_ref, eu_ref, ed_ref, lng_ref, lnb_ref,
                       p_ref, wp_ref, wg_ref, lng2_ref, lnb2_ref, o_ref, xs_ref):
    s = pl.program_id(0)
    e = pl.program_id(1)
    tn = x_ref.shape[0]

    n_feat = D_MODEL // LANES

    def tile_rows(first_row):
        return pl.ds(pl.multiple_of(first_row, SUBLANES), SUBLANES)

    @pl.when((s == 0) & (e == 0))
    def _():
        xs_ref[pl.ds(2 * tn * SUBLANES, MOE_WIN * SUBLANES), :] = jnp.zeros((MOE_WIN * SUBLANES, LANES), F32)

    @pl.when(e == 0)
    def _():
        def scatter(i, carry):
            t0 = pl.multiple_of(i * SUBLANES, SUBLANES)
            tiles = x_ref[pl.ds(t0, SUBLANES), :].reshape(SUBLANES, n_feat, LANES)
            for j in range(SUBLANES):
                xs_ref[tile_rows(pos_ref[0, 0, t0 + j]), :] = tiles[j]
                xs_ref[tile_rows(pos_ref[0, 0, tn + t0 + j]), :] = tiles[j]
            return carry
        lax.fori_loop(0, tn // SUBLANES, scatter, 0)

    ridx = lax.broadcasted_iota(jnp.int32, (MOE_WIN, 1), 0)

    base = s * N_EXPERTS + e * MOE_EBLK
    offs = [off_ref[base + j] for j in range(MOE_EBLK)]
    cnts = [cnt_ref[base + j] for j in range(MOE_EBLK)]

    def feat_rows(j, w, f):
        return pl.ds((offs[j] + w * MOE_WIN) * SUBLANES + f, MOE_WIN, stride=SUBLANES)

    def load_win(j, w):
        return jnp.concatenate([xs_ref[feat_rows(j, w, f), :] for f in range(n_feat)], axis=1)

    def hidden(j, xw):
        xb = xw.astype(BF16)
        hg = _dot(xb, eg_ref[j])
        hu = _dot(xb, eu_ref[j])
        return (hg * _sigmoid(hg) * hu).astype(BF16)

    def project(j, w, xw, hh):
        yw = _dot(hh, ed_ref[j])
        return jnp.where(ridx + w * MOE_WIN < cnts[j], yw, xw)

    def expert(j, w, xw):
        return project(j, w, xw, hidden(j, xw))

    def store_win(j, w, yw):
        for f in range(n_feat):
            xs_ref[feat_rows(j, w, f), :] = yw[:, f * LANES:(f + 1) * LANES]

    xw = [None] * MOE_EBLK
    hh = [None] * MOE_EBLK
    for j in range(MOE_EBLK + 1):
        if j < MOE_EBLK:
            xw[j] = load_win(j, 0)
            hh[j] = hidden(j, xw[j])
        if j > 0:
            store_win(j - 1, 0, project(j - 1, 0, xw[j - 1], hh[j - 1]))

    for j in range(MOE_EBLK):
        def window(w, carry, j=j):
            store_win(j, w, expert(j, w, load_win(j, w)))
            return carry
        lax.fori_loop(1, (cnts[j] + MOE_WIN - 1) // MOE_WIN, window, 0)

    @pl.when(e == N_EXPERTS // MOE_EBLK - 1)
    def _():
        def combine(i, carry):
            t0 = pl.multiple_of(i * SUBLANES, SUBLANES)
            tiles = []
            for j in range(SUBLANES):
                y1 = xs_ref[tile_rows(pos_ref[0, 0, t0 + j]), :]
                y2 = xs_ref[tile_rows(pos_ref[0, 0, tn + t0 + j]), :]
                tiles.append(w_ref[0, 0, t0 + j] * y1 + w_ref[0, 0, tn + t0 + j] * y2)
            o_ref[pl.ds(t0, SUBLANES), :] = jnp.stack(tiles, axis=0).reshape(SUBLANES, D_MODEL)
            return carry
        lax.fori_loop(0, tn // SUBLANES, combine, 0)
        def finish(j, carry):
            rs = pl.ds(pl.multiple_of(j * RANK_BLK, RANK_BLK), RANK_BLK)
            x2 = _layer_norm(DEEPNORM_ALPHA * x_ref[rs, :] + o_ref[rs, :], lng_ref[...], lnb_ref[...])
            emb = _dot(p_ref[rs, :].astype(BF16), wp_ref[...])
            gate = _sigmoid(_dot(x2.astype(BF16), wg_ref[...]))
            o_ref[rs, :] = _layer_norm(DEEPNORM_ALPHA * x2 + emb * gate, lng2_ref[...], lnb2_ref[...])
            return carry
        lax.fori_loop(0, tn // RANK_BLK, finish, 0)


def _moe_sparse(x, p, layer, lp):
    t = x.shape[0]
    tn = min(MOE_SLAB, t)
    n_slabs = t // tn
    w, pos, seg = _route(x, lp)
    off = seg[:, 0, :N_EXPERTS].reshape(-1)
    cnt = seg[:, 1, :N_EXPERTS].reshape(-1)
    smem_row = pl.BlockSpec((1, 1, ROUTE_SLOTS * tn), lambda s, e, *_: (s, 0, 0), memory_space=pltpu.SMEM)
    slab = pl.BlockSpec((tn, D_MODEL), lambda s, e, *_: (s, 0), pipeline_mode=pl.Buffered(1))
    const = lambda a: pl.BlockSpec(a.shape, lambda s, e, *_: (0,) * a.ndim, pipeline_mode=pl.Buffered(1))
    p_slab = pl.BlockSpec((None, tn, p.shape[2]), lambda s, e, *_: (layer, s, 0), pipeline_mode=pl.Buffered(1))
    tail = [lp["w_ple"], lp["w_pg"], lp["ln_g2"], lp["ln_b2"]]
    grid_spec = pltpu.PrefetchScalarGridSpec(
        num_scalar_prefetch=2,
        grid=(n_slabs, N_EXPERTS // MOE_EBLK),
        in_specs=[smem_row, smem_row, slab,
                  pl.BlockSpec((MOE_EBLK, D_MODEL, D_EXPERT), lambda s, e, *_: (e, 0, 0)),
                  pl.BlockSpec((MOE_EBLK, D_MODEL, D_EXPERT), lambda s, e, *_: (e, 0, 0)),
                  pl.BlockSpec((MOE_EBLK, D_EXPERT, D_MODEL), lambda s, e, *_: (e, 0, 0)),
                  const(lp["ln_g1"]), const(lp["ln_b1"]), p_slab] + [const(a) for a in tail],
        out_specs=slab,
        scratch_shapes=[pltpu.VMEM((MOE_ROWS * SUBLANES, LANES), F32)],
    )
    return pl.pallas_call(
        _moe_sparse_kernel,
        grid_spec=grid_spec,
        out_shape=jax.ShapeDtypeStruct((t, D_MODEL), F32),
        compiler_params=_cparams("arbitrary", "arbitrary"),
        name="moe_sparse",
    )(off, cnt, pos, w, x, lp["e_gate"], lp["e_up"], lp["e_down"], lp["ln_g1"], lp["ln_b1"],
      p, *tail)


def _pad_to(a, axis, size):
    pad = [(0, 0)] * a.ndim
    pad[axis] = (0, size - a.shape[axis])
    return jnp.pad(a, pad)


def _row(v):
    return v.reshape(1, -1).astype(F32)


def _layer_params(i, w_in, tok_mix, decay_base, decay_up, aaa_base, aaa_up, gate_up, k_k, k_a, r_k,
                  vres_base, vres_down, vres_up, gn_g, gn_b, w_branch_attn, w_branch_rwkv, w_out,
                  router_grp, router_grp_bias, router_exp, router_exp_bias, exp_gate, exp_up, exp_down,
                  ple_proj, ple_gate, ln_g, ln_b, consts):
    d = D_HEADS
    w = w_in[i]
    rw0 = 3 * d
    gate0 = rw0 + 3 * d + DECAY_LORA + AAA_LORA + GATE_LORA
    wqkv = jnp.concatenate([w[:, 0:d] * (HEAD_DIM ** -0.5), w[:, d:rw0]], axis=1).astype(BF16)
    wr = w[:, rw0:gate0]
    lora0 = 3 * d
    pieces = [wr[:, :lora0],
              _pad_to(wr[:, lora0:lora0 + DECAY_LORA], 1, LANES),
              _pad_to(wr[:, lora0 + DECAY_LORA:lora0 + DECAY_LORA + AAA_LORA], 1, LANES),
              _pad_to(wr[:, lora0 + DECAY_LORA + AAA_LORA:], 1, 2 * LANES)]
    wrw = jnp.concatenate(pieces, axis=1).astype(BF16)
    mu = tok_mix[i]
    mu_pieces = [mu[:lora0],
                 _pad_to(mu[lora0:lora0 + DECAY_LORA], 0, LANES),
                 _pad_to(mu[lora0 + DECAY_LORA:lora0 + DECAY_LORA + AAA_LORA], 0, LANES),
                 _pad_to(mu[lora0 + DECAY_LORA + AAA_LORA:], 0, 2 * LANES)]
    lp = dict(consts)
    lp.update(
        wqkv=wqkv, wrw=wrw, wgate=w[:, gate0:].astype(BF16),
        mu=_row(jnp.concatenate(mu_pieces)),
        decay_base=_row(decay_base[i]), decay_up=_pad_to(decay_up[i], 0, LANES).astype(F32),
        aaa_base=_row(aaa_base[i]), aaa_up=_pad_to(aaa_up[i], 0, LANES).astype(F32),
        gate_up=_pad_to(gate_up[i], 0, 2 * LANES).astype(F32),
        k_k=_row(k_k[i]), k_a=_row(k_a[i]), r_k=_row(r_k[i]),
        gn_g=_row(gn_g[i]), gn_b=_row(gn_b[i]),
        w_a=w_branch_attn[i].astype(BF16), w_b=w_branch_rwkv[i].astype(BF16), w_o=w_out[i].astype(BF16),
        w_route=_pad_to(jnp.concatenate([router_exp[i], router_grp[i]], axis=1), 1, LANES).astype(F32),
        b_route=_row(_pad_to(jnp.concatenate([router_exp_bias[i], router_grp_bias[i]]), 0, LANES)),
        e_gate=exp_gate[i].astype(BF16), e_up=exp_up[i].astype(BF16), e_down=exp_down[i].astype(BF16),
        w_ple=ple_proj[i].astype(BF16), w_pg=ple_gate[i].astype(BF16),
        ln_g0=_row(ln_g[i, 0]), ln_b0=_row(ln_b[i, 0]),
        ln_g1=_row(ln_g[i, 1]), ln_b1=_row(ln_b[i, 1]),
        ln_g2=_row(ln_g[i, 2]), ln_b2=_row(ln_b[i, 2]),
    )
    if i > 0:
        lp.update(vres_base=_row(vres_base[i - 1]),
                  vres_down=_pad_to(vres_down[i - 1], 1, LANES).astype(BF16),
                  vres_up=_pad_to(vres_up[i - 1], 0, LANES).astype(BF16))
    return lp


def _const_mats(tm):
    tok = jnp.arange(tm)
    same = (tok[:, None] // CHUNK) == (tok[None, :] // CHUNK)
    tri = (same & (tok[:, None] >= tok[None, :])).astype(BF16)
    rtok = jnp.arange(RANK_BLK)
    tri_rank = (rtok[None, :] < rtok[:, None]).astype(BF16)
    return dict(tri=tri, tri_rank=tri_rank)


def kernel(x, p, ln_in_g, ln_in_b, rel_bias, w_in, tok_mix, decay_base, decay_up, aaa_base, aaa_up,
           gate_up, k_k, k_a, r_k, vres_base, vres_down, vres_up, gn_g, gn_b, w_branch_attn,
           w_branch_rwkv, w_out, router_grp, router_grp_bias, router_exp, router_exp_bias, exp_gate,
           exp_up, exp_down, ple_proj, ple_gate, ln_g, ln_b):
    batch, seq, _ = x.shape
    t = batch * seq
    consts = _const_mats(256)
    bias_tabs = _attn_bias_tables(rel_bias)
    xt = x.reshape(t, D_MODEL)
    pt = p.reshape(p.shape[0], t, p.shape[-1])
    v_first = None
    for i in range(DEPTH):
        lp = _layer_params(i, w_in, tok_mix, decay_base, decay_up, aaa_base, aaa_up, gate_up, k_k, k_a,
                           r_k, vres_base, vres_down, vres_up, gn_g, gn_b, w_branch_attn, w_branch_rwkv,
                           w_out, router_grp, router_grp_bias, router_exp, router_exp_bias, exp_gate,
                           exp_up, exp_down, ple_proj, ple_gate, ln_g, ln_b, consts)
        if i == 0:
            xt, qkv, rw, gate = _proj(xt, _row(ln_in_g), _row(ln_in_b), lp["wqkv"], lp["wrw"],
                                      lp["wgate"], apply_ln=True)
        else:
            qkv, rw, gate = _proj(xt, _row(ln_in_g), _row(ln_in_b), lp["wqkv"], lp["wrw"],
                                  lp["wgate"], apply_ln=False)
        y_attn = _attention(qkv, bias_tabs, batch, seq)
        if i == 0:
            q1, y0, m, n0, g, bonus, v_first = _rwkv_chunk(rw, lp, seq, None)
        else:
            q1, y0, m, n0, g, bonus = _rwkv_chunk(rw, lp, seq, v_first)
        y = _rwkv_scan(q1, y0, m, n0, batch, seq)
        xt = _mix(y, g, bonus, y_attn, gate, xt, lp)
        xt = _moe_sparse(xt, pt, i, lp)
    return xt.reshape(batch, seq, D_MODEL)
```

```python
import functools

import jax
import jax.numpy as jnp
from jax import lax
from jax.experimental import pallas as pl
from jax.experimental.pallas import tpu as pltpu

F32 = jnp.float32
BF16 = jnp.bfloat16

D_MODEL = 1024
CHUNK = 64
N_PREV_CHUNKS = 8
HEAD_DIM = 64
N_HEADS = 8
D_HEADS = N_HEADS * HEAD_DIM
REL_CLIP = 128
DECAY_LORA = 64
AAA_LORA = 64
GATE_LORA = 160
N_GROUPS = 4
EXPERTS_PER_GROUP = 8
N_EXPERTS = N_GROUPS * EXPERTS_PER_GROUP
D_EXPERT = 256
DEPTH = 2
DEEPNORM_ALPHA = (2 * DEPTH) ** 0.25
LN_EPS = 1e-5
GN_EPS = 64e-5
NEG_INF = -1e30

LANES = 128
PAIR = 2 * HEAD_DIM
GRP_HEADS = 2
GRP = GRP_HEADS * HEAD_DIM
SUMMARY_BATCH = 16
ATTN_QBLK = 2 * CHUNK
ATTN_WIN = (N_PREV_CHUNKS + 2) * CHUNK
ATTN_GROUP = 4
RW_COLS_PAD = 2048
VMEM_LIMIT = 56 * 1024 * 1024


def _cparams(*sem):
    return pltpu.CompilerParams(dimension_semantics=sem, vmem_limit_bytes=VMEM_LIMIT)


def _dot(a, b):
    return jnp.dot(a, b, preferred_element_type=F32)


def _dot_nt(a, b):
    return lax.dot_general(a, b, (((1,), (1,)), ((), ())), preferred_element_type=F32)


def _split2(x):
    hi = x.astype(BF16)
    lo = (x - hi.astype(F32)).astype(BF16)
    return hi, lo


def _dot3(a, b):
    ah, al = _split2(a)
    bh, bl = _split2(b)
    return _dot(ah, bh) + (_dot(al, bh) + _dot(ah, bl))


def _dot_exact_lhs(a_bf16, x):
    x1 = x.astype(BF16)
    r1 = x - x1.astype(F32)
    x2 = r1.astype(BF16)
    x3 = (r1 - x2.astype(F32)).astype(BF16)
    return _dot(a_bf16, x1) + (_dot(a_bf16, x2) + _dot(a_bf16, x3))


def _seg_sum(x):
    left = lax.broadcasted_iota(jnp.int32, (x.shape[0], PAIR), 1) < HEAD_DIM
    outs = []
    for p in range(x.shape[1] // PAIR):
        xp = x[:, p * PAIR:(p + 1) * PAIR]
        s0 = jnp.sum(jnp.where(left, xp, 0.0), axis=-1, keepdims=True)
        s1 = jnp.sum(jnp.where(left, 0.0, xp), axis=-1, keepdims=True)
        outs.append(jnp.where(left, s0, s1))
    return jnp.concatenate(outs, axis=-1)


def _layer_norm(x, g, b):
    mu = jnp.mean(x, axis=-1, keepdims=True)
    xc = x - mu
    var = jnp.mean(xc * xc, axis=-1, keepdims=True)
    return xc * lax.rsqrt(var + LN_EPS) * g + b


def _sigmoid(x):
    return 1.0 / (1.0 + jnp.exp(-x))


def _full(shape):
    nd = len(shape)
    return pl.BlockSpec(shape, lambda *_: (0,) * nd)


def _proj_kernel(x_ref, g_ref, b_ref, wqkv_ref, wrw_ref, wgate_ref, *out_refs, apply_ln):
    x = x_ref[...]
    if apply_ln:
        xn_ref, qkv_ref, rw_ref, gate_ref = out_refs
        x = _layer_norm(x, g_ref[...], b_ref[...])
        xn_ref[...] = x
    else:
        qkv_ref, rw_ref, gate_ref = out_refs
    xb = x.astype(BF16)
    qkv_ref[...] = _dot(xb, wqkv_ref[...]).astype(BF16)
    rw_ref[...] = _dot(xb, wrw_ref[...])
    gate_ref[...] = _sigmoid(_dot(xb, wgate_ref[...])).astype(BF16)


def _proj(x, g, b, wqkv, wrw, wgate, apply_ln):
    t = x.shape[0]
    tm = min(256, t)
    row = lambda w: pl.BlockSpec((tm, w), lambda i: (i, 0))
    out_shape = [jax.ShapeDtypeStruct((t, 3 * D_HEADS), BF16),
                 jax.ShapeDtypeStruct((t, RW_COLS_PAD), F32),
                 jax.ShapeDtypeStruct((t, 2 * D_MODEL), BF16)]
    out_specs = [row(3 * D_HEADS), row(RW_COLS_PAD), row(2 * D_MODEL)]
    if apply_ln:
        out_shape = [jax.ShapeDtypeStruct((t, D_MODEL), F32)] + out_shape
        out_specs = [row(D_MODEL)] + out_specs
    return pl.pallas_call(
        functools.partial(_proj_kernel, apply_ln=apply_ln),
        grid=(t // tm,),
        in_specs=[row(D_MODEL), _full(g.shape), _full(b.shape),
                  _full(wqkv.shape), _full(wrw.shape), _full(wgate.shape)],
        out_specs=out_specs,
        out_shape=out_shape,
        compiler_params=_cparams("parallel"),
        name="proj_ln" if apply_ln else "proj",
    )(x, g, b, wqkv, wrw, wgate)


def _attn_kernel(q_ref, k_ref, v_ref, bias_ref, o_ref):
    qb = pl.program_id(1)
    ws = pl.multiple_of(jnp.maximum(qb * ATTN_QBLK - N_PREV_CHUNKS * CHUNK, 0), ATTN_QBLK)
    q = q_ref[0]
    kw = k_ref[0, pl.ds(ws, ATTN_WIN), :]
    vw = v_ref[0, pl.ds(ws, ATTN_WIN), :]
    left = lax.broadcasted_iota(jnp.int32, (ATTN_QBLK, PAIR), 1) < HEAD_DIM
    zero = jnp.zeros((ATTN_QBLK, PAIR), BF16)
    heads = [(h // 2, h % 2) for h in range(N_HEADS)]
    spans = [slice(p * PAIR, (p + 1) * PAIR) for p in range(N_HEADS // 2)]
    o = []
    for g0 in range(0, N_HEADS, ATTN_GROUP):
        grp = list(enumerate(heads))[g0:g0 + ATTN_GROUP]
        qm = [jnp.where(left, q[:, spans[p]], zero) if side == 0 else jnp.where(left, zero, q[:, spans[p]])
              for _, (p, side) in grp]
        s = [_dot_nt(x, kw[:, spans[p]]) + bias_ref[0, h] for x, (h, (p, _)) in zip(qm, grp)]
        m = [jnp.max(x, axis=-1, keepdims=True) for x in s]
        e = [jnp.exp(x - mx) for x, mx in zip(s, m)]
        inv = [1.0 / jnp.sum(x, axis=-1, keepdims=True) for x in e]
        o += [_dot(x.astype(BF16), vw[:, spans[p]]) * r for x, r, (_, (p, _)) in zip(e, inv, grp)]
    pairs = [jnp.where(left, o[2 * p], o[2 * p + 1]) for p in range(N_HEADS // 2)]
    o_ref[0] = jnp.concatenate(pairs, axis=-1).astype(BF16)


def _attention(qkv, bias_tabs, batch, seq):
    qkv3 = qkv.reshape(batch, seq, 3 * D_HEADS)
    n_tabs = bias_tabs.shape[0]
    out = pl.pallas_call(
        _attn_kernel,
        grid=(batch, seq // ATTN_QBLK),
        in_specs=[
            pl.BlockSpec((1, ATTN_QBLK, D_HEADS), lambda b, i: (b, i, 0)),
            pl.BlockSpec((1, seq, D_HEADS), lambda b, i: (b, 0, 1)),
            pl.BlockSpec((1, seq, D_HEADS), lambda b, i: (b, 0, 2)),
            pl.BlockSpec((1, N_HEADS, ATTN_QBLK, ATTN_WIN),
                         lambda b, i: (jnp.minimum(i, n_tabs - 1), 0, 0, 0)),
        ],
        out_specs=pl.BlockSpec((1, ATTN_QBLK, D_HEADS), lambda b, i: (b, i, 0)),
        out_shape=jax.ShapeDtypeStruct((batch, seq, D_HEADS), BF16),
        compiler_params=_cparams("parallel", "arbitrary"),
        name="band_attn",
    )(qkv3, qkv3, qkv3, bias_tabs)
    return out.reshape(batch * seq, D_HEADS)


def _attn_bias_tables(rel_bias):
    n_tabs = N_PREV_CHUNKS * CHUNK // ATTN_QBLK + 1
    start = (jnp.arange(n_tabs) * ATTN_QBLK)[:, None, None]
    qpos = start + jnp.arange(ATTN_QBLK)[None, :, None]
    kpos = jnp.arange(ATTN_WIN)[None, None, :]
    qc, kc = qpos // CHUNK, kpos // CHUNK
    valid = (kc <= qc) & (kc >= qc - N_PREV_CHUNKS)
    n = jnp.arange(ATTN_QBLK - 1 + ATTN_WIN)
    dist = start[:, :, 0] + (ATTN_QBLK - 1) - n[None, :]
    vals = rel_bias.astype(F32)[:, jnp.clip(dist, -REL_CLIP, REL_CLIP) + REL_CLIP]
    length = ATTN_QBLK - 1 + ATTN_WIN
    tiled = jnp.tile(_pad_to(vals, 2, length + 1), (1, 1, ATTN_QBLK))
    cut = tiled[:, :, :ATTN_QBLK * length].reshape(N_HEADS, n_tabs, ATTN_QBLK, length)
    bias = cut[:, :, :, ATTN_QBLK - 1:ATTN_QBLK - 1 + ATTN_WIN]
    bias = jnp.where(valid[None], bias, NEG_INF)
    return jnp.transpose(bias, (1, 0, 2, 3))


def _rwkv_chunk_kernel(*refs, seq, tm, has_vres):
    if has_vres:
        (rw_ref, prev_ref, mu_ref, dbase_ref, dup_ref, abase_ref, aup_ref, gup_ref, kk_ref, ka_ref,
         rk_ref, tri_ref, vfirst_ref, vbase_ref, vdown_ref, vup_ref,
         q1_ref, y0_ref, m_ref, n0_ref, g_ref, bonus_ref,
         rw_s, aw_s, bi_s, ki_s, be_s, ke_s, v_s, wc_s) = refs
    else:
        (rw_ref, prev_ref, mu_ref, dbase_ref, dup_ref, abase_ref, aup_ref, gup_ref, kk_ref, ka_ref,
         rk_ref, tri_ref,
         q1_ref, y0_ref, m_ref, n0_ref, g_ref, bonus_ref, vfirst_out_ref,
         rw_s, aw_s, bi_s, ki_s, be_s, ke_s, v_s, wc_s) = refs

    cols = rw_ref[...]
    first = (pl.program_id(0) % (seq // tm)) == 0
    prev_row = jnp.where(first, 0.0, prev_ref[7:8, :])
    rows = lax.broadcasted_iota(jnp.int32, cols.shape, 0)
    prev = jnp.where(rows == 0, prev_row, pltpu.roll(cols, 1, axis=0))
    xs = cols + (prev - cols) * mu_ref[...]

    d = D_HEADS
    r, k, v = xs[:, 0:d], xs[:, d:2 * d], xs[:, 2 * d:3 * d]
    wd = xs[:, 3 * d:3 * d + LANES]
    ad = xs[:, 3 * d + LANES:3 * d + 2 * LANES]
    gd = xs[:, 3 * d + 2 * LANES:3 * d + 4 * LANES]

    u = dbase_ref[...] + _dot3(jnp.tanh(wd), dup_ref[...])
    w_log = jnp.minimum(u, 0.0) - jnp.log(1.0 + jnp.exp(-jnp.abs(u))) - 0.5
    logw = -jnp.exp(w_log)
    a_icl = _sigmoid(abase_ref[...] + _dot3(ad, aup_ref[...]))
    g_ref[...] = _dot3(_sigmoid(gd), gup_ref[...]).astype(BF16)
    kk = k * kk_ref[...]
    kk = kk * lax.rsqrt(jnp.maximum(_seg_sum(kk * kk), 1e-24))
    k2 = k * (1.0 + (a_icl - 1.0) * ka_ref[...])
    if has_vres:
        low = _dot(v.astype(BF16), vdown_ref[...])
        mix = _sigmoid(vbase_ref[...] + _dot(low.astype(BF16), vup_ref[...]))
        v2 = v + (vfirst_ref[...] - v) * mix
    else:
        v2 = v
        vfirst_out_ref[...] = v
    a_vec = -kk
    b_vec = kk * a_icl
    bonus_ref[...] = (_seg_sum(r * k2 * rk_ref[...]) * v2).astype(BF16)

    cw = _dot_exact_lhs(tri_ref[...], logw)
    bi_ = lax.broadcasted_iota(jnp.int32, (tm, tm), 0) // CHUNK
    bj_ = lax.broadcasted_iota(jnp.int32, (tm, tm), 1) // CHUNK
    cw_tot = _dot_exact_lhs(jnp.where(bi_ == bj_, 1.0, 0.0).astype(BF16), logw)
    rw_s[...] = r * jnp.exp(cw)
    aw_s[...] = a_vec * jnp.exp(cw - logw)
    inv = jnp.exp(-cw)
    bi_s[...] = b_vec * inv
    ki_s[...] = k2 * inv
    rest = jnp.exp(cw_tot - cw)
    be_s[...] = b_vec * rest
    ke_s[...] = k2 * rest
    v_s[...] = v2
    wc_s[...] = jnp.exp(cw_tot)

    shape = (CHUNK, GRP)
    ti = lax.broadcasted_iota(jnp.int32, shape, 0)
    li = lax.broadcasted_iota(jnp.int32, shape, 1)
    si = li & (HEAD_DIM - 1)
    hid = li // HEAD_DIM
    strict = ti > si
    incl = ti >= si
    eye = ti == si

    def bd(x):
        z = jnp.zeros_like(x)
        return jnp.concatenate([jnp.where(hid == h, x, z) for h in range(GRP_HEADS)], axis=0)

    def tpair(x):
        return jnp.concatenate([x[:, h * HEAD_DIM:(h + 1) * HEAD_DIM].T for h in range(GRP_HEADS)], axis=1)

    def b16(xs):
        return [x.astype(BF16) for x in xs]

    def summaries(insts):
        sl = [(slice(c * CHUNK, (c + 1) * CHUNK), slice(p * GRP, (p + 1) * GRP)) for c, p in insts]
        rw = [rw_s[s] for s in sl]
        awb = b16([aw_s[s] for s in sl])
        vb = b16([v_s[s] for s in sl])
        lhs = [jnp.concatenate([a, r.astype(BF16)], axis=0) for a, r in zip(awb, rw)]
        bik = [jnp.concatenate([bd(b), bd(k)], axis=0)
               for b, k in zip(b16([bi_s[s] for s in sl]), b16([ki_s[s] for s in sl]))]
        g = [_dot_nt(l, x) for l, x in zip(lhs, bik)]
        a_ab = [jnp.where(strict, x[:CHUNK, :GRP], 0.0) for x in g]
        a_rb = b16([jnp.where(incl, x[CHUNK:, :GRP], 0.0) for x in g])
        a_ak = b16([jnp.where(strict, x[:CHUNK, GRP:], 0.0) for x in g])
        a_rk = b16([jnp.where(incl, x[CHUNK:, GRP:], 0.0) for x in g])
        bdv = [bd(x) for x in vb]
        x0 = [_dot(a, v) for a, v in zip(a_ak, bdv)]
        tinv = [jnp.where(eye, 1.0, a) for a in a_ab]
        lpb = b16(a_ab)
        lpb = b16([_dot(l, bd(l)) for l in lpb])
        for _ in range(4):
            both = [_dot(jnp.concatenate([t.astype(BF16), l], axis=0), bd(l)) for t, l in zip(tinv, lpb)]
            tinv = [t + x[:CHUNK] for t, x in zip(tinv, both)]
            lpb = b16([x[CHUNK:] for x in both])
        tinv = [t + _dot(t.astype(BF16), bd(l)) for t, l in zip(tinv, lpb)]
        tb = b16(tinv)
        pu = b16([_dot(t, jnp.concatenate([bd(a), bd(x.astype(BF16))], axis=1))
                  for t, a, x in zip(tb, awb, x0)])
        pu_bd = [jnp.concatenate([bd(x[:, :GRP]), bd(x[:, GRP:])], axis=1) for x in pu]
        qy = [_dot(a, r) for a, r in zip(a_rb, pu_bd)]
        q1 = [r + x[:, :GRP] for r, x in zip(rw, qy)]
        y0 = [x[:, GRP:] + _dot(k, v) for x, k, v in zip(qy, a_rk, bdv)]
        bet = b16([tpair(be_s[s]) for s in sl])
        ket = b16([tpair(ke_s[s]) for s in sl])
        mn = [_dot(b, r) for b, r in zip(bet, pu_bd)]
        mm = [x[:, :GRP] for x in mn]
        nn = [x[:, GRP:] + _dot(k, v) for x, k, v in zip(mn, ket, bdv)]
        for i, (s, (c, p)) in enumerate(zip(sl, insts)):
            wc = wc_s[c * CHUNK:c * CHUNK + 1, s[1]]
            q1_ref[s] = q1[i]
            y0_ref[s] = y0[i]
            m_ref[s] = jnp.where(eye, wc, 0.0) + mm[i]
            n0_ref[s] = nn[i]

    insts = [(c, p) for c in range(tm // CHUNK) for p in range(D_HEADS // GRP)]
    for i0 in range(0, len(insts), SUMMARY_BATCH):
        summaries(insts[i0:i0 + SUMMARY_BATCH])


def _rwkv_chunk(rw, lp, seq, v_first):
    t = rw.shape[0]
    tm = 256
    has_vres = v_first is not None
    nct = tm // CHUNK
    row = lambda w: pl.BlockSpec((tm, w), lambda i: (i, 0))
    prev_spec = pl.BlockSpec((8, RW_COLS_PAD), lambda i: (jnp.maximum(i * (tm // 8) - 1, 0), 0))
    params = [lp["mu"], lp["decay_base"], lp["decay_up"], lp["aaa_base"], lp["aaa_up"], lp["gate_up"],
              lp["k_k"], lp["k_a"], lp["r_k"], lp["tri"]]
    inputs = [rw, rw] + params
    in_specs = [row(RW_COLS_PAD), prev_spec] + [_full(p.shape) for p in params]
    if has_vres:
        extra = [lp["vres_base"], lp["vres_down"], lp["vres_up"]]
        inputs += [v_first] + extra
        in_specs += [row(D_HEADS)] + [_full(p.shape) for p in extra]
    tok_shape = jax.ShapeDtypeStruct((t, D_HEADS), F32)
    half_shape = jax.ShapeDtypeStruct((t, D_HEADS), BF16)
    out_shape = [tok_shape] * 4 + [half_shape] * 2
    out_specs = [row(D_HEADS)] * 6
    if not has_vres:
        out_shape.append(tok_shape)
        out_specs.append(row(D_HEADS))
    scratch = [pltpu.VMEM((tm, D_HEADS), F32) for _ in range(8)]
    return pl.pallas_call(
        functools.partial(_rwkv_chunk_kernel, seq=seq, tm=tm, has_vres=has_vres),
        grid=(t // tm,),
        in_specs=in_specs,
        out_specs=out_specs,
        out_shape=out_shape,
        scratch_shapes=scratch,
        compiler_params=_cparams("parallel"),
        name="rwkv_chunk_vres" if has_vres else "rwkv_chunk",
    )(*inputs)


def _rwkv_scan_kernel(q1_ref, y0_ref, m_ref, n0_ref, y_ref, st_ref, *, n_chunks):
    @pl.when(pl.program_id(1) == 0)
    def _():
        st_ref[...] = jnp.zeros_like(st_ref)

    left = lax.broadcasted_iota(jnp.int32, (CHUNK, PAIR), 1) < HEAD_DIM

    def bd(x):
        z = jnp.zeros_like(x)
        return jnp.concatenate([jnp.where(left, x, z), jnp.where(left, z, x)], axis=0)

    n_pairs = D_HEADS // PAIR
    st = [st_ref[:, p * PAIR:(p + 1) * PAIR] for p in range(n_pairs)]
    for c in range(n_chunks):
        rs = slice(c * CHUNK, (c + 1) * CHUNK)
        for p in range(n_pairs):
            ls = slice(p * PAIR, (p + 1) * PAIR)
            lh, ll = _split2(jnp.concatenate([q1_ref[rs, ls], m_ref[rs, ls]], axis=0))
            sh, sl = _split2(st[p])
            bh = bd(sh)
            res = _dot(lh, bh) + (_dot(ll, bh) + _dot(lh, bd(sl)))
            y_ref[rs, ls] = res[:CHUNK] + y0_ref[rs, ls]
            st[p] = res[CHUNK:] + n0_ref[rs, ls]
    for p in range(n_pairs):
        st_ref[:, p * PAIR:(p + 1) * PAIR] = st[p]


def _rwkv_scan(q1, y0, m, n0, batch, seq):
    t = q1.shape[0]
    cb = 8
    steps = seq // (cb * CHUNK)
    tok_spec = pl.BlockSpec((cb * CHUNK, D_HEADS), lambda b, j: (b * steps + j, 0))
    return pl.pallas_call(
        functools.partial(_rwkv_scan_kernel, n_chunks=cb),
        grid=(batch, steps),
        in_specs=[tok_spec] * 4,
        out_specs=tok_spec,
        out_shape=jax.ShapeDtypeStruct((t, D_HEADS), F32),
        scratch_shapes=[pltpu.VMEM((HEAD_DIM, D_HEADS), F32)],
        compiler_params=_cparams("parallel", "arbitrary"),
        name="rwkv_scan",
    )(q1, y0, m, n0)


def _mix_kernel(y_ref, g_ref, bonus_ref, ya_ref, gate_ref, x_ref, wa_ref, wb_ref, wo_ref,
                gng_ref, gnb_ref, lng_ref, lnb_ref, o_ref):
    y = y_ref[...]
    mean = _seg_sum(y) * (1.0 / HEAD_DIM)
    yc = y - mean
    var = _seg_sum(yc * yc) * (1.0 / HEAD_DIM)
    yn = yc * lax.rsqrt(var + GN_EPS) * gng_ref[...] + gnb_ref[...]
    yr = (yn + bonus_ref[...].astype(F32)) * g_ref[...].astype(F32)
    za = _dot(ya_ref[...], wa_ref[...])
    zb = _dot(yr.astype(BF16), wb_ref[...])
    gate = gate_ref[...]
    mixed_in = gate[:, :D_MODEL].astype(F32) * za + gate[:, D_MODEL:].astype(F32) * zb
    mixed = _dot(mixed_in.astype(BF16), wo_ref[...])
    o_ref[...] = _layer_norm(DEEPNORM_ALPHA * x_ref[...] + mixed, lng_ref[...], lnb_ref[...])


def _mix(y, g, bonus, y_attn, gate, x, lp):
    t = x.shape[0]
    tm = min(256, t)
    row = lambda w: pl.BlockSpec((tm, w), lambda i: (i, 0))
    params = [lp["w_a"], lp["w_b"], lp["w_o"], lp["gn_g"], lp["gn_b"], lp["ln_g0"], lp["ln_b0"]]
    return pl.pallas_call(
        _mix_kernel,
        grid=(t // tm,),
        in_specs=[row(D_HEADS), row(D_HEADS), row(D_HEADS), row(D_HEADS), row(2 * D_MODEL), row(D_MODEL)]
        + [_full(p.shape) for p in params],
        out_specs=row(D_MODEL),
        out_shape=jax.ShapeDtypeStruct((t, D_MODEL), F32),
        compiler_params=_cparams("parallel"),
        name="branch_mix",
    )(y, g, bonus, y_attn, gate, x, *params)


MOE_SLAB = 2048
MOE_WIN = 160
MOE_EBLK = 4
MOE_ROWS = 2 * MOE_SLAB + MOE_WIN
SUBLANES = 8
RANK_BLK = 256
ROUTE_SLOTS = 2


def _route_kernel(x_ref, wr_ref, br_ref, tri_ref, w_ref, pos_ref, seg_ref):
    x = x_ref[...]
    tn = x.shape[0]
    lane = lax.broadcasted_iota(jnp.int32, (tn, LANES), 1)
    lanef = lane.astype(F32)
    hp = lax.Precision.HIGHEST
    elog = _dot3(x, wr_ref[...]) + br_ref[...]
    big = float(LANES)
    glog = jnp.where((lane >= N_EXPERTS) & (lane < N_EXPERTS + N_GROUPS), elog, -jnp.inf)
    gmax = jnp.max(glog, axis=-1, keepdims=True)
    g_gate = 1.0 / jnp.sum(jnp.exp(glog - gmax), axis=-1, keepdims=True)
    grp = jnp.min(jnp.where(glog == gmax, lanef, big), axis=-1, keepdims=True) - N_EXPERTS
    in_grp = (lanef >= grp * EXPERTS_PER_GROUP) & (lanef < (grp + 1.0) * EXPERTS_PER_GROUP)
    sel = jnp.where(in_grp, elog, -jnp.inf)
    v1 = jnp.max(sel, axis=-1, keepdims=True)
    i1 = jnp.min(jnp.where(sel == v1, lanef, big), axis=-1, keepdims=True)
    sel2 = jnp.where(lanef == i1, -jnp.inf, sel)
    v2 = jnp.max(sel2, axis=-1, keepdims=True)
    i2 = jnp.min(jnp.where(sel2 == v2, lanef, big), axis=-1, keepdims=True)
    e2 = jnp.exp(v2 - v1)
    w1 = g_gate / (1.0 + e2)
    w2 = g_gate * e2 / (1.0 + e2)
    pick = jnp.where(lax.broadcasted_iota(jnp.int32, (8, LANES), 0) == lax.broadcasted_iota(jnp.int32, (8, LANES), 1),
                     1.0, 0.0)

    def token_rows(c0, c1):
        cols = jnp.where(lane == 0, c0, jnp.where(lane == 1, c1, 0.0))
        rows = lax.dot_general(pick, cols, (((1,), (1,)), ((), ())), precision=hp, preferred_element_type=F32)
        return jnp.concatenate([rows[0:1], rows[1:2]], axis=1)

    w_ref[0] = token_rows(w1, w2)

    hit1 = lanef == i1
    hit2 = lanef == i2
    onehot = jnp.where(hit1 | hit2, 1.0, 0.0)
    tri = tri_ref[...]
    carry = jnp.zeros((1, LANES), F32)
    ranks = []
    for j in range(tn // RANK_BLK):
        blk = onehot[j * RANK_BLK:(j + 1) * RANK_BLK]
        ranks.append(_dot(tri, blk.astype(BF16)) + carry)
        carry = carry + jnp.sum(blk, axis=0, keepdims=True)
    rank = jnp.concatenate(ranks, axis=0)
    cnt = carry
    li = lax.broadcasted_iota(jnp.int32, (LANES, LANES), 0)
    lj = lax.broadcasted_iota(jnp.int32, (LANES, LANES), 1)
    before = jnp.where(li < lj, 1.0, 0.0)
    off = jnp.dot(jnp.broadcast_to(cnt, (8, LANES)), before, precision=hp,
                  preferred_element_type=F32)[0:1]
    slot = (off + rank) * SUBLANES
    pos1 = jnp.sum(jnp.where(hit1, slot, 0.0), axis=-1, keepdims=True)
    pos2 = jnp.sum(jnp.where(hit2, slot, 0.0), axis=-1, keepdims=True)
    pos_ref[0] = token_rows(pos1, pos2).astype(jnp.int32)
    row8 = lax.broadcasted_iota(jnp.int32, (8, LANES), 0)
    seg_ref[0] = jnp.where(row8 == 0, off, jnp.where(row8 == 1, cnt, 0.0)).astype(jnp.int32)


def _route(x, lp):
    t = x.shape[0]
    tn = min(MOE_SLAB, t)
    n_slabs = t // tn
    small = [lp["w_route"], lp["b_route"], lp["tri_rank"]]
    row = lambda w: pl.BlockSpec((tn, w), lambda i: (i, 0))
    return pl.pallas_call(
        _route_kernel,
        grid=(n_slabs,),
        in_specs=[row(D_MODEL)] + [_full(p.shape) for p in small],
        out_specs=[pl.BlockSpec((1, 1, ROUTE_SLOTS * tn), lambda i: (i, 0, 0)),
                   pl.BlockSpec((1, 1, ROUTE_SLOTS * tn), lambda i: (i, 0, 0)),
                   pl.BlockSpec((1, 8, LANES), lambda i: (i, 0, 0))],
        out_shape=[jax.ShapeDtypeStruct((n_slabs, 1, ROUTE_SLOTS * tn), F32),
                   jax.ShapeDtypeStruct((n_slabs, 1, ROUTE_SLOTS * tn), jnp.int32),
                   jax.ShapeDtypeStruct((n_slabs, 8, LANES), jnp.int32)],
        compiler_params=_cparams("parallel"),
        name="moe_route",
    )(x, *small)


def _moe_sparse_kernel(off_ref, cnt_ref, pos_ref, w_ref, x_ref, eg_ref, eu_ref, ed_ref, lng_ref, lnb_ref,
                       p_ref, wp_ref, wg_ref, lng2_ref, lnb2_ref, o_ref, xs_ref):
    s = pl.program_id(0)
    e = pl.program_id(1)
    tn = x_ref.shape[0]

    n_feat = D_MODEL // LANES

    def tile_rows(first_row):
        return pl.ds(pl.multiple_of(first_row, SUBLANES), SUBLANES)

    @pl.when((s == 0) & (e == 0))
    def _():
        xs_ref[pl.ds(2 * tn * SUBLANES, MOE_WIN * SUBLANES), :] = jnp.zeros((MOE_WIN * SUBLANES, LANES), F32)

    @pl.when(e == 0)
    def _():
        def scatter(i, carry):
            t0 = pl.multiple_of(i * SUBLANES, SUBLANES)
            tiles = x_ref[pl.ds(t0, SUBLANES), :].reshape(SUBLANES, n_feat, LANES)
            for j in range(SUBLANES):
                xs_ref[tile_rows(pos_ref[0, 0, t0 + j]), :] = tiles[j]
                xs_ref[tile_rows(pos_ref[0, 0, tn + t0 + j]), :] = tiles[j]
            return carry
        lax.fori_loop(0, tn // SUBLANES, scatter, 0)

    ridx = lax.broadcasted_iota(jnp.int32, (MOE_WIN, 1), 0)

    base = s * N_EXPERTS + e * MOE_EBLK
    offs = [off_ref[base + j] for j in range(MOE_EBLK)]
    cnts = [cnt_ref[base + j] for j in range(MOE_EBLK)]

    def feat_rows(j, w, f):
        return pl.ds((offs[j] + w * MOE_WIN) * SUBLANES + f, MOE_WIN, stride=SUBLANES)

    def load_win(j, w):
        return jnp.concatenate([xs_ref[feat_rows(j, w, f), :] for f in range(n_feat)], axis=1)

    def hidden(j, xw):
        xb = xw.astype(BF16)
        hg = _dot(xb, eg_ref[j])
        hu = _dot(xb, eu_ref[j])
        return (hg * _sigmoid(hg) * hu).astype(BF16)

    def project(j, w, xw, hh):
        yw = _dot(hh, ed_ref[j])
        return jnp.where(ridx + w * MOE_WIN < cnts[j], yw, xw)

    def expert(j, w, xw):
        return project(j, w, xw, hidden(j, xw))

    def store_win(j, w, yw):
        for f in range(n_feat):
            xs_ref[feat_rows(j, w, f), :] = yw[:, f * LANES:(f + 1) * LANES]

    xw = [None] * MOE_EBLK
    hh = [None] * MOE_EBLK
    for j in range(MOE_EBLK + 1):
        if j < MOE_EBLK:
            xw[j] = load_win(j, 0)
            hh[j] = hidden(j, xw[j])
        if j > 0:
            store_win(j - 1, 0, project(j - 1, 0, xw[j - 1], hh[j - 1]))

    for j in range(MOE_EBLK):
        def window(w, carry, j=j):
            store_win(j, w, expert(j, w, load_win(j, w)))
            return carry
        lax.fori_loop(1, (cnts[j] + MOE_WIN - 1) // MOE_WIN, window, 0)

    @pl.when(e == N_EXPERTS // MOE_EBLK - 1)
    def _():
        def combine(i, carry):
            t0 = pl.multiple_of(i * SUBLANES, SUBLANES)
            tiles = []
            for j in range(SUBLANES):
                y1 = xs_ref[tile_rows(pos_ref[0, 0, t0 + j]), :]
                y2 = xs_ref[tile_rows(pos_ref[0, 0, tn + t0 + j]), :]
                tiles.append(w_ref[0, 0, t0 + j] * y1 + w_ref[0, 0, tn + t0 + j] * y2)
            o_ref[pl.ds(t0, SUBLANES), :] = jnp.stack(tiles, axis=0).reshape(SUBLANES, D_MODEL)
            return carry
        lax.fori_loop(0, tn // SUBLANES, combine, 0)
        def finish(j, carry):
            rs = pl.ds(pl.multiple_of(j * RANK_BLK, RANK_BLK), RANK_BLK)
            x2 = _layer_norm(DEEPNORM_ALPHA * x_ref[rs, :] + o_ref[rs, :], lng_ref[...], lnb_ref[...])
            emb = _dot(p_ref[rs, :].astype(BF16), wp_ref[...])
            gate = _sigmoid(_dot(x2.astype(BF16), wg_ref[...]))
            o_ref[rs, :] = _layer_norm(DEEPNORM_ALPHA * x2 + emb * gate, lng2_ref[...], lnb2_ref[...])
            return carry
        lax.fori_loop(0, tn // RANK_BLK, finish, 0)


def _moe_sparse(x, p, layer, lp):
    t = x.shape[0]
    tn = min(MOE_SLAB, t)
    n_slabs = t // tn
    w, pos, seg = _route(x, lp)
    off = seg[:, 0, :N_EXPERTS].reshape(-1)
    cnt = seg[:, 1, :N_EXPERTS].reshape(-1)
    smem_row = pl.BlockSpec((1, 1, ROUTE_SLOTS * tn), lambda s, e, *_: (s, 0, 0), memory_space=pltpu.SMEM)
    slab = pl.BlockSpec((tn, D_MODEL), lambda s, e, *_: (s, 0), pipeline_mode=pl.Buffered(1))
    const = lambda a: pl.BlockSpec(a.shape, lambda s, e, *_: (0,) * a.ndim, pipeline_mode=pl.Buffered(1))
    p_slab = pl.BlockSpec((None, tn, p.shape[2]), lambda s, e, *_: (layer, s, 0), pipeline_mode=pl.Buffered(1))
    tail = [lp["w_ple"], lp["w_pg"], lp["ln_g2"], lp["ln_b2"]]
    grid_spec = pltpu.PrefetchScalarGridSpec(
        num_scalar_prefetch=2,
        grid=(n_slabs, N_EXPERTS // MOE_EBLK),
        in_specs=[smem_row, smem_row, slab,
                  pl.BlockSpec((MOE_EBLK, D_MODEL, D_EXPERT), lambda s, e, *_: (e, 0, 0)),
                  pl.BlockSpec((MOE_EBLK, D_MODEL, D_EXPERT), lambda s, e, *_: (e, 0, 0)),
                  pl.BlockSpec((MOE_EBLK, D_EXPERT, D_MODEL), lambda s, e, *_: (e, 0, 0)),
                  const(lp["ln_g1"]), const(lp["ln_b1"]), p_slab] + [const(a) for a in tail],
        out_specs=slab,
        scratch_shapes=[pltpu.VMEM((MOE_ROWS * SUBLANES, LANES), F32)],
    )
    return pl.pallas_call(
        _moe_sparse_kernel,
        grid_spec=grid_spec,
        out_shape=jax.ShapeDtypeStruct((t, D_MODEL), F32),
        compiler_params=_cparams("arbitrary", "arbitrary"),
        name="moe_sparse",
    )(off, cnt, pos, w, x, lp["e_gate"], lp["e_up"], lp["e_down"], lp["ln_g1"], lp["ln_b1"],
      p, *tail)


def _pad_to(a, axis, size):
    pad = [(0, 0)] * a.ndim
    pad[axis] = (0, size - a.shape[axis])
    return jnp.pad(a, pad)


def _row(v):
    return v.reshape(1, -1).astype(F32)


def _layer_params(i, w_in, tok_mix, decay_base, decay_up, aaa_base, aaa_up, gate_up, k_k, k_a, r_k,
                  vres_base, vres_down, vres_up, gn_g, gn_b, w_branch_attn, w_branch_rwkv, w_out,
                  router_grp, router_grp_bias, router_exp, router_exp_bias, exp_gate, exp_up, exp_down,
                  ple_proj, ple_gate, ln_g, ln_b, consts):
    d = D_HEADS
    w = w_in[i]
    rw0 = 3 * d
    gate0 = rw0 + 3 * d + DECAY_LORA + AAA_LORA + GATE_LORA
    wqkv = jnp.concatenate([w[:, 0:d] * (HEAD_DIM ** -0.5), w[:, d:rw0]], axis=1).astype(BF16)
    wr = w[:, rw0:gate0]
    lora0 = 3 * d
    pieces = [wr[:, :lora0],
              _pad_to(wr[:, lora0:lora0 + DECAY_LORA], 1, LANES),
              _pad_to(wr[:, lora0 + DECAY_LORA:lora0 + DECAY_LORA + AAA_LORA], 1, LANES),
              _pad_to(wr[:, lora0 + DECAY_LORA + AAA_LORA:], 1, 2 * LANES)]
    wrw = jnp.concatenate(pieces, axis=1).astype(BF16)
    mu = tok_mix[i]
    mu_pieces = [mu[:lora0],
                 _pad_to(mu[lora0:lora0 + DECAY_LORA], 0, LANES),
                 _pad_to(mu[lora0 + DECAY_LORA:lora0 + DECAY_LORA + AAA_LORA], 0, LANES),
                 _pad_to(mu[lora0 + DECAY_LORA + AAA_LORA:], 0, 2 * LANES)]
    lp = dict(consts)
    lp.update(
        wqkv=wqkv, wrw=wrw, wgate=w[:, gate0:].astype(BF16),
        mu=_row(jnp.concatenate(mu_pieces)),
        decay_base=_row(decay_base[i]), decay_up=_pad_to(decay_up[i], 0, LANES).astype(F32),
        aaa_base=_row(aaa_base[i]), aaa_up=_pad_to(aaa_up[i], 0, LANES).astype(F32),
        gate_up=_pad_to(gate_up[i], 0, 2 * LANES).astype(F32),
        k_k=_row(k_k[i]), k_a=_row(k_a[i]), r_k=_row(r_k[i]),
        gn_g=_row(gn_g[i]), gn_b=_row(gn_b[i]),
        w_a=w_branch_attn[i].astype(BF16), w_b=w_branch_rwkv[i].astype(BF16), w_o=w_out[i].astype(BF16),
        w_route=_pad_to(jnp.concatenate([router_exp[i], router_grp[i]], axis=1), 1, LANES).astype(F32),
        b_route=_row(_pad_to(jnp.concatenate([router_exp_bias[i], router_grp_bias[i]]), 0, LANES)),
        e_gate=exp_gate[i].astype(BF16), e_up=exp_up[i].astype(BF16), e_down=exp_down[i].astype(BF16),
        w_ple=ple_proj[i].astype(BF16), w_pg=ple_gate[i].astype(BF16),
        ln_g0=_row(ln_g[i, 0]), ln_b0=_row(ln_b[i, 0]),
        ln_g1=_row(ln_g[i, 1]), ln_b1=_row(ln_b[i, 1]),
        ln_g2=_row(ln_g[i, 2]), ln_b2=_row(ln_b[i, 2]),
    )
    if i > 0:
        lp.update(vres_base=_row(vres_base[i - 1]),
                  vres_down=_pad_to(vres_down[i - 1], 1, LANES).astype(BF16),
                  vres_up=_pad_to(vres_up[i - 1], 0, LANES).astype(BF16))
    return lp


def _const_mats(tm):
    tok = jnp.arange(tm)
    same = (tok[:, None] // CHUNK) == (tok[None, :] // CHUNK)
    tri = (same & (tok[:, None] >= tok[None, :])).astype(BF16)
    rtok = jnp.arange(RANK_BLK)
    tri_rank = (rtok[None, :] < rtok[:, None]).astype(BF16)
    return dict(tri=tri, tri_rank=tri_rank)


def kernel(x, p, ln_in_g, ln_in_b, rel_bias, w_in, tok_mix, decay_base, decay_up, aaa_base, aaa_up,
           gate_up, k_k, k_a, r_k, vres_base, vres_down, vres_up, gn_g, gn_b, w_branch_attn,
           w_branch_rwkv, w_out, router_grp, router_grp_bias, router_exp, router_exp_bias, exp_gate,
           exp_up, exp_down, ple_proj, ple_gate, ln_g, ln_b):
    batch, seq, _ = x.shape
    t = batch * seq
    consts = _const_mats(256)
    bias_tabs = _attn_bias_tables(rel_bias)
    xt = x.reshape(t, D_MODEL)
    pt = p.reshape(p.shape[0], t, p.shape[-1])
    v_first = None
    for i in range(DEPTH):
        lp = _layer_params(i, w_in, tok_mix, decay_base, decay_up, aaa_base, aaa_up, gate_up, k_k, k_a,
                           r_k, vres_base, vres_down, vres_up, gn_g, gn_b, w_branch_attn, w_branch_rwkv,
                           w_out, router_grp, router_grp_bias, router_exp, router_exp_bias, exp_gate,
                           exp_up, exp_down, ple_proj, ple_gate, ln_g, ln_b, consts)
        if i == 0:
            xt, qkv, rw, gate = _proj(xt, _row(ln_in_g), _row(ln_in_b), lp["wqkv"], lp["wrw"],
                                      lp["wgate"], apply_ln=True)
        else:
            qkv, rw, gate = _proj(xt, _row(ln_in_g), _row(ln_in_b), lp["wqkv"], lp["wrw"],
                                  lp["wgate"], apply_ln=False)
        y_attn = _attention(qkv, bias_tabs, batch, seq)
        if i == 0:
            q1, y0, m, n0, g, bonus, v_first = _rwkv_chunk(rw, lp, seq, None)
        else:
            q1, y0, m, n0, g, bonus = _rwkv_chunk(rw, lp, seq, v_first)
        y = _rwkv_scan(q1, y0, m, n0, batch, seq)
        xt = _mix(y, g, bonus, y_attn, gate, xt, lp)
        xt = _moe_sparse(xt, pt, i, lp)
    return xt.reshape(batch, seq, D_MODEL)
```

```python
import functools

import jax
import jax.numpy as jnp
from jax import lax
from jax.experimental import pallas as pl
from jax.experimental.pallas import tpu as pltpu

F32 = jnp.float32
BF16 = jnp.bfloat16

D_MODEL = 1024
CHUNK = 64
N_PREV_CHUNKS = 8
HEAD_DIM = 64
N_HEADS = 8
D_HEADS = N_HEADS * HEAD_DIM
REL_CLIP = 128
DECAY_LORA = 64
AAA_LORA = 64
GATE_LORA = 160
N_GROUPS = 4
EXPERTS_PER_GROUP = 8
N_EXPERTS = N_GROUPS * EXPERTS_PER_GROUP
D_EXPERT = 256
DEPTH = 2
DEEPNORM_ALPHA = (2 * DEPTH) ** 0.25
LN_EPS = 1e-5
GN_EPS = 64e-5
NEG_INF = -1e30

LANES = 128
SUBLANES = 8
PAIR = 2 * HEAD_DIM
GRP_HEADS = 2
GRP = GRP_HEADS * HEAD_DIM
SUMMARY_BATCH = 16
ATTN_QBLK = 2 * CHUNK
ATTN_WIN = (N_PREV_CHUNKS + 2) * CHUNK
ATTN_GROUP = 4
RW_COLS_PAD = 2048
VMEM_LIMIT = 56 * 1024 * 1024


def _cparams(*sem):
    return pltpu.CompilerParams(dimension_semantics=sem, vmem_limit_bytes=VMEM_LIMIT)


def _dot(a, b):
    return jnp.dot(a, b, preferred_element_type=F32)


def _dot_nt(a, b):
    return lax.dot_general(a, b, (((1,), (1,)), ((), ())), preferred_element_type=F32)


def _split2(x):
    hi = x.astype(BF16)
    lo = (x - hi.astype(F32)).astype(BF16)
    return hi, lo


def _dot3(a, b):
    ah, al = _split2(a)
    bh, bl = _split2(b)
    return _dot(ah, bh) + (_dot(al, bh) + _dot(ah, bl))


def _dot_exact_lhs(a_bf16, x):
    x1 = x.astype(BF16)
    r1 = x - x1.astype(F32)
    x2 = r1.astype(BF16)
    x3 = (r1 - x2.astype(F32)).astype(BF16)
    return _dot(a_bf16, x1) + (_dot(a_bf16, x2) + _dot(a_bf16, x3))


def _seg_sum(x):
    left = lax.broadcasted_iota(jnp.int32, (x.shape[0], PAIR), 1) < HEAD_DIM
    outs = []
    for p in range(x.shape[1] // PAIR):
        xp = x[:, p * PAIR:(p + 1) * PAIR]
        s0 = jnp.sum(jnp.where(left, xp, 0.0), axis=-1, keepdims=True)
        s1 = jnp.sum(jnp.where(left, 0.0, xp), axis=-1, keepdims=True)
        outs.append(jnp.where(left, s0, s1))
    return jnp.concatenate(outs, axis=-1)


def _layer_norm(x, g, b):
    mu = jnp.mean(x, axis=-1, keepdims=True)
    xc = x - mu
    var = jnp.mean(xc * xc, axis=-1, keepdims=True)
    return xc * lax.rsqrt(var + LN_EPS) * g + b


def _sigmoid(x):
    return 1.0 / (1.0 + jnp.exp(-x))


def _full(shape):
    nd = len(shape)
    return pl.BlockSpec(shape, lambda *_: (0,) * nd)


def _proj_kernel(x_ref, g_ref, b_ref, wqkv_ref, wrw_ref, wgate_ref, *out_refs, apply_ln):
    x = x_ref[...]
    if apply_ln:
        xn_ref, qkv_ref, rw_ref, gate_ref = out_refs
        x = _layer_norm(x, g_ref[...], b_ref[...])
        xn_ref[...] = x
    else:
        qkv_ref, rw_ref, gate_ref = out_refs
    xb = x.astype(BF16)
    qkv_ref[...] = _dot(xb, wqkv_ref[...]).astype(BF16)
    rw_ref[...] = _dot(xb, wrw_ref[...])
    gate_ref[...] = _sigmoid(_dot(xb, wgate_ref[...])).astype(BF16)


def _proj(x, g, b, wqkv, wrw, wgate, apply_ln):
    t = x.shape[0]
    tm = min(256, t)
    row = lambda w: pl.BlockSpec((tm, w), lambda i: (i, 0))
    out_shape = [jax.ShapeDtypeStruct((t, 3 * D_HEADS), BF16),
                 jax.ShapeDtypeStruct((t, RW_COLS_PAD), F32),
                 jax.ShapeDtypeStruct((t, 2 * D_MODEL), BF16)]
    out_specs = [row(3 * D_HEADS), row(RW_COLS_PAD), row(2 * D_MODEL)]
    if apply_ln:
        out_shape = [jax.ShapeDtypeStruct((t, D_MODEL), F32)] + out_shape
        out_specs = [row(D_MODEL)] + out_specs
    return pl.pallas_call(
        functools.partial(_proj_kernel, apply_ln=apply_ln),
        grid=(t // tm,),
        in_specs=[row(D_MODEL), _full(g.shape), _full(b.shape),
                  _full(wqkv.shape), _full(wrw.shape), _full(wgate.shape)],
        out_specs=out_specs,
        out_shape=out_shape,
        compiler_params=_cparams("parallel"),
        name="proj_ln" if apply_ln else "proj",
    )(x, g, b, wqkv, wrw, wgate)


def _attn_kernel(q_ref, k_ref, v_ref, bias_ref, o_ref):
    qb = pl.program_id(1)
    ws = pl.multiple_of(jnp.maximum(qb * ATTN_QBLK - N_PREV_CHUNKS * CHUNK, 0), ATTN_QBLK)
    q = q_ref[0]
    kw = k_ref[0, pl.ds(ws, ATTN_WIN), :]
    vw = v_ref[0, pl.ds(ws, ATTN_WIN), :]
    left = lax.broadcasted_iota(jnp.int32, (ATTN_QBLK, PAIR), 1) < HEAD_DIM
    zero = jnp.zeros((ATTN_QBLK, PAIR), BF16)
    heads = [(h // 2, h % 2) for h in range(N_HEADS)]
    spans = [slice(p * PAIR, (p + 1) * PAIR) for p in range(N_HEADS // 2)]
    o = []
    for g0 in range(0, N_HEADS, ATTN_GROUP):
        grp = list(enumerate(heads))[g0:g0 + ATTN_GROUP]
        qm = [jnp.where(left, q[:, spans[p]], zero) if side == 0 else jnp.where(left, zero, q[:, spans[p]])
              for _, (p, side) in grp]
        s = [_dot_nt(x, kw[:, spans[p]]) + bias_ref[0, h] for x, (h, (p, _)) in zip(qm, grp)]
        m = [jnp.max(x, axis=-1, keepdims=True) for x in s]
        e = [jnp.exp(x - mx) for x, mx in zip(s, m)]
        inv = [1.0 / jnp.sum(x, axis=-1, keepdims=True) for x in e]
        o += [_dot(x.astype(BF16), vw[:, spans[p]]) * r for x, r, (_, (p, _)) in zip(e, inv, grp)]
    pairs = [jnp.where(left, o[2 * p], o[2 * p + 1]) for p in range(N_HEADS // 2)]
    o_ref[0] = jnp.concatenate(pairs, axis=-1).astype(BF16)


def _attention(qkv, bias_tabs, batch, seq):
    qkv3 = qkv.reshape(batch, seq, 3 * D_HEADS)
    n_tabs = bias_tabs.shape[0]
    out = pl.pallas_call(
        _attn_kernel,
        grid=(batch, seq // ATTN_QBLK),
        in_specs=[
            pl.BlockSpec((1, ATTN_QBLK, D_HEADS), lambda b, i: (b, i, 0)),
            pl.BlockSpec((1, seq, D_HEADS), lambda b, i: (b, 0, 1)),
            pl.BlockSpec((1, seq, D_HEADS), lambda b, i: (b, 0, 2)),
            pl.BlockSpec((1, N_HEADS, ATTN_QBLK, ATTN_WIN),
                         lambda b, i: (jnp.minimum(i, n_tabs - 1), 0, 0, 0)),
        ],
        out_specs=pl.BlockSpec((1, ATTN_QBLK, D_HEADS), lambda b, i: (b, i, 0)),
        out_shape=jax.ShapeDtypeStruct((batch, seq, D_HEADS), BF16),
        compiler_params=_cparams("parallel", "arbitrary"),
        name="band_attn",
    )(qkv3, qkv3, qkv3, bias_tabs)
    return out.reshape(batch * seq, D_HEADS)


def _attn_bias_tables(rel_bias):
    n_tabs = N_PREV_CHUNKS * CHUNK // ATTN_QBLK + 1
    start = (jnp.arange(n_tabs) * ATTN_QBLK)[:, None, None]
    qpos = start + jnp.arange(ATTN_QBLK)[None, :, None]
    kpos = jnp.arange(ATTN_WIN)[None, None, :]
    qc, kc = qpos // CHUNK, kpos // CHUNK
    valid = (kc <= qc) & (kc >= qc - N_PREV_CHUNKS)
    back = N_PREV_CHUNKS * CHUNK
    width = back + ATTN_WIN
    length = ATTN_QBLK - 1 + width
    dist = back + (ATTN_QBLK - 1) - jnp.arange(length)
    vals = rel_bias.astype(F32)[:, jnp.clip(dist, -REL_CLIP, REL_CLIP) + REL_CLIP]
    tiled = jnp.tile(_pad_to(vals, 1, length + 1), (1, ATTN_QBLK))
    cut = tiled[:, :ATTN_QBLK * length].reshape(N_HEADS, ATTN_QBLK, length)
    big = cut[:, :, ATTN_QBLK - 1:ATTN_QBLK - 1 + width]
    bias = jnp.stack([big[:, :, back - t * ATTN_QBLK:back - t * ATTN_QBLK + ATTN_WIN] for t in range(n_tabs)],
                     axis=0)
    return jnp.where(valid[:, None], bias, NEG_INF)


def _rwkv_chunk_kernel(*refs, seq, tm, has_vres):
    if has_vres:
        (rw_ref, prev_ref, mu_ref, dbase_ref, dup_ref, abase_ref, aup_ref, gup_ref, kk_ref, ka_ref,
         rk_ref, tri_ref, vfirst_ref, vbase_ref, vdown_ref, vup_ref,
         q1_ref, y0_ref, m_ref, n0_ref, g_ref, bonus_ref,
         rw_s, aw_s, bi_s, ki_s, be_s, ke_s, v_s, wc_s) = refs
    else:
        (rw_ref, prev_ref, mu_ref, dbase_ref, dup_ref, abase_ref, aup_ref, gup_ref, kk_ref, ka_ref,
         rk_ref, tri_ref,
         q1_ref, y0_ref, m_ref, n0_ref, g_ref, bonus_ref, vfirst_out_ref,
         rw_s, aw_s, bi_s, ki_s, be_s, ke_s, v_s, wc_s) = refs

    cols = rw_ref[...]
    first = (pl.program_id(0) % (seq // tm)) == 0
    prev_row = jnp.where(first, 0.0, prev_ref[SUBLANES - 1:SUBLANES, :])
    rows = lax.broadcasted_iota(jnp.int32, cols.shape, 0)
    prev = jnp.where(rows == 0, prev_row, pltpu.roll(cols, 1, axis=0))
    xs = cols + (prev - cols) * mu_ref[...]

    d = D_HEADS
    r, k, v = xs[:, 0:d], xs[:, d:2 * d], xs[:, 2 * d:3 * d]
    wd = xs[:, 3 * d:3 * d + LANES]
    ad = xs[:, 3 * d + LANES:3 * d + 2 * LANES]
    gd = xs[:, 3 * d + 2 * LANES:3 * d + 4 * LANES]

    u = dbase_ref[...] + _dot3(jnp.tanh(wd), dup_ref[...])
    w_log = jnp.minimum(u, 0.0) - jnp.log(1.0 + jnp.exp(-jnp.abs(u))) - 0.5
    logw = -jnp.exp(w_log)
    a_icl = _sigmoid(abase_ref[...] + _dot3(ad, aup_ref[...]))
    g_ref[...] = _dot3(_sigmoid(gd), gup_ref[...]).astype(BF16)
    kk = k * kk_ref[...]
    kk = kk * lax.rsqrt(jnp.maximum(_seg_sum(kk * kk), 1e-24))
    k2 = k * (1.0 + (a_icl - 1.0) * ka_ref[...])
    if has_vres:
        low = _dot(v.astype(BF16), vdown_ref[...])
        mix = _sigmoid(vbase_ref[...] + _dot(low.astype(BF16), vup_ref[...]))
        v2 = v + (vfirst_ref[...] - v) * mix
    else:
        v2 = v
        vfirst_out_ref[...] = v
    a_vec = -kk
    b_vec = kk * a_icl
    bonus_ref[...] = (_seg_sum(r * k2 * rk_ref[...]) * v2).astype(BF16)

    cw = _dot_exact_lhs(tri_ref[...], logw)
    row_chunk = lax.broadcasted_iota(jnp.int32, (tm, tm), 0) // CHUNK
    col_chunk = lax.broadcasted_iota(jnp.int32, (tm, tm), 1) // CHUNK
    cw_tot = _dot_exact_lhs(jnp.where(row_chunk == col_chunk, 1.0, 0.0).astype(BF16), logw)
    rw_s[...] = r * jnp.exp(cw)
    aw_s[...] = a_vec * jnp.exp(cw - logw)
    inv = jnp.exp(-cw)
    bi_s[...] = b_vec * inv
    ki_s[...] = k2 * inv
    rest = jnp.exp(cw_tot - cw)
    be_s[...] = b_vec * rest
    ke_s[...] = k2 * rest
    v_s[...] = v2
    wc_s[...] = jnp.exp(cw_tot)

    shape = (CHUNK, GRP)
    ti = lax.broadcasted_iota(jnp.int32, shape, 0)
    li = lax.broadcasted_iota(jnp.int32, shape, 1)
    si = li & (HEAD_DIM - 1)
    hid = li // HEAD_DIM
    strict = ti > si
    incl = ti >= si
    eye = ti == si

    def bd(x):
        z = jnp.zeros_like(x)
        return jnp.concatenate([jnp.where(hid == h, x, z) for h in range(GRP_HEADS)], axis=0)

    def tpair(x):
        return jnp.concatenate([x[:, h * HEAD_DIM:(h + 1) * HEAD_DIM].T for h in range(GRP_HEADS)], axis=1)

    def b16(xs):
        return [x.astype(BF16) for x in xs]

    def summaries(insts):
        sl = [(slice(c * CHUNK, (c + 1) * CHUNK), slice(p * GRP, (p + 1) * GRP)) for c, p in insts]
        rw = [rw_s[s] for s in sl]
        awb = b16([aw_s[s] for s in sl])
        vb = b16([v_s[s] for s in sl])
        lhs = [jnp.concatenate([a, r.astype(BF16)], axis=0) for a, r in zip(awb, rw)]
        bik = [jnp.concatenate([bd(b), bd(k)], axis=0)
               for b, k in zip(b16([bi_s[s] for s in sl]), b16([ki_s[s] for s in sl]))]
        g = [_dot_nt(l, x) for l, x in zip(lhs, bik)]
        a_ab = [jnp.where(strict, x[:CHUNK, :GRP], 0.0) for x in g]
        a_rb = b16([jnp.where(incl, x[CHUNK:, :GRP], 0.0) for x in g])
        a_ak = b16([jnp.where(strict, x[:CHUNK, GRP:], 0.0) for x in g])
        a_rk = b16([jnp.where(incl, x[CHUNK:, GRP:], 0.0) for x in g])
        bdv = [bd(x) for x in vb]
        x0 = [_dot(a, v) for a, v in zip(a_ak, bdv)]
        tinv = [jnp.where(eye, 1.0, a) for a in a_ab]
        lpb = b16(a_ab)
        lpb = b16([_dot(l, bd(l)) for l in lpb])
        for _ in range(4):
            both = [_dot(jnp.concatenate([t.astype(BF16), l], axis=0), bd(l)) for t, l in zip(tinv, lpb)]
            tinv = [t + x[:CHUNK] for t, x in zip(tinv, both)]
            lpb = b16([x[CHUNK:] for x in both])
        tinv = [t + _dot(t.astype(BF16), bd(l)) for t, l in zip(tinv, lpb)]
        tb = b16(tinv)
        pu = b16([_dot(t, jnp.concatenate([bd(a), bd(x.astype(BF16))], axis=1))
                  for t, a, x in zip(tb, awb, x0)])
        pu_bd = [jnp.concatenate([bd(x[:, :GRP]), bd(x[:, GRP:])], axis=1) for x in pu]
        qy = [_dot(a, r) for a, r in zip(a_rb, pu_bd)]
        q1 = [r + x[:, :GRP] for r, x in zip(rw, qy)]
        y0 = [x[:, GRP:] + _dot(k, v) for x, k, v in zip(qy, a_rk, bdv)]
        bet = b16([tpair(be_s[s]) for s in sl])
        ket = b16([tpair(ke_s[s]) for s in sl])
        mn = [_dot(b, r) for b, r in zip(bet, pu_bd)]
        mm = [x[:, :GRP] for x in mn]
        nn = [x[:, GRP:] + _dot(k, v) for x, k, v in zip(mn, ket, bdv)]
        for i, (s, (c, p)) in enumerate(zip(sl, insts)):
            wc = wc_s[c * CHUNK:c * CHUNK + 1, s[1]]
            q1_ref[s] = q1[i]
            y0_ref[s] = y0[i]
            m_ref[s] = jnp.where(eye, wc, 0.0) + mm[i]
            n0_ref[s] = nn[i]

    insts = [(c, p) for c in range(tm // CHUNK) for p in range(D_HEADS // GRP)]
    for i0 in range(0, len(insts), SUMMARY_BATCH):
        summaries(insts[i0:i0 + SUMMARY_BATCH])


def _rwkv_chunk(rw, lp, seq, v_first):
    t = rw.shape[0]
    tm = 256
    has_vres = v_first is not None
    row = lambda w: pl.BlockSpec((tm, w), lambda i: (i, 0))
    prev_spec = pl.BlockSpec((SUBLANES, RW_COLS_PAD), lambda i: (jnp.maximum(i * (tm // SUBLANES) - 1, 0), 0))
    params = [lp["mu"], lp["decay_base"], lp["decay_up"], lp["aaa_base"], lp["aaa_up"], lp["gate_up"],
              lp["k_k"], lp["k_a"], lp["r_k"], lp["tri"]]
    inputs = [rw, rw] + params
    in_specs = [row(RW_COLS_PAD), prev_spec] + [_full(p.shape) for p in params]
    if has_vres:
        extra = [lp["vres_base"], lp["vres_down"], lp["vres_up"]]
        inputs += [v_first] + extra
        in_specs += [row(D_HEADS)] + [_full(p.shape) for p in extra]
    tok_shape = jax.ShapeDtypeStruct((t, D_HEADS), F32)
    half_shape = jax.ShapeDtypeStruct((t, D_HEADS), BF16)
    out_shape = [tok_shape] * 4 + [half_shape] * 2
    out_specs = [row(D_HEADS)] * 6
    if not has_vres:
        out_shape.append(tok_shape)
        out_specs.append(row(D_HEADS))
    scratch = [pltpu.VMEM((tm, D_HEADS), F32) for _ in range(8)]
    return pl.pallas_call(
        functools.partial(_rwkv_chunk_kernel, seq=seq, tm=tm, has_vres=has_vres),
        grid=(t // tm,),
        in_specs=in_specs,
        out_specs=out_specs,
        out_shape=out_shape,
        scratch_shapes=scratch,
        compiler_params=_cparams("parallel"),
        name="rwkv_chunk_vres" if has_vres else "rwkv_chunk",
    )(*inputs)


def _rwkv_scan_kernel(q1_ref, y0_ref, m_ref, n0_ref, y_ref, st_ref, *, n_chunks):
    @pl.when(pl.program_id(1) == 0)
    def _():
        st_ref[...] = jnp.zeros_like(st_ref)

    left = lax.broadcasted_iota(jnp.int32, (CHUNK, PAIR), 1) < HEAD_DIM

    def bd(x):
        z = jnp.zeros_like(x)
        return jnp.concatenate([jnp.where(left, x, z), jnp.where(left, z, x)], axis=0)

    n_pairs = D_HEADS // PAIR
    st = [st_ref[:, p * PAIR:(p + 1) * PAIR] for p in range(n_pairs)]
    for c in range(n_chunks):
        rs = slice(c * CHUNK, (c + 1) * CHUNK)
        for p in range(n_pairs):
            ls = slice(p * PAIR, (p + 1) * PAIR)
            lh, ll = _split2(jnp.concatenate([q1_ref[rs, ls], m_ref[rs, ls]], axis=0))
            sh, sl = _split2(st[p])
            bh = bd(sh)
            res = _dot(lh, bh) + (_dot(ll, bh) + _dot(lh, bd(sl)))
            y_ref[rs, ls] = res[:CHUNK] + y0_ref[rs, ls]
            st[p] = res[CHUNK:] + n0_ref[rs, ls]
    for p in range(n_pairs):
        st_ref[:, p * PAIR:(p + 1) * PAIR] = st[p]


def _rwkv_scan(q1, y0, m, n0, batch, seq):
    t = q1.shape[0]
    cb = 8
    steps = seq // (cb * CHUNK)
    tok_spec = pl.BlockSpec((cb * CHUNK, D_HEADS), lambda b, j: (b * steps + j, 0))
    return pl.pallas_call(
        functools.partial(_rwkv_scan_kernel, n_chunks=cb),
        grid=(batch, steps),
        in_specs=[tok_spec] * 4,
        out_specs=tok_spec,
        out_shape=jax.ShapeDtypeStruct((t, D_HEADS), F32),
        scratch_shapes=[pltpu.VMEM((HEAD_DIM, D_HEADS), F32)],
        compiler_params=_cparams("parallel", "arbitrary"),
        name="rwkv_scan",
    )(q1, y0, m, n0)


def _mix_kernel(y_ref, g_ref, bonus_ref, ya_ref, gate_ref, x_ref, wa_ref, wb_ref, wo_ref,
                gng_ref, gnb_ref, lng_ref, lnb_ref, o_ref):
    y = y_ref[...]
    mean = _seg_sum(y) * (1.0 / HEAD_DIM)
    yc = y - mean
    var = _seg_sum(yc * yc) * (1.0 / HEAD_DIM)
    yn = yc * lax.rsqrt(var + GN_EPS) * gng_ref[...] + gnb_ref[...]
    yr = (yn + bonus_ref[...].astype(F32)) * g_ref[...].astype(F32)
    za = _dot(ya_ref[...], wa_ref[...])
    zb = _dot(yr.astype(BF16), wb_ref[...])
    gate = gate_ref[...]
    mixed_in = gate[:, :D_MODEL].astype(F32) * za + gate[:, D_MODEL:].astype(F32) * zb
    mixed = _dot(mixed_in.astype(BF16), wo_ref[...])
    o_ref[...] = _layer_norm(DEEPNORM_ALPHA * x_ref[...] + mixed, lng_ref[...], lnb_ref[...])


def _mix(y, g, bonus, y_attn, gate, x, lp):
    t = x.shape[0]
    tm = min(256, t)
    row = lambda w: pl.BlockSpec((tm, w), lambda i: (i, 0))
    params = [lp["w_a"], lp["w_b"], lp["w_o"], lp["gn_g"], lp["gn_b"], lp["ln_g0"], lp["ln_b0"]]
    return pl.pallas_call(
        _mix_kernel,
        grid=(t // tm,),
        in_specs=[row(D_HEADS), row(D_HEADS), row(D_HEADS), row(D_HEADS), row(2 * D_MODEL), row(D_MODEL)]
        + [_full(p.shape) for p in params],
        out_specs=row(D_MODEL),
        out_shape=jax.ShapeDtypeStruct((t, D_MODEL), F32),
        compiler_params=_cparams("parallel"),
        name="branch_mix",
    )(y, g, bonus, y_attn, gate, x, *params)


MOE_SLAB = 2048
MOE_WIN = 160
MOE_EBLK = 4
MOE_ROWS = 2 * MOE_SLAB + MOE_WIN
RANK_BLK = 256
ROUTE_SLOTS = 2


def _route_kernel(x_ref, wr_ref, br_ref, tri_ref, w_ref, pos_ref, seg_ref):
    x = x_ref[...]
    tn = x.shape[0]
    lane = lax.broadcasted_iota(jnp.int32, (tn, LANES), 1)
    lanef = lane.astype(F32)
    hp = lax.Precision.HIGHEST
    elog = _dot3(x, wr_ref[...]) + br_ref[...]
    big = float(LANES)
    glog = jnp.where((lane >= N_EXPERTS) & (lane < N_EXPERTS + N_GROUPS), elog, -jnp.inf)
    gmax = jnp.max(glog, axis=-1, keepdims=True)
    g_gate = 1.0 / jnp.sum(jnp.exp(glog - gmax), axis=-1, keepdims=True)
    grp = jnp.min(jnp.where(glog == gmax, lanef, big), axis=-1, keepdims=True) - N_EXPERTS
    in_grp = (lanef >= grp * EXPERTS_PER_GROUP) & (lanef < (grp + 1.0) * EXPERTS_PER_GROUP)
    sel = jnp.where(in_grp, elog, -jnp.inf)
    v1 = jnp.max(sel, axis=-1, keepdims=True)
    i1 = jnp.min(jnp.where(sel == v1, lanef, big), axis=-1, keepdims=True)
    sel2 = jnp.where(lanef == i1, -jnp.inf, sel)
    v2 = jnp.max(sel2, axis=-1, keepdims=True)
    i2 = jnp.min(jnp.where(sel2 == v2, lanef, big), axis=-1, keepdims=True)
    e2 = jnp.exp(v2 - v1)
    w1 = g_gate / (1.0 + e2)
    w2 = g_gate * e2 / (1.0 + e2)
    tile = (SUBLANES, LANES)
    pick = jnp.where(lax.broadcasted_iota(jnp.int32, tile, 0) == lax.broadcasted_iota(jnp.int32, tile, 1), 1.0, 0.0)

    def token_rows(c0, c1):
        cols = jnp.where(lane == 0, c0, jnp.where(lane == 1, c1, 0.0))
        rows = lax.dot_general(pick, cols, (((1,), (1,)), ((), ())), precision=hp, preferred_element_type=F32)
        return jnp.concatenate([rows[0:1], rows[1:2]], axis=1)

    w_ref[0] = token_rows(w1, w2)

    hit1 = lanef == i1
    hit2 = lanef == i2
    onehot = jnp.where(hit1 | hit2, 1.0, 0.0)
    tri = tri_ref[...]
    carry = jnp.zeros((1, LANES), F32)
    ranks = []
    for j in range(tn // RANK_BLK):
        blk = onehot[j * RANK_BLK:(j + 1) * RANK_BLK]
        ranks.append(_dot(tri, blk.astype(BF16)) + carry)
        carry = carry + jnp.sum(blk, axis=0, keepdims=True)
    rank = jnp.concatenate(ranks, axis=0)
    cnt = carry
    li = lax.broadcasted_iota(jnp.int32, (LANES, LANES), 0)
    lj = lax.broadcasted_iota(jnp.int32, (LANES, LANES), 1)
    before = jnp.where(li < lj, 1.0, 0.0)
    off = jnp.dot(jnp.broadcast_to(cnt, tile), before, precision=hp, preferred_element_type=F32)[0:1]
    slot = (off + rank) * SUBLANES
    pos1 = jnp.sum(jnp.where(hit1, slot, 0.0), axis=-1, keepdims=True)
    pos2 = jnp.sum(jnp.where(hit2, slot, 0.0), axis=-1, keepdims=True)
    pos_ref[0] = token_rows(pos1, pos2).astype(jnp.int32)
    seg_row = lax.broadcasted_iota(jnp.int32, tile, 0)
    seg_ref[0] = jnp.where(seg_row == 0, off, jnp.where(seg_row == 1, cnt, 0.0)).astype(jnp.int32)


def _route(x, lp):
    t = x.shape[0]
    tn = min(MOE_SLAB, t)
    n_slabs = t // tn
    small = [lp["w_route"], lp["b_route"], lp["tri_rank"]]
    row = lambda w: pl.BlockSpec((tn, w), lambda i: (i, 0))
    return pl.pallas_call(
        _route_kernel,
        grid=(n_slabs,),
        in_specs=[row(D_MODEL)] + [_full(p.shape) for p in small],
        out_specs=[pl.BlockSpec((1, 1, ROUTE_SLOTS * tn), lambda i: (i, 0, 0)),
                   pl.BlockSpec((1, 1, ROUTE_SLOTS * tn), lambda i: (i, 0, 0)),
                   pl.BlockSpec((1, SUBLANES, LANES), lambda i: (i, 0, 0))],
        out_shape=[jax.ShapeDtypeStruct((n_slabs, 1, ROUTE_SLOTS * tn), F32),
                   jax.ShapeDtypeStruct((n_slabs, 1, ROUTE_SLOTS * tn), jnp.int32),
                   jax.ShapeDtypeStruct((n_slabs, SUBLANES, LANES), jnp.int32)],
        compiler_params=_cparams("parallel"),
        name="moe_route",
    )(x, *small)


def _moe_sparse_kernel(off_ref, cnt_ref, pos_ref, w_ref, x_ref, eg_ref, eu_ref, ed_ref, lng_ref, lnb_ref,
                       p_ref, wp_ref, wg_ref, lng2_ref, lnb2_ref, o_ref, xs_ref):
    s = pl.program_id(0)
    e = pl.program_id(1)
    tn = x_ref.shape[0]

    n_feat = D_MODEL // LANES

    def tile_rows(first_row):
        return pl.ds(pl.multiple_of(first_row, SUBLANES), SUBLANES)

    @pl.when((s == 0) & (e == 0))
    def _():
        xs_ref[pl.ds(2 * tn * SUBLANES, MOE_WIN * SUBLANES), :] = jnp.zeros((MOE_WIN * SUBLANES, LANES), F32)

    @pl.when(e == 0)
    def _():
        def scatter(i, carry):
            t0 = pl.multiple_of(i * SUBLANES, SUBLANES)
            tiles = x_ref[pl.ds(t0, SUBLANES), :].reshape(SUBLANES, n_feat, LANES)
            for j in range(SUBLANES):
                xs_ref[tile_rows(pos_ref[0, 0, t0 + j]), :] = tiles[j]
                xs_ref[tile_rows(pos_ref[0, 0, tn + t0 + j]), :] = tiles[j]
            return carry
        lax.fori_loop(0, tn // SUBLANES, scatter, 0)

    ridx = lax.broadcasted_iota(jnp.int32, (MOE_WIN, 1), 0)

    base = s * N_EXPERTS + e * MOE_EBLK
    offs = [off_ref[base + j] for j in range(MOE_EBLK)]
    cnts = [cnt_ref[base + j] for j in range(MOE_EBLK)]

    def feat_rows(j, w, f):
        return pl.ds((offs[j] + w * MOE_WIN) * SUBLANES + f, MOE_WIN, stride=SUBLANES)

    def load_win(j, w):
        return jnp.concatenate([xs_ref[feat_rows(j, w, f), :] for f in range(n_feat)], axis=1)

    def hidden(j, xw):
        xb = xw.astype(BF16)
        hg = _dot(xb, eg_ref[j])
        hu = _dot(xb, eu_ref[j])
        return (hg * _sigmoid(hg) * hu).astype(BF16)

    def project(j, w, xw, hh):
        yw = _dot(hh, ed_ref[j])
        return jnp.where(ridx + w * MOE_WIN < cnts[j], yw, xw)

    def expert(j, w, xw):
        return project(j, w, xw, hidden(j, xw))

    def store_win(j, w, yw):
        for f in range(n_feat):
            xs_ref[feat_rows(j, w, f), :] = yw[:, f * LANES:(f + 1) * LANES]

    xw = [None] * MOE_EBLK
    hh = [None] * MOE_EBLK
    for j in range(MOE_EBLK + 1):
        if j < MOE_EBLK:
            xw[j] = load_win(j, 0)
            hh[j] = hidden(j, xw[j])
        if j > 0:
            store_win(j - 1, 0, project(j - 1, 0, xw[j - 1], hh[j - 1]))

    for j in range(MOE_EBLK):
        def window(w, carry, j=j):
            store_win(j, w, expert(j, w, load_win(j, w)))
            return carry
        lax.fori_loop(1, (cnts[j] + MOE_WIN - 1) // MOE_WIN, window, 0)

    @pl.when(e == N_EXPERTS // MOE_EBLK - 1)
    def _():
        def combine(i, carry):
            t0 = pl.multiple_of(i * SUBLANES, SUBLANES)
            tiles = []
            for j in range(SUBLANES):
                y1 = xs_ref[tile_rows(pos_ref[0, 0, t0 + j]), :]
                y2 = xs_ref[tile_rows(pos_ref[0, 0, tn + t0 + j]), :]
                tiles.append(w_ref[0, 0, t0 + j] * y1 + w_ref[0, 0, tn + t0 + j] * y2)
            o_ref[pl.ds(t0, SUBLANES), :] = jnp.stack(tiles, axis=0).reshape(SUBLANES, D_MODEL)
            return carry
        lax.fori_loop(0, tn // SUBLANES, combine, 0)
        def finish(j, carry):
            rs = pl.ds(pl.multiple_of(j * RANK_BLK, RANK_BLK), RANK_BLK)
            x2 = _layer_norm(DEEPNORM_ALPHA * x_ref[rs, :] + o_ref[rs, :], lng_ref[...], lnb_ref[...])
            emb = _dot(p_ref[rs, :].astype(BF16), wp_ref[...])
            gate = _sigmoid(_dot(x2.astype(BF16), wg_ref[...]))
            o_ref[rs, :] = _layer_norm(DEEPNORM_ALPHA * x2 + emb * gate, lng2_ref[...], lnb2_ref[...])
            return carry
        lax.fori_loop(0, tn // RANK_BLK, finish, 0)


def _moe_sparse(x, p, layer, lp):
    t = x.shape[0]
    tn = min(MOE_SLAB, t)
    n_slabs = t // tn
    w, pos, seg = _route(x, lp)
    off = seg[:, 0, :N_EXPERTS].reshape(-1)
    cnt = seg[:, 1, :N_EXPERTS].reshape(-1)
    smem_row = pl.BlockSpec((1, 1, ROUTE_SLOTS * tn), lambda s, e, *_: (s, 0, 0), memory_space=pltpu.SMEM)
    slab = pl.BlockSpec((tn, D_MODEL), lambda s, e, *_: (s, 0), pipeline_mode=pl.Buffered(1))
    const = lambda a: pl.BlockSpec(a.shape, lambda s, e, *_: (0,) * a.ndim, pipeline_mode=pl.Buffered(1))
    p_slab = pl.BlockSpec((None, tn, p.shape[2]), lambda s, e, *_: (layer, s, 0), pipeline_mode=pl.Buffered(1))
    tail = [lp["w_ple"], lp["w_pg"], lp["ln_g2"], lp["ln_b2"]]
    grid_spec = pltpu.PrefetchScalarGridSpec(
        num_scalar_prefetch=2,
        grid=(n_slabs, N_EXPERTS // MOE_EBLK),
        in_specs=[smem_row, smem_row, slab,
                  pl.BlockSpec((MOE_EBLK, D_MODEL, D_EXPERT), lambda s, e, *_: (e, 0, 0)),
                  pl.BlockSpec((MOE_EBLK, D_MODEL, D_EXPERT), lambda s, e, *_: (e, 0, 0)),
                  pl.BlockSpec((MOE_EBLK, D_EXPERT, D_MODEL), lambda s, e, *_: (e, 0, 0)),
                  const(lp["ln_g1"]), const(lp["ln_b1"]), p_slab] + [const(a) for a in tail],
        out_specs=slab,
        scratch_shapes=[pltpu.VMEM((MOE_ROWS * SUBLANES, LANES), F32)],
    )
    return pl.pallas_call(
        _moe_sparse_kernel,
        grid_spec=grid_spec,
        out_shape=jax.ShapeDtypeStruct((t, D_MODEL), F32),
        compiler_params=_cparams("arbitrary", "arbitrary"),
        name="moe_sparse",
    )(off, cnt, pos, w, x, lp["e_gate"], lp["e_up"], lp["e_down"], lp["ln_g1"], lp["ln_b1"],
      p, *tail)


def _pad_to(a, axis, size):
    pad = [(0, 0)] * a.ndim
    pad[axis] = (0, size - a.shape[axis])
    return jnp.pad(a, pad)


def _row(v):
    return v.reshape(1, -1).astype(F32)


def _layer_params(i, w_in, tok_mix, decay_base, decay_up, aaa_base, aaa_up, gate_up, k_k, k_a, r_k,
                  vres_base, vres_down, vres_up, gn_g, gn_b, w_branch_attn, w_branch_rwkv, w_out,
                  router_grp, router_grp_bias, router_exp, router_exp_bias, exp_gate, exp_up, exp_down,
                  ple_proj, ple_gate, ln_g, ln_b, consts):
    d = D_HEADS
    w = w_in[i]
    rw0 = 3 * d
    gate0 = rw0 + 3 * d + DECAY_LORA + AAA_LORA + GATE_LORA
    wqkv = jnp.concatenate([w[:, 0:d] * (HEAD_DIM ** -0.5), w[:, d:rw0]], axis=1).astype(BF16)
    wr = w[:, rw0:gate0]
    lora0 = 3 * d
    pieces = [wr[:, :lora0],
              _pad_to(wr[:, lora0:lora0 + DECAY_LORA], 1, LANES),
              _pad_to(wr[:, lora0 + DECAY_LORA:lora0 + DECAY_LORA + AAA_LORA], 1, LANES),
              _pad_to(wr[:, lora0 + DECAY_LORA + AAA_LORA:], 1, 2 * LANES)]
    wrw = jnp.concatenate(pieces, axis=1).astype(BF16)
    mu = tok_mix[i]
    mu_pieces = [mu[:lora0],
                 _pad_to(mu[lora0:lora0 + DECAY_LORA], 0, LANES),
                 _pad_to(mu[lora0 + DECAY_LORA:lora0 + DECAY_LORA + AAA_LORA], 0, LANES),
                 _pad_to(mu[lora0 + DECAY_LORA + AAA_LORA:], 0, 2 * LANES)]
    lp = dict(consts)
    lp.update(
        wqkv=wqkv, wrw=wrw, wgate=w[:, gate0:].astype(BF16),
        mu=_row(jnp.concatenate(mu_pieces)),
        decay_base=_row(decay_base[i]), decay_up=_pad_to(decay_up[i], 0, LANES).astype(F32),
        aaa_base=_row(aaa_base[i]), aaa_up=_pad_to(aaa_up[i], 0, LANES).astype(F32),
        gate_up=_pad_to(gate_up[i], 0, 2 * LANES).astype(F32),
        k_k=_row(k_k[i]), k_a=_row(k_a[i]), r_k=_row(r_k[i]),
        gn_g=_row(gn_g[i]), gn_b=_row(gn_b[i]),
        w_a=w_branch_attn[i].astype(BF16), w_b=w_branch_rwkv[i].astype(BF16), w_o=w_out[i].astype(BF16),
        w_route=_pad_to(jnp.concatenate([router_exp[i], router_grp[i]], axis=1), 1, LANES).astype(F32),
        b_route=_row(_pad_to(jnp.concatenate([router_exp_bias[i], router_grp_bias[i]]), 0, LANES)),
        e_gate=exp_gate[i].astype(BF16), e_up=exp_up[i].astype(BF16), e_down=exp_down[i].astype(BF16),
        w_ple=ple_proj[i].astype(BF16), w_pg=ple_gate[i].astype(BF16),
        ln_g0=_row(ln_g[i, 0]), ln_b0=_row(ln_b[i, 0]),
        ln_g1=_row(ln_g[i, 1]), ln_b1=_row(ln_b[i, 1]),
        ln_g2=_row(ln_g[i, 2]), ln_b2=_row(ln_b[i, 2]),
    )
    if i > 0:
        lp.update(vres_base=_row(vres_base[i - 1]),
                  vres_down=_pad_to(vres_down[i - 1], 1, LANES).astype(BF16),
                  vres_up=_pad_to(vres_up[i - 1], 0, LANES).astype(BF16))
    return lp


def _const_mats(tm):
    tok = jnp.arange(tm)
    same = (tok[:, None] // CHUNK) == (tok[None, :] // CHUNK)
    tri = (same & (tok[:, None] >= tok[None, :])).astype(BF16)
    rtok = jnp.arange(RANK_BLK)
    tri_rank = (rtok[None, :] < rtok[:, None]).astype(BF16)
    return dict(tri=tri, tri_rank=tri_rank)


def kernel(x, p, ln_in_g, ln_in_b, rel_bias, w_in, tok_mix, decay_base, decay_up, aaa_base, aaa_up,
           gate_up, k_k, k_a, r_k, vres_base, vres_down, vres_up, gn_g, gn_b, w_branch_attn,
           w_branch_rwkv, w_out, router_grp, router_grp_bias, router_exp, router_exp_bias, exp_gate,
           exp_up, exp_down, ple_proj, ple_gate, ln_g, ln_b):
    batch, seq, _ = x.shape
    t = batch * seq
    consts = _const_mats(256)
    bias_tabs = _attn_bias_tables(rel_bias)
    xt = x.reshape(t, D_MODEL)
    pt = p.reshape(p.shape[0], t, p.shape[-1])
    v_first = None
    for i in range(DEPTH):
        lp = _layer_params(i, w_in, tok_mix, decay_base, decay_up, aaa_base, aaa_up, gate_up, k_k, k_a,
                           r_k, vres_base, vres_down, vres_up, gn_g, gn_b, w_branch_attn, w_branch_rwkv,
                           w_out, router_grp, router_grp_bias, router_exp, router_exp_bias, exp_gate,
                           exp_up, exp_down, ple_proj, ple_gate, ln_g, ln_b, consts)
        if i == 0:
            xt, qkv, rw, gate = _proj(xt, _row(ln_in_g), _row(ln_in_b), lp["wqkv"], lp["wrw"],
                                      lp["wgate"], apply_ln=True)
        else:
            qkv, rw, gate = _proj(xt, _row(ln_in_g), _row(ln_in_b), lp["wqkv"], lp["wrw"],
                                  lp["wgate"], apply_ln=False)
        y_attn = _attention(qkv, bias_tabs, batch, seq)
        if i == 0:
            q1, y0, m, n0, g, bonus, v_first = _rwkv_chunk(rw, lp, seq, None)
        else:
            q1, y0, m, n0, g, bonus = _rwkv_chunk(rw, lp, seq, v_first)
        y = _rwkv_scan(q1, y0, m, n0, batch, seq)
        xt = _mix(y, g, bonus, y_attn, gate, xt, lp)
        xt = _moe_sparse(xt, pt, i, lp)
    return xt.reshape(batch, seq, D_MODEL)
```

```python
import functools

import jax
import jax.numpy as jnp
from jax import lax
from jax.experimental import pallas as pl
from jax.experimental.pallas import tpu as pltpu

F32 = jnp.float32
BF16 = jnp.bfloat16

D_MODEL = 1024
CHUNK = 64
N_PREV_CHUNKS = 8
HEAD_DIM = 64
N_HEADS = 8
D_HEADS = N_HEADS * HEAD_DIM
REL_CLIP = 128
DECAY_LORA = 64
AAA_LORA = 64
GATE_LORA = 160
N_GROUPS = 4
EXPERTS_PER_GROUP = 8
N_EXPERTS = N_GROUPS * EXPERTS_PER_GROUP
D_EXPERT = 256
DEPTH = 2
DEEPNORM_ALPHA = (2 * DEPTH) ** 0.25
LN_EPS = 1e-5
GN_EPS = 64e-5
NEG_INF = -1e30

LANES = 128
SUBLANES = 8
PAIR = 2 * HEAD_DIM
GRP_HEADS = 2
GRP = GRP_HEADS * HEAD_DIM
SUMMARY_BATCH = 16
ATTN_QBLK = 2 * CHUNK
ATTN_WIN = (N_PREV_CHUNKS + 2) * CHUNK
ATTN_GROUP = 4
RW_COLS_PAD = 2048
VMEM_LIMIT = 56 * 1024 * 1024


def _cparams(*sem):
    return pltpu.CompilerParams(dimension_semantics=sem, vmem_limit_bytes=VMEM_LIMIT)


def _dot(a, b):
    return jnp.dot(a, b, preferred_element_type=F32)


def _dot_nt(a, b):
    return lax.dot_general(a, b, (((1,), (1,)), ((), ())), preferred_element_type=F32)


def _split2(x):
    hi = x.astype(BF16)
    lo = (x - hi.astype(F32)).astype(BF16)
    return hi, lo


def _dot3(a, b):
    ah, al = _split2(a)
    bh, bl = _split2(b)
    return _dot(ah, bh) + (_dot(al, bh) + _dot(ah, bl))


def _dot_exact_lhs(a_bf16, x):
    x1 = x.astype(BF16)
    r1 = x - x1.astype(F32)
    x2 = r1.astype(BF16)
    x3 = (r1 - x2.astype(F32)).astype(BF16)
    return _dot(a_bf16, x1) + (_dot(a_bf16, x2) + _dot(a_bf16, x3))


def _seg_sum(x):
    left = lax.broadcasted_iota(jnp.int32, (x.shape[0], PAIR), 1) < HEAD_DIM
    outs = []
    for p in range(x.shape[1] // PAIR):
        xp = x[:, p * PAIR:(p + 1) * PAIR]
        s0 = jnp.sum(jnp.where(left, xp, 0.0), axis=-1, keepdims=True)
        s1 = jnp.sum(jnp.where(left, 0.0, xp), axis=-1, keepdims=True)
        outs.append(jnp.where(left, s0, s1))
    return jnp.concatenate(outs, axis=-1)


def _layer_norm(x, g, b):
    mu = jnp.mean(x, axis=-1, keepdims=True)
    xc = x - mu
    var = jnp.mean(xc * xc, axis=-1, keepdims=True)
    return xc * lax.rsqrt(var + LN_EPS) * g + b


def _sigmoid(x):
    return 1.0 / (1.0 + jnp.exp(-x))


def _full(shape):
    nd = len(shape)
    return pl.BlockSpec(shape, lambda *_: (0,) * nd)


def _proj_kernel(x_ref, g_ref, b_ref, wqkv_ref, wrw_ref, wgate_ref, *out_refs, apply_ln):
    x = x_ref[...]
    if apply_ln:
        xn_ref, qkv_ref, rw_ref, gate_ref = out_refs
        x = _layer_norm(x, g_ref[...], b_ref[...])
        xn_ref[...] = x
    else:
        qkv_ref, rw_ref, gate_ref = out_refs
    xb = x.astype(BF16)
    qkv_ref[...] = _dot(xb, wqkv_ref[...]).astype(BF16)
    rw_ref[...] = _dot(xb, wrw_ref[...])
    gate_ref[...] = _sigmoid(_dot(xb, wgate_ref[...])).astype(BF16)


def _proj(x, g, b, wqkv, wrw, wgate, apply_ln):
    t = x.shape[0]
    tm = min(512, t)
    row = lambda w: pl.BlockSpec((tm, w), lambda i: (i, 0))
    held = lambda a: pl.BlockSpec(a.shape, lambda i: (0,) * a.ndim, pipeline_mode=pl.Buffered(1))
    out_shape = [jax.ShapeDtypeStruct((t, 3 * D_HEADS), BF16),
                 jax.ShapeDtypeStruct((t, RW_COLS_PAD), F32),
                 jax.ShapeDtypeStruct((t, 2 * D_MODEL), BF16)]
    out_specs = [row(3 * D_HEADS), row(RW_COLS_PAD), row(2 * D_MODEL)]
    if apply_ln:
        out_shape = [jax.ShapeDtypeStruct((t, D_MODEL), F32)] + out_shape
        out_specs = [row(D_MODEL)] + out_specs
    return pl.pallas_call(
        functools.partial(_proj_kernel, apply_ln=apply_ln),
        grid=(t // tm,),
        in_specs=[row(D_MODEL), _full(g.shape), _full(b.shape), held(wqkv), held(wrw), held(wgate)],
        out_specs=out_specs,
        out_shape=out_shape,
        compiler_params=_cparams("parallel"),
        name="proj_ln" if apply_ln else "proj",
    )(x, g, b, wqkv, wrw, wgate)


def _attn_kernel(q_ref, k_ref, v_ref, bias_ref, o_ref):
    qb = pl.program_id(1)
    ws = pl.multiple_of(jnp.maximum(qb * ATTN_QBLK - N_PREV_CHUNKS * CHUNK, 0), ATTN_QBLK)
    q = q_ref[0]
    kw = k_ref[0, pl.ds(ws, ATTN_WIN), :]
    vw = v_ref[0, pl.ds(ws, ATTN_WIN), :]
    left = lax.broadcasted_iota(jnp.int32, (ATTN_QBLK, PAIR), 1) < HEAD_DIM
    zero = jnp.zeros((ATTN_QBLK, PAIR), BF16)
    heads = [(h // 2, h % 2) for h in range(N_HEADS)]
    spans = [slice(p * PAIR, (p + 1) * PAIR) for p in range(N_HEADS // 2)]
    o = []
    for g0 in range(0, N_HEADS, ATTN_GROUP):
        grp = list(enumerate(heads))[g0:g0 + ATTN_GROUP]
        qm = [jnp.where(left, q[:, spans[p]], zero) if side == 0 else jnp.where(left, zero, q[:, spans[p]])
              for _, (p, side) in grp]
        s = [_dot_nt(x, kw[:, spans[p]]) + bias_ref[0, h] for x, (h, (p, _)) in zip(qm, grp)]
        m = [jnp.max(x, axis=-1, keepdims=True) for x in s]
        e = [jnp.exp(x - mx) for x, mx in zip(s, m)]
        inv = [1.0 / jnp.sum(x, axis=-1, keepdims=True) for x in e]
        o += [_dot(x.astype(BF16), vw[:, spans[p]]) * r for x, r, (_, (p, _)) in zip(e, inv, grp)]
    pairs = [jnp.where(left, o[2 * p], o[2 * p + 1]) for p in range(N_HEADS // 2)]
    o_ref[0] = jnp.concatenate(pairs, axis=-1).astype(BF16)


def _attention(qkv, bias_tabs, batch, seq):
    qkv3 = qkv.reshape(batch, seq, 3 * D_HEADS)
    n_tabs = bias_tabs.shape[0]
    out = pl.pallas_call(
        _attn_kernel,
        grid=(batch, seq // ATTN_QBLK),
        in_specs=[
            pl.BlockSpec((1, ATTN_QBLK, D_HEADS), lambda b, i: (b, i, 0)),
            pl.BlockSpec((1, seq, D_HEADS), lambda b, i: (b, 0, 1)),
            pl.BlockSpec((1, seq, D_HEADS), lambda b, i: (b, 0, 2)),
            pl.BlockSpec((1, N_HEADS, ATTN_QBLK, ATTN_WIN),
                         lambda b, i: (jnp.minimum(i, n_tabs - 1), 0, 0, 0)),
        ],
        out_specs=pl.BlockSpec((1, ATTN_QBLK, D_HEADS), lambda b, i: (b, i, 0)),
        out_shape=jax.ShapeDtypeStruct((batch, seq, D_HEADS), BF16),
        compiler_params=_cparams("parallel", "arbitrary"),
        name="band_attn",
    )(qkv3, qkv3, qkv3, bias_tabs)
    return out.reshape(batch * seq, D_HEADS)


def _attn_bias_tables(rel_bias):
    n_tabs = N_PREV_CHUNKS * CHUNK // ATTN_QBLK + 1
    start = (jnp.arange(n_tabs) * ATTN_QBLK)[:, None, None]
    qpos = start + jnp.arange(ATTN_QBLK)[None, :, None]
    kpos = jnp.arange(ATTN_WIN)[None, None, :]
    qc, kc = qpos // CHUNK, kpos // CHUNK
    valid = (kc <= qc) & (kc >= qc - N_PREV_CHUNKS)
    back = N_PREV_CHUNKS * CHUNK
    width = back + ATTN_WIN
    length = ATTN_QBLK - 1 + width
    dist = back + (ATTN_QBLK - 1) - jnp.arange(length)
    vals = rel_bias.astype(F32)[:, jnp.clip(dist, -REL_CLIP, REL_CLIP) + REL_CLIP]
    tiled = jnp.tile(_pad_to(vals, 1, length + 1), (1, ATTN_QBLK))
    cut = tiled[:, :ATTN_QBLK * length].reshape(N_HEADS, ATTN_QBLK, length)
    big = cut[:, :, ATTN_QBLK - 1:ATTN_QBLK - 1 + width]
    bias = jnp.stack([big[:, :, back - t * ATTN_QBLK:back - t * ATTN_QBLK + ATTN_WIN] for t in range(n_tabs)],
                     axis=0)
    return jnp.where(valid[:, None], bias, NEG_INF)


def _rwkv_chunk_kernel(*refs, seq, tm, has_vres):
    if has_vres:
        (rw_ref, prev_ref, mu_ref, dbase_ref, dup_ref, abase_ref, aup_ref, gup_ref, kk_ref, ka_ref,
         rk_ref, tri_ref, vfirst_ref, vbase_ref, vdown_ref, vup_ref,
         q1_ref, y0_ref, m_ref, n0_ref, g_ref, bonus_ref,
         rw_s, aw_s, bi_s, ki_s, be_s, ke_s, v_s, wc_s) = refs
    else:
        (rw_ref, prev_ref, mu_ref, dbase_ref, dup_ref, abase_ref, aup_ref, gup_ref, kk_ref, ka_ref,
         rk_ref, tri_ref,
         q1_ref, y0_ref, m_ref, n0_ref, g_ref, bonus_ref, vfirst_out_ref,
         rw_s, aw_s, bi_s, ki_s, be_s, ke_s, v_s, wc_s) = refs

    cols = rw_ref[...]
    first = (pl.program_id(0) % (seq // tm)) == 0
    prev_row = jnp.where(first, 0.0, prev_ref[SUBLANES - 1:SUBLANES, :])
    rows = lax.broadcasted_iota(jnp.int32, cols.shape, 0)
    prev = jnp.where(rows == 0, prev_row, pltpu.roll(cols, 1, axis=0))
    xs = cols + (prev - cols) * mu_ref[...]

    d = D_HEADS
    r, k, v = xs[:, 0:d], xs[:, d:2 * d], xs[:, 2 * d:3 * d]
    wd = xs[:, 3 * d:3 * d + LANES]
    ad = xs[:, 3 * d + LANES:3 * d + 2 * LANES]
    gd = xs[:, 3 * d + 2 * LANES:3 * d + 4 * LANES]

    u = dbase_ref[...] + _dot3(jnp.tanh(wd), dup_ref[...])
    w_log = jnp.minimum(u, 0.0) - jnp.log(1.0 + jnp.exp(-jnp.abs(u))) - 0.5
    logw = -jnp.exp(w_log)
    a_icl = _sigmoid(abase_ref[...] + _dot3(ad, aup_ref[...]))
    g_ref[...] = _dot3(_sigmoid(gd), gup_ref[...]).astype(BF16)
    kk = k * kk_ref[...]
    kk = kk * lax.rsqrt(jnp.maximum(_seg_sum(kk * kk), 1e-24))
    k2 = k * (1.0 + (a_icl - 1.0) * ka_ref[...])
    if has_vres:
        low = _dot(v.astype(BF16), vdown_ref[...])
        mix = _sigmoid(vbase_ref[...] + _dot(low.astype(BF16), vup_ref[...]))
        v2 = v + (vfirst_ref[...] - v) * mix
    else:
        v2 = v
        vfirst_out_ref[...] = v
    a_vec = -kk
    b_vec = kk * a_icl
    bonus_ref[...] = (_seg_sum(r * k2 * rk_ref[...]) * v2).astype(BF16)

    cw = _dot_exact_lhs(tri_ref[...], logw)
    row_chunk = lax.broadcasted_iota(jnp.int32, (tm, tm), 0) // CHUNK
    col_chunk = lax.broadcasted_iota(jnp.int32, (tm, tm), 1) // CHUNK
    cw_tot = _dot_exact_lhs(jnp.where(row_chunk == col_chunk, 1.0, 0.0).astype(BF16), logw)
    rw_s[...] = r * jnp.exp(cw)
    aw_s[...] = a_vec * jnp.exp(cw - logw)
    inv = jnp.exp(-cw)
    bi_s[...] = b_vec * inv
    ki_s[...] = k2 * inv
    rest = jnp.exp(cw_tot - cw)
    be_s[...] = b_vec * rest
    ke_s[...] = k2 * rest
    v_s[...] = v2
    wc_s[...] = jnp.exp(cw_tot)

    shape = (CHUNK, GRP)
    ti = lax.broadcasted_iota(jnp.int32, shape, 0)
    li = lax.broadcasted_iota(jnp.int32, shape, 1)
    si = li & (HEAD_DIM - 1)
    hid = li // HEAD_DIM
    strict = ti > si
    incl = ti >= si
    eye = ti == si

    def bd(x):
        z = jnp.zeros_like(x)
        return jnp.concatenate([jnp.where(hid == h, x, z) for h in range(GRP_HEADS)], axis=0)

    def tpair(x):
        return jnp.concatenate([x[:, h * HEAD_DIM:(h + 1) * HEAD_DIM].T for h in range(GRP_HEADS)], axis=1)

    def b16(xs):
        return [x.astype(BF16) for x in xs]

    def summaries(insts):
        sl = [(slice(c * CHUNK, (c + 1) * CHUNK), slice(p * GRP, (p + 1) * GRP)) for c, p in insts]
        rw = [rw_s[s] for s in sl]
        awb = b16([aw_s[s] for s in sl])
        vb = b16([v_s[s] for s in sl])
        lhs = [jnp.concatenate([a, r.astype(BF16)], axis=0) for a, r in zip(awb, rw)]
        bik = [jnp.concatenate([bd(b), bd(k)], axis=0)
               for b, k in zip(b16([bi_s[s] for s in sl]), b16([ki_s[s] for s in sl]))]
        g = [_dot_nt(l, x) for l, x in zip(lhs, bik)]
        a_ab = [jnp.where(strict, x[:CHUNK, :GRP], 0.0) for x in g]
        a_rb = b16([jnp.where(incl, x[CHUNK:, :GRP], 0.0) for x in g])
        a_ak = b16([jnp.where(strict, x[:CHUNK, GRP:], 0.0) for x in g])
        a_rk = b16([jnp.where(incl, x[CHUNK:, GRP:], 0.0) for x in g])
        bdv = [bd(x) for x in vb]
        x0 = [_dot(a, v) for a, v in zip(a_ak, bdv)]
        tinv = [jnp.where(eye, 1.0, a) for a in a_ab]
        lpb = b16(a_ab)
        lpb = b16([_dot(l, bd(l)) for l in lpb])
        for _ in range(4):
            both = [_dot(jnp.concatenate([t.astype(BF16), l], axis=0), bd(l)) for t, l in zip(tinv, lpb)]
            tinv = [t + x[:CHUNK] for t, x in zip(tinv, both)]
            lpb = b16([x[CHUNK:] for x in both])
        tinv = [t + _dot(t.astype(BF16), bd(l)) for t, l in zip(tinv, lpb)]
        tb = b16(tinv)
        pu = b16([_dot(t, jnp.concatenate([bd(a), bd(x.astype(BF16))], axis=1))
                  for t, a, x in zip(tb, awb, x0)])
        pu_bd = [jnp.concatenate([bd(x[:, :GRP]), bd(x[:, GRP:])], axis=1) for x in pu]
        qy = [_dot(a, r) for a, r in zip(a_rb, pu_bd)]
        q1 = [r + x[:, :GRP] for r, x in zip(rw, qy)]
        y0 = [x[:, GRP:] + _dot(k, v) for x, k, v in zip(qy, a_rk, bdv)]
        bet = b16([tpair(be_s[s]) for s in sl])
        ket = b16([tpair(ke_s[s]) for s in sl])
        mn = [_dot(b, r) for b, r in zip(bet, pu_bd)]
        mm = [x[:, :GRP] for x in mn]
        nn = [x[:, GRP:] + _dot(k, v) for x, k, v in zip(mn, ket, bdv)]
        for i, (s, (c, p)) in enumerate(zip(sl, insts)):
            wc = wc_s[c * CHUNK:c * CHUNK + 1, s[1]]
            q1_ref[s] = q1[i]
            y0_ref[s] = y0[i]
            m_ref[s] = jnp.where(eye, wc, 0.0) + mm[i]
            n0_ref[s] = nn[i]

    insts = [(c, p) for c in range(tm // CHUNK) for p in range(D_HEADS // GRP)]
    for i0 in range(0, len(insts), SUMMARY_BATCH):
        summaries(insts[i0:i0 + SUMMARY_BATCH])


def _rwkv_chunk(rw, lp, seq, v_first):
    t = rw.shape[0]
    tm = 256
    has_vres = v_first is not None
    row = lambda w: pl.BlockSpec((tm, w), lambda i: (i, 0))
    prev_spec = pl.BlockSpec((SUBLANES, RW_COLS_PAD), lambda i: (jnp.maximum(i * (tm // SUBLANES) - 1, 0), 0))
    params = [lp["mu"], lp["decay_base"], lp["decay_up"], lp["aaa_base"], lp["aaa_up"], lp["gate_up"],
              lp["k_k"], lp["k_a"], lp["r_k"], lp["tri"]]
    inputs = [rw, rw] + params
    in_specs = [row(RW_COLS_PAD), prev_spec] + [_full(p.shape) for p in params]
    if has_vres:
        extra = [lp["vres_base"], lp["vres_down"], lp["vres_up"]]
        inputs += [v_first] + extra
        in_specs += [row(D_HEADS)] + [_full(p.shape) for p in extra]
    tok_shape = jax.ShapeDtypeStruct((t, D_HEADS), F32)
    half_shape = jax.ShapeDtypeStruct((t, D_HEADS), BF16)
    out_shape = [tok_shape] * 4 + [half_shape] * 2
    out_specs = [row(D_HEADS)] * 6
    if not has_vres:
        out_shape.append(tok_shape)
        out_specs.append(row(D_HEADS))
    scratch = [pltpu.VMEM((tm, D_HEADS), F32) for _ in range(8)]
    return pl.pallas_call(
        functools.partial(_rwkv_chunk_kernel, seq=seq, tm=tm, has_vres=has_vres),
        grid=(t // tm,),
        in_specs=in_specs,
        out_specs=out_specs,
        out_shape=out_shape,
        scratch_shapes=scratch,
        compiler_params=_cparams("parallel"),
        name="rwkv_chunk_vres" if has_vres else "rwkv_chunk",
    )(*inputs)


def _rwkv_scan_kernel(q1_ref, y0_ref, m_ref, n0_ref, y_ref, st_ref, *, n_chunks):
    @pl.when(pl.program_id(1) == 0)
    def _():
        st_ref[...] = jnp.zeros_like(st_ref)

    left = lax.broadcasted_iota(jnp.int32, (CHUNK, PAIR), 1) < HEAD_DIM

    def bd(x):
        z = jnp.zeros_like(x)
        return jnp.concatenate([jnp.where(left, x, z), jnp.where(left, z, x)], axis=0)

    n_pairs = D_HEADS // PAIR
    st = [st_ref[:, p * PAIR:(p + 1) * PAIR] for p in range(n_pairs)]
    for c in range(n_chunks):
        rs = slice(c * CHUNK, (c + 1) * CHUNK)
        for p in range(n_pairs):
            ls = slice(p * PAIR, (p + 1) * PAIR)
            lh, ll = _split2(jnp.concatenate([q1_ref[rs, ls], m_ref[rs, ls]], axis=0))
            sh, sl = _split2(st[p])
            bh = bd(sh)
            res = _dot(lh, bh) + (_dot(ll, bh) + _dot(lh, bd(sl)))
            y_ref[rs, ls] = res[:CHUNK] + y0_ref[rs, ls]
            st[p] = res[CHUNK:] + n0_ref[rs, ls]
    for p in range(n_pairs):
        st_ref[:, p * PAIR:(p + 1) * PAIR] = st[p]


def _rwkv_scan(q1, y0, m, n0, batch, seq):
    t = q1.shape[0]
    cb = 8
    steps = seq // (cb * CHUNK)
    tok_spec = pl.BlockSpec((cb * CHUNK, D_HEADS), lambda b, j: (b * steps + j, 0))
    return pl.pallas_call(
        functools.partial(_rwkv_scan_kernel, n_chunks=cb),
        grid=(batch, steps),
        in_specs=[tok_spec] * 4,
        out_specs=tok_spec,
        out_shape=jax.ShapeDtypeStruct((t, D_HEADS), F32),
        scratch_shapes=[pltpu.VMEM((HEAD_DIM, D_HEADS), F32)],
        compiler_params=_cparams("parallel", "arbitrary"),
        name="rwkv_scan",
    )(q1, y0, m, n0)


def _mix_kernel(y_ref, g_ref, bonus_ref, ya_ref, gate_ref, x_ref, wa_ref, wb_ref, wo_ref,
                gng_ref, gnb_ref, lng_ref, lnb_ref, o_ref):
    y = y_ref[...]
    mean = _seg_sum(y) * (1.0 / HEAD_DIM)
    yc = y - mean
    var = _seg_sum(yc * yc) * (1.0 / HEAD_DIM)
    yn = yc * lax.rsqrt(var + GN_EPS) * gng_ref[...] + gnb_ref[...]
    yr = (yn + bonus_ref[...].astype(F32)) * g_ref[...].astype(F32)
    za = _dot(ya_ref[...], wa_ref[...])
    zb = _dot(yr.astype(BF16), wb_ref[...])
    gate = gate_ref[...]
    mixed_in = gate[:, :D_MODEL].astype(F32) * za + gate[:, D_MODEL:].astype(F32) * zb
    mixed = _dot(mixed_in.astype(BF16), wo_ref[...])
    o_ref[...] = _layer_norm(DEEPNORM_ALPHA * x_ref[...] + mixed, lng_ref[...], lnb_ref[...])


def _mix(y, g, bonus, y_attn, gate, x, lp):
    t = x.shape[0]
    tm = min(512, t)
    row = lambda w: pl.BlockSpec((tm, w), lambda i: (i, 0))
    params = [lp["w_a"], lp["w_b"], lp["w_o"], lp["gn_g"], lp["gn_b"], lp["ln_g0"], lp["ln_b0"]]
    return pl.pallas_call(
        _mix_kernel,
        grid=(t // tm,),
        in_specs=[row(D_HEADS), row(D_HEADS), row(D_HEADS), row(D_HEADS), row(2 * D_MODEL), row(D_MODEL)]
        + [_full(p.shape) for p in params],
        out_specs=row(D_MODEL),
        out_shape=jax.ShapeDtypeStruct((t, D_MODEL), F32),
        compiler_params=_cparams("parallel"),
        name="branch_mix",
    )(y, g, bonus, y_attn, gate, x, *params)


MOE_SLAB = 2048
MOE_WIN = 160
MOE_EBLK = 4
MOE_ROWS = 2 * MOE_SLAB + MOE_WIN
RANK_BLK = 256
ROUTE_SLOTS = 2


def _route_kernel(x_ref, wr_ref, br_ref, tri_ref, w_ref, pos_ref, seg_ref):
    x = x_ref[...]
    tn = x.shape[0]
    lane = lax.broadcasted_iota(jnp.int32, (tn, LANES), 1)
    lanef = lane.astype(F32)
    hp = lax.Precision.HIGHEST
    elog = _dot3(x, wr_ref[...]) + br_ref[...]
    big = float(LANES)
    glog = jnp.where((lane >= N_EXPERTS) & (lane < N_EXPERTS + N_GROUPS), elog, -jnp.inf)
    gmax = jnp.max(glog, axis=-1, keepdims=True)
    g_gate = 1.0 / jnp.sum(jnp.exp(glog - gmax), axis=-1, keepdims=True)
    grp = jnp.min(jnp.where(glog == gmax, lanef, big), axis=-1, keepdims=True) - N_EXPERTS
    in_grp = (lanef >= grp * EXPERTS_PER_GROUP) & (lanef < (grp + 1.0) * EXPERTS_PER_GROUP)
    sel = jnp.where(in_grp, elog, -jnp.inf)
    v1 = jnp.max(sel, axis=-1, keepdims=True)
    i1 = jnp.min(jnp.where(sel == v1, lanef, big), axis=-1, keepdims=True)
    sel2 = jnp.where(lanef == i1, -jnp.inf, sel)
    v2 = jnp.max(sel2, axis=-1, keepdims=True)
    i2 = jnp.min(jnp.where(sel2 == v2, lanef, big), axis=-1, keepdims=True)
    e2 = jnp.exp(v2 - v1)
    w1 = g_gate / (1.0 + e2)
    w2 = g_gate * e2 / (1.0 + e2)
    tile = (SUBLANES, LANES)
    pick = jnp.where(lax.broadcasted_iota(jnp.int32, tile, 0) == lax.broadcasted_iota(jnp.int32, tile, 1), 1.0, 0.0)

    def token_rows(c0, c1):
        cols = jnp.where(lane == 0, c0, jnp.where(lane == 1, c1, 0.0))
        rows = lax.dot_general(pick, cols, (((1,), (1,)), ((), ())), precision=hp, preferred_element_type=F32)
        return jnp.concatenate([rows[0:1], rows[1:2]], axis=1)

    w_ref[0] = token_rows(w1, w2)

    hit1 = lanef == i1
    hit2 = lanef == i2
    onehot = jnp.where(hit1 | hit2, 1.0, 0.0)
    tri = tri_ref[...]
    carry = jnp.zeros((1, LANES), F32)
    ranks = []
    for j in range(tn // RANK_BLK):
        blk = onehot[j * RANK_BLK:(j + 1) * RANK_BLK]
        ranks.append(_dot(tri, blk.astype(BF16)) + carry)
        carry = carry + jnp.sum(blk, axis=0, keepdims=True)
    rank = jnp.concatenate(ranks, axis=0)
    cnt = carry
    li = lax.broadcasted_iota(jnp.int32, (LANES, LANES), 0)
    lj = lax.broadcasted_iota(jnp.int32, (LANES, LANES), 1)
    before = jnp.where(li < lj, 1.0, 0.0)
    off = jnp.dot(jnp.broadcast_to(cnt, tile), before, precision=hp, preferred_element_type=F32)[0:1]
    slot = (off + rank) * SUBLANES
    pos1 = jnp.sum(jnp.where(hit1, slot, 0.0), axis=-1, keepdims=True)
    pos2 = jnp.sum(jnp.where(hit2, slot, 0.0), axis=-1, keepdims=True)
    pos_ref[0] = token_rows(pos1, pos2).astype(jnp.int32)
    seg_row = lax.broadcasted_iota(jnp.int32, tile, 0)
    seg_ref[0] = jnp.where(seg_row == 0, off, jnp.where(seg_row == 1, cnt, 0.0)).astype(jnp.int32)


def _route(x, lp):
    t = x.shape[0]
    tn = min(MOE_SLAB, t)
    n_slabs = t // tn
    small = [lp["w_route"], lp["b_route"], lp["tri_rank"]]
    row = lambda w: pl.BlockSpec((tn, w), lambda i: (i, 0))
    return pl.pallas_call(
        _route_kernel,
        grid=(n_slabs,),
        in_specs=[row(D_MODEL)] + [_full(p.shape) for p in small],
        out_specs=[pl.BlockSpec((1, 1, ROUTE_SLOTS * tn), lambda i: (i, 0, 0)),
                   pl.BlockSpec((1, 1, ROUTE_SLOTS * tn), lambda i: (i, 0, 0)),
                   pl.BlockSpec((1, SUBLANES, LANES), lambda i: (i, 0, 0))],
        out_shape=[jax.ShapeDtypeStruct((n_slabs, 1, ROUTE_SLOTS * tn), F32),
                   jax.ShapeDtypeStruct((n_slabs, 1, ROUTE_SLOTS * tn), jnp.int32),
                   jax.ShapeDtypeStruct((n_slabs, SUBLANES, LANES), jnp.int32)],
        compiler_params=_cparams("parallel"),
        name="moe_route",
    )(x, *small)


def _moe_sparse_kernel(off_ref, cnt_ref, pos_ref, w_ref, x_ref, eg_ref, eu_ref, ed_ref, lng_ref, lnb_ref,
                       p_ref, wp_ref, wg_ref, lng2_ref, lnb2_ref, o_ref, xs_ref):
    s = pl.program_id(0)
    e = pl.program_id(1)
    tn = x_ref.shape[0]

    n_feat = D_MODEL // LANES

    def tile_rows(first_row):
        return pl.ds(pl.multiple_of(first_row, SUBLANES), SUBLANES)

    @pl.when((s == 0) & (e == 0))
    def _():
        xs_ref[pl.ds(2 * tn * SUBLANES, MOE_WIN * SUBLANES), :] = jnp.zeros((MOE_WIN * SUBLANES, LANES), F32)

    @pl.when(e == 0)
    def _():
        def scatter(i, carry):
            t0 = pl.multiple_of(i * SUBLANES, SUBLANES)
            tiles = x_ref[pl.ds(t0, SUBLANES), :].reshape(SUBLANES, n_feat, LANES)
            for j in range(SUBLANES):
                xs_ref[tile_rows(pos_ref[0, 0, t0 + j]), :] = tiles[j]
                xs_ref[tile_rows(pos_ref[0, 0, tn + t0 + j]), :] = tiles[j]
            return carry
        lax.fori_loop(0, tn // SUBLANES, scatter, 0)

    ridx = lax.broadcasted_iota(jnp.int32, (MOE_WIN, 1), 0)

    base = s * N_EXPERTS + e * MOE_EBLK
    offs = [off_ref[base + j] for j in range(MOE_EBLK)]
    cnts = [cnt_ref[base + j] for j in range(MOE_EBLK)]

    def feat_rows(j, w, f):
        return pl.ds((offs[j] + w * MOE_WIN) * SUBLANES + f, MOE_WIN, stride=SUBLANES)

    def load_win(j, w):
        return jnp.concatenate([xs_ref[feat_rows(j, w, f), :] for f in range(n_feat)], axis=1)

    def hidden(j, xw):
        xb = xw.astype(BF16)
        hg = _dot(xb, eg_ref[j])
        hu = _dot(xb, eu_ref[j])
        return (hg * _sigmoid(hg) * hu).astype(BF16)

    def project(j, w, xw, hh):
        yw = _dot(hh, ed_ref[j])
        return jnp.where(ridx + w * MOE_WIN < cnts[j], yw, xw)

    def expert(j, w, xw):
        return project(j, w, xw, hidden(j, xw))

    def store_win(j, w, yw):
        for f in range(n_feat):
            xs_ref[feat_rows(j, w, f), :] = yw[:, f * LANES:(f + 1) * LANES]

    xw = [None] * MOE_EBLK
    hh = [None] * MOE_EBLK
    for j in range(MOE_EBLK + 1):
        if j < MOE_EBLK:
            xw[j] = load_win(j, 0)
            hh[j] = hidden(j, xw[j])
        if j > 0:
            store_win(j - 1, 0, project(j - 1, 0, xw[j - 1], hh[j - 1]))

    for j in range(MOE_EBLK):
        def window(w, carry, j=j):
            store_win(j, w, expert(j, w, load_win(j, w)))
            return carry
        lax.fori_loop(1, (cnts[j] + MOE_WIN - 1) // MOE_WIN, window, 0)

    @pl.when(e == N_EXPERTS // MOE_EBLK - 1)
    def _():
        def combine(i, carry):
            t0 = pl.multiple_of(i * SUBLANES, SUBLANES)
            tiles = []
            for j in range(SUBLANES):
                y1 = xs_ref[tile_rows(pos_ref[0, 0, t0 + j]), :]
                y2 = xs_ref[tile_rows(pos_ref[0, 0, tn + t0 + j]), :]
                tiles.append(w_ref[0, 0, t0 + j] * y1 + w_ref[0, 0, tn + t0 + j] * y2)
            o_ref[pl.ds(t0, SUBLANES), :] = jnp.stack(tiles, axis=0).reshape(SUBLANES, D_MODEL)
            return carry
        lax.fori_loop(0, tn // SUBLANES, combine, 0)
        def finish(j, carry):
            rs = pl.ds(pl.multiple_of(j * RANK_BLK, RANK_BLK), RANK_BLK)
            x2 = _layer_norm(DEEPNORM_ALPHA * x_ref[rs, :] + o_ref[rs, :], lng_ref[...], lnb_ref[...])
            emb = _dot(p_ref[rs, :].astype(BF16), wp_ref[...])
            gate = _sigmoid(_dot(x2.astype(BF16), wg_ref[...]))
            o_ref[rs, :] = _layer_norm(DEEPNORM_ALPHA * x2 + emb * gate, lng2_ref[...], lnb2_ref[...])
            return carry
        lax.fori_loop(0, tn // RANK_BLK, finish, 0)


def _moe_sparse(x, p, layer, lp):
    t = x.shape[0]
    tn = min(MOE_SLAB, t)
    n_slabs = t // tn
    w, pos, seg = _route(x, lp)
    off = seg[:, 0, :N_EXPERTS].reshape(-1)
    cnt = seg[:, 1, :N_EXPERTS].reshape(-1)
    smem_row = pl.BlockSpec((1, 1, ROUTE_SLOTS * tn), lambda s, e, *_: (s, 0, 0), memory_space=pltpu.SMEM)
    slab = pl.BlockSpec((tn, D_MODEL), lambda s, e, *_: (s, 0), pipeline_mode=pl.Buffered(1))
    const = lambda a: pl.BlockSpec(a.shape, lambda s, e, *_: (0,) * a.ndim, pipeline_mode=pl.Buffered(1))
    p_slab = pl.BlockSpec((None, tn, p.shape[2]), lambda s, e, *_: (layer, s, 0), pipeline_mode=pl.Buffered(1))
    tail = [lp["w_ple"], lp["w_pg"], lp["ln_g2"], lp["ln_b2"]]
    grid_spec = pltpu.PrefetchScalarGridSpec(
        num_scalar_prefetch=2,
        grid=(n_slabs, N_EXPERTS // MOE_EBLK),
        in_specs=[smem_row, smem_row, slab,
                  pl.BlockSpec((MOE_EBLK, D_MODEL, D_EXPERT), lambda s, e, *_: (e, 0, 0)),
                  pl.BlockSpec((MOE_EBLK, D_MODEL, D_EXPERT), lambda s, e, *_: (e, 0, 0)),
                  pl.BlockSpec((MOE_EBLK, D_EXPERT, D_MODEL), lambda s, e, *_: (e, 0, 0)),
                  const(lp["ln_g1"]), const(lp["ln_b1"]), p_slab] + [const(a) for a in tail],
        out_specs=slab,
        scratch_shapes=[pltpu.VMEM((MOE_ROWS * SUBLANES, LANES), F32)],
    )
    return pl.pallas_call(
        _moe_sparse_kernel,
        grid_spec=grid_spec,
        out_shape=jax.ShapeDtypeStruct((t, D_MODEL), F32),
        compiler_params=_cparams("arbitrary", "arbitrary"),
        name="moe_sparse",
    )(off, cnt, pos, w, x, lp["e_gate"], lp["e_up"], lp["e_down"], lp["ln_g1"], lp["ln_b1"],
      p, *tail)


def _pad_to(a, axis, size):
    pad = [(0, 0)] * a.ndim
    pad[axis] = (0, size - a.shape[axis])
    return jnp.pad(a, pad)


def _row(v):
    return v.reshape(1, -1).astype(F32)


def _layer_params(i, w_in, tok_mix, decay_base, decay_up, aaa_base, aaa_up, gate_up, k_k, k_a, r_k,
                  vres_base, vres_down, vres_up, gn_g, gn_b, w_branch_attn, w_branch_rwkv, w_out,
                  router_grp, router_grp_bias, router_exp, router_exp_bias, exp_gate, exp_up, exp_down,
                  ple_proj, ple_gate, ln_g, ln_b, consts):
    d = D_HEADS
    w = w_in[i]
    rw0 = 3 * d
    gate0 = rw0 + 3 * d + DECAY_LORA + AAA_LORA + GATE_LORA
    wqkv = jnp.concatenate([w[:, 0:d] * (HEAD_DIM ** -0.5), w[:, d:rw0]], axis=1).astype(BF16)
    wr = w[:, rw0:gate0]
    lora0 = 3 * d
    pieces = [wr[:, :lora0],
              _pad_to(wr[:, lora0:lora0 + DECAY_LORA], 1, LANES),
              _pad_to(wr[:, lora0 + DECAY_LORA:lora0 + DECAY_LORA + AAA_LORA], 1, LANES),
              _pad_to(wr[:, lora0 + DECAY_LORA + AAA_LORA:], 1, 2 * LANES)]
    wrw = jnp.concatenate(pieces, axis=1).astype(BF16)
    mu = tok_mix[i]
    mu_pieces = [mu[:lora0],
                 _pad_to(mu[lora0:lora0 + DECAY_LORA], 0, LANES),
                 _pad_to(mu[lora0 + DECAY_LORA:lora0 + DECAY_LORA + AAA_LORA], 0, LANES),
                 _pad_to(mu[lora0 + DECAY_LORA + AAA_LORA:], 0, 2 * LANES)]
    lp = dict(consts)
    lp.update(
        wqkv=wqkv, wrw=wrw, wgate=w[:, gate0:].astype(BF16),
        mu=_row(jnp.concatenate(mu_pieces)),
        decay_base=_row(decay_base[i]), decay_up=_pad_to(decay_up[i], 0, LANES).astype(F32),
        aaa_base=_row(aaa_base[i]), aaa_up=_pad_to(aaa_up[i], 0, LANES).astype(F32),
        gate_up=_pad_to(gate_up[i], 0, 2 * LANES).astype(F32),
        k_k=_row(k_k[i]), k_a=_row(k_a[i]), r_k=_row(r_k[i]),
        gn_g=_row(gn_g[i]), gn_b=_row(gn_b[i]),
        w_a=w_branch_attn[i].astype(BF16), w_b=w_branch_rwkv[i].astype(BF16), w_o=w_out[i].astype(BF16),
        w_route=_pad_to(jnp.concatenate([router_exp[i], router_grp[i]], axis=1), 1, LANES).astype(F32),
        b_route=_row(_pad_to(jnp.concatenate([router_exp_bias[i], router_grp_bias[i]]), 0, LANES)),
        e_gate=exp_gate[i].astype(BF16), e_up=exp_up[i].astype(BF16), e_down=exp_down[i].astype(BF16),
        w_ple=ple_proj[i].astype(BF16), w_pg=ple_gate[i].astype(BF16),
        ln_g0=_row(ln_g[i, 0]), ln_b0=_row(ln_b[i, 0]),
        ln_g1=_row(ln_g[i, 1]), ln_b1=_row(ln_b[i, 1]),
        ln_g2=_row(ln_g[i, 2]), ln_b2=_row(ln_b[i, 2]),
    )
    if i > 0:
        lp.update(vres_base=_row(vres_base[i - 1]),
                  vres_down=_pad_to(vres_down[i - 1], 1, LANES).astype(BF16),
                  vres_up=_pad_to(vres_up[i - 1], 0, LANES).astype(BF16))
    return lp


def _const_mats(tm):
    tok = jnp.arange(tm)
    same = (tok[:, None] // CHUNK) == (tok[None, :] // CHUNK)
    tri = (same & (tok[:, None] >= tok[None, :])).astype(BF16)
    rtok = jnp.arange(RANK_BLK)
    tri_rank = (rtok[None, :] < rtok[:, None]).astype(BF16)
    return dict(tri=tri, tri_rank=tri_rank)


def kernel(x, p, ln_in_g, ln_in_b, rel_bias, w_in, tok_mix, decay_base, decay_up, aaa_base, aaa_up,
           gate_up, k_k, k_a, r_k, vres_base, vres_down, vres_up, gn_g, gn_b, w_branch_attn,
           w_branch_rwkv, w_out, router_grp, router_grp_bias, router_exp, router_exp_bias, exp_gate,
           exp_up, exp_down, ple_proj, ple_gate, ln_g, ln_b):
    batch, seq, _ = x.shape
    t = batch * seq
    consts = _const_mats(256)
    bias_tabs = _attn_bias_tables(rel_bias)
    xt = x.reshape(t, D_MODEL)
    pt = p.reshape(p.shape[0], t, p.shape[-1])
    v_first = None
    for i in range(DEPTH):
        lp = _layer_params(i, w_in, tok_mix, decay_base, decay_up, aaa_base, aaa_up, gate_up, k_k, k_a,
                           r_k, vres_base, vres_down, vres_up, gn_g, gn_b, w_branch_attn, w_branch_rwkv,
                           w_out, router_grp, router_grp_bias, router_exp, router_exp_bias, exp_gate,
                           exp_up, exp_down, ple_proj, ple_gate, ln_g, ln_b, consts)
        if i == 0:
            xt, qkv, rw, gate = _proj(xt, _row(ln_in_g), _row(ln_in_b), lp["wqkv"], lp["wrw"],
                                      lp["wgate"], apply_ln=True)
        else:
            qkv, rw, gate = _proj(xt, _row(ln_in_g), _row(ln_in_b), lp["wqkv"], lp["wrw"],
                                  lp["wgate"], apply_ln=False)
        y_attn = _attention(qkv, bias_tabs, batch, seq)
        if i == 0:
            q1, y0, m, n0, g, bonus, v_first = _rwkv_chunk(rw, lp, seq, None)
        else:
            q1, y0, m, n0, g, bonus = _rwkv_chunk(rw, lp, seq, v_first)
        y = _rwkv_scan(q1, y0, m, n0, batch, seq)
        xt = _mix(y, g, bonus, y_attn, gate, xt, lp)
        xt = _moe_sparse(xt, pt, i, lp)
    return xt.reshape(batch, seq, D_MODEL)
```

```python
import functools

import jax
import jax.numpy as jnp
from jax import lax
from jax.experimental import pallas as pl
from jax.experimental.pallas import tpu as pltpu

F32 = jnp.float32
BF16 = jnp.bfloat16

D_MODEL = 1024
CHUNK = 64
N_PREV_CHUNKS = 8
HEAD_DIM = 64
N_HEADS = 8
D_HEADS = N_HEADS * HEAD_DIM
REL_CLIP = 128
DECAY_LORA = 64
AAA_LORA = 64
GATE_LORA = 160
N_GROUPS = 4
EXPERTS_PER_GROUP = 8
N_EXPERTS = N_GROUPS * EXPERTS_PER_GROUP
D_EXPERT = 256
DEPTH = 2
DEEPNORM_ALPHA = (2 * DEPTH) ** 0.25
LN_EPS = 1e-5
GN_EPS = 64e-5
NEG_INF = -1e30

LANES = 128
SUBLANES = 8
PAIR = 2 * HEAD_DIM
GRP_HEADS = 2
GRP = GRP_HEADS * HEAD_DIM
SUMMARY_BATCH = 16
ATTN_QBLK = 2 * CHUNK
ATTN_WIN = (N_PREV_CHUNKS + 2) * CHUNK
ATTN_GROUP = 4
ATTN_STEP_BLOCKS = 2
RW_COLS_PAD = 2048
VMEM_LIMIT = 56 * 1024 * 1024


def _cparams(*sem):
    return pltpu.CompilerParams(dimension_semantics=sem, vmem_limit_bytes=VMEM_LIMIT)


def _dot(a, b):
    return jnp.dot(a, b, preferred_element_type=F32)


def _dot_nt(a, b):
    return lax.dot_general(a, b, (((1,), (1,)), ((), ())), preferred_element_type=F32)


def _split2(x):
    hi = x.astype(BF16)
    lo = (x - hi.astype(F32)).astype(BF16)
    return hi, lo


def _dot3(a, b):
    ah, al = _split2(a)
    bh, bl = _split2(b)
    return _dot(ah, bh) + (_dot(al, bh) + _dot(ah, bl))


def _dot_exact_lhs(a_bf16, x):
    x1 = x.astype(BF16)
    r1 = x - x1.astype(F32)
    x2 = r1.astype(BF16)
    x3 = (r1 - x2.astype(F32)).astype(BF16)
    return _dot(a_bf16, x1) + (_dot(a_bf16, x2) + _dot(a_bf16, x3))


def _seg_sum(x):
    left = lax.broadcasted_iota(jnp.int32, (x.shape[0], PAIR), 1) < HEAD_DIM
    outs = []
    for p in range(x.shape[1] // PAIR):
        xp = x[:, p * PAIR:(p + 1) * PAIR]
        s0 = jnp.sum(jnp.where(left, xp, 0.0), axis=-1, keepdims=True)
        s1 = jnp.sum(jnp.where(left, 0.0, xp), axis=-1, keepdims=True)
        outs.append(jnp.where(left, s0, s1))
    return jnp.concatenate(outs, axis=-1)


def _layer_norm(x, g, b):
    mu = jnp.mean(x, axis=-1, keepdims=True)
    xc = x - mu
    var = jnp.mean(xc * xc, axis=-1, keepdims=True)
    return xc * lax.rsqrt(var + LN_EPS) * g + b


def _sigmoid(x):
    return 1.0 / (1.0 + jnp.exp(-x))


def _full(shape):
    nd = len(shape)
    return pl.BlockSpec(shape, lambda *_: (0,) * nd)


def _proj_kernel(x_ref, g_ref, b_ref, wqkv_ref, wrw_ref, wgate_ref, *out_refs, apply_ln):
    x = x_ref[...]
    if apply_ln:
        xn_ref, qkv_ref, rw_ref, gate_ref = out_refs
        x = _layer_norm(x, g_ref[...], b_ref[...])
        xn_ref[...] = x
    else:
        qkv_ref, rw_ref, gate_ref = out_refs
    xb = x.astype(BF16)
    qkv_ref[...] = _dot(xb, wqkv_ref[...]).astype(BF16)
    rw_ref[...] = _dot(xb, wrw_ref[...])
    gate_ref[...] = _sigmoid(_dot(xb, wgate_ref[...])).astype(BF16)


def _proj(x, g, b, wqkv, wrw, wgate, apply_ln):
    t = x.shape[0]
    tm = min(512, t)
    row = lambda w: pl.BlockSpec((tm, w), lambda i: (i, 0))
    held = lambda a: pl.BlockSpec(a.shape, lambda i: (0,) * a.ndim, pipeline_mode=pl.Buffered(1))
    out_shape = [jax.ShapeDtypeStruct((t, 3 * D_HEADS), BF16),
                 jax.ShapeDtypeStruct((t, RW_COLS_PAD), F32),
                 jax.ShapeDtypeStruct((t, 2 * D_MODEL), BF16)]
    out_specs = [row(3 * D_HEADS), row(RW_COLS_PAD), row(2 * D_MODEL)]
    if apply_ln:
        out_shape = [jax.ShapeDtypeStruct((t, D_MODEL), F32)] + out_shape
        out_specs = [row(D_MODEL)] + out_specs
    return pl.pallas_call(
        functools.partial(_proj_kernel, apply_ln=apply_ln),
        grid=(t // tm,),
        in_specs=[row(D_MODEL), _full(g.shape), _full(b.shape), held(wqkv), held(wrw), held(wgate)],
        out_specs=out_specs,
        out_shape=out_shape,
        compiler_params=_cparams("parallel"),
        name="proj_ln" if apply_ln else "proj",
    )(x, g, b, wqkv, wrw, wgate)


def _attn_kernel(q_ref, k_ref, v_ref, *rest):
    bias_refs, o_ref = rest[:-1], rest[-1]
    for sb, bias_ref in enumerate(bias_refs):
        rows = slice(sb * ATTN_QBLK, (sb + 1) * ATTN_QBLK)
        qb = pl.program_id(1) * ATTN_STEP_BLOCKS + sb
        o_ref[0, rows, :] = _attn_block(qb, q_ref[0, rows, :], k_ref, v_ref, bias_ref)


def _attn_block(qb, q, k_ref, v_ref, bias_ref):
    ws = pl.multiple_of(jnp.maximum(qb * ATTN_QBLK - N_PREV_CHUNKS * CHUNK, 0), ATTN_QBLK)
    kw = k_ref[0, pl.ds(ws, ATTN_WIN), :]
    vw = v_ref[0, pl.ds(ws, ATTN_WIN), :]
    left = lax.broadcasted_iota(jnp.int32, (ATTN_QBLK, PAIR), 1) < HEAD_DIM
    zero = jnp.zeros((ATTN_QBLK, PAIR), BF16)
    heads = [(h // 2, h % 2) for h in range(N_HEADS)]
    spans = [slice(p * PAIR, (p + 1) * PAIR) for p in range(N_HEADS // 2)]
    o = []
    for g0 in range(0, N_HEADS, ATTN_GROUP):
        grp = list(enumerate(heads))[g0:g0 + ATTN_GROUP]
        qm = [jnp.where(left, q[:, spans[p]], zero) if side == 0 else jnp.where(left, zero, q[:, spans[p]])
              for _, (p, side) in grp]
        s = [_dot_nt(x, kw[:, spans[p]]) + bias_ref[0, h] for x, (h, (p, _)) in zip(qm, grp)]
        m = [jnp.max(x, axis=-1, keepdims=True) for x in s]
        e = [jnp.exp(x - mx) for x, mx in zip(s, m)]
        inv = [1.0 / jnp.sum(x, axis=-1, keepdims=True) for x in e]
        o += [_dot(x.astype(BF16), vw[:, spans[p]]) * r for x, r, (_, (p, _)) in zip(e, inv, grp)]
    pairs = [jnp.where(left, o[2 * p], o[2 * p + 1]) for p in range(N_HEADS // 2)]
    return jnp.concatenate(pairs, axis=-1).astype(BF16)


def _attention(qkv, bias_tabs, batch, seq):
    qkv3 = qkv.reshape(batch, seq, 3 * D_HEADS)
    n_tabs = bias_tabs.shape[0]
    step_rows = ATTN_STEP_BLOCKS * ATTN_QBLK

    def table_spec(sb):
        return pl.BlockSpec((1, N_HEADS, ATTN_QBLK, ATTN_WIN),
                            lambda b, i: (jnp.minimum(i * ATTN_STEP_BLOCKS + sb, n_tabs - 1), 0, 0, 0))

    out = pl.pallas_call(
        _attn_kernel,
        grid=(batch, seq // step_rows),
        in_specs=[
            pl.BlockSpec((1, step_rows, D_HEADS), lambda b, i: (b, i, 0)),
            pl.BlockSpec((1, seq, D_HEADS), lambda b, i: (b, 0, 1)),
            pl.BlockSpec((1, seq, D_HEADS), lambda b, i: (b, 0, 2)),
        ] + [table_spec(sb) for sb in range(ATTN_STEP_BLOCKS)],
        out_specs=pl.BlockSpec((1, step_rows, D_HEADS), lambda b, i: (b, i, 0)),
        out_shape=jax.ShapeDtypeStruct((batch, seq, D_HEADS), BF16),
        compiler_params=_cparams("parallel", "arbitrary"),
        name="band_attn",
    )(qkv3, qkv3, qkv3, *([bias_tabs] * ATTN_STEP_BLOCKS))
    return out.reshape(batch * seq, D_HEADS)


def _attn_bias_tables(rel_bias):
    n_tabs = N_PREV_CHUNKS * CHUNK // ATTN_QBLK + 1
    start = (jnp.arange(n_tabs) * ATTN_QBLK)[:, None, None]
    qpos = start + jnp.arange(ATTN_QBLK)[None, :, None]
    kpos = jnp.arange(ATTN_WIN)[None, None, :]
    qc, kc = qpos // CHUNK, kpos // CHUNK
    valid = (kc <= qc) & (kc >= qc - N_PREV_CHUNKS)
    back = N_PREV_CHUNKS * CHUNK
    width = back + ATTN_WIN
    length = ATTN_QBLK - 1 + width
    dist = back + (ATTN_QBLK - 1) - jnp.arange(length)
    vals = rel_bias.astype(F32)[:, jnp.clip(dist, -REL_CLIP, REL_CLIP) + REL_CLIP]
    tiled = jnp.tile(_pad_to(vals, 1, length + 1), (1, ATTN_QBLK))
    cut = tiled[:, :ATTN_QBLK * length].reshape(N_HEADS, ATTN_QBLK, length)
    big = cut[:, :, ATTN_QBLK - 1:ATTN_QBLK - 1 + width]
    bias = jnp.stack([big[:, :, back - t * ATTN_QBLK:back - t * ATTN_QBLK + ATTN_WIN] for t in range(n_tabs)],
                     axis=0)
    return jnp.where(valid[:, None], bias, NEG_INF)


def _rwkv_chunk_kernel(*refs, seq, tm, has_vres):
    if has_vres:
        (rw_ref, prev_ref, mu_ref, dbase_ref, dup_ref, abase_ref, aup_ref, gup_ref, kk_ref, ka_ref,
         rk_ref, tri_ref, vfirst_ref, vbase_ref, vdown_ref, vup_ref,
         q1_ref, y0_ref, m_ref, n0_ref, g_ref, bonus_ref,
         rw_s, aw_s, bi_s, ki_s, be_s, ke_s, v_s, wc_s) = refs
    else:
        (rw_ref, prev_ref, mu_ref, dbase_ref, dup_ref, abase_ref, aup_ref, gup_ref, kk_ref, ka_ref,
         rk_ref, tri_ref,
         q1_ref, y0_ref, m_ref, n0_ref, g_ref, bonus_ref, vfirst_out_ref,
         rw_s, aw_s, bi_s, ki_s, be_s, ke_s, v_s, wc_s) = refs

    cols = rw_ref[...]
    first = (pl.program_id(0) % (seq // tm)) == 0
    prev_row = jnp.where(first, 0.0, prev_ref[SUBLANES - 1:SUBLANES, :])
    rows = lax.broadcasted_iota(jnp.int32, cols.shape, 0)
    prev = jnp.where(rows == 0, prev_row, pltpu.roll(cols, 1, axis=0))
    xs = cols + (prev - cols) * mu_ref[...]

    d = D_HEADS
    r, k, v = xs[:, 0:d], xs[:, d:2 * d], xs[:, 2 * d:3 * d]
    wd = xs[:, 3 * d:3 * d + LANES]
    ad = xs[:, 3 * d + LANES:3 * d + 2 * LANES]
    gd = xs[:, 3 * d + 2 * LANES:3 * d + 4 * LANES]

    u = dbase_ref[...] + _dot3(jnp.tanh(wd), dup_ref[...])
    w_log = jnp.minimum(u, 0.0) - jnp.log(1.0 + jnp.exp(-jnp.abs(u))) - 0.5
    logw = -jnp.exp(w_log)
    a_icl = _sigmoid(abase_ref[...] + _dot3(ad, aup_ref[...]))
    g_ref[...] = _dot3(_sigmoid(gd), gup_ref[...]).astype(BF16)
    kk = k * kk_ref[...]
    kk = kk * lax.rsqrt(jnp.maximum(_seg_sum(kk * kk), 1e-24))
    k2 = k * (1.0 + (a_icl - 1.0) * ka_ref[...])
    if has_vres:
        low = _dot(v.astype(BF16), vdown_ref[...])
        mix = _sigmoid(vbase_ref[...] + _dot(low.astype(BF16), vup_ref[...]))
        v2 = v + (vfirst_ref[...] - v) * mix
    else:
        v2 = v
        vfirst_out_ref[...] = v
    a_vec = -kk
    b_vec = kk * a_icl
    bonus_ref[...] = (_seg_sum(r * k2 * rk_ref[...]) * v2).astype(BF16)

    cw = _dot_exact_lhs(tri_ref[...], logw)
    row_chunk = lax.broadcasted_iota(jnp.int32, (tm, tm), 0) // CHUNK
    col_chunk = lax.broadcasted_iota(jnp.int32, (tm, tm), 1) // CHUNK
    cw_tot = _dot_exact_lhs(jnp.where(row_chunk == col_chunk, 1.0, 0.0).astype(BF16), logw)
    rw_s[...] = r * jnp.exp(cw)
    aw_s[...] = a_vec * jnp.exp(cw - logw)
    inv = jnp.exp(-cw)
    bi_s[...] = b_vec * inv
    ki_s[...] = k2 * inv
    rest = jnp.exp(cw_tot - cw)
    be_s[...] = b_vec * rest
    ke_s[...] = k2 * rest
    v_s[...] = v2
    wc_s[...] = jnp.exp(cw_tot)

    shape = (CHUNK, GRP)
    ti = lax.broadcasted_iota(jnp.int32, shape, 0)
    li = lax.broadcasted_iota(jnp.int32, shape, 1)
    si = li & (HEAD_DIM - 1)
    hid = li // HEAD_DIM
    strict = ti > si
    incl = ti >= si
    eye = ti == si

    def bd(x):
        z = jnp.zeros_like(x)
        return jnp.concatenate([jnp.where(hid == h, x, z) for h in range(GRP_HEADS)], axis=0)

    def tpair(x):
        return jnp.concatenate([x[:, h * HEAD_DIM:(h + 1) * HEAD_DIM].T for h in range(GRP_HEADS)], axis=1)

    def b16(xs):
        return [x.astype(BF16) for x in xs]

    def summaries(insts):
        sl = [(slice(c * CHUNK, (c + 1) * CHUNK), slice(p * GRP, (p + 1) * GRP)) for c, p in insts]
        rw = [rw_s[s] for s in sl]
        awb = b16([aw_s[s] for s in sl])
        vb = b16([v_s[s] for s in sl])
        lhs = [jnp.concatenate([a, r.astype(BF16)], axis=0) for a, r in zip(awb, rw)]
        bik = [jnp.concatenate([bd(b), bd(k)], axis=0)
               for b, k in zip(b16([bi_s[s] for s in sl]), b16([ki_s[s] for s in sl]))]
        g = [_dot_nt(l, x) for l, x in zip(lhs, bik)]
        a_ab = [jnp.where(strict, x[:CHUNK, :GRP], 0.0) for x in g]
        a_rb = b16([jnp.where(incl, x[CHUNK:, :GRP], 0.0) for x in g])
        a_ak = b16([jnp.where(strict, x[:CHUNK, GRP:], 0.0) for x in g])
        a_rk = b16([jnp.where(incl, x[CHUNK:, GRP:], 0.0) for x in g])
        bdv = [bd(x) for x in vb]
        x0 = [_dot(a, v) for a, v in zip(a_ak, bdv)]
        tinv = [jnp.where(eye, 1.0, a) for a in a_ab]
        lpb = b16(a_ab)
        lpb = b16([_dot(l, bd(l)) for l in lpb])
        for _ in range(4):
            both = [_dot(jnp.concatenate([t.astype(BF16), l], axis=0), bd(l)) for t, l in zip(tinv, lpb)]
            tinv = [t + x[:CHUNK] for t, x in zip(tinv, both)]
            lpb = b16([x[CHUNK:] for x in both])
        tinv = [t + _dot(t.astype(BF16), bd(l)) for t, l in zip(tinv, lpb)]
        tb = b16(tinv)
        pu = b16([_dot(t, jnp.concatenate([bd(a), bd(x.astype(BF16))], axis=1))
                  for t, a, x in zip(tb, awb, x0)])
        pu_bd = [jnp.concatenate([bd(x[:, :GRP]), bd(x[:, GRP:])], axis=1) for x in pu]
        qy = [_dot(a, r) for a, r in zip(a_rb, pu_bd)]
        q1 = [r + x[:, :GRP] for r, x in zip(rw, qy)]
        y0 = [x[:, GRP:] + _dot(k, v) for x, k, v in zip(qy, a_rk, bdv)]
        bet = b16([tpair(be_s[s]) for s in sl])
        ket = b16([tpair(ke_s[s]) for s in sl])
        mn = [_dot(b, r) for b, r in zip(bet, pu_bd)]
        mm = [x[:, :GRP] for x in mn]
        nn = [x[:, GRP:] + _dot(k, v) for x, k, v in zip(mn, ket, bdv)]
        for i, (s, (c, p)) in enumerate(zip(sl, insts)):
            wc = wc_s[c * CHUNK:c * CHUNK + 1, s[1]]
            q1_ref[s] = q1[i]
            y0_ref[s] = y0[i]
            m_ref[s] = jnp.where(eye, wc, 0.0) + mm[i]
            n0_ref[s] = nn[i]

    insts = [(c, p) for c in range(tm // CHUNK) for p in range(D_HEADS // GRP)]
    for i0 in range(0, len(insts), SUMMARY_BATCH):
        summaries(insts[i0:i0 + SUMMARY_BATCH])


def _rwkv_chunk(rw, lp, seq, v_first):
    t = rw.shape[0]
    tm = 256
    has_vres = v_first is not None
    row = lambda w: pl.BlockSpec((tm, w), lambda i: (i, 0))
    prev_spec = pl.BlockSpec((SUBLANES, RW_COLS_PAD), lambda i: (jnp.maximum(i * (tm // SUBLANES) - 1, 0), 0))
    params = [lp["mu"], lp["decay_base"], lp["decay_up"], lp["aaa_base"], lp["aaa_up"], lp["gate_up"],
              lp["k_k"], lp["k_a"], lp["r_k"], lp["tri"]]
    inputs = [rw, rw] + params
    in_specs = [row(RW_COLS_PAD), prev_spec] + [_full(p.shape) for p in params]
    if has_vres:
        extra = [lp["vres_base"], lp["vres_down"], lp["vres_up"]]
        inputs += [v_first] + extra
        in_specs += [row(D_HEADS)] + [_full(p.shape) for p in extra]
    tok_shape = jax.ShapeDtypeStruct((t, D_HEADS), F32)
    half_shape = jax.ShapeDtypeStruct((t, D_HEADS), BF16)
    out_shape = [tok_shape] * 4 + [half_shape] * 2
    out_specs = [row(D_HEADS)] * 6
    if not has_vres:
        out_shape.append(tok_shape)
        out_specs.append(row(D_HEADS))
    scratch = [pltpu.VMEM((tm, D_HEADS), F32) for _ in range(8)]
    return pl.pallas_call(
        functools.partial(_rwkv_chunk_kernel, seq=seq, tm=tm, has_vres=has_vres),
        grid=(t // tm,),
        in_specs=in_specs,
        out_specs=out_specs,
        out_shape=out_shape,
        scratch_shapes=scratch,
        compiler_params=_cparams("parallel"),
        name="rwkv_chunk_vres" if has_vres else "rwkv_chunk",
    )(*inputs)


def _rwkv_scan_kernel(q1_ref, y0_ref, m_ref, n0_ref, y_ref, st_ref, *, n_chunks):
    @pl.when(pl.program_id(1) == 0)
    def _():
        st_ref[...] = jnp.zeros_like(st_ref)

    left = lax.broadcasted_iota(jnp.int32, (CHUNK, PAIR), 1) < HEAD_DIM

    def bd(x):
        z = jnp.zeros_like(x)
        return jnp.concatenate([jnp.where(left, x, z), jnp.where(left, z, x)], axis=0)

    n_pairs = D_HEADS // PAIR
    st = [st_ref[:, p * PAIR:(p + 1) * PAIR] for p in range(n_pairs)]
    for c in range(n_chunks):
        rs = slice(c * CHUNK, (c + 1) * CHUNK)
        for p in range(n_pairs):
            ls = slice(p * PAIR, (p + 1) * PAIR)
            lh, ll = _split2(jnp.concatenate([q1_ref[rs, ls], m_ref[rs, ls]], axis=0))
            sh, sl = _split2(st[p])
            bh = bd(sh)
            res = _dot(lh, bh) + (_dot(ll, bh) + _dot(lh, bd(sl)))
            y_ref[rs, ls] = res[:CHUNK] + y0_ref[rs, ls]
            st[p] = res[CHUNK:] + n0_ref[rs, ls]
    for p in range(n_pairs):
        st_ref[:, p * PAIR:(p + 1) * PAIR] = st[p]


def _rwkv_scan(q1, y0, m, n0, batch, seq):
    t = q1.shape[0]
    cb = 8
    steps = seq // (cb * CHUNK)
    tok_spec = pl.BlockSpec((cb * CHUNK, D_HEADS), lambda b, j: (b * steps + j, 0))
    return pl.pallas_call(
        functools.partial(_rwkv_scan_kernel, n_chunks=cb),
        grid=(batch, steps),
        in_specs=[tok_spec] * 4,
        out_specs=tok_spec,
        out_shape=jax.ShapeDtypeStruct((t, D_HEADS), F32),
        scratch_shapes=[pltpu.VMEM((HEAD_DIM, D_HEADS), F32)],
        compiler_params=_cparams("parallel", "arbitrary"),
        name="rwkv_scan",
    )(q1, y0, m, n0)


def _mix_kernel(y_ref, g_ref, bonus_ref, ya_ref, gate_ref, x_ref, wa_ref, wb_ref, wo_ref,
                gng_ref, gnb_ref, lng_ref, lnb_ref, o_ref):
    y = y_ref[...]
    mean = _seg_sum(y) * (1.0 / HEAD_DIM)
    yc = y - mean
    var = _seg_sum(yc * yc) * (1.0 / HEAD_DIM)
    yn = yc * lax.rsqrt(var + GN_EPS) * gng_ref[...] + gnb_ref[...]
    yr = (yn + bonus_ref[...].astype(F32)) * g_ref[...].astype(F32)
    za = _dot(ya_ref[...], wa_ref[...])
    zb = _dot(yr.astype(BF16), wb_ref[...])
    gate = gate_ref[...]
    mixed_in = gate[:, :D_MODEL].astype(F32) * za + gate[:, D_MODEL:].astype(F32) * zb
    mixed = _dot(mixed_in.astype(BF16), wo_ref[...])
    o_ref[...] = _layer_norm(DEEPNORM_ALPHA * x_ref[...] + mixed, lng_ref[...], lnb_ref[...])


def _mix(y, g, bonus, y_attn, gate, x, lp):
    t = x.shape[0]
    tm = min(1024, t)
    row = lambda w: pl.BlockSpec((tm, w), lambda i: (i, 0))
    params = [lp["w_a"], lp["w_b"], lp["w_o"], lp["gn_g"], lp["gn_b"], lp["ln_g0"], lp["ln_b0"]]
    return pl.pallas_call(
        _mix_kernel,
        grid=(t // tm,),
        in_specs=[row(D_HEADS), row(D_HEADS), row(D_HEADS), row(D_HEADS), row(2 * D_MODEL), row(D_MODEL)]
        + [_full(p.shape) for p in params],
        out_specs=row(D_MODEL),
        out_shape=jax.ShapeDtypeStruct((t, D_MODEL), F32),
        compiler_params=_cparams("parallel"),
        name="branch_mix",
    )(y, g, bonus, y_attn, gate, x, *params)


MOE_SLAB = 2048
MOE_WIN = 160
MOE_EBLK = 4
MOE_ROWS = 2 * MOE_SLAB + MOE_WIN
RANK_BLK = 256
ROUTE_SLOTS = 2


def _route_kernel(x_ref, wr_ref, br_ref, tri_ref, w_ref, pos_ref, seg_ref):
    x = x_ref[...]
    tn = x.shape[0]
    lane = lax.broadcasted_iota(jnp.int32, (tn, LANES), 1)
    lanef = lane.astype(F32)
    hp = lax.Precision.HIGHEST
    elog = _dot3(x, wr_ref[...]) + br_ref[...]
    big = float(LANES)
    glog = jnp.where((lane >= N_EXPERTS) & (lane < N_EXPERTS + N_GROUPS), elog, -jnp.inf)
    gmax = jnp.max(glog, axis=-1, keepdims=True)
    g_gate = 1.0 / jnp.sum(jnp.exp(glog - gmax), axis=-1, keepdims=True)
    grp = jnp.min(jnp.where(glog == gmax, lanef, big), axis=-1, keepdims=True) - N_EXPERTS
    in_grp = (lanef >= grp * EXPERTS_PER_GROUP) & (lanef < (grp + 1.0) * EXPERTS_PER_GROUP)
    sel = jnp.where(in_grp, elog, -jnp.inf)
    v1 = jnp.max(sel, axis=-1, keepdims=True)
    i1 = jnp.min(jnp.where(sel == v1, lanef, big), axis=-1, keepdims=True)
    sel2 = jnp.where(lanef == i1, -jnp.inf, sel)
    v2 = jnp.max(sel2, axis=-1, keepdims=True)
    i2 = jnp.min(jnp.where(sel2 == v2, lanef, big), axis=-1, keepdims=True)
    e2 = jnp.exp(v2 - v1)
    w1 = g_gate / (1.0 + e2)
    w2 = g_gate * e2 / (1.0 + e2)
    tile = (SUBLANES, LANES)
    pick = jnp.where(lax.broadcasted_iota(jnp.int32, tile, 0) == lax.broadcasted_iota(jnp.int32, tile, 1), 1.0, 0.0)

    def token_rows(c0, c1):
        cols = jnp.where(lane == 0, c0, jnp.where(lane == 1, c1, 0.0))
        rows = lax.dot_general(pick, cols, (((1,), (1,)), ((), ())), precision=hp, preferred_element_type=F32)
        return jnp.concatenate([rows[0:1], rows[1:2]], axis=1)

    w_ref[0] = token_rows(w1, w2)

    hit1 = lanef == i1
    hit2 = lanef == i2
    onehot = jnp.where(hit1 | hit2, 1.0, 0.0)
    tri = tri_ref[...]
    carry = jnp.zeros((1, LANES), F32)
    ranks = []
    for j in range(tn // RANK_BLK):
        blk = onehot[j * RANK_BLK:(j + 1) * RANK_BLK]
        ranks.append(_dot(tri, blk.astype(BF16)) + carry)
        carry = carry + jnp.sum(blk, axis=0, keepdims=True)
    rank = jnp.concatenate(ranks, axis=0)
    cnt = carry
    li = lax.broadcasted_iota(jnp.int32, (LANES, LANES), 0)
    lj = lax.broadcasted_iota(jnp.int32, (LANES, LANES), 1)
    before = jnp.where(li < lj, 1.0, 0.0)
    off = jnp.dot(jnp.broadcast_to(cnt, tile), before, precision=hp, preferred_element_type=F32)[0:1]
    slot = (off + rank) * SUBLANES
    pos1 = jnp.sum(jnp.where(hit1, slot, 0.0), axis=-1, keepdims=True)
    pos2 = jnp.sum(jnp.where(hit2, slot, 0.0), axis=-1, keepdims=True)
    pos_ref[0] = token_rows(pos1, pos2).astype(jnp.int32)
    seg_row = lax.broadcasted_iota(jnp.int32, tile, 0)
    seg_ref[0] = jnp.where(seg_row == 0, off, jnp.where(seg_row == 1, cnt, 0.0)).astype(jnp.int32)


def _route(x, lp):
    t = x.shape[0]
    tn = min(MOE_SLAB, t)
    n_slabs = t // tn
    small = [lp["w_route"], lp["b_route"], lp["tri_rank"]]
    row = lambda w: pl.BlockSpec((tn, w), lambda i: (i, 0))
    return pl.pallas_call(
        _route_kernel,
        grid=(n_slabs,),
        in_specs=[row(D_MODEL)] + [_full(p.shape) for p in small],
        out_specs=[pl.BlockSpec((1, 1, ROUTE_SLOTS * tn), lambda i: (i, 0, 0)),
                   pl.BlockSpec((1, 1, ROUTE_SLOTS * tn), lambda i: (i, 0, 0)),
                   pl.BlockSpec((1, SUBLANES, LANES), lambda i: (i, 0, 0))],
        out_shape=[jax.ShapeDtypeStruct((n_slabs, 1, ROUTE_SLOTS * tn), F32),
                   jax.ShapeDtypeStruct((n_slabs, 1, ROUTE_SLOTS * tn), jnp.int32),
                   jax.ShapeDtypeStruct((n_slabs, SUBLANES, LANES), jnp.int32)],
        compiler_params=_cparams("parallel"),
        name="moe_route",
    )(x, *small)


def _moe_sparse_kernel(off_ref, cnt_ref, pos_ref, w_ref, x_ref, eg_ref, eu_ref, ed_ref, lng_ref, lnb_ref,
                       p_ref, wp_ref, wg_ref, lng2_ref, lnb2_ref, o_ref, xs_ref):
    s = pl.program_id(0)
    e = pl.program_id(1)
    tn = x_ref.shape[0]

    n_feat = D_MODEL // LANES

    def tile_rows(first_row):
        return pl.ds(pl.multiple_of(first_row, SUBLANES), SUBLANES)

    @pl.when((s == 0) & (e == 0))
    def _():
        xs_ref[pl.ds(2 * tn * SUBLANES, MOE_WIN * SUBLANES), :] = jnp.zeros((MOE_WIN * SUBLANES, LANES), F32)

    @pl.when(e == 0)
    def _():
        def scatter(i, carry):
            t0 = pl.multiple_of(i * SUBLANES, SUBLANES)
            tiles = x_ref[pl.ds(t0, SUBLANES), :].reshape(SUBLANES, n_feat, LANES)
            for j in range(SUBLANES):
                xs_ref[tile_rows(pos_ref[0, 0, t0 + j]), :] = tiles[j]
                xs_ref[tile_rows(pos_ref[0, 0, tn + t0 + j]), :] = tiles[j]
            return carry
        lax.fori_loop(0, tn // SUBLANES, scatter, 0)

    ridx = lax.broadcasted_iota(jnp.int32, (MOE_WIN, 1), 0)

    base = s * N_EXPERTS + e * MOE_EBLK
    offs = [off_ref[base + j] for j in range(MOE_EBLK)]
    cnts = [cnt_ref[base + j] for j in range(MOE_EBLK)]

    def feat_rows(j, w, f):
        return pl.ds((offs[j] + w * MOE_WIN) * SUBLANES + f, MOE_WIN, stride=SUBLANES)

    def load_win(j, w):
        return jnp.concatenate([xs_ref[feat_rows(j, w, f), :] for f in range(n_feat)], axis=1)

    def hidden(j, xw):
        xb = xw.astype(BF16)
        hg = _dot(xb, eg_ref[j])
        hu = _dot(xb, eu_ref[j])
        return (hg * _sigmoid(hg) * hu).astype(BF16)

    def project(j, w, xw, hh):
        yw = _dot(hh, ed_ref[j])
        return jnp.where(ridx + w * MOE_WIN < cnts[j], yw, xw)

    def expert(j, w, xw):
        return project(j, w, xw, hidden(j, xw))

    def store_win(j, w, yw):
        for f in range(n_feat):
            xs_ref[feat_rows(j, w, f), :] = yw[:, f * LANES:(f + 1) * LANES]

    xw = [None] * MOE_EBLK
    hh = [None] * MOE_EBLK
    for j in range(MOE_EBLK + 1):
        if j < MOE_EBLK:
            xw[j] = load_win(j, 0)
            hh[j] = hidden(j, xw[j])
        if j > 0:
            store_win(j - 1, 0, project(j - 1, 0, xw[j - 1], hh[j - 1]))

    for j in range(MOE_EBLK):
        def window(w, carry, j=j):
            store_win(j, w, expert(j, w, load_win(j, w)))
            return carry
        lax.fori_loop(1, (cnts[j] + MOE_WIN - 1) // MOE_WIN, window, 0)

    @pl.when(e == N_EXPERTS // MOE_EBLK - 1)
    def _():
        def combine(i, carry):
            t0 = pl.multiple_of(i * SUBLANES, SUBLANES)
            tiles = []
            for j in range(SUBLANES):
                y1 = xs_ref[tile_rows(pos_ref[0, 0, t0 + j]), :]
                y2 = xs_ref[tile_rows(pos_ref[0, 0, tn + t0 + j]), :]
                tiles.append(w_ref[0, 0, t0 + j] * y1 + w_ref[0, 0, tn + t0 + j] * y2)
            o_ref[pl.ds(t0, SUBLANES), :] = jnp.stack(tiles, axis=0).reshape(SUBLANES, D_MODEL)
            return carry
        lax.fori_loop(0, tn // SUBLANES, combine, 0)
        def finish(j, carry):
            rs = pl.ds(pl.multiple_of(j * RANK_BLK, RANK_BLK), RANK_BLK)
            x2 = _layer_norm(DEEPNORM_ALPHA * x_ref[rs, :] + o_ref[rs, :], lng_ref[...], lnb_ref[...])
            emb = _dot(p_ref[rs, :].astype(BF16), wp_ref[...])
            gate = _sigmoid(_dot(x2.astype(BF16), wg_ref[...]))
            o_ref[rs, :] = _layer_norm(DEEPNORM_ALPHA * x2 + emb * gate, lng2_ref[...], lnb2_ref[...])
            return carry
        lax.fori_loop(0, tn // RANK_BLK, finish, 0)


def _moe_sparse(x, p, layer, lp):
    t = x.shape[0]
    tn = min(MOE_SLAB, t)
    n_slabs = t // tn
    w, pos, seg = _route(x, lp)
    off = seg[:, 0, :N_EXPERTS].reshape(-1)
    cnt = seg[:, 1, :N_EXPERTS].reshape(-1)
    smem_row = pl.BlockSpec((1, 1, ROUTE_SLOTS * tn), lambda s, e, *_: (s, 0, 0), memory_space=pltpu.SMEM)
    slab = pl.BlockSpec((tn, D_MODEL), lambda s, e, *_: (s, 0), pipeline_mode=pl.Buffered(1))
    const = lambda a: pl.BlockSpec(a.shape, lambda s, e, *_: (0,) * a.ndim, pipeline_mode=pl.Buffered(1))
    p_slab = pl.BlockSpec((None, tn, p.shape[2]), lambda s, e, *_: (layer, s, 0), pipeline_mode=pl.Buffered(1))
    tail = [lp["w_ple"], lp["w_pg"], lp["ln_g2"], lp["ln_b2"]]
    grid_spec = pltpu.PrefetchScalarGridSpec(
        num_scalar_prefetch=2,
        grid=(n_slabs, N_EXPERTS // MOE_EBLK),
        in_specs=[smem_row, smem_row, slab,
                  pl.BlockSpec((MOE_EBLK, D_MODEL, D_EXPERT), lambda s, e, *_: (e, 0, 0)),
                  pl.BlockSpec((MOE_EBLK, D_MODEL, D_EXPERT), lambda s, e, *_: (e, 0, 0)),
                  pl.BlockSpec((MOE_EBLK, D_EXPERT, D_MODEL), lambda s, e, *_: (e, 0, 0)),
                  const(lp["ln_g1"]), const(lp["ln_b1"]), p_slab] + [const(a) for a in tail],
        out_specs=slab,
        scratch_shapes=[pltpu.VMEM((MOE_ROWS * SUBLANES, LANES), F32)],
    )
    return pl.pallas_call(
        _moe_sparse_kernel,
        grid_spec=grid_spec,
        out_shape=jax.ShapeDtypeStruct((t, D_MODEL), F32),
        compiler_params=_cparams("arbitrary", "arbitrary"),
        name="moe_sparse",
    )(off, cnt, pos, w, x, lp["e_gate"], lp["e_up"], lp["e_down"], lp["ln_g1"], lp["ln_b1"],
      p, *tail)


def _pad_to(a, axis, size):
    pad = [(0, 0)] * a.ndim
    pad[axis] = (0, size - a.shape[axis])
    return jnp.pad(a, pad)


def _row(v):
    return v.reshape(1, -1).astype(F32)


def _layer_params(i, w_in, tok_mix, decay_base, decay_up, aaa_base, aaa_up, gate_up, k_k, k_a, r_k,
                  vres_base, vres_down, vres_up, gn_g, gn_b, w_branch_attn, w_branch_rwkv, w_out,
                  router_grp, router_grp_bias, router_exp, router_exp_bias, exp_gate, exp_up, exp_down,
                  ple_proj, ple_gate, ln_g, ln_b, consts):
    d = D_HEADS
    w = w_in[i]
    rw0 = 3 * d
    gate0 = rw0 + 3 * d + DECAY_LORA + AAA_LORA + GATE_LORA
    wqkv = jnp.concatenate([w[:, 0:d] * (HEAD_DIM ** -0.5), w[:, d:rw0]], axis=1).astype(BF16)
    wr = w[:, rw0:gate0]
    lora0 = 3 * d
    pieces = [wr[:, :lora0],
              _pad_to(wr[:, lora0:lora0 + DECAY_LORA], 1, LANES),
              _pad_to(wr[:, lora0 + DECAY_LORA:lora0 + DECAY_LORA + AAA_LORA], 1, LANES),
              _pad_to(wr[:, lora0 + DECAY_LORA + AAA_LORA:], 1, 2 * LANES)]
    wrw = jnp.concatenate(pieces, axis=1).astype(BF16)
    mu = tok_mix[i]
    mu_pieces = [mu[:lora0],
                 _pad_to(mu[lora0:lora0 + DECAY_LORA], 0, LANES),
                 _pad_to(mu[lora0 + DECAY_LORA:lora0 + DECAY_LORA + AAA_LORA], 0, LANES),
                 _pad_to(mu[lora0 + DECAY_LORA + AAA_LORA:], 0, 2 * LANES)]
    lp = dict(consts)
    lp.update(
        wqkv=wqkv, wrw=wrw, wgate=w[:, gate0:].astype(BF16),
        mu=_row(jnp.concatenate(mu_pieces)),
        decay_base=_row(decay_base[i]), decay_up=_pad_to(decay_up[i], 0, LANES).astype(F32),
        aaa_base=_row(aaa_base[i]), aaa_up=_pad_to(aaa_up[i], 0, LANES).astype(F32),
        gate_up=_pad_to(gate_up[i], 0, 2 * LANES).astype(F32),
        k_k=_row(k_k[i]), k_a=_row(k_a[i]), r_k=_row(r_k[i]),
        gn_g=_row(gn_g[i]), gn_b=_row(gn_b[i]),
        w_a=w_branch_attn[i].astype(BF16), w_b=w_branch_rwkv[i].astype(BF16), w_o=w_out[i].astype(BF16),
        w_route=_pad_to(jnp.concatenate([router_exp[i], router_grp[i]], axis=1), 1, LANES).astype(F32),
        b_route=_row(_pad_to(jnp.concatenate([router_exp_bias[i], router_grp_bias[i]]), 0, LANES)),
        e_gate=exp_gate[i].astype(BF16), e_up=exp_up[i].astype(BF16), e_down=exp_down[i].astype(BF16),
        w_ple=ple_proj[i].astype(BF16), w_pg=ple_gate[i].astype(BF16),
        ln_g0=_row(ln_g[i, 0]), ln_b0=_row(ln_b[i, 0]),
        ln_g1=_row(ln_g[i, 1]), ln_b1=_row(ln_b[i, 1]),
        ln_g2=_row(ln_g[i, 2]), ln_b2=_row(ln_b[i, 2]),
    )
    if i > 0:
        lp.update(vres_base=_row(vres_base[i - 1]),
                  vres_down=_pad_to(vres_down[i - 1], 1, LANES).astype(BF16),
                  vres_up=_pad_to(vres_up[i - 1], 0, LANES).astype(BF16))
    return lp


def _const_mats(tm):
    tok = jnp.arange(tm)
    same = (tok[:, None] // CHUNK) == (tok[None, :] // CHUNK)
    tri = (same & (tok[:, None] >= tok[None, :])).astype(BF16)
    rtok = jnp.arange(RANK_BLK)
    tri_rank = (rtok[None, :] < rtok[:, None]).astype(BF16)
    return dict(tri=tri, tri_rank=tri_rank)


def kernel(x, p, ln_in_g, ln_in_b, rel_bias, w_in, tok_mix, decay_base, decay_up, aaa_base, aaa_up,
           gate_up, k_k, k_a, r_k, vres_base, vres_down, vres_up, gn_g, gn_b, w_branch_attn,
           w_branch_rwkv, w_out, router_grp, router_grp_bias, router_exp, router_exp_bias, exp_gate,
           exp_up, exp_down, ple_proj, ple_gate, ln_g, ln_b):
    batch, seq, _ = x.shape
    t = batch * seq
    consts = _const_mats(256)
    bias_tabs = _attn_bias_tables(rel_bias)
    xt = x.reshape(t, D_MODEL)
    pt = p.reshape(p.shape[0], t, p.shape[-1])
    v_first = None
    for i in range(DEPTH):
        lp = _layer_params(i, w_in, tok_mix, decay_base, decay_up, aaa_base, aaa_up, gate_up, k_k, k_a,
                           r_k, vres_base, vres_down, vres_up, gn_g, gn_b, w_branch_attn, w_branch_rwkv,
                           w_out, router_grp, router_grp_bias, router_exp, router_exp_bias, exp_gate,
                           exp_up, exp_down, ple_proj, ple_gate, ln_g, ln_b, consts)
        if i == 0:
            xt, qkv, rw, gate = _proj(xt, _row(ln_in_g), _row(ln_in_b), lp["wqkv"], lp["wrw"],
                                      lp["wgate"], apply_ln=True)
        else:
            qkv, rw, gate = _proj(xt, _row(ln_in_g), _row(ln_in_b), lp["wqkv"], lp["wrw"],
                                  lp["wgate"], apply_ln=False)
        y_attn = _attention(qkv, bias_tabs, batch, seq)
        if i == 0:
            q1, y0, m, n0, g, bonus, v_first = _rwkv_chunk(rw, lp, seq, None)
        else:
            q1, y0, m, n0, g, bonus = _rwkv_chunk(rw, lp, seq, v_first)
        y = _rwkv_scan(q1, y0, m, n0, batch, seq)
        xt = _mix(y, g, bonus, y_attn, gate, xt, lp)
        xt = _moe_sparse(xt, pt, i, lp)
    return xt.reshape(batch, seq, D_MODEL)
```

```python
import functools

import jax
import jax.numpy as jnp
from jax import lax
from jax.experimental import pallas as pl
from jax.experimental.pallas import tpu as pltpu

F32 = jnp.float32
BF16 = jnp.bfloat16

D_MODEL = 1024
CHUNK = 64
N_PREV_CHUNKS = 8
HEAD_DIM = 64
N_HEADS = 8
D_HEADS = N_HEADS * HEAD_DIM
REL_CLIP = 128
DECAY_LORA = 64
AAA_LORA = 64
GATE_LORA = 160
N_GROUPS = 4
EXPERTS_PER_GROUP = 8
N_EXPERTS = N_GROUPS * EXPERTS_PER_GROUP
D_EXPERT = 256
DEPTH = 2
DEEPNORM_ALPHA = (2 * DEPTH) ** 0.25
LN_EPS = 1e-5
GN_EPS = 64e-5
NEG_INF = -1e30

LANES = 128
SUBLANES = 8
PAIR = 2 * HEAD_DIM
GRP_HEADS = 2
GRP = GRP_HEADS * HEAD_DIM
SUMMARY_BATCH = 16
ATTN_QBLK = 2 * CHUNK
ATTN_WIN = (N_PREV_CHUNKS + 2) * CHUNK
ATTN_GROUP = 4
ATTN_STEP_BLOCKS = 2
RW_COLS_PAD = 2048
VMEM_LIMIT = 56 * 1024 * 1024


def _cparams(*sem):
    return pltpu.CompilerParams(dimension_semantics=sem, vmem_limit_bytes=VMEM_LIMIT)


def _dot(a, b):
    return jnp.dot(a, b, preferred_element_type=F32)


def _dot_nt(a, b):
    return lax.dot_general(a, b, (((1,), (1,)), ((), ())), preferred_element_type=F32)


def _split2(x):
    hi = x.astype(BF16)
    lo = (x - hi.astype(F32)).astype(BF16)
    return hi, lo


def _dot3(a, b):
    ah, al = _split2(a)
    bh, bl = _split2(b)
    return _dot(ah, bh) + (_dot(al, bh) + _dot(ah, bl))


def _dot_exact_lhs(a_bf16, x):
    x1 = x.astype(BF16)
    r1 = x - x1.astype(F32)
    x2 = r1.astype(BF16)
    x3 = (r1 - x2.astype(F32)).astype(BF16)
    return _dot(a_bf16, x1) + (_dot(a_bf16, x2) + _dot(a_bf16, x3))


def _seg_sum(x):
    left = lax.broadcasted_iota(jnp.int32, (x.shape[0], PAIR), 1) < HEAD_DIM
    outs = []
    for p in range(x.shape[1] // PAIR):
        xp = x[:, p * PAIR:(p + 1) * PAIR]
        s0 = jnp.sum(jnp.where(left, xp, 0.0), axis=-1, keepdims=True)
        s1 = jnp.sum(jnp.where(left, 0.0, xp), axis=-1, keepdims=True)
        outs.append(jnp.where(left, s0, s1))
    return jnp.concatenate(outs, axis=-1)


def _layer_norm(x, g, b):
    mu = jnp.mean(x, axis=-1, keepdims=True)
    xc = x - mu
    var = jnp.mean(xc * xc, axis=-1, keepdims=True)
    return xc * lax.rsqrt(var + LN_EPS) * g + b


def _sigmoid(x):
    return 1.0 / (1.0 + jnp.exp(-x))


def _full(shape):
    nd = len(shape)
    return pl.BlockSpec(shape, lambda *_: (0,) * nd)


def _proj_kernel(x_ref, g_ref, b_ref, wqkv_ref, wrw_ref, *out_refs, apply_ln):
    x = x_ref[...]
    if apply_ln:
        xn_ref, qkv_ref, rw_ref = out_refs
        x = _layer_norm(x, g_ref[...], b_ref[...])
        xn_ref[...] = x
    else:
        qkv_ref, rw_ref = out_refs
    xb = x.astype(BF16)
    qkv_ref[...] = _dot(xb, wqkv_ref[...]).astype(BF16)
    rw_ref[...] = _dot(xb, wrw_ref[...])


def _proj(x, g, b, wqkv, wrw, apply_ln):
    t = x.shape[0]
    tm = min(512, t)
    row = lambda w: pl.BlockSpec((tm, w), lambda i: (i, 0))
    held = lambda a: pl.BlockSpec(a.shape, lambda i: (0,) * a.ndim, pipeline_mode=pl.Buffered(1))
    out_shape = [jax.ShapeDtypeStruct((t, 3 * D_HEADS), BF16),
                 jax.ShapeDtypeStruct((t, RW_COLS_PAD), F32)]
    out_specs = [row(3 * D_HEADS), row(RW_COLS_PAD)]
    if apply_ln:
        out_shape = [jax.ShapeDtypeStruct((t, D_MODEL), F32)] + out_shape
        out_specs = [row(D_MODEL)] + out_specs
    return pl.pallas_call(
        functools.partial(_proj_kernel, apply_ln=apply_ln),
        grid=(t // tm,),
        in_specs=[row(D_MODEL), _full(g.shape), _full(b.shape), held(wqkv), held(wrw)],
        out_specs=out_specs,
        out_shape=out_shape,
        compiler_params=_cparams("parallel"),
        name="proj_ln" if apply_ln else "proj",
    )(x, g, b, wqkv, wrw)


def _attn_kernel(q_ref, k_ref, v_ref, *rest):
    bias_refs, o_ref = rest[:-1], rest[-1]
    for sb, bias_ref in enumerate(bias_refs):
        rows = slice(sb * ATTN_QBLK, (sb + 1) * ATTN_QBLK)
        qb = pl.program_id(1) * ATTN_STEP_BLOCKS + sb
        o_ref[0, rows, :] = _attn_block(qb, q_ref[0, rows, :], k_ref, v_ref, bias_ref)


def _attn_block(qb, q, k_ref, v_ref, bias_ref):
    ws = pl.multiple_of(jnp.maximum(qb * ATTN_QBLK - N_PREV_CHUNKS * CHUNK, 0), ATTN_QBLK)
    kw = k_ref[0, pl.ds(ws, ATTN_WIN), :]
    vw = v_ref[0, pl.ds(ws, ATTN_WIN), :]
    left = lax.broadcasted_iota(jnp.int32, (ATTN_QBLK, PAIR), 1) < HEAD_DIM
    zero = jnp.zeros((ATTN_QBLK, PAIR), BF16)
    heads = [(h // 2, h % 2) for h in range(N_HEADS)]
    spans = [slice(p * PAIR, (p + 1) * PAIR) for p in range(N_HEADS // 2)]
    o = []
    for g0 in range(0, N_HEADS, ATTN_GROUP):
        grp = list(enumerate(heads))[g0:g0 + ATTN_GROUP]
        qm = [jnp.where(left, q[:, spans[p]], zero) if side == 0 else jnp.where(left, zero, q[:, spans[p]])
              for _, (p, side) in grp]
        s = [_dot_nt(x, kw[:, spans[p]]) + bias_ref[0, h] for x, (h, (p, _)) in zip(qm, grp)]
        m = [jnp.max(x, axis=-1, keepdims=True) for x in s]
        e = [jnp.exp(x - mx) for x, mx in zip(s, m)]
        inv = [1.0 / jnp.sum(x, axis=-1, keepdims=True) for x in e]
        o += [_dot(x.astype(BF16), vw[:, spans[p]]) * r for x, r, (_, (p, _)) in zip(e, inv, grp)]
    pairs = [jnp.where(left, o[2 * p], o[2 * p + 1]) for p in range(N_HEADS // 2)]
    return jnp.concatenate(pairs, axis=-1).astype(BF16)


def _attention(qkv, bias_tabs, batch, seq):
    qkv3 = qkv.reshape(batch, seq, 3 * D_HEADS)
    n_tabs = bias_tabs.shape[0]
    step_rows = ATTN_STEP_BLOCKS * ATTN_QBLK

    def table_spec(sb):
        return pl.BlockSpec((1, N_HEADS, ATTN_QBLK, ATTN_WIN),
                            lambda b, i: (jnp.minimum(i * ATTN_STEP_BLOCKS + sb, n_tabs - 1), 0, 0, 0))

    out = pl.pallas_call(
        _attn_kernel,
        grid=(batch, seq // step_rows),
        in_specs=[
            pl.BlockSpec((1, step_rows, D_HEADS), lambda b, i: (b, i, 0)),
            pl.BlockSpec((1, seq, D_HEADS), lambda b, i: (b, 0, 1)),
            pl.BlockSpec((1, seq, D_HEADS), lambda b, i: (b, 0, 2)),
        ] + [table_spec(sb) for sb in range(ATTN_STEP_BLOCKS)],
        out_specs=pl.BlockSpec((1, step_rows, D_HEADS), lambda b, i: (b, i, 0)),
        out_shape=jax.ShapeDtypeStruct((batch, seq, D_HEADS), BF16),
        compiler_params=_cparams("parallel", "arbitrary"),
        name="band_attn",
    )(qkv3, qkv3, qkv3, *([bias_tabs] * ATTN_STEP_BLOCKS))
    return out.reshape(batch * seq, D_HEADS)


def _attn_bias_tables(rel_bias):
    n_tabs = N_PREV_CHUNKS * CHUNK // ATTN_QBLK + 1
    start = (jnp.arange(n_tabs) * ATTN_QBLK)[:, None, None]
    qpos = start + jnp.arange(ATTN_QBLK)[None, :, None]
    kpos = jnp.arange(ATTN_WIN)[None, None, :]
    qc, kc = qpos // CHUNK, kpos // CHUNK
    valid = (kc <= qc) & (kc >= qc - N_PREV_CHUNKS)
    back = N_PREV_CHUNKS * CHUNK
    width = back + ATTN_WIN
    length = ATTN_QBLK - 1 + width
    dist = back + (ATTN_QBLK - 1) - jnp.arange(length)
    vals = rel_bias.astype(F32)[:, jnp.clip(dist, -REL_CLIP, REL_CLIP) + REL_CLIP]
    tiled = jnp.tile(_pad_to(vals, 1, length + 1), (1, ATTN_QBLK))
    cut = tiled[:, :ATTN_QBLK * length].reshape(N_HEADS, ATTN_QBLK, length)
    big = cut[:, :, ATTN_QBLK - 1:ATTN_QBLK - 1 + width]
    bias = jnp.stack([big[:, :, back - t * ATTN_QBLK:back - t * ATTN_QBLK + ATTN_WIN] for t in range(n_tabs)],
                     axis=0)
    return jnp.where(valid[:, None], bias, NEG_INF)


def _rwkv_chunk_kernel(*refs, seq, tm, has_vres):
    if has_vres:
        (rw_ref, prev_ref, mu_ref, dbase_ref, dup_ref, abase_ref, aup_ref, gup_ref, kk_ref, ka_ref,
         rk_ref, tri_ref, vfirst_ref, vbase_ref, vdown_ref, vup_ref,
         q1_ref, y0_ref, m_ref, n0_ref, g_ref, bonus_ref,
         rw_s, aw_s, bi_s, ki_s, be_s, ke_s, v_s, wc_s) = refs
    else:
        (rw_ref, prev_ref, mu_ref, dbase_ref, dup_ref, abase_ref, aup_ref, gup_ref, kk_ref, ka_ref,
         rk_ref, tri_ref,
         q1_ref, y0_ref, m_ref, n0_ref, g_ref, bonus_ref, vfirst_out_ref,
         rw_s, aw_s, bi_s, ki_s, be_s, ke_s, v_s, wc_s) = refs

    cols = rw_ref[...]
    first = (pl.program_id(0) % (seq // tm)) == 0
    prev_row = jnp.where(first, 0.0, prev_ref[SUBLANES - 1:SUBLANES, :])
    rows = lax.broadcasted_iota(jnp.int32, cols.shape, 0)
    prev = jnp.where(rows == 0, prev_row, pltpu.roll(cols, 1, axis=0))
    xs = cols + (prev - cols) * mu_ref[...]

    d = D_HEADS
    r, k, v = xs[:, 0:d], xs[:, d:2 * d], xs[:, 2 * d:3 * d]
    wd = xs[:, 3 * d:3 * d + LANES]
    ad = xs[:, 3 * d + LANES:3 * d + 2 * LANES]
    gd = xs[:, 3 * d + 2 * LANES:3 * d + 4 * LANES]

    u = dbase_ref[...] + _dot3(jnp.tanh(wd), dup_ref[...])
    w_log = jnp.minimum(u, 0.0) - jnp.log(1.0 + jnp.exp(-jnp.abs(u))) - 0.5
    logw = -jnp.exp(w_log)
    a_icl = _sigmoid(abase_ref[...] + _dot3(ad, aup_ref[...]))
    g_ref[...] = _dot3(_sigmoid(gd), gup_ref[...]).astype(BF16)
    kk = k * kk_ref[...]
    kk = kk * lax.rsqrt(jnp.maximum(_seg_sum(kk * kk), 1e-24))
    k2 = k * (1.0 + (a_icl - 1.0) * ka_ref[...])
    if has_vres:
        low = _dot(v.astype(BF16), vdown_ref[...])
        mix = _sigmoid(vbase_ref[...] + _dot(low.astype(BF16), vup_ref[...]))
        v2 = v + (vfirst_ref[...] - v) * mix
    else:
        v2 = v
        vfirst_out_ref[...] = v
    a_vec = -kk
    b_vec = kk * a_icl
    bonus_ref[...] = (_seg_sum(r * k2 * rk_ref[...]) * v2).astype(BF16)

    cw = _dot_exact_lhs(tri_ref[...], logw)
    row_chunk = lax.broadcasted_iota(jnp.int32, (tm, tm), 0) // CHUNK
    col_chunk = lax.broadcasted_iota(jnp.int32, (tm, tm), 1) // CHUNK
    cw_tot = _dot_exact_lhs(jnp.where(row_chunk == col_chunk, 1.0, 0.0).astype(BF16), logw)
    rw_s[...] = r * jnp.exp(cw)
    aw_s[...] = a_vec * jnp.exp(cw - logw)
    inv = jnp.exp(-cw)
    bi_s[...] = b_vec * inv
    ki_s[...] = k2 * inv
    rest = jnp.exp(cw_tot - cw)
    be_s[...] = b_vec * rest
    ke_s[...] = k2 * rest
    v_s[...] = v2
    wc_s[...] = jnp.exp(cw_tot)

    shape = (CHUNK, GRP)
    ti = lax.broadcasted_iota(jnp.int32, shape, 0)
    li = lax.broadcasted_iota(jnp.int32, shape, 1)
    si = li & (HEAD_DIM - 1)
    hid = li // HEAD_DIM
    strict = ti > si
    incl = ti >= si
    eye = ti == si

    def bd(x):
        z = jnp.zeros_like(x)
        return jnp.concatenate([jnp.where(hid == h, x, z) for h in range(GRP_HEADS)], axis=0)

    def tpair(x):
        return jnp.concatenate([x[:, h * HEAD_DIM:(h + 1) * HEAD_DIM].T for h in range(GRP_HEADS)], axis=1)

    def b16(xs):
        return [x.astype(BF16) for x in xs]

    def summaries(insts):
        sl = [(slice(c * CHUNK, (c + 1) * CHUNK), slice(p * GRP, (p + 1) * GRP)) for c, p in insts]
        rw = [rw_s[s] for s in sl]
        awb = b16([aw_s[s] for s in sl])
        vb = b16([v_s[s] for s in sl])
        lhs = [jnp.concatenate([a, r.astype(BF16)], axis=0) for a, r in zip(awb, rw)]
        bik = [jnp.concatenate([bd(b), bd(k)], axis=0)
               for b, k in zip(b16([bi_s[s] for s in sl]), b16([ki_s[s] for s in sl]))]
        g = [_dot_nt(l, x) for l, x in zip(lhs, bik)]
        a_ab = [jnp.where(strict, x[:CHUNK, :GRP], 0.0) for x in g]
        a_rb = b16([jnp.where(incl, x[CHUNK:, :GRP], 0.0) for x in g])
        a_ak = b16([jnp.where(strict, x[:CHUNK, GRP:], 0.0) for x in g])
        a_rk = b16([jnp.where(incl, x[CHUNK:, GRP:], 0.0) for x in g])
        bdv = [bd(x) for x in vb]
        x0 = [_dot(a, v) for a, v in zip(a_ak, bdv)]
        tinv = [jnp.where(eye, 1.0, a) for a in a_ab]
        lpb = b16(a_ab)
        lpb = b16([_dot(l, bd(l)) for l in lpb])
        for _ in range(4):
            both = [_dot(jnp.concatenate([t.astype(BF16), l], axis=0), bd(l)) for t, l in zip(tinv, lpb)]
            tinv = [t + x[:CHUNK] for t, x in zip(tinv, both)]
            lpb = b16([x[CHUNK:] for x in both])
        tinv = [t + _dot(t.astype(BF16), bd(l)) for t, l in zip(tinv, lpb)]
        tb = b16(tinv)
        pu = b16([_dot(t, jnp.concatenate([bd(a), bd(x.astype(BF16))], axis=1))
                  for t, a, x in zip(tb, awb, x0)])
        pu_bd = [jnp.concatenate([bd(x[:, :GRP]), bd(x[:, GRP:])], axis=1) for x in pu]
        qy = [_dot(a, r) for a, r in zip(a_rb, pu_bd)]
        q1 = [r + x[:, :GRP] for r, x in zip(rw, qy)]
        y0 = [x[:, GRP:] + _dot(k, v) for x, k, v in zip(qy, a_rk, bdv)]
        bet = b16([tpair(be_s[s]) for s in sl])
        ket = b16([tpair(ke_s[s]) for s in sl])
        mn = [_dot(b, r) for b, r in zip(bet, pu_bd)]
        mm = [x[:, :GRP] for x in mn]
        nn = [x[:, GRP:] + _dot(k, v) for x, k, v in zip(mn, ket, bdv)]
        for i, (s, (c, p)) in enumerate(zip(sl, insts)):
            wc = wc_s[c * CHUNK:c * CHUNK + 1, s[1]]
            q1_ref[s] = q1[i]
            y0_ref[s] = y0[i]
            m_ref[s] = jnp.where(eye, wc, 0.0) + mm[i]
            n0_ref[s] = nn[i]

    insts = [(c, p) for c in range(tm // CHUNK) for p in range(D_HEADS // GRP)]
    for i0 in range(0, len(insts), SUMMARY_BATCH):
        summaries(insts[i0:i0 + SUMMARY_BATCH])


def _rwkv_chunk(rw, lp, seq, v_first):
    t = rw.shape[0]
    tm = 256
    has_vres = v_first is not None
    row = lambda w: pl.BlockSpec((tm, w), lambda i: (i, 0))
    prev_spec = pl.BlockSpec((SUBLANES, RW_COLS_PAD), lambda i: (jnp.maximum(i * (tm // SUBLANES) - 1, 0), 0))
    params = [lp["mu"], lp["decay_base"], lp["decay_up"], lp["aaa_base"], lp["aaa_up"], lp["gate_up"],
              lp["k_k"], lp["k_a"], lp["r_k"], lp["tri"]]
    inputs = [rw, rw] + params
    in_specs = [row(RW_COLS_PAD), prev_spec] + [_full(p.shape) for p in params]
    if has_vres:
        extra = [lp["vres_base"], lp["vres_down"], lp["vres_up"]]
        inputs += [v_first] + extra
        in_specs += [row(D_HEADS)] + [_full(p.shape) for p in extra]
    tok_shape = jax.ShapeDtypeStruct((t, D_HEADS), F32)
    half_shape = jax.ShapeDtypeStruct((t, D_HEADS), BF16)
    out_shape = [tok_shape] * 4 + [half_shape] * 2
    out_specs = [row(D_HEADS)] * 6
    if not has_vres:
        out_shape.append(tok_shape)
        out_specs.append(row(D_HEADS))
    scratch = [pltpu.VMEM((tm, D_HEADS), F32) for _ in range(8)]
    return pl.pallas_call(
        functools.partial(_rwkv_chunk_kernel, seq=seq, tm=tm, has_vres=has_vres),
        grid=(t // tm,),
        in_specs=in_specs,
        out_specs=out_specs,
        out_shape=out_shape,
        scratch_shapes=scratch,
        compiler_params=_cparams("parallel"),
        name="rwkv_chunk_vres" if has_vres else "rwkv_chunk",
    )(*inputs)


def _rwkv_scan_kernel(q1_ref, y0_ref, m_ref, n0_ref, y_ref, st_ref, *, n_chunks):
    @pl.when(pl.program_id(1) == 0)
    def _():
        st_ref[...] = jnp.zeros_like(st_ref)

    left = lax.broadcasted_iota(jnp.int32, (CHUNK, PAIR), 1) < HEAD_DIM

    def bd(x):
        z = jnp.zeros_like(x)
        return jnp.concatenate([jnp.where(left, x, z), jnp.where(left, z, x)], axis=0)

    n_pairs = D_HEADS // PAIR
    st = [st_ref[:, p * PAIR:(p + 1) * PAIR] for p in range(n_pairs)]
    for c in range(n_chunks):
        rs = slice(c * CHUNK, (c + 1) * CHUNK)
        for p in range(n_pairs):
            ls = slice(p * PAIR, (p + 1) * PAIR)
            lh, ll = _split2(jnp.concatenate([q1_ref[rs, ls], m_ref[rs, ls]], axis=0))
            sh, sl = _split2(st[p])
            bh = bd(sh)
            res = _dot(lh, bh) + (_dot(ll, bh) + _dot(lh, bd(sl)))
            y_ref[rs, ls] = res[:CHUNK] + y0_ref[rs, ls]
            st[p] = res[CHUNK:] + n0_ref[rs, ls]
    for p in range(n_pairs):
        st_ref[:, p * PAIR:(p + 1) * PAIR] = st[p]


def _rwkv_scan(q1, y0, m, n0, batch, seq):
    t = q1.shape[0]
    cb = 8
    steps = seq // (cb * CHUNK)
    tok_spec = pl.BlockSpec((cb * CHUNK, D_HEADS), lambda b, j: (b * steps + j, 0))
    return pl.pallas_call(
        functools.partial(_rwkv_scan_kernel, n_chunks=cb),
        grid=(batch, steps),
        in_specs=[tok_spec] * 4,
        out_specs=tok_spec,
        out_shape=jax.ShapeDtypeStruct((t, D_HEADS), F32),
        scratch_shapes=[pltpu.VMEM((HEAD_DIM, D_HEADS), F32)],
        compiler_params=_cparams("parallel", "arbitrary"),
        name="rwkv_scan",
    )(q1, y0, m, n0)


def _mix_kernel(y_ref, g_ref, bonus_ref, ya_ref, x_ref, wgate_ref, wa_ref, wb_ref, wo_ref,
                gng_ref, gnb_ref, lng_ref, lnb_ref, o_ref):
    y = y_ref[...]
    mean = _seg_sum(y) * (1.0 / HEAD_DIM)
    yc = y - mean
    var = _seg_sum(yc * yc) * (1.0 / HEAD_DIM)
    yn = yc * lax.rsqrt(var + GN_EPS) * gng_ref[...] + gnb_ref[...]
    yr = (yn + bonus_ref[...].astype(F32)) * g_ref[...].astype(F32)
    za = _dot(ya_ref[...], wa_ref[...])
    zb = _dot(yr.astype(BF16), wb_ref[...])
    x = x_ref[...]
    gate = _sigmoid(_dot(x.astype(BF16), wgate_ref[...]))
    mixed_in = gate[:, :D_MODEL] * za + gate[:, D_MODEL:] * zb
    mixed = _dot(mixed_in.astype(BF16), wo_ref[...])
    o_ref[...] = _layer_norm(DEEPNORM_ALPHA * x + mixed, lng_ref[...], lnb_ref[...])


def _mix(y, g, bonus, y_attn, x, lp):
    t = x.shape[0]
    tm = min(1024, t)
    row = lambda w: pl.BlockSpec((tm, w), lambda i: (i, 0))
    params = [lp["wgate"], lp["w_a"], lp["w_b"], lp["w_o"], lp["gn_g"], lp["gn_b"], lp["ln_g0"], lp["ln_b0"]]
    return pl.pallas_call(
        _mix_kernel,
        grid=(t // tm,),
        in_specs=[row(D_HEADS), row(D_HEADS), row(D_HEADS), row(D_HEADS), row(D_MODEL)]
        + [_full(p.shape) for p in params],
        out_specs=row(D_MODEL),
        out_shape=jax.ShapeDtypeStruct((t, D_MODEL), F32),
        compiler_params=_cparams("parallel"),
        name="branch_mix",
    )(y, g, bonus, y_attn, x, *params)


MOE_SLAB = 2048
MOE_WIN = 160
MOE_EBLK = 4
MOE_ROWS = 2 * MOE_SLAB + MOE_WIN
RANK_BLK = 256
ROUTE_SLOTS = 2


def _route_kernel(x_ref, wr_ref, br_ref, tri_ref, w_ref, pos_ref, seg_ref):
    x = x_ref[...]
    tn = x.shape[0]
    lane = lax.broadcasted_iota(jnp.int32, (tn, LANES), 1)
    lanef = lane.astype(F32)
    hp = lax.Precision.HIGHEST
    elog = _dot3(x, wr_ref[...]) + br_ref[...]
    big = float(LANES)
    glog = jnp.where((lane >= N_EXPERTS) & (lane < N_EXPERTS + N_GROUPS), elog, -jnp.inf)
    gmax = jnp.max(glog, axis=-1, keepdims=True)
    g_gate = 1.0 / jnp.sum(jnp.exp(glog - gmax), axis=-1, keepdims=True)
    grp = jnp.min(jnp.where(glog == gmax, lanef, big), axis=-1, keepdims=True) - N_EXPERTS
    in_grp = (lanef >= grp * EXPERTS_PER_GROUP) & (lanef < (grp + 1.0) * EXPERTS_PER_GROUP)
    sel = jnp.where(in_grp, elog, -jnp.inf)
    v1 = jnp.max(sel, axis=-1, keepdims=True)
    i1 = jnp.min(jnp.where(sel == v1, lanef, big), axis=-1, keepdims=True)
    sel2 = jnp.where(lanef == i1, -jnp.inf, sel)
    v2 = jnp.max(sel2, axis=-1, keepdims=True)
    i2 = jnp.min(jnp.where(sel2 == v2, lanef, big), axis=-1, keepdims=True)
    e2 = jnp.exp(v2 - v1)
    w1 = g_gate / (1.0 + e2)
    w2 = g_gate * e2 / (1.0 + e2)
    tile = (SUBLANES, LANES)
    pick = jnp.where(lax.broadcasted_iota(jnp.int32, tile, 0) == lax.broadcasted_iota(jnp.int32, tile, 1), 1.0, 0.0)

    def token_rows(c0, c1):
        cols = jnp.where(lane == 0, c0, jnp.where(lane == 1, c1, 0.0))
        rows = lax.dot_general(pick, cols, (((1,), (1,)), ((), ())), precision=hp, preferred_element_type=F32)
        return jnp.concatenate([rows[0:1], rows[1:2]], axis=1)

    w_ref[0] = token_rows(w1, w2)

    hit1 = lanef == i1
    hit2 = lanef == i2
    onehot = jnp.where(hit1 | hit2, 1.0, 0.0)
    tri = tri_ref[...]
    carry = jnp.zeros((1, LANES), F32)
    ranks = []
    for j in range(tn // RANK_BLK):
        blk = onehot[j * RANK_BLK:(j + 1) * RANK_BLK]
        ranks.append(_dot(tri, blk.astype(BF16)) + carry)
        carry = carry + jnp.sum(blk, axis=0, keepdims=True)
    rank = jnp.concatenate(ranks, axis=0)
    cnt = carry
    li = lax.broadcasted_iota(jnp.int32, (LANES, LANES), 0)
    lj = lax.broadcasted_iota(jnp.int32, (LANES, LANES), 1)
    before = jnp.where(li < lj, 1.0, 0.0)
    off = jnp.dot(jnp.broadcast_to(cnt, tile), before, precision=hp, preferred_element_type=F32)[0:1]
    slot = (off + rank) * SUBLANES
    pos1 = jnp.sum(jnp.where(hit1, slot, 0.0), axis=-1, keepdims=True)
    pos2 = jnp.sum(jnp.where(hit2, slot, 0.0), axis=-1, keepdims=True)
    pos_ref[0] = token_rows(pos1, pos2).astype(jnp.int32)
    seg_row = lax.broadcasted_iota(jnp.int32, tile, 0)
    seg_ref[0] = jnp.where(seg_row == 0, off, jnp.where(seg_row == 1, cnt, 0.0)).astype(jnp.int32)


def _route(x, lp):
    t = x.shape[0]
    tn = min(MOE_SLAB, t)
    n_slabs = t // tn
    small = [lp["w_route"], lp["b_route"], lp["tri_rank"]]
    row = lambda w: pl.BlockSpec((tn, w), lambda i: (i, 0))
    return pl.pallas_call(
        _route_kernel,
        grid=(n_slabs,),
        in_specs=[row(D_MODEL)] + [_full(p.shape) for p in small],
        out_specs=[pl.BlockSpec((1, 1, ROUTE_SLOTS * tn), lambda i: (i, 0, 0)),
                   pl.BlockSpec((1, 1, ROUTE_SLOTS * tn), lambda i: (i, 0, 0)),
                   pl.BlockSpec((1, SUBLANES, LANES), lambda i: (i, 0, 0))],
        out_shape=[jax.ShapeDtypeStruct((n_slabs, 1, ROUTE_SLOTS * tn), F32),
                   jax.ShapeDtypeStruct((n_slabs, 1, ROUTE_SLOTS * tn), jnp.int32),
                   jax.ShapeDtypeStruct((n_slabs, SUBLANES, LANES), jnp.int32)],
        compiler_params=_cparams("parallel"),
        name="moe_route",
    )(x, *small)


def _moe_sparse_kernel(off_ref, cnt_ref, pos_ref, w_ref, x_ref, eg_ref, eu_ref, ed_ref, lng_ref, lnb_ref,
                       p_ref, wp_ref, wg_ref, lng2_ref, lnb2_ref, o_ref, xs_ref):
    s = pl.program_id(0)
    e = pl.program_id(1)
    tn = x_ref.shape[0]

    n_feat = D_MODEL // LANES

    def tile_rows(first_row):
        return pl.ds(pl.multiple_of(first_row, SUBLANES), SUBLANES)

    @pl.when((s == 0) & (e == 0))
    def _():
        xs_ref[pl.ds(2 * tn * SUBLANES, MOE_WIN * SUBLANES), :] = jnp.zeros((MOE_WIN * SUBLANES, LANES), F32)

    @pl.when(e == 0)
    def _():
        def scatter(i, carry):
            t0 = pl.multiple_of(i * SUBLANES, SUBLANES)
            tiles = x_ref[pl.ds(t0, SUBLANES), :].reshape(SUBLANES, n_feat, LANES)
            for j in range(SUBLANES):
                xs_ref[tile_rows(pos_ref[0, 0, t0 + j]), :] = tiles[j]
                xs_ref[tile_rows(pos_ref[0, 0, tn + t0 + j]), :] = tiles[j]
            return carry
        lax.fori_loop(0, tn // SUBLANES, scatter, 0)

    ridx = lax.broadcasted_iota(jnp.int32, (MOE_WIN, 1), 0)

    base = s * N_EXPERTS + e * MOE_EBLK
    offs = [off_ref[base + j] for j in range(MOE_EBLK)]
    cnts = [cnt_ref[base + j] for j in range(MOE_EBLK)]

    def feat_rows(j, w, f):
        return pl.ds((offs[j] + w * MOE_WIN) * SUBLANES + f, MOE_WIN, stride=SUBLANES)

    def load_win(j, w):
        return jnp.concatenate([xs_ref[feat_rows(j, w, f), :] for f in range(n_feat)], axis=1)

    def hidden(j, xw):
        xb = xw.astype(BF16)
        hg = _dot(xb, eg_ref[j])
        hu = _dot(xb, eu_ref[j])
        return (hg * _sigmoid(hg) * hu).astype(BF16)

    def project(j, w, xw, hh):
        yw = _dot(hh, ed_ref[j])
        return jnp.where(ridx + w * MOE_WIN < cnts[j], yw, xw)

    def expert(j, w, xw):
        return project(j, w, xw, hidden(j, xw))

    def store_win(j, w, yw):
        for f in range(n_feat):
            xs_ref[feat_rows(j, w, f), :] = yw[:, f * LANES:(f + 1) * LANES]

    xw = [None] * MOE_EBLK
    hh = [None] * MOE_EBLK
    for j in range(MOE_EBLK + 1):
        if j < MOE_EBLK:
            xw[j] = load_win(j, 0)
            hh[j] = hidden(j, xw[j])
        if j > 0:
            store_win(j - 1, 0, project(j - 1, 0, xw[j - 1], hh[j - 1]))

    for j in range(MOE_EBLK):
        def window(w, carry, j=j):
            store_win(j, w, expert(j, w, load_win(j, w)))
            return carry
        lax.fori_loop(1, (cnts[j] + MOE_WIN - 1) // MOE_WIN, window, 0)

    @pl.when(e == N_EXPERTS // MOE_EBLK - 1)
    def _():
        def combine(i, carry):
            t0 = pl.multiple_of(i * SUBLANES, SUBLANES)
            tiles = []
            for j in range(SUBLANES):
                y1 = xs_ref[tile_rows(pos_ref[0, 0, t0 + j]), :]
                y2 = xs_ref[tile_rows(pos_ref[0, 0, tn + t0 + j]), :]
                tiles.append(w_ref[0, 0, t0 + j] * y1 + w_ref[0, 0, tn + t0 + j] * y2)
            o_ref[pl.ds(t0, SUBLANES), :] = jnp.stack(tiles, axis=0).reshape(SUBLANES, D_MODEL)
            return carry
        lax.fori_loop(0, tn // SUBLANES, combine, 0)
        def finish(j, carry):
            rs = pl.ds(pl.multiple_of(j * RANK_BLK, RANK_BLK), RANK_BLK)
            x2 = _layer_norm(DEEPNORM_ALPHA * x_ref[rs, :] + o_ref[rs, :], lng_ref[...], lnb_ref[...])
            emb = _dot(p_ref[rs, :].astype(BF16), wp_ref[...])
            gate = _sigmoid(_dot(x2.astype(BF16), wg_ref[...]))
            o_ref[rs, :] = _layer_norm(DEEPNORM_ALPHA * x2 + emb * gate, lng2_ref[...], lnb2_ref[...])
            return carry
        lax.fori_loop(0, tn // RANK_BLK, finish, 0)


def _moe_sparse(x, p, layer, lp):
    t = x.shape[0]
    tn = min(MOE_SLAB, t)
    n_slabs = t // tn
    w, pos, seg = _route(x, lp)
    off = seg[:, 0, :N_EXPERTS].reshape(-1)
    cnt = seg[:, 1, :N_EXPERTS].reshape(-1)
    smem_row = pl.BlockSpec((1, 1, ROUTE_SLOTS * tn), lambda s, e, *_: (s, 0, 0), memory_space=pltpu.SMEM)
    slab = pl.BlockSpec((tn, D_MODEL), lambda s, e, *_: (s, 0), pipeline_mode=pl.Buffered(1))
    const = lambda a: pl.BlockSpec(a.shape, lambda s, e, *_: (0,) * a.ndim, pipeline_mode=pl.Buffered(1))
    p_slab = pl.BlockSpec((None, tn, p.shape[2]), lambda s, e, *_: (layer, s, 0), pipeline_mode=pl.Buffered(1))
    tail = [lp["w_ple"], lp["w_pg"], lp["ln_g2"], lp["ln_b2"]]
    grid_spec = pltpu.PrefetchScalarGridSpec(
        num_scalar_prefetch=2,
        grid=(n_slabs, N_EXPERTS // MOE_EBLK),
        in_specs=[smem_row, smem_row, slab,
                  pl.BlockSpec((MOE_EBLK, D_MODEL, D_EXPERT), lambda s, e, *_: (e, 0, 0)),
                  pl.BlockSpec((MOE_EBLK, D_MODEL, D_EXPERT), lambda s, e, *_: (e, 0, 0)),
                  pl.BlockSpec((MOE_EBLK, D_EXPERT, D_MODEL), lambda s, e, *_: (e, 0, 0)),
                  const(lp["ln_g1"]), const(lp["ln_b1"]), p_slab] + [const(a) for a in tail],
        out_specs=slab,
        scratch_shapes=[pltpu.VMEM((MOE_ROWS * SUBLANES, LANES), F32)],
    )
    return pl.pallas_call(
        _moe_sparse_kernel,
        grid_spec=grid_spec,
        out_shape=jax.ShapeDtypeStruct((t, D_MODEL), F32),
        compiler_params=_cparams("arbitrary", "arbitrary"),
        name="moe_sparse",
    )(off, cnt, pos, w, x, lp["e_gate"], lp["e_up"], lp["e_down"], lp["ln_g1"], lp["ln_b1"],
      p, *tail)


def _pad_to(a, axis, size):
    pad = [(0, 0)] * a.ndim
    pad[axis] = (0, size - a.shape[axis])
    return jnp.pad(a, pad)


def _row(v):
    return v.reshape(1, -1).astype(F32)


def _layer_params(i, w_in, tok_mix, decay_base, decay_up, aaa_base, aaa_up, gate_up, k_k, k_a, r_k,
                  vres_base, vres_down, vres_up, gn_g, gn_b, w_branch_attn, w_branch_rwkv, w_out,
                  router_grp, router_grp_bias, router_exp, router_exp_bias, exp_gate, exp_up, exp_down,
                  ple_proj, ple_gate, ln_g, ln_b, consts):
    d = D_HEADS
    w = w_in[i]
    rw0 = 3 * d
    gate0 = rw0 + 3 * d + DECAY_LORA + AAA_LORA + GATE_LORA
    wqkv = jnp.concatenate([w[:, 0:d] * (HEAD_DIM ** -0.5), w[:, d:rw0]], axis=1).astype(BF16)
    wr = w[:, rw0:gate0]
    lora0 = 3 * d
    pieces = [wr[:, :lora0],
              _pad_to(wr[:, lora0:lora0 + DECAY_LORA], 1, LANES),
              _pad_to(wr[:, lora0 + DECAY_LORA:lora0 + DECAY_LORA + AAA_LORA], 1, LANES),
              _pad_to(wr[:, lora0 + DECAY_LORA + AAA_LORA:], 1, 2 * LANES)]
    wrw = jnp.concatenate(pieces, axis=1).astype(BF16)
    mu = tok_mix[i]
    mu_pieces = [mu[:lora0],
                 _pad_to(mu[lora0:lora0 + DECAY_LORA], 0, LANES),
                 _pad_to(mu[lora0 + DECAY_LORA:lora0 + DECAY_LORA + AAA_LORA], 0, LANES),
                 _pad_to(mu[lora0 + DECAY_LORA + AAA_LORA:], 0, 2 * LANES)]
    lp = dict(consts)
    lp.update(
        wqkv=wqkv, wrw=wrw, wgate=w[:, gate0:].astype(BF16),
        mu=_row(jnp.concatenate(mu_pieces)),
        decay_base=_row(decay_base[i]), decay_up=_pad_to(decay_up[i], 0, LANES).astype(F32),
        aaa_base=_row(aaa_base[i]), aaa_up=_pad_to(aaa_up[i], 0, LANES).astype(F32),
        gate_up=_pad_to(gate_up[i], 0, 2 * LANES).astype(F32),
        k_k=_row(k_k[i]), k_a=_row(k_a[i]), r_k=_row(r_k[i]),
        gn_g=_row(gn_g[i]), gn_b=_row(gn_b[i]),
        w_a=w_branch_attn[i].astype(BF16), w_b=w_branch_rwkv[i].astype(BF16), w_o=w_out[i].astype(BF16),
        w_route=_pad_to(jnp.concatenate([router_exp[i], router_grp[i]], axis=1), 1, LANES).astype(F32),
        b_route=_row(_pad_to(jnp.concatenate([router_exp_bias[i], router_grp_bias[i]]), 0, LANES)),
        e_gate=exp_gate[i].astype(BF16), e_up=exp_up[i].astype(BF16), e_down=exp_down[i].astype(BF16),
        w_ple=ple_proj[i].astype(BF16), w_pg=ple_gate[i].astype(BF16),
        ln_g0=_row(ln_g[i, 0]), ln_b0=_row(ln_b[i, 0]),
        ln_g1=_row(ln_g[i, 1]), ln_b1=_row(ln_b[i, 1]),
        ln_g2=_row(ln_g[i, 2]), ln_b2=_row(ln_b[i, 2]),
    )
    if i > 0:
        lp.update(vres_base=_row(vres_base[i - 1]),
                  vres_down=_pad_to(vres_down[i - 1], 1, LANES).astype(BF16),
                  vres_up=_pad_to(vres_up[i - 1], 0, LANES).astype(BF16))
    return lp


def _const_mats(tm):
    tok = jnp.arange(tm)
    same = (tok[:, None] // CHUNK) == (tok[None, :] // CHUNK)
    tri = (same & (tok[:, None] >= tok[None, :])).astype(BF16)
    rtok = jnp.arange(RANK_BLK)
    tri_rank = (rtok[None, :] < rtok[:, None]).astype(BF16)
    return dict(tri=tri, tri_rank=tri_rank)


def kernel(x, p, ln_in_g, ln_in_b, rel_bias, w_in, tok_mix, decay_base, decay_up, aaa_base, aaa_up,
           gate_up, k_k, k_a, r_k, vres_base, vres_down, vres_up, gn_g, gn_b, w_branch_attn,
           w_branch_rwkv, w_out, router_grp, router_grp_bias, router_exp, router_exp_bias, exp_gate,
           exp_up, exp_down, ple_proj, ple_gate, ln_g, ln_b):
    batch, seq, _ = x.shape
    t = batch * seq
    consts = _const_mats(256)
    bias_tabs = _attn_bias_tables(rel_bias)
    xt = x.reshape(t, D_MODEL)
    pt = p.reshape(p.shape[0], t, p.shape[-1])
    v_first = None
    for i in range(DEPTH):
        lp = _layer_params(i, w_in, tok_mix, decay_base, decay_up, aaa_base, aaa_up, gate_up, k_k, k_a,
                           r_k, vres_base, vres_down, vres_up, gn_g, gn_b, w_branch_attn, w_branch_rwkv,
                           w_out, router_grp, router_grp_bias, router_exp, router_exp_bias, exp_gate,
                           exp_up, exp_down, ple_proj, ple_gate, ln_g, ln_b, consts)
        if i == 0:
            xt, qkv, rw = _proj(xt, _row(ln_in_g), _row(ln_in_b), lp["wqkv"], lp["wrw"], apply_ln=True)
        else:
            qkv, rw = _proj(xt, _row(ln_in_g), _row(ln_in_b), lp["wqkv"], lp["wrw"], apply_ln=False)
        y_attn = _attention(qkv, bias_tabs, batch, seq)
        if i == 0:
            q1, y0, m, n0, g, bonus, v_first = _rwkv_chunk(rw, lp, seq, None)
        else:
            q1, y0, m, n0, g, bonus = _rwkv_chunk(rw, lp, seq, v_first)
        y = _rwkv_scan(q1, y0, m, n0, batch, seq)
        xt = _mix(y, g, bonus, y_attn, xt, lp)
        xt = _moe_sparse(xt, pt, i, lp)
    return xt.reshape(batch, seq, D_MODEL)
```
